```python
import math
import jax, jax.numpy as jnp
from jax import lax
import numpy as np

D_MODEL = 1024
BATCH = 32
SEQ = 256
DEPTH = 2
DEC_BATCH = 2
DEC_SEQ = 4096
PAST_LEN = 256

GRID_W = 64
N_EVEN = (DEPTH + 1) // 2
N_ODD = DEPTH // 2
EPS = 1e-6
CONV_W = 3
D_A = D_MODEL
H_A = 16
P_A = D_A // H_A
N_A = 128
G_A = 2
CHUNK = 128
D_XBC = D_A + 2 * G_A * N_A
DT_MIN = 1e-3
DT_MAX = 1e-1
D_B = D_MODEL
HY_EMB = 33
HY_BANDS = (HY_EMB - 1) // 2
HY_HID = 64
HY_TARGET = 1e-2
HY_DECAY_PCT_HI = 0.3
HY_DECAY_PCT_LO = 1.5
D_C = D_MODEL
H_C = 16
HD_C = D_C // H_C
WIN_H = 8
WIN_W = 16
N_IN_E = D_A + D_XBC + 2 * H_A + 3 * D_B + D_B
SPLIT_E = (D_A, D_A + D_XBC, D_A + D_XBC + 2 * H_A, D_A + D_XBC + 2 * H_A + 3 * D_B)
N_IN_O = 4 * D_C

kernel_name = "hybrid_ssd_hyena_natten_diffusion_step"


def rms_norm(x, g):
    xf = x.astype(jnp.float32)
    y = xf * lax.rsqrt(jnp.mean(xf * xf, axis=-1, keepdims=True) + EPS)
    return (y * g.astype(jnp.float32)).astype(x.dtype)


def ada_mod(cvec, w, b):
    m = jax.nn.silu(cvec) @ w + b
    shift, scale, gate = jnp.split(m, 3, axis=-1)
    return shift[:, None], scale[:, None], gate[:, None]


def depthwise_conv(x, w, b):
    L = x.shape[1]
    pad = CONV_W // 2
    xp = jnp.pad(x, ((0, 0), (pad, CONV_W - 1 - pad), (0, 0)))
    out = b
    for j in range(CONV_W):
        out = out + xp[:, j:j + L] * w[j]
    return out


def segsum(x):
    T = x.shape[-1]
    cs = jnp.cumsum(x, axis=-1)
    diff = cs[..., :, None] - cs[..., None, :]
    mask = jnp.tril(jnp.ones((T, T), dtype=bool))
    return jnp.where(mask, diff, -jnp.inf)


def ssd_scan(x, dt, a, bh, ch, init):
    b, L, h, p = x.shape
    n = bh.shape[-1]
    nc = L // CHUNK
    f32 = jnp.float32
    xdt = (x.astype(f32) * dt[..., None]).reshape(b, nc, CHUNK, h, p)
    adt = (a * dt).reshape(b, nc, CHUNK, h).transpose(0, 3, 1, 2)
    bc = bh.astype(f32).reshape(b, nc, CHUNK, h, n)
    cc = ch.astype(f32).reshape(b, nc, CHUNK, h, n)
    acs = jnp.cumsum(adt, axis=-1)
    lmat = jnp.exp(segsum(adt))
    scores = jnp.einsum('bclhn,bcshn->bhcls', cc, bc) * lmat
    y_diag = jnp.einsum('bhcls,bcshp->bclhp', scores, xdt)
    decay_states = jnp.exp(acs[..., -1:] - acs)
    states = jnp.einsum('bclhn,bhcl,bclhp->bchpn', bc, decay_states, xdt)
    states = jnp.concatenate([init.astype(f32)[:, None], states], axis=1)
    chunk_a = jnp.pad(acs[..., -1], ((0, 0), (0, 0), (1, 0)))
    decay_chunk = jnp.exp(segsum(chunk_a))
    new_states = jnp.einsum('bhzc,bchpn->bzhpn', decay_chunk, states)
    y_off = jnp.einsum('bclhn,bchpn->bclhp', cc, new_states[:, :-1]) * jnp.exp(acs).transpose(0, 2, 3, 1)[..., None]
    return (y_diag + y_off).reshape(b, L, h, p), new_states[:, -1]


def hyena_filter(L, w1, b1, w2, b2, w3, freq):
    f32 = jnp.float32
    t = jnp.linspace(0.0, 1.0, L, dtype=f32)[:, None]
    w = 2.0 * math.pi * jnp.arange(L, dtype=f32)[:, None] / L
    f = jnp.linspace(1e-4, HY_BANDS - 1, HY_BANDS, dtype=f32)[None]
    feats = jnp.concatenate([t, jnp.cos(f * w), -jnp.sin(f * w)], axis=-1)
    hdn = jnp.sin(freq * (feats @ w1 + b1))
    hdn = jnp.sin(freq * (hdn @ w2 + b2))
    filt = (hdn @ w3).astype(f32)
    deltas = jnp.linspace(math.log(HY_TARGET) / HY_DECAY_PCT_HI, math.log(HY_TARGET) / HY_DECAY_PCT_LO, D_B, dtype=f32)
    decay = jnp.exp(-t * jnp.abs(deltas)[None])
    h_fwd = filt[:, :D_B] * decay
    h_bwd = filt[:, D_B:] * decay
    return jnp.concatenate([h_fwd, jnp.zeros((1, D_B), f32), h_bwd[:0:-1]], axis=0)


def fft_long_conv(u, k, bias):
    L = u.shape[1]
    n = 2 * L
    uf = jnp.fft.rfft(u.astype(jnp.float32), n=n, axis=1)
    kf = jnp.fft.rfft(k.astype(jnp.float32), n=n, axis=0)
    y = jnp.fft.irfft(uf * kf[None], n=n, axis=1)[:, :L]
    return (y + u.astype(jnp.float32) * bias.astype(jnp.float32)).astype(u.dtype)


def mixer_ssd_hyena(h, init_state, w_in, w_out, conv_a_w, conv_a_b, dt_bias, a_log, d_skip, norm_a_w,
                    conv_b_w, conv_b_b, hf_w1, hf_b1, hf_w2, hf_b2, hf_w3, hf_freq, hy_bias):
    b, L, _ = h.shape
    z_a, xbc, dt_raw, u_b, g_b = jnp.split(h @ w_in, SPLIT_E, axis=-1)
    xbc = jax.nn.silu(depthwise_conv(xbc, conv_a_w, conv_a_b))
    xa, bm, cm = jnp.split(xbc, (D_A, D_A + G_A * N_A), axis=-1)
    xa = xa.reshape(b, L, H_A, P_A)
    bh = jnp.repeat(bm.reshape(b, L, G_A, N_A), H_A // G_A, axis=2)
    ch = jnp.repeat(cm.reshape(b, L, G_A, N_A), H_A // G_A, axis=2)
    dt_raw = dt_raw.astype(jnp.float32).reshape(b, L, 2, H_A)
    ys, finals = [], []
    for d in range(2):
        rev = (lambda t: jnp.flip(t, axis=1)) if d == 1 else (lambda t: t)
        dt = jax.nn.softplus(dt_raw[:, :, d] + dt_bias[d].astype(jnp.float32))
        a = -jnp.exp(a_log[d].astype(jnp.float32))
        y_d, s_d = ssd_scan(rev(xa), rev(dt), a, rev(bh), rev(ch), init_state[:, d])
        ys.append(rev(y_d) + xa.astype(jnp.float32) * d_skip[d].astype(jnp.float32)[:, None])
        finals.append(s_d)
    y_a = (ys[0] + ys[1]).reshape(b, L, D_A).astype(h.dtype)
    y_a = rms_norm(y_a * jax.nn.silu(z_a), norm_a_w)
    u = depthwise_conv(u_b, conv_b_w, conv_b_b)
    x0, x1, v = jnp.split(u, 3, axis=-1)
    k = hyena_filter(L, hf_w1, hf_b1, hf_w2, hf_b2, hf_w3, hf_freq)
    y_b = x0 * fft_long_conv(v * x1, k, hy_bias)
    y_b = (y_b * jax.nn.silu(g_b)).astype(y_a.dtype)
    out = jnp.concatenate([y_a, y_b], axis=-1) @ w_out
    return out, jnp.stack(finals, axis=1).astype(h.dtype)


def na_context(h, w_in, w_out):
    b, L, _ = h.shape
    q, k, v, g = jnp.split(h @ w_in, 4, axis=-1)
    heads = lambda t: t.reshape(b, L, H_C, HD_C).transpose(0, 2, 1, 3)
    q, k, v = heads(q), heads(k), heads(v)
    s = jnp.einsum('bhqd,bhkd->bhqk', q, k).astype(jnp.float32) * (HD_C ** -0.5)
    p = jax.nn.softmax(s, axis=-1).astype(v.dtype)
    o = jnp.einsum('bhqk,bhkd->bhqd', p, v).transpose(0, 2, 1, 3).reshape(b, L, D_C)
    return (o * jax.nn.silu(g)) @ w_out, k, v


def na_latent(h, ck, cv, rpb, w_in, w_out):
    b, L, _ = h.shape
    R = L // GRID_W
    kh = min(WIN_H, R)
    q, k, v, g = jnp.split(h @ w_in, 4, axis=-1)
    grid = lambda t: t.reshape(b, R, GRID_W, H_C, HD_C).transpose(0, 3, 1, 2, 4)
    q, k, v = grid(q), grid(k), grid(v)
    r_idx = jnp.arange(R)
    c_idx = jnp.arange(GRID_W)
    rows = jnp.clip(r_idx - kh // 2, 0, R - kh)[:, None] + jnp.arange(kh)[None]
    col0 = jnp.clip(c_idx - WIN_W // 2, 0, GRID_W - WIN_W)
    col_in = (c_idx[None, :] >= col0[:, None]) & (c_idx[None, :] < col0[:, None] + WIN_W)
    k_rows = k[:, :, rows]
    v_rows = v[:, :, rows]
    s_loc = jnp.einsum('bhrqd,bhrikd->bhrqik', q, k_rows).astype(jnp.float32) * (HD_C ** -0.5)
    dr_idx = rows - r_idx[:, None] + WIN_H - 1
    dc_idx = jnp.clip(c_idx[None, :] - c_idx[:, None] + WIN_W - 1, 0, 2 * WIN_W - 2)
    bias = rpb[:, dr_idx][..., dc_idx]
    s_loc = s_loc + bias.transpose(0, 1, 3, 2, 4)[None].astype(jnp.float32)
    s_loc = jnp.where(col_in[:, None, :], s_loc, -jnp.inf)
    s_ctx = jnp.einsum('bhrqd,bhkd->bhrqk', q, ck).astype(jnp.float32) * (HD_C ** -0.5)
    s = jnp.concatenate([s_loc.reshape(b, H_C, R, GRID_W, kh * GRID_W), s_ctx], axis=-1)
    p = jax.nn.softmax(s, axis=-1).astype(v.dtype)
    p_loc = p[..., :kh * GRID_W].reshape(b, H_C, R, GRID_W, kh, GRID_W)
    p_ctx = p[..., kh * GRID_W:]
    o = jnp.einsum('bhrqik,bhrikd->bhrqd', p_loc, v_rows) + jnp.einsum('bhrqk,bhkd->bhrqd', p_ctx, cv)
    o = o.transpose(0, 2, 3, 1, 4).reshape(b, L, D_C)
    return (o * jax.nn.silu(g)) @ w_out


def setup_inputs(seed: int = 0) -> dict:
    key = jax.random.key(seed)
    ks = iter(jax.random.split(key, 40))

    def nrm(shape, scale):
        return scale * jax.random.normal(next(ks), shape, jnp.float32)

    x_prompt = nrm((BATCH, SEQ, D_MODEL), 1.0)
    x_sample = nrm((DEC_BATCH, DEC_SEQ, D_MODEL), 1.0)
    state_ssd = nrm((DEC_BATCH, N_EVEN, 2, H_A, P_A, N_A), 0.1)
    cache_k = nrm((DEC_BATCH, N_ODD, H_C, PAST_LEN, HD_C), 1.0)
    cache_v = nrm((DEC_BATCH, N_ODD, H_C, PAST_LEN, HD_C), 1.0)
    c = nrm((DEC_BATCH, D_MODEL), 1.0)
    c_ctx = nrm((D_MODEL,), 1.0)
    norm_w = 1.0 + nrm((DEPTH, D_MODEL), 0.02)
    w_ada = nrm((DEPTH, D_MODEL, 3 * D_MODEL), D_MODEL ** -0.5)
    b_ada = nrm((DEPTH, 3 * D_MODEL), 0.02)
    w_in_e = nrm((N_EVEN, D_MODEL, N_IN_E), D_MODEL ** -0.5)
    w_out_e = nrm((N_EVEN, D_A + D_B, D_MODEL), (D_A + D_B) ** -0.5)
    conv_a_w = nrm((N_EVEN, CONV_W, D_XBC), CONV_W ** -0.5)
    conv_a_b = nrm((N_EVEN, D_XBC), 0.02)
    dt0 = jnp.exp(jax.random.uniform(next(ks), (N_EVEN, 2, H_A), jnp.float32, math.log(DT_MIN), math.log(DT_MAX)))
    dt_bias = dt0 + jnp.log(-jnp.expm1(-dt0))
    a_log = jnp.log(jax.random.uniform(next(ks), (N_EVEN, 2, H_A), jnp.float32, 1.0, 16.0))
    d_skip = 1.0 + nrm((N_EVEN, 2, H_A), 0.1)
    norm_a_w = 1.0 + nrm((N_EVEN, D_A), 0.02)
    conv_b_w = nrm((N_EVEN, CONV_W, 3 * D_B), CONV_W ** -0.5)
    conv_b_b = nrm((N_EVEN, 3 * D_B), 0.02)
    hf_w1 = nrm((N_EVEN, HY_EMB, HY_HID), HY_EMB ** -0.5)
    hf_b1 = nrm((N_EVEN, HY_HID), 0.02)
    hf_w2 = nrm((N_EVEN, HY_HID, HY_HID), HY_HID ** -0.5)
    hf_b2 = nrm((N_EVEN, HY_HID), 0.02)
    hf_w3 = nrm((N_EVEN, HY_HID, 2 * D_B), 0.02 * HY_HID ** -0.5)
    hf_freq = 1.0 + nrm((N_EVEN, HY_HID), 0.1)
    hy_bias = nrm((N_EVEN, D_B), 1.0)
    w_in_o = nrm((N_ODD, D_MODEL, N_IN_O), D_MODEL ** -0.5)
    w_out_o = nrm((N_ODD, D_C, D_MODEL), D_C ** -0.5)
    rpb = nrm((N_ODD, H_C, 2 * WIN_H - 1, 2 * WIN_W - 1), 0.05)
    final_norm_w = 1.0 + nrm((D_MODEL,), 0.02)
    return {"x_prompt": x_prompt, "x_sample": x_sample, "state_ssd": state_ssd, "cache_k": cache_k,
            "cache_v": cache_v, "c": c, "c_ctx": c_ctx, "norm_w": norm_w, "w_ada": w_ada, "b_ada": b_ada,
            "w_in_e": w_in_e, "w_out_e": w_out_e, "conv_a_w": conv_a_w, "conv_a_b": conv_a_b,
            "dt_bias": dt_bias, "a_log": a_log, "d_skip": d_skip, "norm_a_w": norm_a_w,
            "conv_b_w": conv_b_w, "conv_b_b": conv_b_b, "hf_w1": hf_w1, "hf_b1": hf_b1, "hf_w2": hf_w2,
            "hf_b2": hf_b2, "hf_w3": hf_w3, "hf_freq": hf_freq, "hy_bias": hy_bias, "w_in_o": w_in_o,
            "w_out_o": w_out_o, "rpb": rpb, "final_norm_w": final_norm_w}


def reference(x_prompt, x_sample, state_ssd, cache_k, cache_v, c, c_ctx, norm_w, w_ada, b_ada,
              w_in_e, w_out_e, conv_a_w, conv_a_b, dt_bias, a_log, d_skip, norm_a_w,
              conv_b_w, conv_b_b, hf_w1, hf_b1, hf_w2, hf_b2, hf_w3, hf_freq, hy_bias,
              w_in_o, w_out_o, rpb, final_norm_w):
    xp, xs = x_prompt, x_sample
    new_ssd, new_k, new_v = [], [], []
    for l in range(DEPTH):
        i = l // 2
        sh_p, sc_p, g_p = ada_mod(c_ctx[None], w_ada[l], b_ada[l])
        sh_s, sc_s, g_s = ada_mod(c, w_ada[l], b_ada[l])
        hp = rms_norm(xp, norm_w[l]) * (1.0 + sc_p) + sh_p
        hs = rms_norm(xs, norm_w[l]) * (1.0 + sc_s) + sh_s
        if l % 2 == 0:
            ep = (w_in_e[i], w_out_e[i], conv_a_w[i], conv_a_b[i], dt_bias[i], a_log[i], d_skip[i], norm_a_w[i],
                  conv_b_w[i], conv_b_b[i], hf_w1[i], hf_b1[i], hf_w2[i], hf_b2[i], hf_w3[i], hf_freq[i], hy_bias[i])
            zeros_state = jnp.zeros((hp.shape[0], 2, H_A, P_A, N_A), hp.dtype)
            op, fin_ctx = mixer_ssd_hyena(hp, zeros_state, *ep)
            os_, _ = mixer_ssd_hyena(hs, state_ssd[:, i], *ep)
            new_ssd.append(fin_ctx)
        else:
            op, k_ctx, v_ctx = na_context(hp, w_in_o[i], w_out_o[i])
            os_ = na_latent(hs, cache_k[:, i], cache_v[:, i], rpb[i], w_in_o[i], w_out_o[i])
            new_k.append(k_ctx)
            new_v.append(v_ctx)
        xp = xp + g_p * op
        xs = xs + g_s * os_
    y_prompt = rms_norm(xp, final_norm_w)
    y_sample = rms_norm(xs, final_norm_w)
    new_state_ssd = jnp.stack(new_ssd, axis=1)
    new_cache_k = jnp.stack(new_k, axis=1)
    new_cache_v = jnp.stack(new_v, axis=1)
    return (y_prompt, y_sample, new_state_ssd, new_cache_k, new_cache_v)
```

```python
import functools
import math

import jax
import jax.numpy as jnp
import numpy as np
from jax import lax
from jax.experimental import pallas as pl
from jax.experimental.pallas import tpu as pltpu

F32 = jnp.float32
BF16 = jnp.bfloat16

EPS = 1e-6
GRID_W = 64
WIN_H = 8
WIN_W = 16
HD = 64
N_HEADS = 16
N_STATE = 128
N_GROUPS = 2
CHUNK = 128
HY_EMB = 33
HY_BANDS = (HY_EMB - 1) // 2
HY_HID = 64
HY_TARGET = 1e-2
HY_DECAY_PCT_HI = 0.3
HY_DECAY_PCT_LO = 1.5

LANE = 128
SUBLANE = 8
VMEM_LIMIT = 56 * 1024 * 1024

ROW_TILE = 256
DFT_BLOCK = 512
NA_QROWS = 8
NEG_INF = float("-inf")


def _bf(x):
    return x.astype(BF16)


def _dot(a, b):
    return jnp.dot(a, b, preferred_element_type=F32)


def _dot_nt(a, b):
    return lax.dot_general(a, b, (((1,), (1,)), ((), ())), preferred_element_type=F32)


def _split2(x):
    hi = _bf(x)
    lo = _bf(x - hi.astype(F32))
    return hi, lo


def _split3(x):
    hi = _bf(x)
    r = x - hi.astype(F32)
    mid = _bf(r)
    lo = _bf(r - mid.astype(F32))
    return hi, mid, lo


def _dot3(a, b):
    ah, al = _split2(a)
    bh, bl = _split2(b)
    return _dot(ah, bh) + (_dot(ah, bl) + _dot(al, bh))


def _dot_lhs_parts(a, b_exact, parts):
    pieces = _split3(a) if parts == 3 else _split2(a)
    acc = _dot(pieces[0], b_exact)
    for p in pieces[1:]:
        acc = acc + _dot(p, b_exact)
    return acc


def _dot_rhs_parts(a_exact, b, parts):
    pieces = _split3(b) if parts == 3 else _split2(b)
    acc = _dot(a_exact, pieces[0])
    for p in pieces[1:]:
        acc = acc + _dot(a_exact, p)
    return acc


def _silu(x):
    return x * jax.nn.sigmoid(x)


def _rms(x, g):
    ms = jnp.mean(x * x, axis=-1, keepdims=True)
    return x * lax.rsqrt(ms + EPS) * g


def _softplus(x):
    return jnp.maximum(x, 0.0) + jnp.log1p(jnp.exp(-jnp.abs(x)))


def _params(*sem):
    return pltpu.CompilerParams(dimension_semantics=sem, vmem_limit_bytes=VMEM_LIMIT)


def _mods_kernel(c_ref, w_ref, b_ref, o_ref):
    a = _silu(c_ref[...])
    o_ref[0] = _dot3(a, w_ref[0]) + b_ref[0]


def _ada_mods(cvecs, w_ada, b_ada):
    depth, d, n3 = w_ada.shape
    tn = n3 // 4
    return pl.pallas_call(
        _mods_kernel,
        grid=(depth, n3 // tn),
        in_specs=[pl.BlockSpec((SUBLANE, d), lambda l, j: (0, 0)),
                  pl.BlockSpec((1, d, tn), lambda l, j: (l, 0, j)),
                  pl.BlockSpec((1, 1, tn), lambda l, j: (l, 0, j))],
        out_specs=pl.BlockSpec((1, SUBLANE, tn), lambda l, j: (l, 0, j)),
        out_shape=jax.ShapeDtypeStruct((depth, SUBLANE, n3), F32),
        compiler_params=_params("arbitrary", "arbitrary"),
        name="ada_mods",
    )(cvecs, w_ada, b_ada.reshape(depth, 1, n3))


def _proj_in_kernel(x_ref, mod_ref, nw_ref, w_ref, *out_refs, segs):
    d = x_ref.shape[1]
    m = mod_ref[0]
    h = _rms(x_ref[...], nw_ref[...]) * (1.0 + m[:, d:2 * d]) + m[:, 0:d]
    hb = _bf(h)
    for (off, width, tiled), o_ref in zip(segs, out_refs):
        step = 4 * LANE
        for c0 in range(0, width, step):
            cw = min(step, width - c0)
            res = _dot(hb, w_ref[:, off + c0:off + c0 + cw])
            if tiled:
                for t in range(cw // LANE):
                    o_ref[(c0 // LANE) + t] = res[:, t * LANE:(t + 1) * LANE].astype(o_ref.dtype)
            else:
                o_ref[:, c0:c0 + cw] = res.astype(o_ref.dtype)


def _proj_in(x2d, mod, norm_w, w_bf, segs, rows_per_mod, mod_base):
    m_rows, d = x2d.shape
    tm = ROW_TILE
    out_shapes, out_specs = [], []
    for (_, width, tiled, dt) in segs:
        if tiled:
            out_shapes.append(jax.ShapeDtypeStruct((width // LANE, m_rows, LANE), dt))
            out_specs.append(pl.BlockSpec((width // LANE, tm, LANE), lambda i: (0, i, 0)))
        else:
            out_shapes.append(jax.ShapeDtypeStruct((m_rows, width), dt))
            out_specs.append(pl.BlockSpec((tm, width), lambda i: (i, 0)))
    kern = functools.partial(_proj_in_kernel, segs=tuple((o, w, t) for (o, w, t, _) in segs))
    return pl.pallas_call(
        kern,
        grid=(m_rows // tm,),
        in_specs=[pl.BlockSpec((tm, d), lambda i: (i, 0)),
                  pl.BlockSpec((1, 1, 3 * d), lambda i: (mod_base + (i * tm) // rows_per_mod, 0, 0)),
                  pl.BlockSpec((1, d), lambda i: (0, 0)),
                  pl.BlockSpec(w_bf.shape, lambda i: (0, 0), pipeline_mode=pl.Buffered(1))],
        out_specs=out_specs,
        out_shape=out_shapes,
        compiler_params=_params("arbitrary"),
        name="proj_in",
    )(x2d, mod, norm_w, w_bf)


def _ssd_kernel(*refs, has_init, want_final):
    it = iter(refs)
    xm = [next(it), None]
    xp = [next(it), None]
    xn = [next(it), None]
    xm[1], xp[1], xn[1] = next(it), next(it), next(it)
    dtr = [next(it), next(it)]
    cw_ref, cb_ref, dtb_ref, alog_ref, dskip_ref, exp_ref = (next(it) for _ in range(6))
    init_ref = next(it) if has_init else None
    y_refs = [next(it), next(it)]
    fin_ref = next(it) if want_final else None
    s_ref = next(it)

    c = pl.program_id(1)
    nc = pl.num_programs(1)
    q = xm[0].shape[1]
    dm = N_HEADS * HD
    gw = dm // N_GROUPS
    hpg = N_HEADS // N_GROUPS

    @pl.when(c == 0)
    def _():
        if has_init:
            for d in range(2):
                for t in range(dm // LANE):
                    s_ref[d, :, t * LANE:(t + 1) * LANE] = init_ref[0, d, t * LANE:(t + 1) * LANE, :].T
        else:
            s_ref[...] = jnp.zeros(s_ref.shape, F32)

    row = lax.broadcasted_iota(jnp.int32, (q, q), 0)
    col = lax.broadcasted_iota(jnp.int32, (q, q), 1)
    lane = lax.broadcasted_iota(jnp.int32, (q, LANE), 1)
    left = lane < HD
    tri = [(row >= col), (row <= col)]

    for d in range(2):
        chunk = c if d == 0 else nc - 1 - c
        x = xm[d][0]
        nchan = x.shape[1]
        r0 = lax.broadcasted_iota(jnp.int32, (q, nchan), 0)
        keep_prev = (chunk > 0).astype(F32)
        keep_next = (chunk < nc - 1).astype(F32)
        prev_row = xp[d][0][SUBLANE - 1:SUBLANE, :] * keep_prev
        next_row = xn[d][0][0:1, :] * keep_next
        x_prev = jnp.where(r0 == 0, prev_row, pltpu.roll(x, 1, 0))
        x_next = jnp.where(r0 == q - 1, next_row, pltpu.roll(x, q - 1, 0))
        cw = cw_ref[...]
        xs = _silu(cb_ref[...] + x_prev * cw[0:1] + x * cw[1:2] + x_next * cw[2:3])
        xa = xs[:, 0:dm]
        bm = xs[:, dm:dm + N_GROUPS * N_STATE]
        cm = xs[:, dm + N_GROUPS * N_STATE:dm + 2 * N_GROUPS * N_STATE]

        dt = _softplus(dtr[d][0] + dtb_ref[...])
        adt = dt * (-jnp.exp(alog_ref[...]))
        tmat = jnp.where(tri[d], 1.0, 0.0).astype(BF16)
        cs = _dot_rhs_parts(tmat, adt, 3)
        cs_t = cs.T
        dt_t = dt.T
        edge = cs[q - 1:q, :] if d == 0 else cs[0:1, :]
        ex = exp_ref[d]
        e_cs = _dot_lhs_parts(jnp.exp(cs), ex, 2)
        w_st = _dot_lhs_parts(jnp.exp(edge - cs) * dt, ex, 2)
        xw = xa * w_st

        y_parts = []
        s_prev = s_ref[d]
        new_state = []
        for g in range(N_GROUPS):
            bg = bm[:, g * N_STATE:(g + 1) * N_STATE]
            cg = cm[:, g * N_STATE:(g + 1) * N_STATE]
            gmat = _dot_nt(_bf(cg), _bf(bg))
            y_off = _dot(_bf(cg), _bf(s_prev[:, g * gw:(g + 1) * gw]))
            new_state.append(_dot(_bf(bg.T), _bf(xw[:, g * gw:(g + 1) * gw])))
            for pr in range(hpg // 2):
                mh = []
                for j in range(2):
                    k = d * N_HEADS + g * hpg + 2 * pr + j
                    diff = cs[:, k:k + 1] - cs_t[k:k + 1, :]
                    lm = jnp.exp(jnp.where(tri[d], diff, NEG_INF))
                    mh.append(_bf(gmat * lm * dt_t[k:k + 1, :]))
                c0 = g * gw + pr * LANE
                xpair = xa[:, c0:c0 + LANE]
                rhs = jnp.concatenate([_bf(jnp.where(left, xpair, 0.0)), _bf(jnp.where(left, 0.0, xpair))], axis=0)
                y_d = _dot(jnp.concatenate(mh, axis=1), rhs)
                y_parts.append(y_d + y_off[:, pr * LANE:(pr + 1) * LANE] * e_cs[:, c0:c0 + LANE])
        y = jnp.concatenate(y_parts, axis=1) + xa * dskip_ref[d:d + 1, :]
        y_refs[d][0] = y
        e_edge = e_cs[q - 1:q, :] if d == 0 else e_cs[0:1, :]
        s_ref[d] = s_prev * e_edge + jnp.concatenate(new_state, axis=1)

    if want_final:
        @pl.when(c == nc - 1)
        def _():
            for d in range(2):
                for t in range(dm // LANE):
                    fin_ref[0, d, t * LANE:(t + 1) * LANE, :] = s_ref[d, :, t * LANE:(t + 1) * LANE].T


def _ssd(xbc, dt_raw, conv_w, conv_b, dt_bias, a_log, d_skip, init, want_final):
    b, l, nchan = xbc.shape
    dm = N_HEADS * HD
    q = CHUNK
    nc = l // q
    sub_per_chunk = q // SUBLANE
    nsub = l // SUBLANE

    def fwd(c):
        return c

    def bwd(c):
        return nc - 1 - c

    def main(cf):
        return pl.BlockSpec((1, q, nchan), lambda bi, c: (bi, cf(c), 0))

    def prev(cf):
        return pl.BlockSpec((1, SUBLANE, nchan), lambda bi, c: (bi, jnp.maximum(cf(c) * sub_per_chunk - 1, 0), 0))

    def nxt(cf):
        return pl.BlockSpec((1, SUBLANE, nchan),
                            lambda bi, c: (bi, jnp.minimum((cf(c) + 1) * sub_per_chunk, nsub - 1), 0))

    def full(a):
        nd = a.ndim
        return pl.BlockSpec(a.shape, lambda bi, c: (0,) * nd)

    ex = np.zeros((2, LANE, dm), np.float32)
    for d in range(2):
        for h in range(N_HEADS):
            ex[d, d * N_HEADS + h, h * HD:(h + 1) * HD] = 1.0
    ex = jnp.asarray(ex, BF16)
    pad = LANE - 2 * N_HEADS
    dtb = jnp.pad(dt_bias.reshape(1, 2 * N_HEADS), ((0, 0), (0, pad)))
    alog = jnp.pad(a_log.reshape(1, 2 * N_HEADS), ((0, 0), (0, pad)))
    dsk = jnp.repeat(d_skip, HD, axis=1)
    cb = conv_b.reshape(1, nchan)

    args = [xbc, xbc, xbc, xbc, xbc, xbc, dt_raw, dt_raw, conv_w, cb, dtb, alog, dsk, ex]
    in_specs = [main(fwd), prev(fwd), nxt(fwd), main(bwd), prev(bwd), nxt(bwd),
                pl.BlockSpec((1, q, LANE), lambda bi, c: (bi, c, 0)),
                pl.BlockSpec((1, q, LANE), lambda bi, c: (bi, nc - 1 - c, 0)),
                full(conv_w), full(cb), full(dtb), full(alog), full(dsk), full(ex)]
    has_init = init is not None
    if has_init:
        args.append(init)
        in_specs.append(pl.BlockSpec((1, 2, dm, N_STATE), lambda bi, c: (bi, 0, 0, 0)))
    out_shapes = [jax.ShapeDtypeStruct((b, l, dm), F32), jax.ShapeDtypeStruct((b, l, dm), F32)]
    out_specs = [pl.BlockSpec((1, q, dm), lambda bi, c: (bi, c, 0)),
                 pl.BlockSpec((1, q, dm), lambda bi, c: (bi, nc - 1 - c, 0))]
    if want_final:
        out_shapes.append(jax.ShapeDtypeStruct((b, 2, dm, N_STATE), F32))
        out_specs.append(pl.BlockSpec((1, 2, dm, N_STATE), lambda bi, c: (bi, 0, 0, 0)))
    return pl.pallas_call(
        functools.partial(_ssd_kernel, has_init=has_init, want_final=want_final),
        grid=(b, nc),
        in_specs=in_specs,
        out_specs=out_specs,
        out_shape=out_shapes,
        scratch_shapes=[pltpu.VMEM((2, N_STATE, dm), F32)],
        compiler_params=_params("arbitrary", "arbitrary"),
        name="ssd_scan",
    )(*args)


def _hyena_tables(l):
    pos = np.abs(np.arange(2 * l, dtype=np.float64) - l)
    t = pos / (l - 1)
    w = 2.0 * math.pi * pos / l
    f = np.linspace(1e-4, HY_BANDS - 1, HY_BANDS)
    feats = np.zeros((2 * l, LANE), np.float64)
    feats[:, 0] = t
    feats[:, 1:1 + HY_BANDS] = np.cos(f[None] * w[:, None])
    feats[:, 1 + HY_BANDS:1 + 2 * HY_BANDS] = -np.sin(f[None] * w[:, None])
    return jnp.asarray(feats, F32)


def _filter_kernel(f_ref, w1_ref, b1_ref, w2_ref, b2_ref, w3_ref, fr_ref, ad_ref, o_ref):
    feats = f_ref[...]
    fr = fr_ref[...]
    h1 = jnp.sin(fr * (_dot3(feats, w1_ref[...]) + b1_ref[...]))
    h2 = jnp.sin(fr * (_dot3(h1, w2_ref[...]) + b2_ref[...]))
    filt = _dot3(h2, w3_ref[...])
    o_ref[...] = filt * jnp.exp(-feats[:, 0:1] * ad_ref[...])


def _hyena_filter_linear(l, w1, b1, w2, b2, w3, freq):
    db = w3.shape[1] // 2
    tr = min(512, l)
    nbk = l // tr
    feats = _hyena_tables(l)
    w1p = jnp.pad(w1, ((0, LANE - HY_EMB), (0, 0)))
    deltas = np.linspace(math.log(HY_TARGET) / HY_DECAY_PCT_HI, math.log(HY_TARGET) / HY_DECAY_PCT_LO, db)
    absd = jnp.asarray(np.abs(deltas)[None], F32)

    def full(a):
        return pl.BlockSpec(a.shape, lambda i: (0, 0))

    b1r, b2r, frr = b1.reshape(1, -1), b2.reshape(1, -1), freq.reshape(1, -1)
    return pl.pallas_call(
        _filter_kernel,
        grid=(2 * nbk,),
        in_specs=[pl.BlockSpec((tr, LANE), lambda i: (i, 0)), full(w1p), full(b1r), full(w2), full(b2r),
                  pl.BlockSpec((HY_HID, db), lambda i: (0, jnp.where(i < nbk, 1, 0))),
                  full(frr), full(absd)],
        out_specs=pl.BlockSpec((tr, db), lambda i: (i, 0)),
        out_shape=jax.ShapeDtypeStruct((2 * l, db), F32),
        compiler_params=_params("arbitrary"),
        name="hyena_filter",
    )(feats, w1p, b1r, w2, b2r, w3, frr, absd)


def _dft_tables(p):
    n = 2 * p
    f = np.arange(p, dtype=np.float64)[:, None] + 0.5
    e = np.arange(n, dtype=np.float64)[None]
    ang = 2.0 * math.pi * f * e / n
    fwd = np.concatenate([np.cos(ang), -np.sin(ang)], axis=0)
    fa = fwd[:, :p]
    fbn = -fwd[:, p:]
    fbn[:, 0] = 0.0
    angt = ang[:, :p].T
    inv = np.concatenate([np.cos(angt), -np.sin(angt)], axis=1) * (2.0 / n)

    def parts(m):
        hi = jnp.asarray(m, F32).astype(BF16)
        lo = (jnp.asarray(m, F32) - hi.astype(F32)).astype(BF16)
        return hi, lo

    return parts(fa), parts(fbn), parts(inv)


def _hyena_kernel(x0_ref, x1_ref, v_ref, g_ref, kl_ref, w0_ref, w1_ref, w2_ref, c0_ref, c1_ref, c2_ref,
                  hb_ref, fah_ref, fal_ref, fbh_ref, fbl_ref, ivh_ref, o_ref, g_scr, u_scr, y_scr, w_scr, *, p):
    l = x0_ref.shape[2]
    nb = l // p
    nseg = 2 * nb - 1
    row = lax.broadcasted_iota(jnp.int32, (l, LANE), 0)

    def conv3(x_ref, w_ref, c_ref):
        x = x_ref[0, 0]
        w = w_ref[...]
        xm1 = jnp.where(row == 0, 0.0, pltpu.roll(x, 1, 0))
        xp1 = jnp.where(row == l - 1, 0.0, pltpu.roll(x, l - 1, 0))
        return c_ref[...] + xm1 * w[0:1] + x * w[1:2] + xp1 * w[2:3]

    @pl.when(pl.program_id(1) == 0)
    def _():
        fah, fal, fbh, fbl = fah_ref[...], fal_ref[...], fbh_ref[...], fbl_ref[...]
        for s in range(nseg):
            k1h, k1l = _split2(kl_ref[(s + 1) * p:(s + 2) * p, :])
            k0h, k0l = _split2(kl_ref[s * p:(s + 1) * p, :])
            g_scr[s] = (_dot(fah, k1h) + (_dot(fah, k1l) + _dot(fal, k1h))
                        + _dot(fbh, k0h) + (_dot(fbh, k0l) + _dot(fbl, k0h)))

    w_scr[...] = conv3(v_ref, w2_ref, c2_ref) * conv3(x1_ref, w1_ref, c1_ref)
    fah = fah_ref[...]
    for j in range(nb):
        u_scr[j] = _dot(fah, _bf(w_scr[j * p:(j + 1) * p, :]))
    x0 = conv3(x0_ref, w0_ref, c0_ref)
    gate = _silu(g_ref[0, 0])
    rt_rows = 64
    for i in range(nb):
        def body(rt, carry):
            r0 = pl.multiple_of(rt * rt_rows, rt_rows)
            acc_re = jnp.zeros((rt_rows, LANE), F32)
            acc_im = jnp.zeros((rt_rows, LANE), F32)
            for j in range(nb):
                s = i - j + nb - 1
                gre = g_scr[s, pl.ds(r0, rt_rows), :]
                gim = g_scr[s, pl.ds(p + r0, rt_rows), :]
                ure = u_scr[j, pl.ds(r0, rt_rows), :]
                uim = u_scr[j, pl.ds(p + r0, rt_rows), :]
                acc_re = acc_re + (gre * ure - gim * uim)
                acc_im = acc_im + (gre * uim + gim * ure)
            y_scr[pl.ds(r0, rt_rows), :] = acc_re
            y_scr[pl.ds(p + r0, rt_rows), :] = acc_im
            return carry
        lax.fori_loop(0, p // rt_rows, body, 0)
        conv = _dot(ivh_ref[...], _bf(y_scr[...]))
        sl = slice(i * p, (i + 1) * p)
        wi = w_scr[sl, :]
        o_ref[0, 0, sl, :] = x0[sl, :] * (conv + wi * hb_ref[...]) * gate[sl, :]


def _hyena(u_t, g_t, klin, conv_w, conv_b, hy_bias, b, l):
    nct = klin.shape[1] // LANE
    p = min(DFT_BLOCK, l)
    nb = l // p
    nseg = 2 * nb - 1
    (fah, fal), (fbh, fbl), (ivh, _) = _dft_tables(p)
    u4 = u_t.reshape(3 * nct, b, l, LANE)
    g4 = g_t.reshape(nct, b, l, LANE)
    cbr = conv_b.reshape(1, -1)
    hbr = hy_bias.reshape(1, -1)

    def act(k):
        return pl.BlockSpec((1, 1, l, LANE), lambda ct, bi: (k * nct + ct, bi, 0, 0))

    def wcol(rows, k):
        return pl.BlockSpec((rows, LANE), lambda ct, bi: (0, k * nct + ct))

    def full(a):
        return pl.BlockSpec(a.shape, lambda ct, bi: (0, 0), pipeline_mode=pl.Buffered(1))

    out = pl.pallas_call(
        functools.partial(_hyena_kernel, p=p),
        grid=(nct, b),
        in_specs=[act(0), act(1), act(2),
                  pl.BlockSpec((1, 1, l, LANE), lambda ct, bi: (ct, bi, 0, 0)),
                  pl.BlockSpec((2 * l, LANE), lambda ct, bi: (0, ct)),
                  wcol(3, 0), wcol(3, 1), wcol(3, 2), wcol(1, 0), wcol(1, 1), wcol(1, 2),
                  pl.BlockSpec((1, LANE), lambda ct, bi: (0, ct)),
                  full(fah), full(fal), full(fbh), full(fbl), full(ivh)],
        out_specs=pl.BlockSpec((1, 1, l, LANE), lambda ct, bi: (ct, bi, 0, 0)),
        out_shape=jax.ShapeDtypeStruct((nct, b, l, LANE), F32),
        scratch_shapes=[pltpu.VMEM((nseg, 2 * p, LANE), F32), pltpu.VMEM((nb, 2 * p, LANE), F32),
                        pltpu.VMEM((2 * p, LANE), F32), pltpu.VMEM((l, LANE), F32)],
        compiler_params=_params("arbitrary", "arbitrary"),
        name="hyena_conv",
    )(u4, u4, u4, g4, klin, conv_w, conv_w, conv_w, cbr, cbr, cbr, hbr, fah, fal, fbh, fbl, ivh)
    return out.reshape(nct, b * l, LANE)


def _proj_out0_kernel(x_ref, yf_ref, yb_ref, z_ref, yh_ref, mod_ref, naw_ref, w_ref, o_ref):
    d = x_ref.shape[1]
    ya = _rms((yf_ref[...] + yb_ref[...]) * _silu(z_ref[...]), naw_ref[...])
    yh = jnp.concatenate([yh_ref[t] for t in range(yh_ref.shape[0])], axis=1)
    da = ya.shape[1]
    acc = _dot(_bf(ya), w_ref[0:da, :]) + _dot(_bf(yh), w_ref[da:, :])
    gate = mod_ref[0][:, 2 * d:3 * d]
    o_ref[...] = x_ref[...] + gate * acc


def _proj_out0(x2d, y_f, y_b, z, yh_t, mod, norm_a_w, w_bf, rows_per_mod, mod_base):
    m_rows, d = x2d.shape
    tm = ROW_TILE
    da = y_f.shape[1]
    nt = yh_t.shape[0]

    def rowspec(wd):
        return pl.BlockSpec((tm, wd), lambda i: (i, 0))

    return pl.pallas_call(
        _proj_out0_kernel,
        grid=(m_rows // tm,),
        in_specs=[rowspec(d), rowspec(da), rowspec(da), rowspec(da),
                  pl.BlockSpec((nt, tm, LANE), lambda i: (0, i, 0)),
                  pl.BlockSpec((1, 1, 3 * d), lambda i: (mod_base + (i * tm) // rows_per_mod, 0, 0)),
                  pl.BlockSpec((1, da), lambda i: (0, 0)),
                  pl.BlockSpec(w_bf.shape, lambda i: (0, 0), pipeline_mode=pl.Buffered(1))],
        out_specs=rowspec(d),
        out_shape=jax.ShapeDtypeStruct((m_rows, d), F32),
        compiler_params=_params("arbitrary"),
        name="proj_out0",
    )(x2d, y_f, y_b, z, yh_t, mod, norm_a_w, w_bf)


def _proj_out1_kernel(x_ref, o_ref_in, g_ref, mod_ref, fw_ref, w_ref, y_ref):
    d = x_ref.shape[1]
    o = jnp.concatenate([o_ref_in[t] for t in range(o_ref_in.shape[0])], axis=1)
    a = o * _silu(g_ref[...])
    acc = _dot(_bf(a), w_ref[...])
    gate = mod_ref[0][:, 2 * d:3 * d]
    y_ref[...] = _rms(x_ref[...] + gate * acc, fw_ref[...])


def _proj_out1(x2d, o_t, g, mod, final_w, w_bf, rows_per_mod, mod_base):
    m_rows, d = x2d.shape
    tm = ROW_TILE
    nt = o_t.shape[0]
    return pl.pallas_call(
        _proj_out1_kernel,
        grid=(m_rows // tm,),
        in_specs=[pl.BlockSpec((tm, d), lambda i: (i, 0)),
                  pl.BlockSpec((nt, tm, LANE), lambda i: (0, i, 0)),
                  pl.BlockSpec((tm, g.shape[1]), lambda i: (i, 0)),
                  pl.BlockSpec((1, 1, 3 * d), lambda i: (mod_base + (i * tm) // rows_per_mod, 0, 0)),
                  pl.BlockSpec((1, d), lambda i: (0, 0)),
                  pl.BlockSpec(w_bf.shape, lambda i: (0, 0), pipeline_mode=pl.Buffered(1))],
        out_specs=pl.BlockSpec((tm, d), lambda i: (i, 0)),
        out_shape=jax.ShapeDtypeStruct((m_rows, d), F32),
        compiler_params=_params("arbitrary"),
        name="proj_out1",
    )(x2d, o_t, g, mod, final_w, w_bf)


def _ctx_attn_kernel(q_ref, k_ref, v_ref, o_ref, ck_ref, cv_ref):
    q = q_ref[0, 0]
    k = k_ref[0, 0]
    v = v_ref[0, 0]
    scale = HD ** -0.5
    outs = []
    for j in range(LANE // HD):
        sl = slice(j * HD, (j + 1) * HD)
        kh = k[:, sl]
        vh = v[:, sl]
        ck_ref[0, 0, j] = kh
        cv_ref[0, 0, j] = vh
        s = _dot_nt(q[:, sl], _bf(kh)) * scale
        m = jnp.max(s, axis=-1, keepdims=True)
        pexp = jnp.exp(s - m)
        den = jnp.sum(pexp, axis=-1, keepdims=True)
        outs.append(_dot(_bf(pexp), _bf(vh)) / den)
    o_ref[0, 0] = jnp.concatenate(outs, axis=1)


def _ctx_attn(q_t, k_t, v_t, b, l):
    npair = q_t.shape[0]
    hpp = LANE // HD
    q4, k4, v4 = (a.reshape(npair, b, l, LANE) for a in (q_t, k_t, v_t))
    blk = pl.BlockSpec((1, 1, l, LANE), lambda hp, bi: (hp, bi, 0, 0))
    cache_spec = pl.BlockSpec((1, 1, hpp, l, HD), lambda hp, bi: (bi, 0, hp, 0, 0))
    cache_shape = jax.ShapeDtypeStruct((b, 1, npair * hpp, l, HD), F32)
    o, ck, cv = pl.pallas_call(
        _ctx_attn_kernel,
        grid=(npair, b),
        in_specs=[blk, blk, blk],
        out_specs=[blk, cache_spec, cache_spec],
        out_shape=[jax.ShapeDtypeStruct((npair, b, l, LANE), F32), cache_shape, cache_shape],
        compiler_params=_params("arbitrary", "arbitrary"),
        name="ctx_attn",
    )(q4, k4, v4)
    return o.reshape(npair, b * l, LANE), ck, cv


def _na_bias_kernel(rpb_ref, o_ref):
    h = pl.program_id(0)
    ndr = 2 * WIN_H - 1
    ndc = 2 * WIN_W - 1
    cq = lax.broadcasted_iota(jnp.int32, (GRID_W, LANE), 0)
    lane = lax.broadcasted_iota(jnp.int32, (GRID_W, LANE), 1)
    ck = lane & (GRID_W - 1)
    first = lane < GRID_W
    dc = jnp.clip(ck - cq + (WIN_W - 1), 0, ndc - 1)
    col0 = jnp.clip(cq - WIN_W // 2, 0, GRID_W - WIN_W)
    col_in = (ck >= col0) & (ck < col0 + WIN_W)
    dc_is = [dc == e for e in range(ndc)]
    tables = []
    for dr in range(ndr):
        t = jnp.full((GRID_W, LANE), NEG_INF, F32)
        for e in range(ndc):
            t = jnp.where(dc_is[e] & col_in, rpb_ref[(h * ndr + dr) * ndc + e], t)
        tables.append(t)
    ninf = jnp.full((GRID_W, LANE), NEG_INF, F32)
    for e in range(ndr + 1):
        lo = tables[e - 1] if e >= 1 else ninf
        hi = tables[e] if e < ndr else ninf
        o_ref[0, e] = jnp.where(first, lo, hi)


def _na_bias_tables(rpb):
    nh = rpb.shape[0]
    ne = 2 * WIN_H
    return pl.pallas_call(
        _na_bias_kernel,
        grid=(nh,),
        in_specs=[pl.BlockSpec(memory_space=pltpu.SMEM)],
        out_specs=pl.BlockSpec((1, ne, GRID_W, LANE), lambda h: (h, 0, 0, 0)),
        out_shape=jax.ShapeDtypeStruct((nh, ne, GRID_W, LANE), F32),
        compiler_params=_params("arbitrary"),
        name="na_bias",
    )(rpb.reshape(-1))


def _na_kernel(q_ref, kp_ref, kc_ref, kn_ref, vp_ref, vc_ref, vn_ref, ck_ref, cv_ref, t2_ref, o_ref, *, n_rows):
    rb = pl.program_id(2)
    qrows = NA_QROWS
    krows = 2 * qrows
    nq = qrows * GRID_W
    nk = krows * GRID_W
    half = (qrows // 2) * GRID_W
    scale = HD ** -0.5
    q = q_ref[0, 0]
    kloc = jnp.concatenate([kp_ref[0, 0][nq - half:nq], kc_ref[0, 0], kn_ref[0, 0][0:half]], axis=0)
    vloc = jnp.concatenate([vp_ref[0, 0][nq - half:nq], vc_ref[0, 0], vn_ref[0, 0][0:half]], axis=0)

    a = lax.broadcasted_iota(jnp.int32, (nq, nk), 0) // GRID_W
    i = lax.broadcasted_iota(jnp.int32, (nq, nk), 1) // GRID_W
    r = rb * qrows + a
    kr = rb * qrows - qrows // 2 + i
    rs = jnp.clip(r - WIN_H // 2, 0, n_rows - WIN_H)
    rowmask = jnp.where((kr >= rs) & (kr < rs + WIN_H), 0.0, NEG_INF)

    ninf_tile = jnp.full((GRID_W, LANE), NEG_INF, F32)
    outs = []
    for j in range(LANE // HD):
        sl = slice(j * HD, (j + 1) * HD)
        qh = q[:, sl]
        strips = []
        for aa in range(qrows):
            tiles = []
            for ip in range(krows // 2):
                e = 2 * ip - aa + (WIN_H - 1) - qrows // 2 + 1
                tiles.append(t2_ref[j, e] if 0 <= e < 2 * WIN_H else ninf_tile)
            strips.append(jnp.concatenate(tiles, axis=1))
        bias = jnp.concatenate(strips, axis=0)
        s = _dot_nt(qh, kloc[:, sl]) * scale + bias + rowmask
        sc = _dot_nt(qh, _bf(ck_ref[0, j])) * scale
        m = jnp.maximum(jnp.max(s, axis=-1, keepdims=True), jnp.max(sc, axis=-1, keepdims=True))
        pl_ = jnp.exp(s - m)
        pc = jnp.exp(sc - m)
        den = jnp.sum(pl_, axis=-1, keepdims=True) + jnp.sum(pc, axis=-1, keepdims=True)
        o = _dot(_bf(pl_), vloc[:, sl]) + _dot(_bf(pc), _bf(cv_ref[0, j]))
        outs.append(o / den)
    o_ref[0, 0] = jnp.concatenate(outs, axis=1)


def _na_attn(q_t, k_t, v_t, cache_k, cache_v, t2, b, l):
    npair = q_t.shape[0]
    hpp = LANE // HD
    n_rows = l // GRID_W
    nrb = n_rows // NA_QROWS
    nq = NA_QROWS * GRID_W
    lc = cache_k.shape[2]
    q4, k4, v4 = (a.reshape(npair, b, l, LANE) for a in (q_t, k_t, v_t))

    def blk(f):
        return pl.BlockSpec((1, 1, nq, LANE), lambda hp, bi, rb: (hp, bi, f(rb), 0))

    cur = blk(lambda rb: rb)
    prv = blk(lambda rb: jnp.maximum(rb - 1, 0))
    nxt = blk(lambda rb: jnp.minimum(rb + 1, nrb - 1))
    cspec = pl.BlockSpec((1, hpp, lc, HD), lambda hp, bi, rb: (bi, hp, 0, 0))
    out = pl.pallas_call(
        functools.partial(_na_kernel, n_rows=n_rows),
        grid=(npair, b, nrb),
        in_specs=[cur, prv, cur, nxt, prv, cur, nxt, cspec, cspec,
                  pl.BlockSpec((hpp, 2 * WIN_H, GRID_W, LANE), lambda hp, bi, rb: (hp, 0, 0, 0))],
        out_specs=cur,
        out_shape=jax.ShapeDtypeStruct((npair, b, l, LANE), F32),
        compiler_params=_params("arbitrary", "arbitrary", "arbitrary"),
        name="na_attn",
    )(q4, k4, k4, k4, v4, v4, v4, cache_k, cache_v, t2)
    return out.reshape(npair, b * l, LANE)


def _layer0(x2d, b, l, mod, rows_per_mod, mod_base, norm_w, w_in_bf, w_out_bf, p, init, want_final, klin):
    dm = N_HEADS * HD
    d_xbc = dm + 2 * N_GROUPS * N_STATE
    d_b = klin.shape[1]
    segs = ((0, dm, False, F32), (dm, d_xbc, False, F32), (dm + d_xbc, 3 * d_b, True, F32),
            (dm + d_xbc + 3 * d_b, d_b, True, F32), (dm + d_xbc + 4 * d_b, LANE, False, F32))
    z, xbc, u_t, g_t, dt_raw = _proj_in(x2d, mod, norm_w, w_in_bf, segs, rows_per_mod, mod_base)
    res = _ssd(xbc.reshape(b, l, d_xbc), dt_raw.reshape(b, l, LANE), p["conv_a_w"], p["conv_a_b"],
               p["dt_bias"], p["a_log"], p["d_skip"], init, want_final)
    y_f, y_b = res[0].reshape(b * l, dm), res[1].reshape(b * l, dm)
    yh_t = _hyena(u_t, g_t, klin, p["conv_b_w"], p["conv_b_b"], p["hy_bias"], b, l)
    x_new = _proj_out0(x2d, y_f, y_b, z, yh_t, mod, p["norm_a_w"], w_out_bf, rows_per_mod, mod_base)
    return x_new, (res[2] if want_final else None)


def kernel(x_prompt, x_sample, state_ssd, cache_k, cache_v, c, c_ctx, norm_w, w_ada, b_ada, w_in_e, w_out_e, conv_a_w, conv_a_b, dt_bias, a_log, d_skip, norm_a_w, conv_b_w, conv_b_b, hf_w1, hf_b1, hf_w2, hf_b2, hf_w3, hf_freq, hy_bias, w_in_o, w_out_o, rpb, final_norm_w):
    bp, lp, d = x_prompt.shape
    bs, ls, _ = x_sample.shape
    dm = N_HEADS * HD
    d_xbc = dm + 2 * N_GROUPS * N_STATE
    n_dt = 2 * N_HEADS

    cvecs = jnp.concatenate([c_ctx[None], c, jnp.zeros((SUBLANE - 1 - bs, d), F32)], axis=0)
    mods = _ada_mods(cvecs, w_ada, b_ada)

    xp = x_prompt.reshape(bp * lp, d)
    xs = x_sample.reshape(bs * ls, d)

    wi = w_in_e[0]
    o_dt = dm + d_xbc
    w_in0 = jnp.concatenate([wi[:, :o_dt], wi[:, o_dt + n_dt:], wi[:, o_dt:o_dt + n_dt],
                             jnp.zeros((d, LANE - n_dt), F32)], axis=1).astype(BF16)
    w_out0 = w_out_e[0].astype(BF16)
    p0 = dict(conv_a_w=conv_a_w[0], conv_a_b=conv_a_b[0], dt_bias=dt_bias[0], a_log=a_log[0], d_skip=d_skip[0],
              norm_a_w=norm_a_w[0].reshape(1, -1), conv_b_w=conv_b_w[0], conv_b_b=conv_b_b[0], hy_bias=hy_bias[0])
    mod0 = mods[0].reshape(SUBLANE, 1, 3 * d)
    nw0 = norm_w[0].reshape(1, d)
    hf = (hf_w1[0], hf_b1[0], hf_w2[0], hf_b2[0], hf_w3[0], hf_freq[0])
    klin_p = _hyena_filter_linear(lp, *hf)
    klin_s = _hyena_filter_linear(ls, *hf)
    xp, fin = _layer0(xp, bp, lp, mod0, bp * lp, 0, nw0, w_in0, w_out0, p0, None, True, klin_p)
    init_s = state_ssd[:, 0].reshape(bs, 2, dm, N_STATE)
    xs, _ = _layer0(xs, bs, ls, mod0, ls, 1, nw0, w_in0, w_out0, p0, init_s, False, klin_s)
    new_state_ssd = fin.reshape(bp, 1, 2, N_HEADS, HD, N_STATE)

    w_in1 = w_in_o[0].astype(BF16)
    w_out1 = w_out_o[0].astype(BF16)
    mod1 = mods[1].reshape(SUBLANE, 1, 3 * d)
    nw1 = norm_w[1].reshape(1, d)
    fw = final_norm_w.reshape(1, d)
    segs_p = ((0, d, True, BF16), (d, d, True, F32), (2 * d, d, True, F32), (3 * d, d, False, F32))
    q_t, k_t, v_t, g = _proj_in(xp, mod1, nw1, w_in1, segs_p, bp * lp, 0)
    o_t, new_cache_k, new_cache_v = _ctx_attn(q_t, k_t, v_t, bp, lp)
    y_prompt = _proj_out1(xp, o_t, g, mod1, fw, w_out1, bp * lp, 0).reshape(bp, lp, d)

    segs_s = ((0, d, True, BF16), (d, d, True, BF16), (2 * d, d, True, BF16), (3 * d, d, False, F32))
    q_t, k_t, v_t, g = _proj_in(xs, mod1, nw1, w_in1, segs_s, ls, 1)
    t2 = _na_bias_tables(rpb[0])
    o_t = _na_attn(q_t, k_t, v_t, cache_k[:, 0], cache_v[:, 0], t2, bs, ls)
    y_sample = _proj_out1(xs, o_t, g, mod1, fw, w_out1, ls, 1).reshape(bs, ls, d)

    return (y_prompt, y_sample, new_state_ssd, new_cache_k, new_cache_v)
```

```python
import functools
import math

import jax
import jax.numpy as jnp
import numpy as np
from jax import lax
from jax.experimental import pallas as pl
from jax.experimental.pallas import tpu as pltpu

F32 = jnp.float32
BF16 = jnp.bfloat16

EPS = 1e-6
GRID_W = 64
WIN_H = 8
WIN_W = 16
HD = 64
N_HEADS = 16
N_STATE = 128
N_GROUPS = 2
CHUNK = 128
HY_EMB = 33
HY_BANDS = (HY_EMB - 1) // 2
HY_HID = 64
HY_TARGET = 1e-2
HY_DECAY_PCT_HI = 0.3
HY_DECAY_PCT_LO = 1.5

LANE = 128
SUBLANE = 8
VMEM_LIMIT = 56 * 1024 * 1024

ROW_TILE = 256
DFT_BLOCK = 512
HY_BATCH = 8
NA_QROWS = 8
NEG_INF = float("-inf")


def _bf(x):
    return x.astype(BF16)


def _dot(a, b):
    return jnp.dot(a, b, preferred_element_type=F32)


def _dot_nt(a, b):
    return lax.dot_general(a, b, (((1,), (1,)), ((), ())), preferred_element_type=F32)


def _split2(x):
    hi = _bf(x)
    lo = _bf(x - hi.astype(F32))
    return hi, lo


def _split3(x):
    hi = _bf(x)
    r = x - hi.astype(F32)
    mid = _bf(r)
    lo = _bf(r - mid.astype(F32))
    return hi, mid, lo


def _dot3(a, b):
    ah, al = _split2(a)
    bh, bl = _split2(b)
    return _dot(ah, bh) + (_dot(ah, bl) + _dot(al, bh))


def _dot_lhs_parts(a, b_exact, parts):
    pieces = _split3(a) if parts == 3 else _split2(a)
    acc = _dot(pieces[0], b_exact)
    for p in pieces[1:]:
        acc = acc + _dot(p, b_exact)
    return acc


def _dot_rhs_parts(a_exact, b, parts):
    pieces = _split3(b) if parts == 3 else _split2(b)
    acc = _dot(a_exact, pieces[0])
    for p in pieces[1:]:
        acc = acc + _dot(a_exact, p)
    return acc


def _silu(x):
    return x * jax.nn.sigmoid(x)


def _rms(x, g):
    ms = jnp.mean(x * x, axis=-1, keepdims=True)
    return x * lax.rsqrt(ms + EPS) * g


def _softplus(x):
    return jnp.maximum(x, 0.0) + jnp.log1p(jnp.exp(-jnp.abs(x)))


def _params(*sem):
    return pltpu.CompilerParams(dimension_semantics=sem, vmem_limit_bytes=VMEM_LIMIT)


def _mods_kernel(c_ref, w_ref, b_ref, o_ref):
    a = _silu(c_ref[...])
    o_ref[0] = _dot3(a, w_ref[0]) + b_ref[0]


def _ada_mods(cvecs, w_ada, b_ada):
    depth, d, n3 = w_ada.shape
    tn = n3 // 4
    return pl.pallas_call(
        _mods_kernel,
        grid=(depth, n3 // tn),
        in_specs=[pl.BlockSpec((SUBLANE, d), lambda l, j: (0, 0)),
                  pl.BlockSpec((1, d, tn), lambda l, j: (l, 0, j)),
                  pl.BlockSpec((1, 1, tn), lambda l, j: (l, 0, j))],
        out_specs=pl.BlockSpec((1, SUBLANE, tn), lambda l, j: (l, 0, j)),
        out_shape=jax.ShapeDtypeStruct((depth, SUBLANE, n3), F32),
        compiler_params=_params("arbitrary", "arbitrary"),
        name="ada_mods",
    )(cvecs, w_ada, b_ada.reshape(depth, 1, n3))


def _proj_in_kernel(x_ref, mod_ref, nw_ref, w_ref, *out_refs, segs):
    d = x_ref.shape[1]
    m = mod_ref[0]
    h = _rms(x_ref[...], nw_ref[...]) * (1.0 + m[:, d:2 * d]) + m[:, 0:d]
    hb = _bf(h)
    for (off, width, tiled), o_ref in zip(segs, out_refs):
        step = 4 * LANE
        for c0 in range(0, width, step):
            cw = min(step, width - c0)
            res = _dot(hb, w_ref[:, off + c0:off + c0 + cw])
            if tiled:
                for t in range(cw // LANE):
                    o_ref[(c0 // LANE) + t] = res[:, t * LANE:(t + 1) * LANE].astype(o_ref.dtype)
            else:
                o_ref[:, c0:c0 + cw] = res.astype(o_ref.dtype)


def _proj_in(x2d, mod, norm_w, w_bf, segs, rows_per_mod, mod_base):
    m_rows, d = x2d.shape
    tm = ROW_TILE
    out_shapes, out_specs = [], []
    for (_, width, tiled, dt) in segs:
        if tiled:
            out_shapes.append(jax.ShapeDtypeStruct((width // LANE, m_rows, LANE), dt))
            out_specs.append(pl.BlockSpec((width // LANE, tm, LANE), lambda i: (0, i, 0)))
        else:
            out_shapes.append(jax.ShapeDtypeStruct((m_rows, width), dt))
            out_specs.append(pl.BlockSpec((tm, width), lambda i: (i, 0)))
    kern = functools.partial(_proj_in_kernel, segs=tuple((o, w, t) for (o, w, t, _) in segs))
    return pl.pallas_call(
        kern,
        grid=(m_rows // tm,),
        in_specs=[pl.BlockSpec((tm, d), lambda i: (i, 0)),
                  pl.BlockSpec((1, 1, 3 * d), lambda i: (mod_base + (i * tm) // rows_per_mod, 0, 0)),
                  pl.BlockSpec((1, d), lambda i: (0, 0)),
                  pl.BlockSpec(w_bf.shape, lambda i: (0, 0), pipeline_mode=pl.Buffered(1))],
        out_specs=out_specs,
        out_shape=out_shapes,
        compiler_params=_params("arbitrary"),
        name="proj_in",
    )(x2d, mod, norm_w, w_bf)


def _ssd_kernel(*refs, has_init, want_final):
    it = iter(refs)
    xm = [next(it), None]
    xp = [next(it), None]
    xn = [next(it), None]
    xm[1], xp[1], xn[1] = next(it), next(it), next(it)
    dtr = [next(it), next(it)]
    cw_ref, cb_ref, dtb_ref, alog_ref, dskip_ref, exp_ref = (next(it) for _ in range(6))
    init_ref = next(it) if has_init else None
    y_refs = [next(it), next(it)]
    fin_ref = next(it) if want_final else None
    s_ref = next(it)

    c = pl.program_id(1)
    nc = pl.num_programs(1)
    q = xm[0].shape[1]
    dm = N_HEADS * HD
    gw = dm // N_GROUPS
    hpg = N_HEADS // N_GROUPS

    @pl.when(c == 0)
    def _():
        if has_init:
            for d in range(2):
                for t in range(dm // LANE):
                    s_ref[d, :, t * LANE:(t + 1) * LANE] = init_ref[0, d, t * LANE:(t + 1) * LANE, :].T
        else:
            s_ref[...] = jnp.zeros(s_ref.shape, F32)

    row = lax.broadcasted_iota(jnp.int32, (q, q), 0)
    col = lax.broadcasted_iota(jnp.int32, (q, q), 1)
    lane = lax.broadcasted_iota(jnp.int32, (q, LANE), 1)
    left = lane < HD
    tri = [(row >= col), (row <= col)]

    for d in range(2):
        chunk = c if d == 0 else nc - 1 - c
        x = xm[d][0]
        nchan = x.shape[1]
        r0 = lax.broadcasted_iota(jnp.int32, (q, nchan), 0)
        keep_prev = (chunk > 0).astype(F32)
        keep_next = (chunk < nc - 1).astype(F32)
        prev_row = xp[d][0][SUBLANE - 1:SUBLANE, :] * keep_prev
        next_row = xn[d][0][0:1, :] * keep_next
        x_prev = jnp.where(r0 == 0, prev_row, pltpu.roll(x, 1, 0))
        x_next = jnp.where(r0 == q - 1, next_row, pltpu.roll(x, q - 1, 0))
        cw = cw_ref[...]
        xs = _silu(cb_ref[...] + x_prev * cw[0:1] + x * cw[1:2] + x_next * cw[2:3])
        xa = xs[:, 0:dm]
        bm = xs[:, dm:dm + N_GROUPS * N_STATE]
        cm = xs[:, dm + N_GROUPS * N_STATE:dm + 2 * N_GROUPS * N_STATE]

        dt = _softplus(dtr[d][0] + dtb_ref[...])
        adt = dt * (-jnp.exp(alog_ref[...]))
        tmat = jnp.where(tri[d], 1.0, 0.0).astype(BF16)
        cs = _dot_rhs_parts(tmat, adt, 3)
        cs_t = cs.T
        dt_t = dt.T
        edge = cs[q - 1:q, :] if d == 0 else cs[0:1, :]
        ex = exp_ref[d]
        e_cs = _dot_lhs_parts(jnp.exp(cs), ex, 2)
        w_st = _dot_lhs_parts(jnp.exp(edge - cs) * dt, ex, 2)
        xw = xa * w_st

        y_parts = []
        s_prev = s_ref[d]
        new_state = []
        for g in range(N_GROUPS):
            bg = bm[:, g * N_STATE:(g + 1) * N_STATE]
            cg = cm[:, g * N_STATE:(g + 1) * N_STATE]
            gmat = _dot_nt(_bf(cg), _bf(bg))
            y_off = _dot(_bf(cg), _bf(s_prev[:, g * gw:(g + 1) * gw]))
            new_state.append(_dot(_bf(bg.T), _bf(xw[:, g * gw:(g + 1) * gw])))
            for pr in range(hpg // 2):
                mh = []
                for j in range(2):
                    k = d * N_HEADS + g * hpg + 2 * pr + j
                    diff = cs[:, k:k + 1] - cs_t[k:k + 1, :]
                    lm = jnp.exp(jnp.where(tri[d], diff, NEG_INF))
                    mh.append(_bf(gmat * lm * dt_t[k:k + 1, :]))
                c0 = g * gw + pr * LANE
                xpair = xa[:, c0:c0 + LANE]
                rhs = jnp.concatenate([_bf(jnp.where(left, xpair, 0.0)), _bf(jnp.where(left, 0.0, xpair))], axis=0)
                y_d = _dot(jnp.concatenate(mh, axis=1), rhs)
                y_parts.append(y_d + y_off[:, pr * LANE:(pr + 1) * LANE] * e_cs[:, c0:c0 + LANE])
        y = jnp.concatenate(y_parts, axis=1) + xa * dskip_ref[d:d + 1, :]
        y_refs[d][0] = y
        e_edge = e_cs[q - 1:q, :] if d == 0 else e_cs[0:1, :]
        s_ref[d] = s_prev * e_edge + jnp.concatenate(new_state, axis=1)

    if want_final:
        @pl.when(c == nc - 1)
        def _():
            for d in range(2):
                for t in range(dm // LANE):
                    fin_ref[0, d, t * LANE:(t + 1) * LANE, :] = s_ref[d, :, t * LANE:(t + 1) * LANE].T


def _ssd(xbc, dt_raw, conv_w, conv_b, dt_bias, a_log, d_skip, init, want_final):
    b, l, nchan = xbc.shape
    dm = N_HEADS * HD
    q = CHUNK
    nc = l // q
    sub_per_chunk = q // SUBLANE
    nsub = l // SUBLANE

    def fwd(c):
        return c

    def bwd(c):
        return nc - 1 - c

    def main(cf):
        return pl.BlockSpec((1, q, nchan), lambda bi, c: (bi, cf(c), 0))

    def prev(cf):
        return pl.BlockSpec((1, SUBLANE, nchan), lambda bi, c: (bi, jnp.maximum(cf(c) * sub_per_chunk - 1, 0), 0))

    def nxt(cf):
        return pl.BlockSpec((1, SUBLANE, nchan),
                            lambda bi, c: (bi, jnp.minimum((cf(c) + 1) * sub_per_chunk, nsub - 1), 0))

    def full(a):
        nd = a.ndim
        return pl.BlockSpec(a.shape, lambda bi, c: (0,) * nd)

    ex = np.zeros((2, LANE, dm), np.float32)
    for d in range(2):
        for h in range(N_HEADS):
            ex[d, d * N_HEADS + h, h * HD:(h + 1) * HD] = 1.0
    ex = jnp.asarray(ex, BF16)
    pad = LANE - 2 * N_HEADS
    dtb = jnp.pad(dt_bias.reshape(1, 2 * N_HEADS), ((0, 0), (0, pad)))
    alog = jnp.pad(a_log.reshape(1, 2 * N_HEADS), ((0, 0), (0, pad)))
    dsk = jnp.repeat(d_skip, HD, axis=1)
    cb = conv_b.reshape(1, nchan)

    args = [xbc, xbc, xbc, xbc, xbc, xbc, dt_raw, dt_raw, conv_w, cb, dtb, alog, dsk, ex]
    in_specs = [main(fwd), prev(fwd), nxt(fwd), main(bwd), prev(bwd), nxt(bwd),
                pl.BlockSpec((1, q, LANE), lambda bi, c: (bi, c, 0)),
                pl.BlockSpec((1, q, LANE), lambda bi, c: (bi, nc - 1 - c, 0)),
                full(conv_w), full(cb), full(dtb), full(alog), full(dsk), full(ex)]
    has_init = init is not None
    if has_init:
        args.append(init)
        in_specs.append(pl.BlockSpec((1, 2, dm, N_STATE), lambda bi, c: (bi, 0, 0, 0)))
    out_shapes = [jax.ShapeDtypeStruct((b, l, dm), F32), jax.ShapeDtypeStruct((b, l, dm), F32)]
    out_specs = [pl.BlockSpec((1, q, dm), lambda bi, c: (bi, c, 0)),
                 pl.BlockSpec((1, q, dm), lambda bi, c: (bi, nc - 1 - c, 0))]
    if want_final:
        out_shapes.append(jax.ShapeDtypeStruct((b, 2, dm, N_STATE), F32))
        out_specs.append(pl.BlockSpec((1, 2, dm, N_STATE), lambda bi, c: (bi, 0, 0, 0)))
    return pl.pallas_call(
        functools.partial(_ssd_kernel, has_init=has_init, want_final=want_final),
        grid=(b, nc),
        in_specs=in_specs,
        out_specs=out_specs,
        out_shape=out_shapes,
        scratch_shapes=[pltpu.VMEM((2, N_STATE, dm), F32)],
        compiler_params=_params("arbitrary", "arbitrary"),
        name="ssd_scan",
    )(*args)


def _hyena_tables(l):
    pos = np.abs(np.arange(2 * l, dtype=np.float64) - l)
    t = pos / (l - 1)
    w = 2.0 * math.pi * pos / l
    f = np.linspace(1e-4, HY_BANDS - 1, HY_BANDS)
    feats = np.zeros((2 * l, LANE), np.float64)
    feats[:, 0] = t
    feats[:, 1:1 + HY_BANDS] = np.cos(f[None] * w[:, None])
    feats[:, 1 + HY_BANDS:1 + 2 * HY_BANDS] = -np.sin(f[None] * w[:, None])
    return jnp.asarray(feats, F32)


def _filter_kernel(f_ref, w1_ref, b1_ref, w2_ref, b2_ref, w3_ref, fr_ref, ad_ref, o_ref):
    feats = f_ref[...]
    fr = fr_ref[...]
    h1 = jnp.sin(fr * (_dot3(feats, w1_ref[...]) + b1_ref[...]))
    h2 = jnp.sin(fr * (_dot3(h1, w2_ref[...]) + b2_ref[...]))
    filt = _dot3(h2, w3_ref[...])
    o_ref[...] = filt * jnp.exp(-feats[:, 0:1] * ad_ref[...])


def _hyena_filter_linear(l, w1, b1, w2, b2, w3, freq):
    db = w3.shape[1] // 2
    tr = min(512, l)
    nbk = l // tr
    feats = _hyena_tables(l)
    w1p = jnp.pad(w1, ((0, LANE - HY_EMB), (0, 0)))
    deltas = np.linspace(math.log(HY_TARGET) / HY_DECAY_PCT_HI, math.log(HY_TARGET) / HY_DECAY_PCT_LO, db)
    absd = jnp.asarray(np.abs(deltas)[None], F32)

    def full(a):
        return pl.BlockSpec(a.shape, lambda i: (0, 0))

    b1r, b2r, frr = b1.reshape(1, -1), b2.reshape(1, -1), freq.reshape(1, -1)
    return pl.pallas_call(
        _filter_kernel,
        grid=(2 * nbk,),
        in_specs=[pl.BlockSpec((tr, LANE), lambda i: (i, 0)), full(w1p), full(b1r), full(w2), full(b2r),
                  pl.BlockSpec((HY_HID, db), lambda i: (0, jnp.where(i < nbk, 1, 0))),
                  full(frr), full(absd)],
        out_specs=pl.BlockSpec((tr, db), lambda i: (i, 0)),
        out_shape=jax.ShapeDtypeStruct((2 * l, db), F32),
        compiler_params=_params("arbitrary"),
        name="hyena_filter",
    )(feats, w1p, b1r, w2, b2r, w3, frr, absd)


def _dft_tables(p):
    n = 2 * p
    f = np.arange(p, dtype=np.float64)[:, None] + 0.5
    e = np.arange(n, dtype=np.float64)[None]
    ang = 2.0 * math.pi * f * e / n
    fwd = np.concatenate([np.cos(ang), -np.sin(ang)], axis=0)
    fa = fwd[:, :p]
    fbn = -fwd[:, p:]
    fbn[:, 0] = 0.0
    angt = ang[:, :p].T
    inv = np.concatenate([np.cos(angt), -np.sin(angt)], axis=1) * (2.0 / n)

    return tuple(jnp.asarray(m, F32).astype(BF16) for m in (fa, fbn, inv))


def _spectra_kernel(k1_ref, k0_ref, fa_ref, fb_ref, o_ref):
    res = _dot(fa_ref[...], _bf(k1_ref[...])) + _dot(fb_ref[...], _bf(k0_ref[...]))
    for t in range(o_ref.shape[0]):
        o_ref[t, 0] = res[:, t * LANE:(t + 1) * LANE]


def _hyena_spectra(klin, p, fa, fbn):
    two_l, db = klin.shape
    nseg = two_l // p - 1
    nct = db // LANE
    return pl.pallas_call(
        _spectra_kernel,
        grid=(nseg,),
        in_specs=[pl.BlockSpec((p, db), lambda s: (s + 1, 0)),
                  pl.BlockSpec((p, db), lambda s: (s, 0)),
                  pl.BlockSpec(fa.shape, lambda s: (0, 0)),
                  pl.BlockSpec(fbn.shape, lambda s: (0, 0))],
        out_specs=pl.BlockSpec((nct, 1, 2 * p, LANE), lambda s: (0, s, 0, 0)),
        out_shape=jax.ShapeDtypeStruct((nct, nseg, 2 * p, LANE), F32),
        compiler_params=_params("arbitrary"),
        name="hyena_spectra",
    )(klin, klin, fa, fbn)


def _hyena_kernel(x0_ref, x1_ref, v_ref, g_ref, gs_ref, w0_ref, w1_ref, w2_ref, c0_ref, c1_ref, c2_ref,
                  hb_ref, fa_ref, iv_ref, o_ref, u_scr, y_scr, w_scr, gate_scr, *, p):
    bt, l = x0_ref.shape[1], x0_ref.shape[2]
    nb = l // p
    row = lax.broadcasted_iota(jnp.int32, (l, LANE), 0)

    def conv3(x, w_ref, c_ref):
        w = w_ref[...]
        xm1 = jnp.where(row == 0, 0.0, pltpu.roll(x, 1, 0))
        xp1 = jnp.where(row == l - 1, 0.0, pltpu.roll(x, l - 1, 0))
        return c_ref[...] + xm1 * w[0:1] + x * w[1:2] + xp1 * w[2:3]

    for bb in range(bt):
        w_scr[bb] = conv3(v_ref[0, bb], w2_ref, c2_ref) * conv3(x1_ref[0, bb], w1_ref, c1_ref)
        gate_scr[bb] = conv3(x0_ref[0, bb], w0_ref, c0_ref) * _silu(g_ref[0, bb])
    fa = fa_ref[...]
    for j in range(nb):
        rhs = jnp.concatenate([_bf(w_scr[bb, j * p:(j + 1) * p, :]) for bb in range(bt)], axis=1)
        u_scr[j] = _dot(fa, rhs)
    rt_rows = 64
    for i in range(nb):
        def body(rt, carry):
            r0 = pl.multiple_of(rt * rt_rows, rt_rows)
            for bb in range(bt):
                ls = slice(bb * LANE, (bb + 1) * LANE)
                acc_re = jnp.zeros((rt_rows, LANE), F32)
                acc_im = jnp.zeros((rt_rows, LANE), F32)
                for j in range(nb):
                    s = i - j + nb - 1
                    gre = gs_ref[0, s, pl.ds(r0, rt_rows), :]
                    gim = gs_ref[0, s, pl.ds(p + r0, rt_rows), :]
                    ure = u_scr[j, pl.ds(r0, rt_rows), ls]
                    uim = u_scr[j, pl.ds(p + r0, rt_rows), ls]
                    acc_re = acc_re + (gre * ure - gim * uim)
                    acc_im = acc_im + (gre * uim + gim * ure)
                y_scr[pl.ds(r0, rt_rows), ls] = acc_re
                y_scr[pl.ds(p + r0, rt_rows), ls] = acc_im
            return carry
        lax.fori_loop(0, p // rt_rows, body, 0)
        conv = _dot(iv_ref[...], _bf(y_scr[...]))
        sl = slice(i * p, (i + 1) * p)
        for bb in range(bt):
            wi = w_scr[bb, sl, :]
            o_ref[0, bb, sl, :] = gate_scr[bb, sl, :] * (conv[:, bb * LANE:(bb + 1) * LANE] + wi * hb_ref[...])


def _hyena(u_t, g_t, klin, conv_w, conv_b, hy_bias, b, l):
    nct = klin.shape[1] // LANE
    p = min(DFT_BLOCK, l)
    nb = l // p
    nseg = 2 * nb - 1
    bt = min(b, HY_BATCH) if nb == 1 else 1
    fa, fbn, iv = _dft_tables(p)
    spectra = _hyena_spectra(klin, p, fa, fbn)
    u4 = u_t.reshape(3 * nct, b, l, LANE)
    g4 = g_t.reshape(nct, b, l, LANE)
    cbr = conv_b.reshape(1, -1)
    hbr = hy_bias.reshape(1, -1)

    def act(k):
        return pl.BlockSpec((1, bt, l, LANE), lambda ct, bi: (k * nct + ct, bi, 0, 0))

    def wcol(rows, k):
        return pl.BlockSpec((rows, LANE), lambda ct, bi: (0, k * nct + ct))

    def full(a):
        return pl.BlockSpec(a.shape, lambda ct, bi: (0, 0), pipeline_mode=pl.Buffered(1))

    out = pl.pallas_call(
        functools.partial(_hyena_kernel, p=p),
        grid=(nct, b // bt),
        in_specs=[act(0), act(1), act(2),
                  pl.BlockSpec((1, bt, l, LANE), lambda ct, bi: (ct, bi, 0, 0)),
                  pl.BlockSpec((1, nseg, 2 * p, LANE), lambda ct, bi: (ct, 0, 0, 0)),
                  wcol(3, 0), wcol(3, 1), wcol(3, 2), wcol(1, 0), wcol(1, 1), wcol(1, 2),
                  pl.BlockSpec((1, LANE), lambda ct, bi: (0, ct)),
                  full(fa), full(iv)],
        out_specs=pl.BlockSpec((1, bt, l, LANE), lambda ct, bi: (ct, bi, 0, 0)),
        out_shape=jax.ShapeDtypeStruct((nct, b, l, LANE), F32),
        scratch_shapes=[pltpu.VMEM((nb, 2 * p, bt * LANE), F32), pltpu.VMEM((2 * p, bt * LANE), F32),
                        pltpu.VMEM((bt, l, LANE), F32), pltpu.VMEM((bt, l, LANE), F32)],
        compiler_params=_params("arbitrary", "arbitrary"),
        name="hyena_conv",
    )(u4, u4, u4, g4, spectra, conv_w, conv_w, conv_w, cbr, cbr, cbr, hbr, fa, iv)
    return out.reshape(nct, b * l, LANE)


def _proj_out0_kernel(x_ref, yf_ref, yb_ref, z_ref, yh_ref, mod_ref, naw_ref, w_ref, o_ref):
    d = x_ref.shape[1]
    ya = _rms((yf_ref[...] + yb_ref[...]) * _silu(z_ref[...]), naw_ref[...])
    yh = jnp.concatenate([yh_ref[t] for t in range(yh_ref.shape[0])], axis=1)
    da = ya.shape[1]
    acc = _dot(_bf(ya), w_ref[0:da, :]) + _dot(_bf(yh), w_ref[da:, :])
    gate = mod_ref[0][:, 2 * d:3 * d]
    o_ref[...] = x_ref[...] + gate * acc


def _proj_out0(x2d, y_f, y_b, z, yh_t, mod, norm_a_w, w_bf, rows_per_mod, mod_base):
    m_rows, d = x2d.shape
    tm = ROW_TILE
    da = y_f.shape[1]
    nt = yh_t.shape[0]

    def rowspec(wd):
        return pl.BlockSpec((tm, wd), lambda i: (i, 0))

    return pl.pallas_call(
        _proj_out0_kernel,
        grid=(m_rows // tm,),
        in_specs=[rowspec(d), rowspec(da), rowspec(da), rowspec(da),
                  pl.BlockSpec((nt, tm, LANE), lambda i: (0, i, 0)),
                  pl.BlockSpec((1, 1, 3 * d), lambda i: (mod_base + (i * tm) // rows_per_mod, 0, 0)),
                  pl.BlockSpec((1, da), lambda i: (0, 0)),
                  pl.BlockSpec(w_bf.shape, lambda i: (0, 0), pipeline_mode=pl.Buffered(1))],
        out_specs=rowspec(d),
        out_shape=jax.ShapeDtypeStruct((m_rows, d), F32),
        compiler_params=_params("arbitrary"),
        name="proj_out0",
    )(x2d, y_f, y_b, z, yh_t, mod, norm_a_w, w_bf)


def _proj_out1_kernel(x_ref, o_ref_in, g_ref, mod_ref, fw_ref, w_ref, y_ref):
    d = x_ref.shape[1]
    o = jnp.concatenate([o_ref_in[t] for t in range(o_ref_in.shape[0])], axis=1)
    a = o * _silu(g_ref[...])
    acc = _dot(_bf(a), w_ref[...])
    gate = mod_ref[0][:, 2 * d:3 * d]
    y_ref[...] = _rms(x_ref[...] + gate * acc, fw_ref[...])


def _proj_out1(x2d, o_t, g, mod, final_w, w_bf, rows_per_mod, mod_base):
    m_rows, d = x2d.shape
    tm = ROW_TILE
    nt = o_t.shape[0]
    return pl.pallas_call(
        _proj_out1_kernel,
        grid=(m_rows // tm,),
        in_specs=[pl.BlockSpec((tm, d), lambda i: (i, 0)),
                  pl.BlockSpec((nt, tm, LANE), lambda i: (0, i, 0)),
                  pl.BlockSpec((tm, g.shape[1]), lambda i: (i, 0)),
                  pl.BlockSpec((1, 1, 3 * d), lambda i: (mod_base + (i * tm) // rows_per_mod, 0, 0)),
                  pl.BlockSpec((1, d), lambda i: (0, 0)),
                  pl.BlockSpec(w_bf.shape, lambda i: (0, 0), pipeline_mode=pl.Buffered(1))],
        out_specs=pl.BlockSpec((tm, d), lambda i: (i, 0)),
        out_shape=jax.ShapeDtypeStruct((m_rows, d), F32),
        compiler_params=_params("arbitrary"),
        name="proj_out1",
    )(x2d, o_t, g, mod, final_w, w_bf)


def _ctx_attn_kernel(q_ref, k_ref, v_ref, o_ref, ck_ref, cv_ref):
    scale = HD ** -0.5
    hpp = LANE // HD
    for hp in range(q_ref.shape[0]):
        q = q_ref[hp, 0]
        k = k_ref[hp, 0]
        v = v_ref[hp, 0]
        outs = []
        for j in range(hpp):
            sl = slice(j * HD, (j + 1) * HD)
            kh = k[:, sl]
            vh = v[:, sl]
            ck_ref[0, 0, hp * hpp + j] = kh
            cv_ref[0, 0, hp * hpp + j] = vh
            s = _dot_nt(q[:, sl], _bf(kh)) * scale
            m = jnp.max(s, axis=-1, keepdims=True)
            pexp = jnp.exp(s - m)
            den = jnp.sum(pexp, axis=-1, keepdims=True)
            outs.append(_dot(_bf(pexp), _bf(vh)) / den)
        o_ref[hp, 0] = jnp.concatenate(outs, axis=1)


def _ctx_attn(q_t, k_t, v_t, b, l):
    npair = q_t.shape[0]
    hpp = LANE // HD
    q4, k4, v4 = (a.reshape(npair, b, l, LANE) for a in (q_t, k_t, v_t))
    blk = pl.BlockSpec((npair, 1, l, LANE), lambda bi: (0, bi, 0, 0))
    cache_spec = pl.BlockSpec((1, 1, npair * hpp, l, HD), lambda bi: (bi, 0, 0, 0, 0))
    cache_shape = jax.ShapeDtypeStruct((b, 1, npair * hpp, l, HD), F32)
    o, ck, cv = pl.pallas_call(
        _ctx_attn_kernel,
        grid=(b,),
        in_specs=[blk, blk, blk],
        out_specs=[blk, cache_spec, cache_spec],
        out_shape=[jax.ShapeDtypeStruct((npair, b, l, LANE), F32), cache_shape, cache_shape],
        compiler_params=_params("arbitrary"),
        name="ctx_attn",
    )(q4, k4, v4)
    return o.reshape(npair, b * l, LANE), ck, cv


def _na_bias_kernel(rpb_ref, o_ref):
    h = pl.program_id(0)
    ndr = 2 * WIN_H - 1
    ndc = 2 * WIN_W - 1
    cq = lax.broadcasted_iota(jnp.int32, (GRID_W, LANE), 0)
    lane = lax.broadcasted_iota(jnp.int32, (GRID_W, LANE), 1)
    ck = lane & (GRID_W - 1)
    first = lane < GRID_W
    dc = jnp.clip(ck - cq + (WIN_W - 1), 0, ndc - 1)
    col0 = jnp.clip(cq - WIN_W // 2, 0, GRID_W - WIN_W)
    col_in = (ck >= col0) & (ck < col0 + WIN_W)
    dc_is = [dc == e for e in range(ndc)]
    tables = []
    for dr in range(ndr):
        t = jnp.full((GRID_W, LANE), NEG_INF, F32)
        for e in range(ndc):
            t = jnp.where(dc_is[e] & col_in, rpb_ref[(h * ndr + dr) * ndc + e], t)
        tables.append(t)
    ninf = jnp.full((GRID_W, LANE), NEG_INF, F32)
    for e in range(ndr + 1):
        lo = tables[e - 1] if e >= 1 else ninf
        hi = tables[e] if e < ndr else ninf
        o_ref[0, e] = jnp.where(first, lo, hi)


def _na_bias_tables(rpb):
    nh = rpb.shape[0]
    ne = 2 * WIN_H
    return pl.pallas_call(
        _na_bias_kernel,
        grid=(nh,),
        in_specs=[pl.BlockSpec(memory_space=pltpu.SMEM)],
        out_specs=pl.BlockSpec((1, ne, GRID_W, LANE), lambda h: (h, 0, 0, 0)),
        out_shape=jax.ShapeDtypeStruct((nh, ne, GRID_W, LANE), F32),
        compiler_params=_params("arbitrary"),
        name="na_bias",
    )(rpb.reshape(-1))


def _na_kernel(q_ref, kp_ref, kc_ref, kn_ref, vp_ref, vc_ref, vn_ref, ck_ref, cv_ref, t2_ref, o_ref, *, n_rows):
    rb = pl.program_id(2)
    qrows = NA_QROWS
    krows = 2 * qrows
    nq = qrows * GRID_W
    nk = krows * GRID_W
    half = (qrows // 2) * GRID_W
    scale = HD ** -0.5
    q = q_ref[0, 0]
    kloc = jnp.concatenate([kp_ref[0, 0][nq - half:nq], kc_ref[0, 0], kn_ref[0, 0][0:half]], axis=0)
    vloc = jnp.concatenate([vp_ref[0, 0][nq - half:nq], vc_ref[0, 0], vn_ref[0, 0][0:half]], axis=0)

    a = lax.broadcasted_iota(jnp.int32, (nq, nk), 0) // GRID_W
    i = lax.broadcasted_iota(jnp.int32, (nq, nk), 1) // GRID_W
    r = rb * qrows + a
    kr = rb * qrows - qrows // 2 + i
    rs = jnp.clip(r - WIN_H // 2, 0, n_rows - WIN_H)
    rowmask = jnp.where((kr >= rs) & (kr < rs + WIN_H), 0.0, NEG_INF)

    ninf_tile = jnp.full((GRID_W, LANE), NEG_INF, F32)
    outs = []
    for j in range(LANE // HD):
        sl = slice(j * HD, (j + 1) * HD)
        qh = q[:, sl]
        strips = []
        for aa in range(qrows):
            tiles = []
            for ip in range(krows // 2):
                e = 2 * ip - aa + (WIN_H - 1) - qrows // 2 + 1
                tiles.append(t2_ref[j, e] if 0 <= e < 2 * WIN_H else ninf_tile)
            strips.append(jnp.concatenate(tiles, axis=1))
        bias = jnp.concatenate(strips, axis=0)
        s = _dot_nt(qh, kloc[:, sl]) * scale + bias + rowmask
        sc = _dot_nt(qh, _bf(ck_ref[0, j])) * scale
        m = jnp.maximum(jnp.max(s, axis=-1, keepdims=True), jnp.max(sc, axis=-1, keepdims=True))
        pl_ = jnp.exp(s - m)
        pc = jnp.exp(sc - m)
        den = jnp.sum(pl_, axis=-1, keepdims=True) + jnp.sum(pc, axis=-1, keepdims=True)
        o = _dot(_bf(pl_), vloc[:, sl]) + _dot(_bf(pc), _bf(cv_ref[0, j]))
        outs.append(o / den)
    o_ref[0, 0] = jnp.concatenate(outs, axis=1)


def _na_attn(q_t, k_t, v_t, cache_k, cache_v, t2, b, l):
    npair = q_t.shape[0]
    hpp = LANE // HD
    n_rows = l // GRID_W
    nrb = n_rows // NA_QROWS
    nq = NA_QROWS * GRID_W
    lc = cache_k.shape[2]
    q4, k4, v4 = (a.reshape(npair, b, l, LANE) for a in (q_t, k_t, v_t))

    def blk(f):
        return pl.BlockSpec((1, 1, nq, LANE), lambda hp, bi, rb: (hp, bi, f(rb), 0))

    cur = blk(lambda rb: rb)
    prv = blk(lambda rb: jnp.maximum(rb - 1, 0))
    nxt = blk(lambda rb: jnp.minimum(rb + 1, nrb - 1))
    cspec = pl.BlockSpec((1, hpp, lc, HD), lambda hp, bi, rb: (bi, hp, 0, 0))
    out = pl.pallas_call(
        functools.partial(_na_kernel, n_rows=n_rows),
        grid=(npair, b, nrb),
        in_specs=[cur, prv, cur, nxt, prv, cur, nxt, cspec, cspec,
                  pl.BlockSpec((hpp, 2 * WIN_H, GRID_W, LANE), lambda hp, bi, rb: (hp, 0, 0, 0))],
        out_specs=cur,
        out_shape=jax.ShapeDtypeStruct((npair, b, l, LANE), F32),
        compiler_params=_params("arbitrary", "arbitrary", "arbitrary"),
        name="na_attn",
    )(q4, k4, k4, k4, v4, v4, v4, cache_k, cache_v, t2)
    return out.reshape(npair, b * l, LANE)


def _layer0(x2d, b, l, mod, rows_per_mod, mod_base, norm_w, w_in_bf, w_out_bf, p, init, want_final, klin):
    dm = N_HEADS * HD
    d_xbc = dm + 2 * N_GROUPS * N_STATE
    d_b = klin.shape[1]
    segs = ((0, dm, False, F32), (dm, d_xbc, False, F32), (dm + d_xbc, 3 * d_b, True, F32),
            (dm + d_xbc + 3 * d_b, d_b, True, F32), (dm + d_xbc + 4 * d_b, LANE, False, F32))
    z, xbc, u_t, g_t, dt_raw = _proj_in(x2d, mod, norm_w, w_in_bf, segs, rows_per_mod, mod_base)
    res = _ssd(xbc.reshape(b, l, d_xbc), dt_raw.reshape(b, l, LANE), p["conv_a_w"], p["conv_a_b"],
               p["dt_bias"], p["a_log"], p["d_skip"], init, want_final)
    y_f, y_b = res[0].reshape(b * l, dm), res[1].reshape(b * l, dm)
    yh_t = _hyena(u_t, g_t, klin, p["conv_b_w"], p["conv_b_b"], p["hy_bias"], b, l)
    x_new = _proj_out0(x2d, y_f, y_b, z, yh_t, mod, p["norm_a_w"], w_out_bf, rows_per_mod, mod_base)
    return x_new, (res[2] if want_final else None)


def kernel(x_prompt, x_sample, state_ssd, cache_k, cache_v, c, c_ctx, norm_w, w_ada, b_ada, w_in_e, w_out_e, conv_a_w, conv_a_b, dt_bias, a_log, d_skip, norm_a_w, conv_b_w, conv_b_b, hf_w1, hf_b1, hf_w2, hf_b2, hf_w3, hf_freq, hy_bias, w_in_o, w_out_o, rpb, final_norm_w):
    bp, lp, d = x_prompt.shape
    bs, ls, _ = x_sample.shape
    dm = N_HEADS * HD
    d_xbc = dm + 2 * N_GROUPS * N_STATE
    n_dt = 2 * N_HEADS

    cvecs = jnp.concatenate([c_ctx[None], c, jnp.zeros((SUBLANE - 1 - bs, d), F32)], axis=0)
    mods = _ada_mods(cvecs, w_ada, b_ada)

    xp = x_prompt.reshape(bp * lp, d)
    xs = x_sample.reshape(bs * ls, d)

    wi = w_in_e[0]
    o_dt = dm + d_xbc
    w_in0 = jnp.concatenate([wi[:, :o_dt], wi[:, o_dt + n_dt:], wi[:, o_dt:o_dt + n_dt],
                             jnp.zeros((d, LANE - n_dt), F32)], axis=1).astype(BF16)
    w_out0 = w_out_e[0].astype(BF16)
    p0 = dict(conv_a_w=conv_a_w[0], conv_a_b=conv_a_b[0], dt_bias=dt_bias[0], a_log=a_log[0], d_skip=d_skip[0],
              norm_a_w=norm_a_w[0].reshape(1, -1), conv_b_w=conv_b_w[0], conv_b_b=conv_b_b[0], hy_bias=hy_bias[0])
    mod0 = mods[0].reshape(SUBLANE, 1, 3 * d)
    nw0 = norm_w[0].reshape(1, d)
    hf = (hf_w1[0], hf_b1[0], hf_w2[0], hf_b2[0], hf_w3[0], hf_freq[0])
    klin_p = _hyena_filter_linear(lp, *hf)
    klin_s = _hyena_filter_linear(ls, *hf)
    xp, fin = _layer0(xp, bp, lp, mod0, bp * lp, 0, nw0, w_in0, w_out0, p0, None, True, klin_p)
    init_s = state_ssd[:, 0].reshape(bs, 2, dm, N_STATE)
    xs, _ = _layer0(xs, bs, ls, mod0, ls, 1, nw0, w_in0, w_out0, p0, init_s, False, klin_s)
    new_state_ssd = fin.reshape(bp, 1, 2, N_HEADS, HD, N_STATE)

    w_in1 = w_in_o[0].astype(BF16)
    w_out1 = w_out_o[0].astype(BF16)
    mod1 = mods[1].reshape(SUBLANE, 1, 3 * d)
    nw1 = norm_w[1].reshape(1, d)
    fw = final_norm_w.reshape(1, d)
    segs_p = ((0, d, True, BF16), (d, d, True, F32), (2 * d, d, True, F32), (3 * d, d, False, F32))
    q_t, k_t, v_t, g = _proj_in(xp, mod1, nw1, w_in1, segs_p, bp * lp, 0)
    o_t, new_cache_k, new_cache_v = _ctx_attn(q_t, k_t, v_t, bp, lp)
    y_prompt = _proj_out1(xp, o_t, g, mod1, fw, w_out1, bp * lp, 0).reshape(bp, lp, d)

    segs_s = ((0, d, True, BF16), (d, d, True, BF16), (2 * d, d, True, BF16), (3 * d, d, False, F32))
    q_t, k_t, v_t, g = _proj_in(xs, mod1, nw1, w_in1, segs_s, ls, 1)
    t2 = _na_bias_tables(rpb[0])
    o_t = _na_attn(q_t, k_t, v_t, cache_k[:, 0], cache_v[:, 0], t2, bs, ls)
    y_sample = _proj_out1(xs, o_t, g, mod1, fw, w_out1, ls, 1).reshape(bs, ls, d)

    return (y_prompt, y_sample, new_state_ssd, new_cache_k, new_cache_v)
```

```python
import functools
import math

import jax
import jax.numpy as jnp
import numpy as np
from jax import lax
from jax.experimental import pallas as pl
from jax.experimental.pallas import tpu as pltpu

F32 = jnp.float32
BF16 = jnp.bfloat16

EPS = 1e-6
GRID_W = 64
WIN_H = 8
WIN_W = 16
HD = 64
N_HEADS = 16
N_STATE = 128
N_GROUPS = 2
CHUNK = 128
HY_EMB = 33
HY_BANDS = (HY_EMB - 1) // 2
HY_HID = 64
HY_TARGET = 1e-2
HY_DECAY_PCT_HI = 0.3
HY_DECAY_PCT_LO = 1.5

LANE = 128
SUBLANE = 8
VMEM_LIMIT = 56 * 1024 * 1024

ROW_TILE = 256
DFT_BLOCK = 512
HY_BATCH = 8
NA_QROWS = 8
NEG_INF = float("-inf")


def _bf(x):
    return x.astype(BF16)


def _dot(a, b):
    return jnp.dot(a, b, preferred_element_type=F32)


def _dot_nt(a, b):
    return lax.dot_general(a, b, (((1,), (1,)), ((), ())), preferred_element_type=F32)


def _split2(x):
    hi = _bf(x)
    lo = _bf(x - hi.astype(F32))
    return hi, lo


def _split3(x):
    hi = _bf(x)
    r = x - hi.astype(F32)
    mid = _bf(r)
    lo = _bf(r - mid.astype(F32))
    return hi, mid, lo


def _dot3(a, b):
    ah, al = _split2(a)
    bh, bl = _split2(b)
    return _dot(ah, bh) + (_dot(ah, bl) + _dot(al, bh))


def _dot_lhs_parts(a, b_exact, parts):
    pieces = _split3(a) if parts == 3 else _split2(a)
    acc = _dot(pieces[0], b_exact)
    for p in pieces[1:]:
        acc = acc + _dot(p, b_exact)
    return acc


def _dot_rhs_parts(a_exact, b, parts):
    pieces = _split3(b) if parts == 3 else _split2(b)
    acc = _dot(a_exact, pieces[0])
    for p in pieces[1:]:
        acc = acc + _dot(a_exact, p)
    return acc


def _silu(x):
    return x * jax.nn.sigmoid(x)


def _rms(x, g):
    ms = jnp.mean(x * x, axis=-1, keepdims=True)
    return x * lax.rsqrt(ms + EPS) * g


def _softplus(x):
    return jnp.maximum(x, 0.0) + jnp.log1p(jnp.exp(-jnp.abs(x)))


def _params(*sem):
    return pltpu.CompilerParams(dimension_semantics=sem, vmem_limit_bytes=VMEM_LIMIT)


def _mods_kernel(c_ref, w_ref, b_ref, o_ref):
    a = _silu(c_ref[...])
    o_ref[0] = _dot3(a, w_ref[0]) + b_ref[0]


def _ada_mods(cvecs, w_ada, b_ada):
    depth, d, n3 = w_ada.shape
    tn = n3 // 4
    return pl.pallas_call(
        _mods_kernel,
        grid=(depth, n3 // tn),
        in_specs=[pl.BlockSpec((SUBLANE, d), lambda l, j: (0, 0)),
                  pl.BlockSpec((1, d, tn), lambda l, j: (l, 0, j)),
                  pl.BlockSpec((1, 1, tn), lambda l, j: (l, 0, j))],
        out_specs=pl.BlockSpec((1, SUBLANE, tn), lambda l, j: (l, 0, j)),
        out_shape=jax.ShapeDtypeStruct((depth, SUBLANE, n3), F32),
        compiler_params=_params("arbitrary", "arbitrary"),
        name="ada_mods",
    )(cvecs, w_ada, b_ada.reshape(depth, 1, n3))


def _proj_in_kernel(x_ref, mod_ref, nw_ref, w_ref, *out_refs, segs):
    d = x_ref.shape[1]
    m = mod_ref[0]
    h = _rms(x_ref[...], nw_ref[...]) * (1.0 + m[:, d:2 * d]) + m[:, 0:d]
    hb = _bf(h)
    for (off, width, tiled), o_ref in zip(segs, out_refs):
        step = 4 * LANE
        for c0 in range(0, width, step):
            cw = min(step, width - c0)
            res = _dot(hb, w_ref[:, off + c0:off + c0 + cw])
            if tiled:
                for t in range(cw // LANE):
                    o_ref[(c0 // LANE) + t] = res[:, t * LANE:(t + 1) * LANE].astype(o_ref.dtype)
            else:
                o_ref[:, c0:c0 + cw] = res.astype(o_ref.dtype)


def _proj_in(x2d, mod, norm_w, w_bf, segs, rows_per_mod, mod_base):
    m_rows, d = x2d.shape
    tm = ROW_TILE
    out_shapes, out_specs = [], []
    for (_, width, tiled, dt) in segs:
        if tiled:
            out_shapes.append(jax.ShapeDtypeStruct((width // LANE, m_rows, LANE), dt))
            out_specs.append(pl.BlockSpec((width // LANE, tm, LANE), lambda i: (0, i, 0)))
        else:
            out_shapes.append(jax.ShapeDtypeStruct((m_rows, width), dt))
            out_specs.append(pl.BlockSpec((tm, width), lambda i: (i, 0)))
    kern = functools.partial(_proj_in_kernel, segs=tuple((o, w, t) for (o, w, t, _) in segs))
    return pl.pallas_call(
        kern,
        grid=(m_rows // tm,),
        in_specs=[pl.BlockSpec((tm, d), lambda i: (i, 0)),
                  pl.BlockSpec((1, 1, 3 * d), lambda i: (mod_base + (i * tm) // rows_per_mod, 0, 0)),
                  pl.BlockSpec((1, d), lambda i: (0, 0)),
                  pl.BlockSpec(w_bf.shape, lambda i: (0, 0), pipeline_mode=pl.Buffered(1))],
        out_specs=out_specs,
        out_shape=out_shapes,
        compiler_params=_params("arbitrary"),
        name="proj_in",
    )(x2d, mod, norm_w, w_bf)


def _ssd_kernel(*refs, has_init, want_final):
    it = iter(refs)
    xm = [next(it), None]
    xp = [next(it), None]
    xn = [next(it), None]
    xm[1], xp[1], xn[1] = next(it), next(it), next(it)
    dtr = [next(it), next(it)]
    cw_ref, cb_ref, dtb_ref, alog_ref, dskip_ref, exp_ref = (next(it) for _ in range(6))
    init_ref = next(it) if has_init else None
    y_refs = [next(it), next(it)]
    fin_ref = next(it) if want_final else None
    s_ref = next(it)

    c = pl.program_id(1)
    nc = pl.num_programs(1)
    q = xm[0].shape[1]
    dm = N_HEADS * HD
    gw = dm // N_GROUPS
    hpg = N_HEADS // N_GROUPS

    @pl.when(c == 0)
    def _():
        if has_init:
            for d in range(2):
                for t in range(dm // LANE):
                    s_ref[d, :, t * LANE:(t + 1) * LANE] = init_ref[0, d, t * LANE:(t + 1) * LANE, :].T
        else:
            s_ref[...] = jnp.zeros(s_ref.shape, F32)

    row = lax.broadcasted_iota(jnp.int32, (q, q), 0)
    col = lax.broadcasted_iota(jnp.int32, (q, q), 1)
    lane = lax.broadcasted_iota(jnp.int32, (q, LANE), 1)
    left = lane < HD
    tri = [(row >= col), (row <= col)]

    for d in range(2):
        chunk = c if d == 0 else nc - 1 - c
        x = xm[d][0].astype(F32)
        nchan = x.shape[1]
        r0 = lax.broadcasted_iota(jnp.int32, (q, nchan), 0)
        keep_prev = (chunk > 0).astype(F32)
        keep_next = (chunk < nc - 1).astype(F32)
        halo = xp[d].shape[1]
        prev_row = xp[d][0].astype(F32)[halo - 1:halo, :] * keep_prev
        next_row = xn[d][0].astype(F32)[0:1, :] * keep_next
        x_prev = jnp.where(r0 == 0, prev_row, pltpu.roll(x, 1, 0))
        x_next = jnp.where(r0 == q - 1, next_row, pltpu.roll(x, q - 1, 0))
        cw = cw_ref[...]
        xs = _silu(cb_ref[...] + x_prev * cw[0:1] + x * cw[1:2] + x_next * cw[2:3])
        xa = xs[:, 0:dm]
        bm = xs[:, dm:dm + N_GROUPS * N_STATE]
        cm = xs[:, dm + N_GROUPS * N_STATE:dm + 2 * N_GROUPS * N_STATE]

        dt = _softplus(dtr[d][0] + dtb_ref[...])
        adt = dt * (-jnp.exp(alog_ref[...]))
        tmat = jnp.where(tri[d], 1.0, 0.0).astype(BF16)
        cs = _dot_rhs_parts(tmat, adt, 3)
        cs_t = cs.T
        dt_t = dt.T
        edge = cs[q - 1:q, :] if d == 0 else cs[0:1, :]
        ex = exp_ref[d]
        e_cs = _dot_lhs_parts(jnp.exp(cs), ex, 2)
        w_st = _dot_lhs_parts(jnp.exp(edge - cs) * dt, ex, 2)
        xw = xa * w_st

        y_parts = []
        s_prev = s_ref[d]
        new_state = []
        for g in range(N_GROUPS):
            bg = bm[:, g * N_STATE:(g + 1) * N_STATE]
            cg = cm[:, g * N_STATE:(g + 1) * N_STATE]
            gmat = _dot_nt(_bf(cg), _bf(bg))
            y_off = _dot(_bf(cg), _bf(s_prev[:, g * gw:(g + 1) * gw]))
            new_state.append(_dot(_bf(bg.T), _bf(xw[:, g * gw:(g + 1) * gw])))
            for pr in range(hpg // 2):
                mh = []
                for j in range(2):
                    k = d * N_HEADS + g * hpg + 2 * pr + j
                    diff = cs[:, k:k + 1] - cs_t[k:k + 1, :]
                    lm = jnp.exp(jnp.where(tri[d], diff, NEG_INF))
                    mh.append(_bf(gmat * lm * dt_t[k:k + 1, :]))
                c0 = g * gw + pr * LANE
                xpair = xa[:, c0:c0 + LANE]
                rhs = jnp.concatenate([_bf(jnp.where(left, xpair, 0.0)), _bf(jnp.where(left, 0.0, xpair))], axis=0)
                y_d = _dot(jnp.concatenate(mh, axis=1), rhs)
                y_parts.append(y_d + y_off[:, pr * LANE:(pr + 1) * LANE] * e_cs[:, c0:c0 + LANE])
        y = jnp.concatenate(y_parts, axis=1) + xa * dskip_ref[d:d + 1, :]
        y_refs[d][0] = y.astype(y_refs[d].dtype)
        e_edge = e_cs[q - 1:q, :] if d == 0 else e_cs[0:1, :]
        s_ref[d] = s_prev * e_edge + jnp.concatenate(new_state, axis=1)

    if want_final:
        @pl.when(c == nc - 1)
        def _():
            for d in range(2):
                for t in range(dm // LANE):
                    fin_ref[0, d, t * LANE:(t + 1) * LANE, :] = s_ref[d, :, t * LANE:(t + 1) * LANE].T


def _ssd(xbc, dt_raw, conv_w, conv_b, dt_bias, a_log, d_skip, init, want_final):
    b, l, nchan = xbc.shape
    dm = N_HEADS * HD
    q = CHUNK
    nc = l // q
    halo = SUBLANE * (4 // xbc.dtype.itemsize)
    sub_per_chunk = q // halo
    nsub = l // halo

    def fwd(c):
        return c

    def bwd(c):
        return nc - 1 - c

    def main(cf):
        return pl.BlockSpec((1, q, nchan), lambda bi, c: (bi, cf(c), 0))

    def prev(cf):
        return pl.BlockSpec((1, halo, nchan), lambda bi, c: (bi, jnp.maximum(cf(c) * sub_per_chunk - 1, 0), 0))

    def nxt(cf):
        return pl.BlockSpec((1, halo, nchan),
                            lambda bi, c: (bi, jnp.minimum((cf(c) + 1) * sub_per_chunk, nsub - 1), 0))

    def full(a):
        nd = a.ndim
        return pl.BlockSpec(a.shape, lambda bi, c: (0,) * nd)

    ex = np.zeros((2, LANE, dm), np.float32)
    for d in range(2):
        for h in range(N_HEADS):
            ex[d, d * N_HEADS + h, h * HD:(h + 1) * HD] = 1.0
    ex = jnp.asarray(ex, BF16)
    pad = LANE - 2 * N_HEADS
    dtb = jnp.pad(dt_bias.reshape(1, 2 * N_HEADS), ((0, 0), (0, pad)))
    alog = jnp.pad(a_log.reshape(1, 2 * N_HEADS), ((0, 0), (0, pad)))
    dsk = jnp.repeat(d_skip, HD, axis=1)
    cb = conv_b.reshape(1, nchan)

    args = [xbc, xbc, xbc, xbc, xbc, xbc, dt_raw, dt_raw, conv_w, cb, dtb, alog, dsk, ex]
    in_specs = [main(fwd), prev(fwd), nxt(fwd), main(bwd), prev(bwd), nxt(bwd),
                pl.BlockSpec((1, q, LANE), lambda bi, c: (bi, c, 0)),
                pl.BlockSpec((1, q, LANE), lambda bi, c: (bi, nc - 1 - c, 0)),
                full(conv_w), full(cb), full(dtb), full(alog), full(dsk), full(ex)]
    has_init = init is not None
    if has_init:
        args.append(init)
        in_specs.append(pl.BlockSpec((1, 2, dm, N_STATE), lambda bi, c: (bi, 0, 0, 0)))
    out_shapes = [jax.ShapeDtypeStruct((b, l, dm), BF16), jax.ShapeDtypeStruct((b, l, dm), BF16)]
    out_specs = [pl.BlockSpec((1, q, dm), lambda bi, c: (bi, c, 0)),
                 pl.BlockSpec((1, q, dm), lambda bi, c: (bi, nc - 1 - c, 0))]
    if want_final:
        out_shapes.append(jax.ShapeDtypeStruct((b, 2, dm, N_STATE), F32))
        out_specs.append(pl.BlockSpec((1, 2, dm, N_STATE), lambda bi, c: (bi, 0, 0, 0)))
    return pl.pallas_call(
        functools.partial(_ssd_kernel, has_init=has_init, want_final=want_final),
        grid=(b, nc),
        in_specs=in_specs,
        out_specs=out_specs,
        out_shape=out_shapes,
        scratch_shapes=[pltpu.VMEM((2, N_STATE, dm), F32)],
        compiler_params=_params("arbitrary", "arbitrary"),
        name="ssd_scan",
    )(*args)


def _hyena_tables(l):
    pos = np.abs(np.arange(2 * l, dtype=np.float64) - l)
    t = pos / (l - 1)
    w = 2.0 * math.pi * pos / l
    f = np.linspace(1e-4, HY_BANDS - 1, HY_BANDS)
    feats = np.zeros((2 * l, LANE), np.float64)
    feats[:, 0] = t
    feats[:, 1:1 + HY_BANDS] = np.cos(f[None] * w[:, None])
    feats[:, 1 + HY_BANDS:1 + 2 * HY_BANDS] = -np.sin(f[None] * w[:, None])
    return jnp.asarray(feats, F32)


def _filter_kernel(f_ref, w1_ref, b1_ref, w2_ref, b2_ref, w3_ref, fr_ref, ad_ref, o_ref):
    feats = f_ref[...]
    fr = fr_ref[...]
    h1 = jnp.sin(fr * (_dot3(feats, w1_ref[...]) + b1_ref[...]))
    h2 = jnp.sin(fr * (_dot3(h1, w2_ref[...]) + b2_ref[...]))
    filt = _dot3(h2, w3_ref[...])
    o_ref[...] = filt * jnp.exp(-feats[:, 0:1] * ad_ref[...])


def _hyena_filter_linear(l, w1, b1, w2, b2, w3, freq):
    db = w3.shape[1] // 2
    tr = min(512, l)
    nbk = l // tr
    feats = _hyena_tables(l)
    w1p = jnp.pad(w1, ((0, LANE - HY_EMB), (0, 0)))
    deltas = np.linspace(math.log(HY_TARGET) / HY_DECAY_PCT_HI, math.log(HY_TARGET) / HY_DECAY_PCT_LO, db)
    absd = jnp.asarray(np.abs(deltas)[None], F32)

    def full(a):
        return pl.BlockSpec(a.shape, lambda i: (0, 0))

    b1r, b2r, frr = b1.reshape(1, -1), b2.reshape(1, -1), freq.reshape(1, -1)
    return pl.pallas_call(
        _filter_kernel,
        grid=(2 * nbk,),
        in_specs=[pl.BlockSpec((tr, LANE), lambda i: (i, 0)), full(w1p), full(b1r), full(w2), full(b2r),
                  pl.BlockSpec((HY_HID, db), lambda i: (0, jnp.where(i < nbk, 1, 0))),
                  full(frr), full(absd)],
        out_specs=pl.BlockSpec((tr, db), lambda i: (i, 0)),
        out_shape=jax.ShapeDtypeStruct((2 * l, db), F32),
        compiler_params=_params("arbitrary"),
        name="hyena_filter",
    )(feats, w1p, b1r, w2, b2r, w3, frr, absd)


def _dft_tables(p):
    n = 2 * p
    f = np.arange(p, dtype=np.float64)[:, None] + 0.5
    e = np.arange(n, dtype=np.float64)[None]
    ang = 2.0 * math.pi * f * e / n
    fwd = np.concatenate([np.cos(ang), -np.sin(ang)], axis=0)
    fa = fwd[:, :p]
    fbn = -fwd[:, p:]
    fbn[:, 0] = 0.0
    angt = ang[:, :p].T
    inv = np.concatenate([np.cos(angt), -np.sin(angt)], axis=1) * (2.0 / n)

    return tuple(jnp.asarray(m, F32).astype(BF16) for m in (fa, fbn, inv))


def _spectra_kernel(k1_ref, k0_ref, fa_ref, fb_ref, o_ref):
    res = _dot(fa_ref[...], _bf(k1_ref[...])) + _dot(fb_ref[...], _bf(k0_ref[...]))
    for t in range(o_ref.shape[0]):
        o_ref[t, 0] = res[:, t * LANE:(t + 1) * LANE]


def _hyena_spectra(klin, p, fa, fbn):
    two_l, db = klin.shape
    nseg = two_l // p - 1
    nct = db // LANE
    return pl.pallas_call(
        _spectra_kernel,
        grid=(nseg,),
        in_specs=[pl.BlockSpec((p, db), lambda s: (s + 1, 0)),
                  pl.BlockSpec((p, db), lambda s: (s, 0)),
                  pl.BlockSpec(fa.shape, lambda s: (0, 0)),
                  pl.BlockSpec(fbn.shape, lambda s: (0, 0))],
        out_specs=pl.BlockSpec((nct, 1, 2 * p, LANE), lambda s: (0, s, 0, 0)),
        out_shape=jax.ShapeDtypeStruct((nct, nseg, 2 * p, LANE), F32),
        compiler_params=_params("arbitrary"),
        name="hyena_spectra",
    )(klin, klin, fa, fbn)


def _hyena_kernel(x0_ref, x1_ref, v_ref, g_ref, gs_ref, w0_ref, w1_ref, w2_ref, c0_ref, c1_ref, c2_ref,
                  hb_ref, fa_ref, iv_ref, o_ref, u_scr, y_scr, w_scr, gate_scr, *, p):
    bt, l = x0_ref.shape[1], x0_ref.shape[2]
    nb = l // p
    row = lax.broadcasted_iota(jnp.int32, (l, LANE), 0)

    def conv3(x, w_ref, c_ref):
        w = w_ref[...]
        xm1 = jnp.where(row == 0, 0.0, pltpu.roll(x, 1, 0))
        xp1 = jnp.where(row == l - 1, 0.0, pltpu.roll(x, l - 1, 0))
        return c_ref[...] + xm1 * w[0:1] + x * w[1:2] + xp1 * w[2:3]

    for bb in range(bt):
        w_scr[bb] = (conv3(v_ref[0, bb].astype(F32), w2_ref, c2_ref)
                     * conv3(x1_ref[0, bb].astype(F32), w1_ref, c1_ref))
        gate_scr[bb] = conv3(x0_ref[0, bb].astype(F32), w0_ref, c0_ref) * _silu(g_ref[0, bb].astype(F32))
    fa = fa_ref[...]
    for j in range(nb):
        rhs = jnp.concatenate([_bf(w_scr[bb, j * p:(j + 1) * p, :]) for bb in range(bt)], axis=1)
        u_scr[j] = _dot(fa, rhs)
    rt_rows = 64
    for i in range(nb):
        def body(rt, carry):
            r0 = pl.multiple_of(rt * rt_rows, rt_rows)
            for bb in range(bt):
                ls = slice(bb * LANE, (bb + 1) * LANE)
                acc_re = jnp.zeros((rt_rows, LANE), F32)
                acc_im = jnp.zeros((rt_rows, LANE), F32)
                for j in range(nb):
                    s = i - j + nb - 1
                    gre = gs_ref[0, s, pl.ds(r0, rt_rows), :]
                    gim = gs_ref[0, s, pl.ds(p + r0, rt_rows), :]
                    ure = u_scr[j, pl.ds(r0, rt_rows), ls]
                    uim = u_scr[j, pl.ds(p + r0, rt_rows), ls]
                    acc_re = acc_re + (gre * ure - gim * uim)
                    acc_im = acc_im + (gre * uim + gim * ure)
                y_scr[pl.ds(r0, rt_rows), ls] = acc_re
                y_scr[pl.ds(p + r0, rt_rows), ls] = acc_im
            return carry
        lax.fori_loop(0, p // rt_rows, body, 0)
        conv = _dot(iv_ref[...], _bf(y_scr[...]))
        sl = slice(i * p, (i + 1) * p)
        for bb in range(bt):
            wi = w_scr[bb, sl, :]
            o_ref[0, bb, sl, :] = (gate_scr[bb, sl, :] * (conv[:, bb * LANE:(bb + 1) * LANE] + wi * hb_ref[...])
                                   ).astype(o_ref.dtype)


def _hyena(u_t, g_t, klin, conv_w, conv_b, hy_bias, b, l):
    nct = klin.shape[1] // LANE
    p = min(DFT_BLOCK, l)
    nb = l // p
    nseg = 2 * nb - 1
    bt = min(b, HY_BATCH) if nb == 1 else 1
    fa, fbn, iv = _dft_tables(p)
    spectra = _hyena_spectra(klin, p, fa, fbn)
    u4 = u_t.reshape(3 * nct, b, l, LANE)
    g4 = g_t.reshape(nct, b, l, LANE)
    cbr = conv_b.reshape(1, -1)
    hbr = hy_bias.reshape(1, -1)

    def act(k):
        return pl.BlockSpec((1, bt, l, LANE), lambda ct, bi: (k * nct + ct, bi, 0, 0))

    def wcol(rows, k):
        return pl.BlockSpec((rows, LANE), lambda ct, bi: (0, k * nct + ct))

    def full(a):
        return pl.BlockSpec(a.shape, lambda ct, bi: (0, 0), pipeline_mode=pl.Buffered(1))

    out = pl.pallas_call(
        functools.partial(_hyena_kernel, p=p),
        grid=(nct, b // bt),
        in_specs=[act(0), act(1), act(2),
                  pl.BlockSpec((1, bt, l, LANE), lambda ct, bi: (ct, bi, 0, 0)),
                  pl.BlockSpec((1, nseg, 2 * p, LANE), lambda ct, bi: (ct, 0, 0, 0)),
                  wcol(3, 0), wcol(3, 1), wcol(3, 2), wcol(1, 0), wcol(1, 1), wcol(1, 2),
                  pl.BlockSpec((1, LANE), lambda ct, bi: (0, ct)),
                  full(fa), full(iv)],
        out_specs=pl.BlockSpec((1, bt, l, LANE), lambda ct, bi: (ct, bi, 0, 0)),
        out_shape=jax.ShapeDtypeStruct((nct, b, l, LANE), BF16),
        scratch_shapes=[pltpu.VMEM((nb, 2 * p, bt * LANE), F32), pltpu.VMEM((2 * p, bt * LANE), F32),
                        pltpu.VMEM((bt, l, LANE), F32), pltpu.VMEM((bt, l, LANE), F32)],
        compiler_params=_params("arbitrary", "arbitrary"),
        name="hyena_conv",
    )(u4, u4, u4, g4, spectra, conv_w, conv_w, conv_w, cbr, cbr, cbr, hbr, fa, iv)
    return out.reshape(nct, b * l, LANE)


def _proj_out0_kernel(x_ref, yf_ref, yb_ref, z_ref, yh_ref, mod_ref, naw_ref, w_ref, o_ref):
    d = x_ref.shape[1]
    ya = _rms((yf_ref[...].astype(F32) + yb_ref[...].astype(F32)) * _silu(z_ref[...].astype(F32)), naw_ref[...])
    yh = jnp.concatenate([yh_ref[t] for t in range(yh_ref.shape[0])], axis=1)
    da = ya.shape[1]
    acc = _dot(_bf(ya), w_ref[0:da, :]) + _dot(_bf(yh), w_ref[da:, :])
    gate = mod_ref[0][:, 2 * d:3 * d]
    o_ref[...] = x_ref[...] + gate * acc


def _proj_out0(x2d, y_f, y_b, z, yh_t, mod, norm_a_w, w_bf, rows_per_mod, mod_base):
    m_rows, d = x2d.shape
    tm = ROW_TILE
    da = y_f.shape[1]
    nt = yh_t.shape[0]

    def rowspec(wd):
        return pl.BlockSpec((tm, wd), lambda i: (i, 0))

    return pl.pallas_call(
        _proj_out0_kernel,
        grid=(m_rows // tm,),
        in_specs=[rowspec(d), rowspec(da), rowspec(da), rowspec(da),
                  pl.BlockSpec((nt, tm, LANE), lambda i: (0, i, 0)),
                  pl.BlockSpec((1, 1, 3 * d), lambda i: (mod_base + (i * tm) // rows_per_mod, 0, 0)),
                  pl.BlockSpec((1, da), lambda i: (0, 0)),
                  pl.BlockSpec(w_bf.shape, lambda i: (0, 0), pipeline_mode=pl.Buffered(1))],
        out_specs=rowspec(d),
        out_shape=jax.ShapeDtypeStruct((m_rows, d), F32),
        compiler_params=_params("arbitrary"),
        name="proj_out0",
    )(x2d, y_f, y_b, z, yh_t, mod, norm_a_w, w_bf)


def _proj_out1_kernel(x_ref, o_ref_in, g_ref, mod_ref, fw_ref, w_ref, y_ref):
    d = x_ref.shape[1]
    o = jnp.concatenate([o_ref_in[t] for t in range(o_ref_in.shape[0])], axis=1).astype(F32)
    a = o * _silu(g_ref[...].astype(F32))
    acc = _dot(_bf(a), w_ref[...])
    gate = mod_ref[0][:, 2 * d:3 * d]
    y_ref[...] = _rms(x_ref[...] + gate * acc, fw_ref[...])


def _proj_out1(x2d, o_t, g, mod, final_w, w_bf, rows_per_mod, mod_base):
    m_rows, d = x2d.shape
    tm = ROW_TILE
    nt = o_t.shape[0]
    return pl.pallas_call(
        _proj_out1_kernel,
        grid=(m_rows // tm,),
        in_specs=[pl.BlockSpec((tm, d), lambda i: (i, 0)),
                  pl.BlockSpec((nt, tm, LANE), lambda i: (0, i, 0)),
                  pl.BlockSpec((tm, g.shape[1]), lambda i: (i, 0)),
                  pl.BlockSpec((1, 1, 3 * d), lambda i: (mod_base + (i * tm) // rows_per_mod, 0, 0)),
                  pl.BlockSpec((1, d), lambda i: (0, 0)),
                  pl.BlockSpec(w_bf.shape, lambda i: (0, 0), pipeline_mode=pl.Buffered(1))],
        out_specs=pl.BlockSpec((tm, d), lambda i: (i, 0)),
        out_shape=jax.ShapeDtypeStruct((m_rows, d), F32),
        compiler_params=_params("arbitrary"),
        name="proj_out1",
    )(x2d, o_t, g, mod, final_w, w_bf)


def _ctx_layer_kernel(x_ref, mod_ref, nw_ref, wi_ref, wo_ref, fw_ref, y_ref, ck_ref, cv_ref):
    l, d = x_ref.shape
    m = mod_ref[0]
    x = x_ref[...]
    hb = _bf(_rms(x, nw_ref[...]) * (1.0 + m[:, d:2 * d]) + m[:, 0:d])
    qb = _bf(_dot(hb, wi_ref[:, 0:d]) * (HD ** -0.5))
    k = _dot(hb, wi_ref[:, d:2 * d])
    v = _dot(hb, wi_ref[:, 2 * d:3 * d])
    g = _dot(hb, wi_ref[:, 3 * d:4 * d])
    nh = d // HD
    scores, vbs = [], []
    for h in range(nh):
        sl = slice(h * HD, (h + 1) * HD)
        kh = k[:, sl]
        vh = v[:, sl]
        ck_ref[0, 0, h] = kh
        cv_ref[0, 0, h] = vh
        vbs.append(_bf(vh))
        scores.append(_dot_nt(qb[:, sl], _bf(kh)))
    s_all = jnp.concatenate(scores, axis=0)
    pexp = jnp.exp(s_all - jnp.max(s_all, axis=-1, keepdims=True))
    den = jnp.sum(pexp, axis=-1, keepdims=True)
    pb = _bf(pexp)
    outs = [_dot(pb[h * l:(h + 1) * l], vbs[h]) / den[h * l:(h + 1) * l] for h in range(nh)]
    a = jnp.concatenate(outs, axis=1) * _silu(g)
    acc = _dot(_bf(a), wo_ref[...])
    y_ref[...] = _rms(x + m[:, 2 * d:3 * d] * acc, fw_ref[...])


def _ctx_layer(x2d, b, l, mod, norm_w, w_in_bf, w_out_bf, final_w):
    d = x2d.shape[1]
    nh = d // HD
    cache_spec = pl.BlockSpec((1, 1, nh, l, HD), lambda bi: (bi, 0, 0, 0, 0))
    cache_shape = jax.ShapeDtypeStruct((b, 1, nh, l, HD), F32)

    def const(a):
        nd = a.ndim
        return pl.BlockSpec(a.shape, lambda bi: (0,) * nd, pipeline_mode=pl.Buffered(1))

    return pl.pallas_call(
        _ctx_layer_kernel,
        grid=(b,),
        in_specs=[pl.BlockSpec((l, d), lambda bi: (bi, 0)),
                  pl.BlockSpec((1, 1, 3 * d), lambda bi: (0, 0, 0)),
                  const(norm_w), const(w_in_bf), const(w_out_bf), const(final_w)],
        out_specs=[pl.BlockSpec((l, d), lambda bi: (bi, 0)), cache_spec, cache_spec],
        out_shape=[jax.ShapeDtypeStruct((b * l, d), F32), cache_shape, cache_shape],
        compiler_params=_params("arbitrary"),
        name="ctx_layer",
    )(x2d, mod, norm_w, w_in_bf, w_out_bf, final_w)


def _na_bias_kernel(rpb_ref, o_ref):
    h = pl.program_id(0)
    ndr = 2 * WIN_H - 1
    ndc = 2 * WIN_W - 1
    cq = lax.broadcasted_iota(jnp.int32, (GRID_W, LANE), 0)
    lane = lax.broadcasted_iota(jnp.int32, (GRID_W, LANE), 1)
    ck = lane & (GRID_W - 1)
    first = lane < GRID_W
    dc = jnp.clip(ck - cq + (WIN_W - 1), 0, ndc - 1)
    col0 = jnp.clip(cq - WIN_W // 2, 0, GRID_W - WIN_W)
    col_in = (ck >= col0) & (ck < col0 + WIN_W)
    dc_is = [dc == e for e in range(ndc)]
    tables = []
    for dr in range(ndr):
        t = jnp.full((GRID_W, LANE), NEG_INF, F32)
        for e in range(ndc):
            t = jnp.where(dc_is[e] & col_in, rpb_ref[(h * ndr + dr) * ndc + e], t)
        tables.append(t)
    ninf = jnp.full((GRID_W, LANE), NEG_INF, F32)
    for e in range(ndr + 1):
        lo = tables[e - 1] if e >= 1 else ninf
        hi = tables[e] if e < ndr else ninf
        o_ref[0, e] = jnp.where(first, lo, hi)


def _na_bias_tables(rpb):
    nh = rpb.shape[0]
    ne = 2 * WIN_H
    return pl.pallas_call(
        _na_bias_kernel,
        grid=(nh,),
        in_specs=[pl.BlockSpec(memory_space=pltpu.SMEM)],
        out_specs=pl.BlockSpec((1, ne, GRID_W, LANE), lambda h: (h, 0, 0, 0)),
        out_shape=jax.ShapeDtypeStruct((nh, ne, GRID_W, LANE), F32),
        compiler_params=_params("arbitrary"),
        name="na_bias",
    )(rpb.reshape(-1))


def _na_kernel(q_ref, kp_ref, kc_ref, kn_ref, vp_ref, vc_ref, vn_ref, ck_ref, cv_ref, t2_ref, o_ref, mask_scr,
               *, n_rows):
    rb = pl.program_id(1)
    qrows = NA_QROWS
    krows = 2 * qrows
    nq = qrows * GRID_W
    nk = krows * GRID_W
    half = (qrows // 2) * GRID_W
    q = q_ref[0, 0] * (HD ** -0.5)
    kloc = jnp.concatenate([kp_ref[0, 0][nq - half:nq], kc_ref[0, 0], kn_ref[0, 0][0:half]], axis=0)
    vloc = jnp.concatenate([vp_ref[0, 0][nq - half:nq], vc_ref[0, 0], vn_ref[0, 0][0:half]], axis=0)

    @pl.when(pl.program_id(2) == 0)
    def _():
        a = lax.broadcasted_iota(jnp.int32, (nq, nk), 0) // GRID_W
        i = lax.broadcasted_iota(jnp.int32, (nq, nk), 1) // GRID_W
        r = rb * qrows + a
        kr = rb * qrows - qrows // 2 + i
        rs = jnp.clip(r - WIN_H // 2, 0, n_rows - WIN_H)
        mask_scr[...] = jnp.where((kr >= rs) & (kr < rs + WIN_H), 0.0, NEG_INF)

    rowmask = mask_scr[...]
    ninf_tile = jnp.full((GRID_W, LANE), NEG_INF, F32)
    outs = []
    for j in range(LANE // HD):
        sl = slice(j * HD, (j + 1) * HD)
        qh = q[:, sl]
        strips = []
        for aa in range(qrows):
            tiles = []
            for ip in range(krows // 2):
                e = 2 * ip - aa + (WIN_H - 1) - qrows // 2 + 1
                tiles.append(t2_ref[j, e] if 0 <= e < 2 * WIN_H else ninf_tile)
            strips.append(jnp.concatenate(tiles, axis=1))
        bias = jnp.concatenate(strips, axis=0)
        s = _dot_nt(qh, kloc[:, sl]) + bias + rowmask
        sc = _dot_nt(qh, _bf(ck_ref[0, j]))
        m = jnp.maximum(jnp.max(s, axis=-1, keepdims=True), jnp.max(sc, axis=-1, keepdims=True))
        pl_ = jnp.exp(s - m)
        pc = jnp.exp(sc - m)
        den = jnp.sum(pl_, axis=-1, keepdims=True) + jnp.sum(pc, axis=-1, keepdims=True)
        o = _dot(_bf(pl_), vloc[:, sl]) + _dot(_bf(pc), _bf(cv_ref[0, j]))
        outs.append(o / den)
    o_ref[0, 0] = jnp.concatenate(outs, axis=1).astype(o_ref.dtype)


def _na_attn(q_t, k_t, v_t, cache_k, cache_v, t2, b, l):
    npair = q_t.shape[0]
    hpp = LANE // HD
    n_rows = l // GRID_W
    nrb = n_rows // NA_QROWS
    nq = NA_QROWS * GRID_W
    lc = cache_k.shape[2]
    q4, k4, v4 = (a.reshape(npair, b, l, LANE) for a in (q_t, k_t, v_t))

    def blk(f):
        return pl.BlockSpec((1, 1, nq, LANE), lambda bi, rb, hp: (hp, bi, f(rb), 0))

    cur = blk(lambda rb: rb)
    prv = blk(lambda rb: jnp.maximum(rb - 1, 0))
    nxt = blk(lambda rb: jnp.minimum(rb + 1, nrb - 1))
    cspec = pl.BlockSpec((1, hpp, lc, HD), lambda bi, rb, hp: (bi, hp, 0, 0))
    out = pl.pallas_call(
        functools.partial(_na_kernel, n_rows=n_rows),
        grid=(b, nrb, npair),
        in_specs=[cur, prv, cur, nxt, prv, cur, nxt, cspec, cspec,
                  pl.BlockSpec((hpp, 2 * WIN_H, GRID_W, LANE), lambda bi, rb, hp: (hp, 0, 0, 0))],
        out_specs=cur,
        out_shape=jax.ShapeDtypeStruct((npair, b, l, LANE), BF16),
        scratch_shapes=[pltpu.VMEM((nq, 2 * nq), F32)],
        compiler_params=_params("arbitrary", "arbitrary", "arbitrary"),
        name="na_attn",
    )(q4, k4, k4, k4, v4, v4, v4, cache_k, cache_v, t2)
    return out.reshape(npair, b * l, LANE)


def _layer0(x2d, b, l, mod, rows_per_mod, mod_base, norm_w, w_in_bf, w_out_bf, p, init, want_final, klin):
    dm = N_HEADS * HD
    d_xbc = dm + 2 * N_GROUPS * N_STATE
    d_b = klin.shape[1]
    segs = ((0, dm, False, BF16), (dm, d_xbc, False, BF16), (dm + d_xbc, 3 * d_b, True, BF16),
            (dm + d_xbc + 3 * d_b, d_b, True, BF16), (dm + d_xbc + 4 * d_b, LANE, False, F32))
    z, xbc, u_t, g_t, dt_raw = _proj_in(x2d, mod, norm_w, w_in_bf, segs, rows_per_mod, mod_base)
    res = _ssd(xbc.reshape(b, l, d_xbc), dt_raw.reshape(b, l, LANE), p["conv_a_w"], p["conv_a_b"],
               p["dt_bias"], p["a_log"], p["d_skip"], init, want_final)
    y_f, y_b = res[0].reshape(b * l, dm), res[1].reshape(b * l, dm)
    yh_t = _hyena(u_t, g_t, klin, p["conv_b_w"], p["conv_b_b"], p["hy_bias"], b, l)
    x_new = _proj_out0(x2d, y_f, y_b, z, yh_t, mod, p["norm_a_w"], w_out_bf, rows_per_mod, mod_base)
    return x_new, (res[2] if want_final else None)


def kernel(x_prompt, x_sample, state_ssd, cache_k, cache_v, c, c_ctx, norm_w, w_ada, b_ada, w_in_e, w_out_e, conv_a_w, conv_a_b, dt_bias, a_log, d_skip, norm_a_w, conv_b_w, conv_b_b, hf_w1, hf_b1, hf_w2, hf_b2, hf_w3, hf_freq, hy_bias, w_in_o, w_out_o, rpb, final_norm_w):
    bp, lp, d = x_prompt.shape
    bs, ls, _ = x_sample.shape
    dm = N_HEADS * HD
    d_xbc = dm + 2 * N_GROUPS * N_STATE
    n_dt = 2 * N_HEADS

    cvecs = jnp.concatenate([c_ctx[None], c, jnp.zeros((SUBLANE - 1 - bs, d), F32)], axis=0)
    mods = _ada_mods(cvecs, w_ada, b_ada)

    xp = x_prompt.reshape(bp * lp, d)
    xs = x_sample.reshape(bs * ls, d)

    wi = w_in_e[0]
    o_dt = dm + d_xbc
    w_in0 = jnp.concatenate([wi[:, :o_dt], wi[:, o_dt + n_dt:], wi[:, o_dt:o_dt + n_dt],
                             jnp.zeros((d, LANE - n_dt), F32)], axis=1).astype(BF16)
    w_out0 = w_out_e[0].astype(BF16)
    p0 = dict(conv_a_w=conv_a_w[0], conv_a_b=conv_a_b[0], dt_bias=dt_bias[0], a_log=a_log[0], d_skip=d_skip[0],
              norm_a_w=norm_a_w[0].reshape(1, -1), conv_b_w=conv_b_w[0], conv_b_b=conv_b_b[0], hy_bias=hy_bias[0])
    mod0 = mods[0].reshape(SUBLANE, 1, 3 * d)
    nw0 = norm_w[0].reshape(1, d)
    hf = (hf_w1[0], hf_b1[0], hf_w2[0], hf_b2[0], hf_w3[0], hf_freq[0])
    klin_p = _hyena_filter_linear(lp, *hf)
    klin_s = _hyena_filter_linear(ls, *hf)
    xp, fin = _layer0(xp, bp, lp, mod0, bp * lp, 0, nw0, w_in0, w_out0, p0, None, True, klin_p)
    init_s = state_ssd[:, 0].reshape(bs, 2, dm, N_STATE)
    xs, _ = _layer0(xs, bs, ls, mod0, ls, 1, nw0, w_in0, w_out0, p0, init_s, False, klin_s)
    new_state_ssd = fin.reshape(bp, 1, 2, N_HEADS, HD, N_STATE)

    w_in1 = w_in_o[0].astype(BF16)
    w_out1 = w_out_o[0].astype(BF16)
    mod1 = mods[1].reshape(SUBLANE, 1, 3 * d)
    nw1 = norm_w[1].reshape(1, d)
    fw = final_norm_w.reshape(1, d)
    y_prompt, new_cache_k, new_cache_v = _ctx_layer(xp, bp, lp, mod1, nw1, w_in1, w_out1, fw)
    y_prompt = y_prompt.reshape(bp, lp, d)

    segs_s = ((0, d, True, BF16), (d, d, True, BF16), (2 * d, d, True, BF16), (3 * d, d, False, BF16))
    q_t, k_t, v_t, g = _proj_in(xs, mod1, nw1, w_in1, segs_s, ls, 1)
    t2 = _na_bias_tables(rpb[0])
    o_t = _na_attn(q_t, k_t, v_t, cache_k[:, 0], cache_v[:, 0], t2, bs, ls)
    y_sample = _proj_out1(xs, o_t, g, mod1, fw, w_out1, ls, 1).reshape(bs, ls, d)

    return (y_prompt, y_sample, new_state_ssd, new_cache_k, new_cache_v)
```

```python
import functools
import math

import jax
import jax.numpy as jnp
import numpy as np
from jax import lax
from jax.experimental import pallas as pl
from jax.experimental.pallas import tpu as pltpu

F32 = jnp.float32
BF16 = jnp.bfloat16

EPS = 1e-6
GRID_W = 64
WIN_H = 8
WIN_W = 16
HD = 64
N_HEADS = 16
N_STATE = 128
N_GROUPS = 2
CHUNK = 128
HY_EMB = 33
HY_BANDS = (HY_EMB - 1) // 2
HY_HID = 64
HY_TARGET = 1e-2
HY_DECAY_PCT_HI = 0.3
HY_DECAY_PCT_LO = 1.5

LANE = 128
SUBLANE = 8
VMEM_LIMIT = 56 * 1024 * 1024

ROW_TILE = 512
DFT_BLOCK = 512
HY_BATCH = 8
HY_BATCH_LONG = 2
NA_QROWS = 8
NA_QTILE = 4
NA_KCHUNK = 128
NA_BAND = NA_QTILE + WIN_H
NEG_INF = float("-inf")


def _bf(x):
    return x.astype(BF16)


def _dot(a, b):
    return jnp.dot(a, b, preferred_element_type=F32)


def _dot_nt(a, b):
    return lax.dot_general(a, b, (((1,), (1,)), ((), ())), preferred_element_type=F32)


def _split2(x):
    hi = _bf(x)
    lo = _bf(x - hi.astype(F32))
    return hi, lo


def _split3(x):
    hi = _bf(x)
    r = x - hi.astype(F32)
    mid = _bf(r)
    lo = _bf(r - mid.astype(F32))
    return hi, mid, lo


def _dot3(a, b):
    ah, al = _split2(a)
    bh, bl = _split2(b)
    return _dot(ah, bh) + (_dot(ah, bl) + _dot(al, bh))


def _dot_rhs_parts(a_exact, b, parts):
    pieces = _split3(b) if parts == 3 else _split2(b)
    acc = _dot(a_exact, pieces[0])
    for p in pieces[1:]:
        acc = acc + _dot(a_exact, p)
    return acc


def _silu(x):
    return x * jax.nn.sigmoid(x)


def _rms(x, g):
    ms = jnp.mean(x * x, axis=-1, keepdims=True)
    return x * lax.rsqrt(ms + EPS) * g


def _softplus(x):
    return jnp.maximum(x, 0.0) + jnp.log1p(jnp.exp(-jnp.abs(x)))


def _params(*sem):
    return pltpu.CompilerParams(dimension_semantics=sem, vmem_limit_bytes=VMEM_LIMIT)


def _mods_kernel(c_ref, w_ref, b_ref, o_ref):
    a = _silu(c_ref[...])
    o_ref[0] = _dot3(a, w_ref[0]) + b_ref[0]


def _ada_mods(cvecs, w_ada, b_ada):
    depth, d, n3 = w_ada.shape
    tn = n3 // 4
    return pl.pallas_call(
        _mods_kernel,
        grid=(depth, n3 // tn),
        in_specs=[pl.BlockSpec((SUBLANE, d), lambda l, j: (0, 0)),
                  pl.BlockSpec((1, d, tn), lambda l, j: (l, 0, j)),
                  pl.BlockSpec((1, 1, tn), lambda l, j: (l, 0, j))],
        out_specs=pl.BlockSpec((1, SUBLANE, tn), lambda l, j: (l, 0, j)),
        out_shape=jax.ShapeDtypeStruct((depth, SUBLANE, n3), F32),
        compiler_params=_params("arbitrary", "arbitrary"),
        name="ada_mods",
    )(cvecs, w_ada, b_ada.reshape(depth, 1, n3))


ROWS, TILED, CHAN = "rows", "tiled", "chan"


def _proj_in_kernel(x_ref, mod_ref, nw_ref, w_ref, *refs, segs, has_wt):
    wt_ref = refs[0] if has_wt else None
    out_refs = refs[1:] if has_wt else refs
    d = x_ref.shape[1]
    m = mod_ref[0]
    h = _rms(x_ref[...], nw_ref[...]) * (1.0 + m[:, d:2 * d]) + m[:, 0:d]
    hb = _bf(h)
    for (off, width, layout), o_ref in zip(segs, out_refs):
        if layout == CHAN:
            o_ref[...] = _dot_nt(wt_ref[off:off + width, :], hb).astype(o_ref.dtype)
            continue
        step = 4 * LANE
        for c0 in range(0, width, step):
            cw = min(step, width - c0)
            res = _dot(hb, w_ref[:, off + c0:off + c0 + cw])
            if layout == TILED:
                for t in range(cw // LANE):
                    o_ref[(c0 // LANE) + t] = res[:, t * LANE:(t + 1) * LANE].astype(o_ref.dtype)
            else:
                o_ref[:, c0:c0 + cw] = res.astype(o_ref.dtype)


def _proj_in(x2d, mod, norm_w, w_bf, segs, rows_per_mod, mod_base, w_t=None):
    m_rows, d = x2d.shape
    tm = ROW_TILE
    out_shapes, out_specs = [], []
    for (_, width, layout, dt) in segs:
        if layout == TILED:
            out_shapes.append(jax.ShapeDtypeStruct((width // LANE, m_rows, LANE), dt))
            out_specs.append(pl.BlockSpec((width // LANE, tm, LANE), lambda i: (0, i, 0)))
        elif layout == CHAN:
            out_shapes.append(jax.ShapeDtypeStruct((width, m_rows), dt))
            out_specs.append(pl.BlockSpec((width, tm), lambda i: (0, i)))
        else:
            out_shapes.append(jax.ShapeDtypeStruct((m_rows, width), dt))
            out_specs.append(pl.BlockSpec((tm, width), lambda i: (i, 0)))
    has_wt = w_t is not None
    kern = functools.partial(_proj_in_kernel, segs=tuple((o, w, t) for (o, w, t, _) in segs), has_wt=has_wt)
    weights = [w_bf, w_t] if has_wt else [w_bf]
    return pl.pallas_call(
        kern,
        grid=(m_rows // tm,),
        in_specs=[pl.BlockSpec((tm, d), lambda i: (i, 0)),
                  pl.BlockSpec((1, 1, 3 * d), lambda i: (mod_base + (i * tm) // rows_per_mod, 0, 0)),
                  pl.BlockSpec((1, d), lambda i: (0, 0))]
                 + [pl.BlockSpec(w.shape, lambda i: (0, 0), pipeline_mode=pl.Buffered(1)) for w in weights],
        out_specs=out_specs,
        out_shape=out_shapes,
        compiler_params=_params("arbitrary"),
        name="proj_in",
    )(x2d, mod, norm_w, *weights)


def _ssd_kernel(*refs, has_init, want_final):
    it = iter(refs)
    xm = [next(it), None]
    xp = [next(it), None]
    xn = [next(it), None]
    xm[1], xp[1], xn[1] = next(it), next(it), next(it)
    dtr = [next(it), next(it)]
    cw_ref, cb_ref, dtb_ref, alog_ref, dskip_ref, exp_ref = (next(it) for _ in range(6))
    init_ref = next(it) if has_init else None
    y_refs = [next(it), next(it)]
    fin_ref = next(it) if want_final else None
    s_ref = next(it)

    c = pl.program_id(1)
    nc = pl.num_programs(1)
    q = xm[0].shape[1]
    dm = N_HEADS * HD
    gw = dm // N_GROUPS
    hpg = N_HEADS // N_GROUPS

    @pl.when(c == 0)
    def _():
        if has_init:
            for d in range(2):
                for t in range(dm // LANE):
                    s_ref[d, :, t * LANE:(t + 1) * LANE] = init_ref[0, d, t * LANE:(t + 1) * LANE, :].T
        else:
            s_ref[...] = jnp.zeros(s_ref.shape, F32)

    row = lax.broadcasted_iota(jnp.int32, (q, q), 0)
    col = lax.broadcasted_iota(jnp.int32, (q, q), 1)
    lane = lax.broadcasted_iota(jnp.int32, (q, LANE), 1)
    left = lane < HD
    tri = [(row >= col), (row <= col)]

    for d in range(2):
        chunk = c if d == 0 else nc - 1 - c
        x = xm[d][0].astype(F32)
        nchan = x.shape[1]
        r0 = lax.broadcasted_iota(jnp.int32, (q, nchan), 0)
        keep_prev = (chunk > 0).astype(F32)
        keep_next = (chunk < nc - 1).astype(F32)
        halo = xp[d].shape[1]
        prev_row = xp[d][0].astype(F32)[halo - 1:halo, :] * keep_prev
        next_row = xn[d][0].astype(F32)[0:1, :] * keep_next
        x_prev = jnp.where(r0 == 0, prev_row, pltpu.roll(x, 1, 0))
        x_next = jnp.where(r0 == q - 1, next_row, pltpu.roll(x, q - 1, 0))
        cw = cw_ref[...]
        xs = _silu(cb_ref[...] + x_prev * cw[0:1] + x * cw[1:2] + x_next * cw[2:3])
        xa = xs[:, 0:dm]
        bm = xs[:, dm:dm + N_GROUPS * N_STATE]
        cm = xs[:, dm + N_GROUPS * N_STATE:dm + 2 * N_GROUPS * N_STATE]

        dt = _softplus(dtr[d][0] + dtb_ref[...])
        adt = dt * (-jnp.exp(alog_ref[...]))
        tmat = jnp.where(tri[d], 1.0, 0.0).astype(BF16)
        cs = _dot_rhs_parts(tmat, adt, 3)
        cs_t = cs.T
        dt_t = dt.T
        edge = cs[q - 1:q, :] if d == 0 else cs[0:1, :]
        ex = exp_ref[d]
        e_cs = _dot(_bf(jnp.exp(cs)), ex)
        w_st = _dot(_bf(jnp.exp(edge - cs) * dt), ex)
        xw = xa * w_st

        y_parts = []
        s_prev = s_ref[d]
        new_state = []
        for g in range(N_GROUPS):
            bg = bm[:, g * N_STATE:(g + 1) * N_STATE]
            cg = cm[:, g * N_STATE:(g + 1) * N_STATE]
            gmat = _dot_nt(_bf(cg), _bf(bg))
            y_off = _dot(_bf(cg), _bf(s_prev[:, g * gw:(g + 1) * gw]))
            new_state.append(_dot(_bf(bg.T), _bf(xw[:, g * gw:(g + 1) * gw])))
            for pr in range(hpg // 2):
                mh = []
                for j in range(2):
                    k = d * N_HEADS + g * hpg + 2 * pr + j
                    diff = cs[:, k:k + 1] - cs_t[k:k + 1, :]
                    lm = jnp.exp(jnp.where(tri[d], diff, NEG_INF))
                    mh.append(_bf(gmat * lm * dt_t[k:k + 1, :]))
                c0 = g * gw + pr * LANE
                xpair = xa[:, c0:c0 + LANE]
                rhs = jnp.concatenate([_bf(jnp.where(left, xpair, 0.0)), _bf(jnp.where(left, 0.0, xpair))], axis=0)
                y_d = _dot(jnp.concatenate(mh, axis=1), rhs)
                y_parts.append(y_d + y_off[:, pr * LANE:(pr + 1) * LANE] * e_cs[:, c0:c0 + LANE])
        y = jnp.concatenate(y_parts, axis=1) + xa * dskip_ref[d:d + 1, :]
        y_refs[d][0] = y.astype(y_refs[d].dtype)
        e_edge = e_cs[q - 1:q, :] if d == 0 else e_cs[0:1, :]
        s_ref[d] = s_prev * e_edge + jnp.concatenate(new_state, axis=1)

    if want_final:
        @pl.when(c == nc - 1)
        def _():
            for d in range(2):
                for t in range(dm // LANE):
                    fin_ref[0, d, t * LANE:(t + 1) * LANE, :] = s_ref[d, :, t * LANE:(t + 1) * LANE].T


def _ssd(xbc, dt_raw, conv_w, conv_b, dt_bias, a_log, d_skip, init, want_final):
    b, l, nchan = xbc.shape
    dm = N_HEADS * HD
    q = CHUNK
    nc = l // q
    halo = SUBLANE * (4 // xbc.dtype.itemsize)
    sub_per_chunk = q // halo
    nsub = l // halo

    def fwd(c):
        return c

    def bwd(c):
        return nc - 1 - c

    def main(cf):
        return pl.BlockSpec((1, q, nchan), lambda bi, c: (bi, cf(c), 0))

    def prev(cf):
        return pl.BlockSpec((1, halo, nchan), lambda bi, c: (bi, jnp.maximum(cf(c) * sub_per_chunk - 1, 0), 0))

    def nxt(cf):
        return pl.BlockSpec((1, halo, nchan),
                            lambda bi, c: (bi, jnp.minimum((cf(c) + 1) * sub_per_chunk, nsub - 1), 0))

    def full(a):
        nd = a.ndim
        return pl.BlockSpec(a.shape, lambda bi, c: (0,) * nd)

    ex = np.zeros((2, LANE, dm), np.float32)
    for d in range(2):
        for h in range(N_HEADS):
            ex[d, d * N_HEADS + h, h * HD:(h + 1) * HD] = 1.0
    ex = jnp.asarray(ex, BF16)
    pad = LANE - 2 * N_HEADS
    dtb = jnp.pad(dt_bias.reshape(1, 2 * N_HEADS), ((0, 0), (0, pad)))
    alog = jnp.pad(a_log.reshape(1, 2 * N_HEADS), ((0, 0), (0, pad)))
    dsk = jnp.repeat(d_skip, HD, axis=1)
    cb = conv_b.reshape(1, nchan)

    args = [xbc, xbc, xbc, xbc, xbc, xbc, dt_raw, dt_raw, conv_w, cb, dtb, alog, dsk, ex]
    in_specs = [main(fwd), prev(fwd), nxt(fwd), main(bwd), prev(bwd), nxt(bwd),
                pl.BlockSpec((1, q, LANE), lambda bi, c: (bi, c, 0)),
                pl.BlockSpec((1, q, LANE), lambda bi, c: (bi, nc - 1 - c, 0)),
                full(conv_w), full(cb), full(dtb), full(alog), full(dsk), full(ex)]
    has_init = init is not None
    if has_init:
        args.append(init)
        in_specs.append(pl.BlockSpec((1, 2, dm, N_STATE), lambda bi, c: (bi, 0, 0, 0)))
    out_shapes = [jax.ShapeDtypeStruct((b, l, dm), BF16), jax.ShapeDtypeStruct((b, l, dm), BF16)]
    out_specs = [pl.BlockSpec((1, q, dm), lambda bi, c: (bi, c, 0)),
                 pl.BlockSpec((1, q, dm), lambda bi, c: (bi, nc - 1 - c, 0))]
    if want_final:
        out_shapes.append(jax.ShapeDtypeStruct((b, 2, dm, N_STATE), F32))
        out_specs.append(pl.BlockSpec((1, 2, dm, N_STATE), lambda bi, c: (bi, 0, 0, 0)))
    return pl.pallas_call(
        functools.partial(_ssd_kernel, has_init=has_init, want_final=want_final),
        grid=(b, nc),
        in_specs=in_specs,
        out_specs=out_specs,
        out_shape=out_shapes,
        scratch_shapes=[pltpu.VMEM((2, N_STATE, dm), F32)],
        compiler_params=_params("arbitrary", "arbitrary"),
        name="ssd_scan",
    )(*args)


def _hyena_tables(l):
    pos = np.abs(np.arange(2 * l, dtype=np.float64) - l)
    t = pos / (l - 1)
    w = 2.0 * math.pi * pos / l
    f = np.linspace(1e-4, HY_BANDS - 1, HY_BANDS)
    feats = np.zeros((2 * l, LANE), np.float64)
    feats[:, 0] = t
    feats[:, 1:1 + HY_BANDS] = np.cos(f[None] * w[:, None])
    feats[:, 1 + HY_BANDS:1 + 2 * HY_BANDS] = -np.sin(f[None] * w[:, None])
    return jnp.asarray(feats, F32)


def _filter_kernel(f_ref, w1_ref, b1_ref, w2_ref, b2_ref, w3_ref, fr_ref, ad_ref, o_ref):
    feats = f_ref[...]
    fr = fr_ref[...]
    h1 = jnp.sin(fr * (_dot3(feats, w1_ref[...]) + b1_ref[...]))
    h2 = jnp.sin(fr * (_dot3(h1, w2_ref[...]) + b2_ref[...]))
    filt = _dot3(h2, w3_ref[...])
    o_ref[...] = filt * jnp.exp(-feats[:, 0:1] * ad_ref[...])


def _hyena_filter_linear(l, w1, b1, w2, b2, w3, freq):
    db = w3.shape[1] // 2
    tr = min(512, l)
    nbk = l // tr
    feats = _hyena_tables(l)
    w1p = jnp.pad(w1, ((0, LANE - HY_EMB), (0, 0)))
    deltas = np.linspace(math.log(HY_TARGET) / HY_DECAY_PCT_HI, math.log(HY_TARGET) / HY_DECAY_PCT_LO, db)
    absd = jnp.asarray(np.abs(deltas)[None], F32)

    def full(a):
        return pl.BlockSpec(a.shape, lambda i: (0, 0))

    b1r, b2r, frr = b1.reshape(1, -1), b2.reshape(1, -1), freq.reshape(1, -1)
    return pl.pallas_call(
        _filter_kernel,
        grid=(2 * nbk,),
        in_specs=[pl.BlockSpec((tr, LANE), lambda i: (i, 0)), full(w1p), full(b1r), full(w2), full(b2r),
                  pl.BlockSpec((HY_HID, db), lambda i: (0, jnp.where(i < nbk, 1, 0))),
                  full(frr), full(absd)],
        out_specs=pl.BlockSpec((tr, db), lambda i: (i, 0)),
        out_shape=jax.ShapeDtypeStruct((2 * l, db), F32),
        compiler_params=_params("arbitrary"),
        name="hyena_filter",
    )(feats, w1p, b1r, w2, b2r, w3, frr, absd)


def _dft_tables(p):
    n = 2 * p
    f = np.arange(p, dtype=np.float64)[:, None] + 0.5
    e = np.arange(n, dtype=np.float64)[None]
    ang = 2.0 * math.pi * f * e / n
    fwd = np.concatenate([np.cos(ang), -np.sin(ang)], axis=0)
    fa = fwd[:, :p]
    fbn = -fwd[:, p:]
    fbn[:, 0] = 0.0
    angt = ang[:, :p].T
    inv = np.concatenate([np.cos(angt), -np.sin(angt)], axis=1) * (2.0 / n)

    return tuple(jnp.asarray(m, F32).astype(BF16) for m in (fa, fbn, inv))


def _spectra_kernel(k1_ref, k0_ref, fa_ref, fb_ref, o_ref):
    res = _dot(fa_ref[...], _bf(k1_ref[...])) + _dot(fb_ref[...], _bf(k0_ref[...]))
    for t in range(o_ref.shape[0]):
        o_ref[t, 0] = res[:, t * LANE:(t + 1) * LANE]


def _hyena_spectra(klin, p, fa, fbn):
    two_l, db = klin.shape
    nseg = two_l // p - 1
    nct = db // LANE
    return pl.pallas_call(
        _spectra_kernel,
        grid=(nseg,),
        in_specs=[pl.BlockSpec((p, db), lambda s: (s + 1, 0)),
                  pl.BlockSpec((p, db), lambda s: (s, 0)),
                  pl.BlockSpec(fa.shape, lambda s: (0, 0)),
                  pl.BlockSpec(fbn.shape, lambda s: (0, 0))],
        out_specs=pl.BlockSpec((nct, 1, 2 * p, LANE), lambda s: (0, s, 0, 0)),
        out_shape=jax.ShapeDtypeStruct((nct, nseg, 2 * p, LANE), F32),
        compiler_params=_params("arbitrary"),
        name="hyena_spectra",
    )(klin, klin, fa, fbn)


def _hyena_kernel(x0_ref, x1_ref, v_ref, g_ref, gs_ref, w0_ref, w1_ref, w2_ref, c0_ref, c1_ref, c2_ref,
                  hb_ref, fa_ref, iv_ref, o_ref, u_scr, y_scr, w_scr, gate_scr, *, p):
    bt, l = x0_ref.shape[1], x0_ref.shape[2]
    nb = l // p
    row = lax.broadcasted_iota(jnp.int32, (l, LANE), 0)

    def conv3(x, w_ref, c_ref):
        w = w_ref[...]
        xm1 = jnp.where(row == 0, 0.0, pltpu.roll(x, 1, 0))
        xp1 = jnp.where(row == l - 1, 0.0, pltpu.roll(x, l - 1, 0))
        return c_ref[...] + xm1 * w[0:1] + x * w[1:2] + xp1 * w[2:3]

    for bb in range(bt):
        w_scr[bb] = (conv3(v_ref[0, bb].astype(F32), w2_ref, c2_ref)
                     * conv3(x1_ref[0, bb].astype(F32), w1_ref, c1_ref))
        gate_scr[bb] = conv3(x0_ref[0, bb].astype(F32), w0_ref, c0_ref) * _silu(g_ref[0, bb].astype(F32))
    fa = fa_ref[...]
    for j in range(nb):
        rhs = jnp.concatenate([_bf(w_scr[bb, j * p:(j + 1) * p, :]) for bb in range(bt)], axis=1)
        u_scr[j] = _dot(fa, rhs)
    rt_rows = 64
    for i in range(nb):
        def body(rt, carry):
            r0 = pl.multiple_of(rt * rt_rows, rt_rows)
            for bb in range(bt):
                ls = slice(bb * LANE, (bb + 1) * LANE)
                acc_re = jnp.zeros((rt_rows, LANE), F32)
                acc_im = jnp.zeros((rt_rows, LANE), F32)
                for j in range(nb):
                    s = i - j + nb - 1
                    gre = gs_ref[0, s, pl.ds(r0, rt_rows), :]
                    gim = gs_ref[0, s, pl.ds(p + r0, rt_rows), :]
                    ure = u_scr[j, pl.ds(r0, rt_rows), ls]
                    uim = u_scr[j, pl.ds(p + r0, rt_rows), ls]
                    acc_re = acc_re + (gre * ure - gim * uim)
                    acc_im = acc_im + (gre * uim + gim * ure)
                y_scr[pl.ds(r0, rt_rows), ls] = acc_re
                y_scr[pl.ds(p + r0, rt_rows), ls] = acc_im
            return carry
        lax.fori_loop(0, p // rt_rows, body, 0)
        conv = _dot(iv_ref[...], _bf(y_scr[...]))
        sl = slice(i * p, (i + 1) * p)
        for bb in range(bt):
            wi = w_scr[bb, sl, :]
            o_ref[0, bb, sl, :] = (gate_scr[bb, sl, :] * (conv[:, bb * LANE:(bb + 1) * LANE] + wi * hb_ref[...])
                                   ).astype(o_ref.dtype)


def _hyena(u_t, g_t, klin, conv_w, conv_b, hy_bias, b, l):
    nct = klin.shape[1] // LANE
    p = min(DFT_BLOCK, l)
    nb = l // p
    nseg = 2 * nb - 1
    bt = min(b, HY_BATCH if nb == 1 else HY_BATCH_LONG)
    fa, fbn, iv = _dft_tables(p)
    spectra = _hyena_spectra(klin, p, fa, fbn)
    u4 = u_t.reshape(3 * nct, b, l, LANE)
    g4 = g_t.reshape(nct, b, l, LANE)
    cbr = conv_b.reshape(1, -1)
    hbr = hy_bias.reshape(1, -1)

    def act(k):
        return pl.BlockSpec((1, bt, l, LANE), lambda ct, bi: (k * nct + ct, bi, 0, 0))

    def wcol(rows, k):
        return pl.BlockSpec((rows, LANE), lambda ct, bi: (0, k * nct + ct))

    def full(a):
        return pl.BlockSpec(a.shape, lambda ct, bi: (0, 0), pipeline_mode=pl.Buffered(1))

    out = pl.pallas_call(
        functools.partial(_hyena_kernel, p=p),
        grid=(nct, b // bt),
        in_specs=[act(0), act(1), act(2),
                  pl.BlockSpec((1, bt, l, LANE), lambda ct, bi: (ct, bi, 0, 0)),
                  pl.BlockSpec((1, nseg, 2 * p, LANE), lambda ct, bi: (ct, 0, 0, 0), pipeline_mode=pl.Buffered(1)),
                  wcol(3, 0), wcol(3, 1), wcol(3, 2), wcol(1, 0), wcol(1, 1), wcol(1, 2),
                  pl.BlockSpec((1, LANE), lambda ct, bi: (0, ct)),
                  full(fa), full(iv)],
        out_specs=pl.BlockSpec((1, bt, l, LANE), lambda ct, bi: (ct, bi, 0, 0)),
        out_shape=jax.ShapeDtypeStruct((nct, b, l, LANE), BF16),
        scratch_shapes=[pltpu.VMEM((nb, 2 * p, bt * LANE), F32), pltpu.VMEM((2 * p, bt * LANE), F32),
                        pltpu.VMEM((bt, l, LANE), F32), pltpu.VMEM((bt, l, LANE), F32)],
        compiler_params=_params("arbitrary", "arbitrary"),
        name="hyena_conv",
    )(u4, u4, u4, g4, spectra, conv_w, conv_w, conv_w, cbr, cbr, cbr, hbr, fa, iv)
    return out.reshape(nct, b * l, LANE)


def _proj_out0_kernel(x_ref, yf_ref, yb_ref, z_ref, yh_ref, mod_ref, naw_ref, w_ref, o_ref):
    d = x_ref.shape[1]
    ya = _rms((yf_ref[...].astype(F32) + yb_ref[...].astype(F32)) * _silu(z_ref[...].astype(F32)), naw_ref[...])
    yh = jnp.concatenate([yh_ref[t] for t in range(yh_ref.shape[0])], axis=1)
    da = ya.shape[1]
    acc = _dot(_bf(ya), w_ref[0:da, :]) + _dot(_bf(yh), w_ref[da:, :])
    gate = mod_ref[0][:, 2 * d:3 * d]
    o_ref[...] = x_ref[...] + gate * acc


def _proj_out0(x2d, y_f, y_b, z, yh_t, mod, norm_a_w, w_bf, rows_per_mod, mod_base):
    m_rows, d = x2d.shape
    tm = ROW_TILE
    da = y_f.shape[1]
    nt = yh_t.shape[0]

    def rowspec(wd):
        return pl.BlockSpec((tm, wd), lambda i: (i, 0))

    return pl.pallas_call(
        _proj_out0_kernel,
        grid=(m_rows // tm,),
        in_specs=[rowspec(d), rowspec(da), rowspec(da), rowspec(da),
                  pl.BlockSpec((nt, tm, LANE), lambda i: (0, i, 0)),
                  pl.BlockSpec((1, 1, 3 * d), lambda i: (mod_base + (i * tm) // rows_per_mod, 0, 0)),
                  pl.BlockSpec((1, da), lambda i: (0, 0)),
                  pl.BlockSpec(w_bf.shape, lambda i: (0, 0), pipeline_mode=pl.Buffered(1))],
        out_specs=rowspec(d),
        out_shape=jax.ShapeDtypeStruct((m_rows, d), F32),
        compiler_params=_params("arbitrary"),
        name="proj_out0",
    )(x2d, y_f, y_b, z, yh_t, mod, norm_a_w, w_bf)


def _proj_out1_kernel(x_ref, o_ref_in, g_ref, mod_ref, fw_ref, w_ref, y_ref):
    d = x_ref.shape[1]
    o = jnp.concatenate([o_ref_in[t] for t in range(o_ref_in.shape[0])], axis=1).astype(F32)
    a = o * _silu(g_ref[...].astype(F32))
    acc = _dot(_bf(a), w_ref[...])
    gate = mod_ref[0][:, 2 * d:3 * d]
    y_ref[...] = _rms(x_ref[...] + gate * acc, fw_ref[...])


def _proj_out1(x2d, o_t, g, mod, final_w, w_bf, rows_per_mod, mod_base):
    m_rows, d = x2d.shape
    tm = ROW_TILE
    nt = o_t.shape[0]
    return pl.pallas_call(
        _proj_out1_kernel,
        grid=(m_rows // tm,),
        in_specs=[pl.BlockSpec((tm, d), lambda i: (i, 0)),
                  pl.BlockSpec((nt, tm, LANE), lambda i: (0, i, 0)),
                  pl.BlockSpec((tm, g.shape[1]), lambda i: (i, 0)),
                  pl.BlockSpec((1, 1, 3 * d), lambda i: (mod_base + (i * tm) // rows_per_mod, 0, 0)),
                  pl.BlockSpec((1, d), lambda i: (0, 0)),
                  pl.BlockSpec(w_bf.shape, lambda i: (0, 0), pipeline_mode=pl.Buffered(1))],
        out_specs=pl.BlockSpec((tm, d), lambda i: (i, 0)),
        out_shape=jax.ShapeDtypeStruct((m_rows, d), F32),
        compiler_params=_params("arbitrary"),
        name="proj_out1",
    )(x2d, o_t, g, mod, final_w, w_bf)


def _ctx_layer_kernel(x_ref, mod_ref, nw_ref, wi_ref, wo_ref, fw_ref, y_ref, ck_ref, cv_ref):
    l, d = x_ref.shape
    m = mod_ref[0]
    x = x_ref[...]
    hb = _bf(_rms(x, nw_ref[...]) * (1.0 + m[:, d:2 * d]) + m[:, 0:d])
    qb = _bf(_dot(hb, wi_ref[:, 0:d]) * (HD ** -0.5))
    k = _dot(hb, wi_ref[:, d:2 * d])
    v = _dot(hb, wi_ref[:, 2 * d:3 * d])
    g = _dot(hb, wi_ref[:, 3 * d:4 * d])
    nh = d // HD
    scores, vbs = [], []
    for h in range(nh):
        sl = slice(h * HD, (h + 1) * HD)
        kh = k[:, sl]
        vh = v[:, sl]
        ck_ref[0, 0, h] = kh
        cv_ref[0, 0, h] = vh
        vbs.append(_bf(vh))
        scores.append(_dot_nt(qb[:, sl], _bf(kh)))
    s_all = jnp.concatenate(scores, axis=0)
    pexp = jnp.exp(s_all - jnp.max(s_all, axis=-1, keepdims=True))
    den = jnp.sum(pexp, axis=-1, keepdims=True)
    pb = _bf(pexp)
    outs = [_dot(pb[h * l:(h + 1) * l], vbs[h]) / den[h * l:(h + 1) * l] for h in range(nh)]
    a = jnp.concatenate(outs, axis=1) * _silu(g)
    acc = _dot(_bf(a), wo_ref[...])
    y_ref[...] = _rms(x + m[:, 2 * d:3 * d] * acc, fw_ref[...])


def _ctx_layer(x2d, b, l, mod, norm_w, w_in_bf, w_out_bf, final_w):
    d = x2d.shape[1]
    nh = d // HD
    cache_spec = pl.BlockSpec((1, 1, nh, l, HD), lambda bi: (bi, 0, 0, 0, 0))
    cache_shape = jax.ShapeDtypeStruct((b, 1, nh, l, HD), F32)

    def const(a):
        nd = a.ndim
        return pl.BlockSpec(a.shape, lambda bi: (0,) * nd, pipeline_mode=pl.Buffered(1))

    return pl.pallas_call(
        _ctx_layer_kernel,
        grid=(b,),
        in_specs=[pl.BlockSpec((l, d), lambda bi: (bi, 0)),
                  pl.BlockSpec((1, 1, 3 * d), lambda bi: (0, 0, 0)),
                  const(norm_w), const(w_in_bf), const(w_out_bf), const(final_w)],
        out_specs=[pl.BlockSpec((l, d), lambda bi: (bi, 0)), cache_spec, cache_spec],
        out_shape=[jax.ShapeDtypeStruct((b * l, d), F32), cache_shape, cache_shape],
        compiler_params=_params("arbitrary"),
        name="ctx_layer",
    )(x2d, mod, norm_w, w_in_bf, w_out_bf, final_w)


def _na_bias_kernel(rpb_ref, o_ref):
    h = pl.program_id(0)
    ndr = 2 * WIN_H - 1
    ndc = 2 * WIN_W - 1
    ck = lax.broadcasted_iota(jnp.int32, (GRID_W, LANE), 0)
    lane = lax.broadcasted_iota(jnp.int32, (GRID_W, LANE), 1)
    cq = lane & (GRID_W - 1)
    first = lane < GRID_W
    dc = jnp.clip(ck - cq + (WIN_W - 1), 0, ndc - 1)
    col0 = jnp.clip(cq - WIN_W // 2, 0, GRID_W - WIN_W)
    col_in = (ck >= col0) & (ck < col0 + WIN_W)
    dc_is = [dc == e for e in range(ndc)]
    tables = []
    for dr in range(ndr):
        t = jnp.full((GRID_W, LANE), NEG_INF, F32)
        for e in range(ndc):
            t = jnp.where(dc_is[e] & col_in, rpb_ref[(h * ndr + dr) * ndc + e], t)
        tables.append(t)
    for ip in range(NA_BAND):
        for ap in range(NA_QTILE // 2):
            x = ip - 2 * ap + (WIN_H - 1) - NA_QROWS // 2
            o_ref[0, ip * GRID_W:(ip + 1) * GRID_W, ap * LANE:(ap + 1) * LANE] = jnp.where(
                first, tables[x], tables[x - 1])


def _na_bias_tables(rpb):
    nh = rpb.shape[0]
    shape = (NA_BAND * GRID_W, NA_QTILE * GRID_W)
    return pl.pallas_call(
        _na_bias_kernel,
        grid=(nh,),
        in_specs=[pl.BlockSpec(memory_space=pltpu.SMEM)],
        out_specs=pl.BlockSpec((1,) + shape, lambda h: (h, 0, 0)),
        out_shape=jax.ShapeDtypeStruct((nh,) + shape, F32),
        compiler_params=_params("arbitrary"),
        name="na_bias",
    )(rpb.reshape(-1))


def _na_kernel(q_ref, kp_ref, kc_ref, kn_ref, vp_ref, vc_ref, vn_ref, ck_ref, cv_ref, bias_ref, o_ref, mask_scr,
               s_scr, *, n_rows):
    rb = pl.program_id(1)
    qrows = NA_QROWS
    nq = qrows * GRID_W
    half = (qrows // 2) * GRID_W
    qt = NA_QTILE * GRID_W
    nband = NA_BAND * GRID_W
    q = q_ref[0, 0] * (HD ** -0.5)
    kloc = jnp.concatenate([kp_ref[0, 0][nq - half:nq], kc_ref[0, 0], kn_ref[0, 0][0:half]], axis=0)
    vloc = jnp.concatenate([vp_ref[:, nq - half:nq], vc_ref[...], vn_ref[:, 0:half]], axis=1)

    @pl.when(pl.program_id(2) == 0)
    def _():
        for t in range(qrows // NA_QTILE):
            i = t * NA_QTILE + lax.broadcasted_iota(jnp.int32, (nband, qt), 0) // GRID_W
            a = t * NA_QTILE + lax.broadcasted_iota(jnp.int32, (nband, qt), 1) // GRID_W
            r = rb * qrows + a
            kr = rb * qrows - qrows // 2 + i
            rs = jnp.clip(r - WIN_H // 2, 0, n_rows - WIN_H)
            mask_scr[t] = jnp.where((kr >= rs) & (kr < rs + WIN_H), 0.0, NEG_INF)

    rows = []
    for j in range(LANE // HD):
        sl = slice(j * HD, (j + 1) * HD)
        ckb = _bf(ck_ref[0, j])
        cvt = cv_ref[0, 0, sl, :]
        tiles = []
        nctx = ckb.shape[0]
        for t in range(qrows // NA_QTILE):
            k0 = t * NA_QTILE * GRID_W
            qh = q[t * qt:(t + 1) * qt, sl]
            m = jnp.full((1, qt), NEG_INF, F32)
            for c0 in range(0, nband + nctx, NA_KCHUNK):
                if c0 < nband:
                    rs_ = slice(c0, c0 + NA_KCHUNK)
                    s = _dot_nt(kloc[k0 + c0:k0 + c0 + NA_KCHUNK, sl], qh) + bias_ref[j, rs_, :] + mask_scr[t, rs_, :]
                else:
                    s = _dot_nt(ckb[c0 - nband:c0 - nband + NA_KCHUNK], qh)
                s_scr[c0:c0 + NA_KCHUNK, :] = s
                m = jnp.maximum(m, jnp.max(s, axis=0, keepdims=True))
            den = jnp.zeros((1, qt), F32)
            o = jnp.zeros((HD, qt), F32)
            for c0 in range(0, nband + nctx, NA_KCHUNK):
                pexp = jnp.exp(s_scr[c0:c0 + NA_KCHUNK, :] - m)
                den = den + jnp.sum(pexp, axis=0, keepdims=True)
                if c0 < nband:
                    vt = vloc[sl, k0 + c0:k0 + c0 + NA_KCHUNK]
                else:
                    vt = cvt[:, c0 - nband:c0 - nband + NA_KCHUNK]
                o = o + _dot(vt, _bf(pexp))
            tiles.append(o / den)
        rows.append(jnp.concatenate(tiles, axis=1))
    o_ref[0, 0] = jnp.concatenate(rows, axis=0).T.astype(o_ref.dtype)


def _na_attn(q_t, k_t, v_c, cache_k, cache_vt, bias, b, l):
    npair = q_t.shape[0]
    hpp = LANE // HD
    n_rows = l // GRID_W
    nrb = n_rows // NA_QROWS
    nq = NA_QROWS * GRID_W
    lc = cache_k.shape[2]
    q4, k4 = (a.reshape(npair, b, l, LANE) for a in (q_t, k_t))

    def prev_blk(rb):
        return jnp.maximum(rb - 1, 0)

    def next_blk(rb):
        return jnp.minimum(rb + 1, nrb - 1)

    def same_blk(rb):
        return rb

    def tok(f):
        return pl.BlockSpec((1, 1, nq, LANE), lambda bi, rb, hp: (hp, bi, f(rb), 0))

    def chan(f):
        return pl.BlockSpec((LANE, nq), lambda bi, rb, hp: (hp, bi * nrb + f(rb)))

    out = pl.pallas_call(
        functools.partial(_na_kernel, n_rows=n_rows),
        grid=(b, nrb, npair),
        in_specs=[tok(same_blk), tok(prev_blk), tok(same_blk), tok(next_blk),
                  chan(prev_blk), chan(same_blk), chan(next_blk),
                  pl.BlockSpec((1, hpp, lc, HD), lambda bi, rb, hp: (bi, hp, 0, 0)),
                  pl.BlockSpec((1, 1, LANE, lc), lambda bi, rb, hp: (bi, hp, 0, 0)),
                  pl.BlockSpec((hpp,) + bias.shape[1:], lambda bi, rb, hp: (hp, 0, 0))],
        out_specs=tok(same_blk),
        out_shape=jax.ShapeDtypeStruct((npair, b, l, LANE), BF16),
        scratch_shapes=[pltpu.VMEM((NA_QROWS // NA_QTILE,) + bias.shape[1:], F32),
                        pltpu.VMEM((bias.shape[1] + lc, bias.shape[2]), F32)],
        compiler_params=_params("arbitrary", "arbitrary", "arbitrary"),
        name="na_attn",
    )(q4, k4, k4, k4, v_c, v_c, v_c, cache_k, cache_vt, bias)
    return out.reshape(npair, b * l, LANE)


def _layer0(x2d, b, l, mod, rows_per_mod, mod_base, norm_w, w_in_bf, w_out_bf, p, init, want_final, klin):
    dm = N_HEADS * HD
    d_xbc = dm + 2 * N_GROUPS * N_STATE
    d_b = klin.shape[1]
    segs = ((0, dm, ROWS, BF16), (dm, d_xbc, ROWS, BF16), (dm + d_xbc, 3 * d_b, TILED, BF16),
            (dm + d_xbc + 3 * d_b, d_b, TILED, BF16), (dm + d_xbc + 4 * d_b, LANE, ROWS, F32))
    z, xbc, u_t, g_t, dt_raw = _proj_in(x2d, mod, norm_w, w_in_bf, segs, rows_per_mod, mod_base)
    res = _ssd(xbc.reshape(b, l, d_xbc), dt_raw.reshape(b, l, LANE), p["conv_a_w"], p["conv_a_b"],
               p["dt_bias"], p["a_log"], p["d_skip"], init, want_final)
    y_f, y_b = res[0].reshape(b * l, dm), res[1].reshape(b * l, dm)
    yh_t = _hyena(u_t, g_t, klin, p["conv_b_w"], p["conv_b_b"], p["hy_bias"], b, l)
    x_new = _proj_out0(x2d, y_f, y_b, z, yh_t, mod, p["norm_a_w"], w_out_bf, rows_per_mod, mod_base)
    return x_new, (res[2] if want_final else None)


def kernel(x_prompt, x_sample, state_ssd, cache_k, cache_v, c, c_ctx, norm_w, w_ada, b_ada, w_in_e, w_out_e, conv_a_w, conv_a_b, dt_bias, a_log, d_skip, norm_a_w, conv_b_w, conv_b_b, hf_w1, hf_b1, hf_w2, hf_b2, hf_w3, hf_freq, hy_bias, w_in_o, w_out_o, rpb, final_norm_w):
    bp, lp, d = x_prompt.shape
    bs, ls, _ = x_sample.shape
    dm = N_HEADS * HD
    d_xbc = dm + 2 * N_GROUPS * N_STATE
    n_dt = 2 * N_HEADS

    cvecs = jnp.concatenate([c_ctx[None], c, jnp.zeros((SUBLANE - 1 - bs, d), F32)], axis=0)
    mods = _ada_mods(cvecs, w_ada, b_ada)

    xp = x_prompt.reshape(bp * lp, d)
    xs = x_sample.reshape(bs * ls, d)

    wi = w_in_e[0]
    o_dt = dm + d_xbc
    w_in0 = jnp.concatenate([wi[:, :o_dt], wi[:, o_dt + n_dt:], wi[:, o_dt:o_dt + n_dt],
                             jnp.zeros((d, LANE - n_dt), F32)], axis=1).astype(BF16)
    w_out0 = w_out_e[0].astype(BF16)
    p0 = dict(conv_a_w=conv_a_w[0], conv_a_b=conv_a_b[0], dt_bias=dt_bias[0], a_log=a_log[0], d_skip=d_skip[0],
              norm_a_w=norm_a_w[0].reshape(1, -1), conv_b_w=conv_b_w[0], conv_b_b=conv_b_b[0], hy_bias=hy_bias[0])
    mod0 = mods[0].reshape(SUBLANE, 1, 3 * d)
    nw0 = norm_w[0].reshape(1, d)
    hf = (hf_w1[0], hf_b1[0], hf_w2[0], hf_b2[0], hf_w3[0], hf_freq[0])
    klin_p = _hyena_filter_linear(lp, *hf)
    klin_s = _hyena_filter_linear(ls, *hf)
    xp, fin = _layer0(xp, bp, lp, mod0, bp * lp, 0, nw0, w_in0, w_out0, p0, None, True, klin_p)
    init_s = state_ssd[:, 0].reshape(bs, 2, dm, N_STATE)
    xs, _ = _layer0(xs, bs, ls, mod0, ls, 1, nw0, w_in0, w_out0, p0, init_s, False, klin_s)
    new_state_ssd = fin.reshape(bp, 1, 2, N_HEADS, HD, N_STATE)

    w_in1 = w_in_o[0].astype(BF16)
    w_out1 = w_out_o[0].astype(BF16)
    mod1 = mods[1].reshape(SUBLANE, 1, 3 * d)
    nw1 = norm_w[1].reshape(1, d)
    fw = final_norm_w.reshape(1, d)
    y_prompt, new_cache_k, new_cache_v = _ctx_layer(xp, bp, lp, mod1, nw1, w_in1, w_out1, fw)
    y_prompt = y_prompt.reshape(bp, lp, d)

    segs_s = ((0, d, TILED, BF16), (d, d, TILED, BF16), (0, d, CHAN, BF16), (3 * d, d, ROWS, BF16))
    wv_t = w_in_o[0][:, 2 * d:3 * d].T.astype(BF16)
    q_t, k_t, v_c, g = _proj_in(xs, mod1, nw1, w_in1, segs_s, ls, 1, w_t=wv_t)
    bias = _na_bias_tables(rpb[0])
    lc = cache_v.shape[3]
    cache_vt = jnp.swapaxes(cache_v[:, 0], 2, 3).reshape(bs, d // LANE, LANE, lc).astype(BF16)
    o_t = _na_attn(q_t, k_t, v_c, cache_k[:, 0], cache_vt, bias, bs, ls)
    y_sample = _proj_out1(xs, o_t, g, mod1, fw, w_out1, ls, 1).reshape(bs, ls, d)

    return (y_prompt, y_sample, new_state_ssd, new_cache_k, new_cache_v)
```

```python
import functools
import math

import jax
import jax.numpy as jnp
import numpy as np
from jax import lax
from jax.experimental import pallas as pl
from jax.experimental.pallas import tpu as pltpu

F32 = jnp.float32
BF16 = jnp.bfloat16

EPS = 1e-6
GRID_W = 64
WIN_H = 8
WIN_W = 16
HD = 64
N_HEADS = 16
N_STATE = 128
N_GROUPS = 2
CHUNK = 128
HY_EMB = 33
HY_BANDS = (HY_EMB - 1) // 2
HY_HID = 64
HY_TARGET = 1e-2
HY_DECAY_PCT_HI = 0.3
HY_DECAY_PCT_LO = 1.5

LANE = 128
SUBLANE = 8
VMEM_LIMIT = 56 * 1024 * 1024

ROW_TILE = 512
DFT_BLOCK = 512
HY_BATCH = 8
HY_BATCH_LONG = 2
NA_QROWS = 8
NA_QTILE = 4
NA_KCHUNK = 128
NA_BAND = NA_QTILE + WIN_H
NEG_INF = float("-inf")


def _bf(x):
    return x.astype(BF16)


def _dot(a, b):
    return jnp.dot(a, b, preferred_element_type=F32)


def _dot_nt(a, b):
    return lax.dot_general(a, b, (((1,), (1,)), ((), ())), preferred_element_type=F32)


def _split2(x):
    hi = _bf(x)
    lo = _bf(x - hi.astype(F32))
    return hi, lo


def _split3(x):
    hi = _bf(x)
    r = x - hi.astype(F32)
    mid = _bf(r)
    lo = _bf(r - mid.astype(F32))
    return hi, mid, lo


def _dot3(a, b):
    ah, al = _split2(a)
    bh, bl = _split2(b)
    return _dot(ah, bh) + (_dot(ah, bl) + _dot(al, bh))


def _dot_rhs_parts(a_exact, b, parts):
    pieces = _split3(b) if parts == 3 else _split2(b)
    acc = _dot(a_exact, pieces[0])
    for p in pieces[1:]:
        acc = acc + _dot(a_exact, p)
    return acc


def _silu(x):
    return x * jax.nn.sigmoid(x)


def _rms(x, g):
    ms = jnp.mean(x * x, axis=-1, keepdims=True)
    return x * lax.rsqrt(ms + EPS) * g


def _softplus(x):
    return jnp.maximum(x, 0.0) + jnp.log1p(jnp.exp(-jnp.abs(x)))


def _params(*sem):
    return pltpu.CompilerParams(dimension_semantics=sem, vmem_limit_bytes=VMEM_LIMIT)


def _mods_kernel(c_ref, w_ref, b_ref, o_ref):
    a = _silu(c_ref[...])
    o_ref[0] = _dot3(a, w_ref[0]) + b_ref[0]


def _ada_mods(cvecs, w_ada, b_ada):
    depth, d, n3 = w_ada.shape
    tn = n3 // 4
    return pl.pallas_call(
        _mods_kernel,
        grid=(depth, n3 // tn),
        in_specs=[pl.BlockSpec((SUBLANE, d), lambda l, j: (0, 0)),
                  pl.BlockSpec((1, d, tn), lambda l, j: (l, 0, j)),
                  pl.BlockSpec((1, 1, tn), lambda l, j: (l, 0, j))],
        out_specs=pl.BlockSpec((1, SUBLANE, tn), lambda l, j: (l, 0, j)),
        out_shape=jax.ShapeDtypeStruct((depth, SUBLANE, n3), F32),
        compiler_params=_params("arbitrary", "arbitrary"),
        name="ada_mods",
    )(cvecs, w_ada, b_ada.reshape(depth, 1, n3))


ROWS, TILED, CHAN = "rows", "tiled", "chan"


def _proj_in_kernel(x_ref, mod_ref, nw_ref, w_ref, *refs, segs, has_wt):
    wt_ref = refs[0] if has_wt else None
    out_refs = refs[1:] if has_wt else refs
    d = x_ref.shape[1]
    m = mod_ref[0]
    h = _rms(x_ref[...], nw_ref[...]) * (1.0 + m[:, d:2 * d]) + m[:, 0:d]
    hb = _bf(h)
    for (off, width, layout), o_ref in zip(segs, out_refs):
        if layout == CHAN:
            o_ref[...] = _dot_nt(wt_ref[off:off + width, :], hb).astype(o_ref.dtype)
            continue
        step = 4 * LANE
        for c0 in range(0, width, step):
            cw = min(step, width - c0)
            res = _dot(hb, w_ref[:, off + c0:off + c0 + cw])
            if layout == TILED:
                for t in range(cw // LANE):
                    o_ref[(c0 // LANE) + t] = res[:, t * LANE:(t + 1) * LANE].astype(o_ref.dtype)
            else:
                o_ref[:, c0:c0 + cw] = res.astype(o_ref.dtype)


def _proj_in(x2d, mod, norm_w, w_bf, segs, rows_per_mod, mod_base, w_t=None):
    m_rows, d = x2d.shape
    tm = ROW_TILE
    assert m_rows % tm == 0 and rows_per_mod % tm == 0
    out_shapes, out_specs = [], []
    for (_, width, layout, dt) in segs:
        if layout == TILED:
            out_shapes.append(jax.ShapeDtypeStruct((width // LANE, m_rows, LANE), dt))
            out_specs.append(pl.BlockSpec((width // LANE, tm, LANE), lambda i: (0, i, 0)))
        elif layout == CHAN:
            out_shapes.append(jax.ShapeDtypeStruct((width, m_rows), dt))
            out_specs.append(pl.BlockSpec((width, tm), lambda i: (0, i)))
        else:
            out_shapes.append(jax.ShapeDtypeStruct((m_rows, width), dt))
            out_specs.append(pl.BlockSpec((tm, width), lambda i: (i, 0)))
    has_wt = w_t is not None
    kern = functools.partial(_proj_in_kernel, segs=tuple((o, w, t) for (o, w, t, _) in segs), has_wt=has_wt)
    weights = [w_bf, w_t] if has_wt else [w_bf]
    return pl.pallas_call(
        kern,
        grid=(m_rows // tm,),
        in_specs=[pl.BlockSpec((tm, d), lambda i: (i, 0)),
                  pl.BlockSpec((1, 1, 3 * d), lambda i: (mod_base + (i * tm) // rows_per_mod, 0, 0)),
                  pl.BlockSpec((1, d), lambda i: (0, 0))]
                 + [pl.BlockSpec(w.shape, lambda i: (0, 0), pipeline_mode=pl.Buffered(1)) for w in weights],
        out_specs=out_specs,
        out_shape=out_shapes,
        compiler_params=_params("arbitrary"),
        name="proj_in",
    )(x2d, mod, norm_w, *weights)


def _proj_in0_kernel(x_ref, xp_ref, xn_ref, mod_ref, nw_ref, w_ref, cwa_ref, cba_ref, cwb_ref, cbb_ref,
                     z_ref, xs_ref, wv_ref, gate_ref, dt_ref, *, seq_len, cols):
    i = pl.program_id(0)
    tm, d = x_ref.shape
    o_z, o_xbc, o_u, o_g, o_dt = cols
    d_a, d_xbc, d_b = o_xbc - o_z, o_u - o_xbc, o_dt - o_g
    m = mod_ref[0]

    def modnorm(x):
        return _bf(_rms(x, nw_ref[...]) * (1.0 + m[:, d:2 * d]) + m[:, 0:d])

    hb = modnorm(x_ref[...])
    hh = modnorm(jnp.concatenate([xp_ref[...], xn_ref[...]], axis=0))
    keep_prev = ((i * tm) % seq_len != 0).astype(F32)
    keep_next = (((i + 1) * tm) % seq_len != 0).astype(F32)
    starts = list(range(0, tm, seq_len))
    ends = [min(s + seq_len, tm) - 1 for s in starts]

    def conv(off, width, cw_ref, cb_ref, coff):
        res = _dot(hb, w_ref[:, off:off + width])
        rh = _dot(hh, w_ref[:, off:off + width])
        sub = lax.broadcasted_iota(jnp.int32, (SUBLANE, width), 0)
        down = pltpu.roll(res, 1, 0)
        up = pltpu.roll(res, tm - 1, 0)
        dparts, uparts, pos = [], [], 0
        for s in starts:
            fill = rh[SUBLANE - 1:SUBLANE] * keep_prev if s == 0 else 0.0
            dparts += [down[pos:s], jnp.where(sub == 0, fill, down[s:s + SUBLANE])]
            pos = s + SUBLANE
        dparts.append(down[pos:tm])
        pos = 0
        for e in ends:
            fill = rh[SUBLANE:SUBLANE + 1] * keep_next if e == tm - 1 else 0.0
            uparts += [up[pos:e + 1 - SUBLANE], jnp.where(sub == SUBLANE - 1, fill, up[e + 1 - SUBLANE:e + 1])]
            pos = e + 1
        uparts.append(up[pos:tm])
        down = jnp.concatenate([p for p in dparts if p.shape[0]], axis=0)
        up = jnp.concatenate([p for p in uparts if p.shape[0]], axis=0)
        cw = cw_ref[:, coff:coff + width]
        return cb_ref[:, coff:coff + width] + down * cw[0:1] + res * cw[1:2] + up * cw[2:3]

    step = 4 * LANE
    for c0 in range(0, d_a, step):
        z_ref[:, c0:c0 + step] = _dot(hb, w_ref[:, o_z + c0:o_z + c0 + step]).astype(z_ref.dtype)
    for c0 in range(0, d_xbc, step):
        xs_ref[:, c0:c0 + step] = _silu(conv(o_xbc + c0, step, cwa_ref, cba_ref, c0)).astype(xs_ref.dtype)
    for c0 in range(0, d_b, step):
        x1 = conv(o_u + d_b + c0, step, cwb_ref, cbb_ref, d_b + c0)
        v = conv(o_u + 2 * d_b + c0, step, cwb_ref, cbb_ref, 2 * d_b + c0)
        wv = x1 * v
        x0 = conv(o_u + c0, step, cwb_ref, cbb_ref, c0)
        gate = x0 * _silu(_dot(hb, w_ref[:, o_g + c0:o_g + c0 + step]))
        for t in range(step // LANE):
            wv_ref[c0 // LANE + t] = wv[:, t * LANE:(t + 1) * LANE].astype(wv_ref.dtype)
            gate_ref[c0 // LANE + t] = gate[:, t * LANE:(t + 1) * LANE].astype(gate_ref.dtype)
    dt_ref[...] = _dot(hb, w_ref[:, o_dt:o_dt + LANE])


def _proj_in0(x2d, mod, norm_w, w_bf, cols, conv_a_w, conv_a_b, conv_b_w, conv_b_b, seq_len, rows_per_mod, mod_base):
    m_rows, d = x2d.shape
    tm = ROW_TILE
    assert m_rows % tm == 0 and rows_per_mod % tm == 0
    assert tm % seq_len == 0 or seq_len % tm == 0
    o_z, o_xbc, o_u, o_g, o_dt = cols
    d_a, d_xbc, d_b = o_xbc - o_z, o_u - o_xbc, o_dt - o_g
    nsub = m_rows // SUBLANE
    spt = tm // SUBLANE
    cba, cbb = conv_a_b.reshape(1, -1), conv_b_b.reshape(1, -1)

    def const(a):
        return pl.BlockSpec(a.shape, lambda i: (0, 0), pipeline_mode=pl.Buffered(1))

    def tiles(n):
        return pl.BlockSpec((n, tm, LANE), lambda i: (0, i, 0))

    return pl.pallas_call(
        functools.partial(_proj_in0_kernel, seq_len=seq_len, cols=cols),
        grid=(m_rows // tm,),
        in_specs=[pl.BlockSpec((tm, d), lambda i: (i, 0)),
                  pl.BlockSpec((SUBLANE, d), lambda i: (jnp.maximum(i * spt - 1, 0), 0)),
                  pl.BlockSpec((SUBLANE, d), lambda i: (jnp.minimum((i + 1) * spt, nsub - 1), 0)),
                  pl.BlockSpec((1, 1, 3 * d), lambda i: (mod_base + (i * tm) // rows_per_mod, 0, 0)),
                  const(norm_w), const(w_bf), const(conv_a_w), const(cba), const(conv_b_w), const(cbb)],
        out_specs=[pl.BlockSpec((tm, d_a), lambda i: (i, 0)), pl.BlockSpec((tm, d_xbc), lambda i: (i, 0)),
                   tiles(d_b // LANE), tiles(d_b // LANE), pl.BlockSpec((tm, LANE), lambda i: (i, 0))],
        out_shape=[jax.ShapeDtypeStruct((m_rows, d_a), BF16), jax.ShapeDtypeStruct((m_rows, d_xbc), BF16),
                   jax.ShapeDtypeStruct((d_b // LANE, m_rows, LANE), BF16),
                   jax.ShapeDtypeStruct((d_b // LANE, m_rows, LANE), BF16),
                   jax.ShapeDtypeStruct((m_rows, LANE), F32)],
        compiler_params=_params("arbitrary"),
        name="proj_in0",
    )(x2d, x2d, x2d, mod, norm_w, w_bf, conv_a_w, cba, conv_b_w, cbb)


def _ssd_kernel(*refs, has_init, want_final):
    it = iter(refs)
    xm = [next(it), next(it)]
    dtr = [next(it), next(it)]
    dtb_ref, alog_ref, dskip_ref, exp_ref = (next(it) for _ in range(4))
    init_ref = next(it) if has_init else None
    y_refs = [next(it), next(it)]
    fin_ref = next(it) if want_final else None
    s_ref = next(it)

    c = pl.program_id(1)
    nc = pl.num_programs(1)
    q = xm[0].shape[1]
    dm = N_HEADS * HD
    gw = dm // N_GROUPS
    hpg = N_HEADS // N_GROUPS

    @pl.when(c == 0)
    def _():
        if has_init:
            for d in range(2):
                for t in range(dm // LANE):
                    s_ref[d, :, t * LANE:(t + 1) * LANE] = init_ref[0, d, t * LANE:(t + 1) * LANE, :].T
        else:
            s_ref[...] = jnp.zeros(s_ref.shape, F32)

    row = lax.broadcasted_iota(jnp.int32, (q, q), 0)
    col = lax.broadcasted_iota(jnp.int32, (q, q), 1)
    lane = lax.broadcasted_iota(jnp.int32, (q, LANE), 1)
    left = lane < HD
    tri = [(row >= col), (row <= col)]

    for d in range(2):
        xa = xm[d][0, :, 0:dm].astype(F32)
        bm = xm[d][0, :, dm:dm + N_GROUPS * N_STATE]
        cm = xm[d][0, :, dm + N_GROUPS * N_STATE:dm + 2 * N_GROUPS * N_STATE]

        dt = _softplus(dtr[d][0] + dtb_ref[...])
        adt = dt * (-jnp.exp(alog_ref[...]))
        tmat = jnp.where(tri[d], 1.0, 0.0).astype(BF16)
        cs = _dot_rhs_parts(tmat, adt, 3)
        cs_t = cs.T
        dt_t = dt.T
        edge = cs[q - 1:q, :] if d == 0 else cs[0:1, :]
        ex = exp_ref[d]
        e_cs = _dot(_bf(jnp.exp(cs)), ex)
        w_st = _dot(_bf(jnp.exp(edge - cs) * dt), ex)
        xw = xa * w_st

        y_parts = []
        s_prev = s_ref[d]
        new_state = []
        for g in range(N_GROUPS):
            bg = bm[:, g * N_STATE:(g + 1) * N_STATE]
            cg = cm[:, g * N_STATE:(g + 1) * N_STATE]
            gmat = _dot_nt(_bf(cg), _bf(bg))
            y_off = _dot(_bf(cg), _bf(s_prev[:, g * gw:(g + 1) * gw]))
            new_state.append(_dot(_bf(bg.astype(F32).T), _bf(xw[:, g * gw:(g + 1) * gw])))
            for pr in range(hpg // 2):
                mh = []
                for j in range(2):
                    k = d * N_HEADS + g * hpg + 2 * pr + j
                    diff = cs[:, k:k + 1] - cs_t[k:k + 1, :]
                    lm = jnp.exp(jnp.where(tri[d], diff, NEG_INF))
                    mh.append(_bf(gmat * lm * dt_t[k:k + 1, :]))
                c0 = g * gw + pr * LANE
                xpair = xa[:, c0:c0 + LANE]
                rhs = jnp.concatenate([_bf(jnp.where(left, xpair, 0.0)), _bf(jnp.where(left, 0.0, xpair))], axis=0)
                y_d = _dot(jnp.concatenate(mh, axis=1), rhs)
                y_parts.append(y_d + y_off[:, pr * LANE:(pr + 1) * LANE] * e_cs[:, c0:c0 + LANE])
        y = jnp.concatenate(y_parts, axis=1) + xa * dskip_ref[d:d + 1, :]
        y_refs[d][0] = y.astype(y_refs[d].dtype)
        e_edge = e_cs[q - 1:q, :] if d == 0 else e_cs[0:1, :]
        s_ref[d] = s_prev * e_edge + jnp.concatenate(new_state, axis=1)

    if want_final:
        @pl.when(c == nc - 1)
        def _():
            for d in range(2):
                for t in range(dm // LANE):
                    fin_ref[0, d, t * LANE:(t + 1) * LANE, :] = s_ref[d, :, t * LANE:(t + 1) * LANE].T


def _ssd(xs, dt_raw, dt_bias, a_log, d_skip, init, want_final):
    b, l, nchan = xs.shape
    dm = N_HEADS * HD
    q = CHUNK
    nc = l // q

    def full(a):
        nd = a.ndim
        return pl.BlockSpec(a.shape, lambda bi, c: (0,) * nd)

    ex = np.zeros((2, LANE, dm), np.float32)
    for d in range(2):
        for h in range(N_HEADS):
            ex[d, d * N_HEADS + h, h * HD:(h + 1) * HD] = 1.0
    ex = jnp.asarray(ex, BF16)
    pad = LANE - 2 * N_HEADS
    dtb = jnp.pad(dt_bias.reshape(1, 2 * N_HEADS), ((0, 0), (0, pad)))
    alog = jnp.pad(a_log.reshape(1, 2 * N_HEADS), ((0, 0), (0, pad)))
    dsk = jnp.repeat(d_skip, HD, axis=1)

    args = [xs, xs, dt_raw, dt_raw, dtb, alog, dsk, ex]
    in_specs = [pl.BlockSpec((1, q, nchan), lambda bi, c: (bi, c, 0)),
                pl.BlockSpec((1, q, nchan), lambda bi, c: (bi, nc - 1 - c, 0)),
                pl.BlockSpec((1, q, LANE), lambda bi, c: (bi, c, 0)),
                pl.BlockSpec((1, q, LANE), lambda bi, c: (bi, nc - 1 - c, 0)),
                full(dtb), full(alog), full(dsk), full(ex)]
    has_init = init is not None
    if has_init:
        args.append(init)
        in_specs.append(pl.BlockSpec((1, 2, dm, N_STATE), lambda bi, c: (bi, 0, 0, 0)))
    out_shapes = [jax.ShapeDtypeStruct((b, l, dm), BF16), jax.ShapeDtypeStruct((b, l, dm), BF16)]
    out_specs = [pl.BlockSpec((1, q, dm), lambda bi, c: (bi, c, 0)),
                 pl.BlockSpec((1, q, dm), lambda bi, c: (bi, nc - 1 - c, 0))]
    if want_final:
        out_shapes.append(jax.ShapeDtypeStruct((b, 2, dm, N_STATE), F32))
        out_specs.append(pl.BlockSpec((1, 2, dm, N_STATE), lambda bi, c: (bi, 0, 0, 0)))
    return pl.pallas_call(
        functools.partial(_ssd_kernel, has_init=has_init, want_final=want_final),
        grid=(b, nc),
        in_specs=in_specs,
        out_specs=out_specs,
        out_shape=out_shapes,
        scratch_shapes=[pltpu.VMEM((2, N_STATE, dm), F32)],
        compiler_params=_params("arbitrary", "arbitrary"),
        name="ssd_scan",
    )(*args)


def _hyena_tables(l):
    pos = np.abs(np.arange(2 * l, dtype=np.float64) - l)
    t = pos / (l - 1)
    w = 2.0 * math.pi * pos / l
    f = np.linspace(1e-4, HY_BANDS - 1, HY_BANDS)
    feats = np.zeros((2 * l, LANE), np.float64)
    feats[:, 0] = t
    feats[:, 1:1 + HY_BANDS] = np.cos(f[None] * w[:, None])
    feats[:, 1 + HY_BANDS:1 + 2 * HY_BANDS] = -np.sin(f[None] * w[:, None])
    return jnp.asarray(feats, F32)


def _filter_kernel(f_ref, w1_ref, b1_ref, w2_ref, b2_ref, w3_ref, fr_ref, ad_ref, o_ref):
    feats = f_ref[...]
    fr = fr_ref[...]
    h1 = jnp.sin(fr * (_dot3(feats, w1_ref[...]) + b1_ref[...]))
    h2 = jnp.sin(fr * (_dot3(h1, w2_ref[...]) + b2_ref[...]))
    filt = _dot3(h2, w3_ref[...])
    o_ref[...] = filt * jnp.exp(-feats[:, 0:1] * ad_ref[...])


def _hyena_filter_linear(l, w1, b1, w2, b2, w3, freq):
    db = w3.shape[1] // 2
    tr = min(512, l)
    nbk = l // tr
    feats = _hyena_tables(l)
    w1p = jnp.pad(w1, ((0, LANE - HY_EMB), (0, 0)))
    deltas = np.linspace(math.log(HY_TARGET) / HY_DECAY_PCT_HI, math.log(HY_TARGET) / HY_DECAY_PCT_LO, db)
    absd = jnp.asarray(np.abs(deltas)[None], F32)

    def full(a):
        return pl.BlockSpec(a.shape, lambda i: (0, 0))

    b1r, b2r, frr = b1.reshape(1, -1), b2.reshape(1, -1), freq.reshape(1, -1)
    return pl.pallas_call(
        _filter_kernel,
        grid=(2 * nbk,),
        in_specs=[pl.BlockSpec((tr, LANE), lambda i: (i, 0)), full(w1p), full(b1r), full(w2), full(b2r),
                  pl.BlockSpec((HY_HID, db), lambda i: (0, jnp.where(i < nbk, 1, 0))),
                  full(frr), full(absd)],
        out_specs=pl.BlockSpec((tr, db), lambda i: (i, 0)),
        out_shape=jax.ShapeDtypeStruct((2 * l, db), F32),
        compiler_params=_params("arbitrary"),
        name="hyena_filter",
    )(feats, w1p, b1r, w2, b2r, w3, frr, absd)


def _dft_tables(p):
    n = 2 * p
    f = np.arange(p, dtype=np.float64)[:, None] + 0.5
    e = np.arange(n, dtype=np.float64)[None]
    ang = 2.0 * math.pi * f * e / n
    fwd = np.concatenate([np.cos(ang), -np.sin(ang)], axis=0)
    fa = fwd[:, :p]
    fbn = -fwd[:, p:]
    fbn[:, 0] = 0.0
    angt = ang[:, :p].T
    inv = np.concatenate([np.cos(angt), -np.sin(angt)], axis=1) * (2.0 / n)

    return tuple(jnp.asarray(m, F32).astype(BF16) for m in (fa, fbn, inv))


def _spectra_kernel(k1_ref, k0_ref, fa_ref, fb_ref, o_ref):
    res = _dot(fa_ref[...], _bf(k1_ref[...])) + _dot(fb_ref[...], _bf(k0_ref[...]))
    for t in range(o_ref.shape[0]):
        o_ref[t, 0] = res[:, t * LANE:(t + 1) * LANE]


def _hyena_spectra(klin, p, fa, fbn):
    two_l, db = klin.shape
    nseg = two_l // p - 1
    nct = db // LANE
    return pl.pallas_call(
        _spectra_kernel,
        grid=(nseg,),
        in_specs=[pl.BlockSpec((p, db), lambda s: (s + 1, 0)),
                  pl.BlockSpec((p, db), lambda s: (s, 0)),
                  pl.BlockSpec(fa.shape, lambda s: (0, 0)),
                  pl.BlockSpec(fbn.shape, lambda s: (0, 0))],
        out_specs=pl.BlockSpec((nct, 1, 2 * p, LANE), lambda s: (0, s, 0, 0)),
        out_shape=jax.ShapeDtypeStruct((nct, nseg, 2 * p, LANE), F32),
        compiler_params=_params("arbitrary"),
        name="hyena_spectra",
    )(klin, klin, fa, fbn)


def _hyena_kernel(w_ref, gate_ref, gs_ref, hb_ref, fa_ref, iv_ref, o_ref, u_scr, y_scr, *, p):
    bt, l = w_ref.shape[1], w_ref.shape[2]
    nb = l // p
    fa = fa_ref[...]
    for j in range(nb):
        rhs = jnp.concatenate([w_ref[0, bb, j * p:(j + 1) * p, :] for bb in range(bt)], axis=1)
        u_scr[j] = _dot(fa, rhs)
    rt_rows = 64
    for i in range(nb):
        def body(rt, carry):
            r0 = pl.multiple_of(rt * rt_rows, rt_rows)
            for bb in range(bt):
                ls = slice(bb * LANE, (bb + 1) * LANE)
                acc_re = jnp.zeros((rt_rows, LANE), F32)
                acc_im = jnp.zeros((rt_rows, LANE), F32)
                for j in range(nb):
                    s = i - j + nb - 1
                    gre = gs_ref[0, s, pl.ds(r0, rt_rows), :]
                    gim = gs_ref[0, s, pl.ds(p + r0, rt_rows), :]
                    ure = u_scr[j, pl.ds(r0, rt_rows), ls]
                    uim = u_scr[j, pl.ds(p + r0, rt_rows), ls]
                    acc_re = acc_re + (gre * ure - gim * uim)
                    acc_im = acc_im + (gre * uim + gim * ure)
                y_scr[pl.ds(r0, rt_rows), ls] = acc_re
                y_scr[pl.ds(p + r0, rt_rows), ls] = acc_im
            return carry
        lax.fori_loop(0, p // rt_rows, body, 0)
        conv = _dot(iv_ref[...], _bf(y_scr[...]))
        sl = slice(i * p, (i + 1) * p)
        for bb in range(bt):
            wi = w_ref[0, bb, sl, :].astype(F32)
            o_ref[0, bb, sl, :] = (gate_ref[0, bb, sl, :].astype(F32)
                                   * (conv[:, bb * LANE:(bb + 1) * LANE] + wi * hb_ref[...])).astype(o_ref.dtype)


def _hyena(w_t, gate_t, klin, hy_bias, b, l):
    nct = klin.shape[1] // LANE
    p = min(DFT_BLOCK, l)
    nb = l // p
    nseg = 2 * nb - 1
    bt = min(b, HY_BATCH if nb == 1 else HY_BATCH_LONG)
    fa, fbn, iv = _dft_tables(p)
    spectra = _hyena_spectra(klin, p, fa, fbn)
    w4 = w_t.reshape(nct, b, l, LANE)
    g4 = gate_t.reshape(nct, b, l, LANE)
    hbr = hy_bias.reshape(1, -1)
    act = pl.BlockSpec((1, bt, l, LANE), lambda ct, bi: (ct, bi, 0, 0))

    def full(a):
        return pl.BlockSpec(a.shape, lambda ct, bi: (0, 0), pipeline_mode=pl.Buffered(1))

    out = pl.pallas_call(
        functools.partial(_hyena_kernel, p=p),
        grid=(nct, b // bt),
        in_specs=[act, act,
                  pl.BlockSpec((1, nseg, 2 * p, LANE), lambda ct, bi: (ct, 0, 0, 0), pipeline_mode=pl.Buffered(1)),
                  pl.BlockSpec((1, LANE), lambda ct, bi: (0, ct)),
                  full(fa), full(iv)],
        out_specs=act,
        out_shape=jax.ShapeDtypeStruct((nct, b, l, LANE), BF16),
        scratch_shapes=[pltpu.VMEM((nb, 2 * p, bt * LANE), F32), pltpu.VMEM((2 * p, bt * LANE), F32)],
        compiler_params=_params("arbitrary", "arbitrary"),
        name="hyena_conv",
    )(w4, g4, spectra, hbr, fa, iv)
    return out.reshape(nct, b * l, LANE)


def _proj_out0_kernel(x_ref, yf_ref, yb_ref, z_ref, yh_ref, mod_ref, naw_ref, w_ref, o_ref):
    d = x_ref.shape[1]
    ya = _rms((yf_ref[...].astype(F32) + yb_ref[...].astype(F32)) * _silu(z_ref[...].astype(F32)), naw_ref[...])
    yh = jnp.concatenate([yh_ref[t] for t in range(yh_ref.shape[0])], axis=1)
    da = ya.shape[1]
    acc = _dot(_bf(ya), w_ref[0:da, :]) + _dot(_bf(yh), w_ref[da:, :])
    gate = mod_ref[0][:, 2 * d:3 * d]
    o_ref[...] = x_ref[...] + gate * acc


def _proj_out0(x2d, y_f, y_b, z, yh_t, mod, norm_a_w, w_bf, rows_per_mod, mod_base):
    m_rows, d = x2d.shape
    tm = ROW_TILE
    assert m_rows % tm == 0 and rows_per_mod % tm == 0
    da = y_f.shape[1]
    nt = yh_t.shape[0]

    def rowspec(wd):
        return pl.BlockSpec((tm, wd), lambda i: (i, 0))

    return pl.pallas_call(
        _proj_out0_kernel,
        grid=(m_rows // tm,),
        in_specs=[rowspec(d), rowspec(da), rowspec(da), rowspec(da),
                  pl.BlockSpec((nt, tm, LANE), lambda i: (0, i, 0)),
                  pl.BlockSpec((1, 1, 3 * d), lambda i: (mod_base + (i * tm) // rows_per_mod, 0, 0)),
                  pl.BlockSpec((1, da), lambda i: (0, 0)),
                  pl.BlockSpec(w_bf.shape, lambda i: (0, 0), pipeline_mode=pl.Buffered(1))],
        out_specs=rowspec(d),
        out_shape=jax.ShapeDtypeStruct((m_rows, d), F32),
        compiler_params=_params("arbitrary"),
        name="proj_out0",
    )(x2d, y_f, y_b, z, yh_t, mod, norm_a_w, w_bf)


def _proj_out1_kernel(x_ref, o_ref_in, g_ref, mod_ref, fw_ref, w_ref, y_ref):
    d = x_ref.shape[1]
    o = jnp.concatenate([o_ref_in[t] for t in range(o_ref_in.shape[0])], axis=1).astype(F32)
    a = o * _silu(g_ref[...].astype(F32))
    acc = _dot(_bf(a), w_ref[...])
    gate = mod_ref[0][:, 2 * d:3 * d]
    y_ref[...] = _rms(x_ref[...] + gate * acc, fw_ref[...])


def _proj_out1(x2d, o_t, g, mod, final_w, w_bf, rows_per_mod, mod_base):
    m_rows, d = x2d.shape
    tm = ROW_TILE
    assert m_rows % tm == 0 and rows_per_mod % tm == 0
    nt = o_t.shape[0]
    return pl.pallas_call(
        _proj_out1_kernel,
        grid=(m_rows // tm,),
        in_specs=[pl.BlockSpec((tm, d), lambda i: (i, 0)),
                  pl.BlockSpec((nt, tm, LANE), lambda i: (0, i, 0)),
                  pl.BlockSpec((tm, g.shape[1]), lambda i: (i, 0)),
                  pl.BlockSpec((1, 1, 3 * d), lambda i: (mod_base + (i * tm) // rows_per_mod, 0, 0)),
                  pl.BlockSpec((1, d), lambda i: (0, 0)),
                  pl.BlockSpec(w_bf.shape, lambda i: (0, 0), pipeline_mode=pl.Buffered(1))],
        out_specs=pl.BlockSpec((tm, d), lambda i: (i, 0)),
        out_shape=jax.ShapeDtypeStruct((m_rows, d), F32),
        compiler_params=_params("arbitrary"),
        name="proj_out1",
    )(x2d, o_t, g, mod, final_w, w_bf)


def _ctx_layer_kernel(x_ref, mod_ref, nw_ref, wi_ref, wo_ref, fw_ref, y_ref, ck_ref, cv_ref):
    l, d = x_ref.shape
    m = mod_ref[0]
    x = x_ref[...]
    hb = _bf(_rms(x, nw_ref[...]) * (1.0 + m[:, d:2 * d]) + m[:, 0:d])
    qb = _bf(_dot(hb, wi_ref[:, 0:d]) * (HD ** -0.5))
    k = _dot(hb, wi_ref[:, d:2 * d])
    v = _dot(hb, wi_ref[:, 2 * d:3 * d])
    g = _dot(hb, wi_ref[:, 3 * d:4 * d])
    nh = d // HD
    scores, vbs = [], []
    for h in range(nh):
        sl = slice(h * HD, (h + 1) * HD)
        kh = k[:, sl]
        vh = v[:, sl]
        ck_ref[0, 0, h] = kh
        cv_ref[0, 0, h] = vh
        vbs.append(_bf(vh))
        scores.append(_dot_nt(qb[:, sl], _bf(kh)))
    s_all = jnp.concatenate(scores, axis=0)
    pexp = jnp.exp(s_all - jnp.max(s_all, axis=-1, keepdims=True))
    den = jnp.sum(pexp, axis=-1, keepdims=True)
    pb = _bf(pexp)
    outs = [_dot(pb[h * l:(h + 1) * l], vbs[h]) / den[h * l:(h + 1) * l] for h in range(nh)]
    a = jnp.concatenate(outs, axis=1) * _silu(g)
    acc = _dot(_bf(a), wo_ref[...])
    y_ref[...] = _rms(x + m[:, 2 * d:3 * d] * acc, fw_ref[...])


def _ctx_layer(x2d, b, l, mod, norm_w, w_in_bf, w_out_bf, final_w):
    d = x2d.shape[1]
    nh = d // HD
    cache_spec = pl.BlockSpec((1, 1, nh, l, HD), lambda bi: (bi, 0, 0, 0, 0))
    cache_shape = jax.ShapeDtypeStruct((b, 1, nh, l, HD), F32)

    def const(a):
        nd = a.ndim
        return pl.BlockSpec(a.shape, lambda bi: (0,) * nd, pipeline_mode=pl.Buffered(1))

    return pl.pallas_call(
        _ctx_layer_kernel,
        grid=(b,),
        in_specs=[pl.BlockSpec((l, d), lambda bi: (bi, 0)),
                  pl.BlockSpec((1, 1, 3 * d), lambda bi: (0, 0, 0)),
                  const(norm_w), const(w_in_bf), const(w_out_bf), const(final_w)],
        out_specs=[pl.BlockSpec((l, d), lambda bi: (bi, 0)), cache_spec, cache_spec],
        out_shape=[jax.ShapeDtypeStruct((b * l, d), F32), cache_shape, cache_shape],
        compiler_params=_params("arbitrary"),
        name="ctx_layer",
    )(x2d, mod, norm_w, w_in_bf, w_out_bf, final_w)


def _na_bias_kernel(rpb_ref, o_ref):
    h = pl.program_id(0)
    ndr = 2 * WIN_H - 1
    ndc = 2 * WIN_W - 1
    ck = lax.broadcasted_iota(jnp.int32, (GRID_W, LANE), 0)
    lane = lax.broadcasted_iota(jnp.int32, (GRID_W, LANE), 1)
    cq = lane & (GRID_W - 1)
    first = lane < GRID_W
    dc = jnp.clip(ck - cq + (WIN_W - 1), 0, ndc - 1)
    col0 = jnp.clip(cq - WIN_W // 2, 0, GRID_W - WIN_W)
    col_in = (ck >= col0) & (ck < col0 + WIN_W)
    dc_is = [dc == e for e in range(ndc)]
    tables = []
    for dr in range(ndr):
        t = jnp.full((GRID_W, LANE), NEG_INF, F32)
        for e in range(ndc):
            t = jnp.where(dc_is[e] & col_in, rpb_ref[(h * ndr + dr) * ndc + e], t)
        tables.append(t)
    for ip in range(NA_BAND):
        for ap in range(NA_QTILE // 2):
            x = ip - 2 * ap + (WIN_H - 1) - NA_QROWS // 2
            o_ref[0, ip * GRID_W:(ip + 1) * GRID_W, ap * LANE:(ap + 1) * LANE] = jnp.where(
                first, tables[x], tables[x - 1])


def _na_bias_tables(rpb):
    nh = rpb.shape[0]
    shape = (NA_BAND * GRID_W, NA_QTILE * GRID_W)
    return pl.pallas_call(
        _na_bias_kernel,
        grid=(nh,),
        in_specs=[pl.BlockSpec(memory_space=pltpu.SMEM)],
        out_specs=pl.BlockSpec((1,) + shape, lambda h: (h, 0, 0)),
        out_shape=jax.ShapeDtypeStruct((nh,) + shape, F32),
        compiler_params=_params("arbitrary"),
        name="na_bias",
    )(rpb.reshape(-1))


def _na_kernel(q_ref, kp_ref, kc_ref, kn_ref, vp_ref, vc_ref, vn_ref, ck_ref, cv_ref, bias_ref, o_ref, mask_scr,
               s_scr, *, n_rows):
    rb = pl.program_id(1)
    qrows = NA_QROWS
    nq = qrows * GRID_W
    half = (qrows // 2) * GRID_W
    qt = NA_QTILE * GRID_W
    nband = NA_BAND * GRID_W
    q = q_ref[0, 0] * (HD ** -0.5)
    kloc = jnp.concatenate([kp_ref[0, 0][nq - half:nq], kc_ref[0, 0], kn_ref[0, 0][0:half]], axis=0)
    vloc = jnp.concatenate([vp_ref[:, nq - half:nq], vc_ref[...], vn_ref[:, 0:half]], axis=1)

    @pl.when(pl.program_id(2) == 0)
    def _():
        for t in range(qrows // NA_QTILE):
            i = t * NA_QTILE + lax.broadcasted_iota(jnp.int32, (nband, qt), 0) // GRID_W
            a = t * NA_QTILE + lax.broadcasted_iota(jnp.int32, (nband, qt), 1) // GRID_W
            r = rb * qrows + a
            kr = rb * qrows - qrows // 2 + i
            rs = jnp.clip(r - WIN_H // 2, 0, n_rows - WIN_H)
            mask_scr[t] = jnp.where((kr >= rs) & (kr < rs + WIN_H), 0.0, NEG_INF)

    rows = []
    for j in range(LANE // HD):
        sl = slice(j * HD, (j + 1) * HD)
        ckb = _bf(ck_ref[0, j])
        cvt = cv_ref[0, 0, sl, :]
        tiles = []
        nctx = ckb.shape[0]
        for t in range(qrows // NA_QTILE):
            k0 = t * NA_QTILE * GRID_W
            qh = q[t * qt:(t + 1) * qt, sl]
            m = jnp.full((1, qt), NEG_INF, F32)
            for c0 in range(0, nband + nctx, NA_KCHUNK):
                if c0 < nband:
                    rs_ = slice(c0, c0 + NA_KCHUNK)
                    s = _dot_nt(kloc[k0 + c0:k0 + c0 + NA_KCHUNK, sl], qh) + bias_ref[j, rs_, :] + mask_scr[t, rs_, :]
                else:
                    s = _dot_nt(ckb[c0 - nband:c0 - nband + NA_KCHUNK], qh)
                s_scr[c0:c0 + NA_KCHUNK, :] = s
                m = jnp.maximum(m, jnp.max(s, axis=0, keepdims=True))
            den = jnp.zeros((1, qt), F32)
            o = jnp.zeros((HD, qt), F32)
            for c0 in range(0, nband + nctx, NA_KCHUNK):
                pexp = jnp.exp(s_scr[c0:c0 + NA_KCHUNK, :] - m)
                den = den + jnp.sum(pexp, axis=0, keepdims=True)
                if c0 < nband:
                    vt = vloc[sl, k0 + c0:k0 + c0 + NA_KCHUNK]
                else:
                    vt = cvt[:, c0 - nband:c0 - nband + NA_KCHUNK]
                o = o + _dot(vt, _bf(pexp))
            tiles.append(o / den)
        rows.append(jnp.concatenate(tiles, axis=1))
    o_ref[0, 0] = jnp.concatenate(rows, axis=0).T.astype(o_ref.dtype)


def _na_attn(q_t, k_t, v_c, cache_k, cache_vt, bias, b, l):
    npair = q_t.shape[0]
    hpp = LANE // HD
    n_rows = l // GRID_W
    nrb = n_rows // NA_QROWS
    nq = NA_QROWS * GRID_W
    lc = cache_k.shape[2]
    q4, k4 = (a.reshape(npair, b, l, LANE) for a in (q_t, k_t))

    def prev_blk(rb):
        return jnp.maximum(rb - 1, 0)

    def next_blk(rb):
        return jnp.minimum(rb + 1, nrb - 1)

    def same_blk(rb):
        return rb

    def tok(f):
        return pl.BlockSpec((1, 1, nq, LANE), lambda bi, rb, hp: (hp, bi, f(rb), 0))

    def chan(f):
        return pl.BlockSpec((LANE, nq), lambda bi, rb, hp: (hp, bi * nrb + f(rb)))

    out = pl.pallas_call(
        functools.partial(_na_kernel, n_rows=n_rows),
        grid=(b, nrb, npair),
        in_specs=[tok(same_blk), tok(prev_blk), tok(same_blk), tok(next_blk),
                  chan(prev_blk), chan(same_blk), chan(next_blk),
                  pl.BlockSpec((1, hpp, lc, HD), lambda bi, rb, hp: (bi, hp, 0, 0)),
                  pl.BlockSpec((1, 1, LANE, lc), lambda bi, rb, hp: (bi, hp, 0, 0)),
                  pl.BlockSpec((hpp,) + bias.shape[1:], lambda bi, rb, hp: (hp, 0, 0))],
        out_specs=tok(same_blk),
        out_shape=jax.ShapeDtypeStruct((npair, b, l, LANE), BF16),
        scratch_shapes=[pltpu.VMEM((NA_QROWS // NA_QTILE,) + bias.shape[1:], F32),
                        pltpu.VMEM((bias.shape[1] + lc, bias.shape[2]), F32)],
        compiler_params=_params("arbitrary", "arbitrary", "arbitrary"),
        name="na_attn",
    )(q4, k4, k4, k4, v_c, v_c, v_c, cache_k, cache_vt, bias)
    return out.reshape(npair, b * l, LANE)


def _layer0(x2d, b, l, mod, rows_per_mod, mod_base, norm_w, w_in_bf, w_out_bf, p, init, want_final, klin):
    dm = N_HEADS * HD
    d_xbc = dm + 2 * N_GROUPS * N_STATE
    d_b = klin.shape[1]
    cols = (0, dm, dm + d_xbc, dm + d_xbc + 3 * d_b, dm + d_xbc + 4 * d_b)
    z, xs, w_t, gate_t, dt_raw = _proj_in0(x2d, mod, norm_w, w_in_bf, cols, p["conv_a_w"], p["conv_a_b"],
                                           p["conv_b_w"], p["conv_b_b"], l, rows_per_mod, mod_base)
    res = _ssd(xs.reshape(b, l, d_xbc), dt_raw.reshape(b, l, LANE), p["dt_bias"], p["a_log"], p["d_skip"],
               init, want_final)
    y_f, y_b = res[0].reshape(b * l, dm), res[1].reshape(b * l, dm)
    yh_t = _hyena(w_t, gate_t, klin, p["hy_bias"], b, l)
    x_new = _proj_out0(x2d, y_f, y_b, z, yh_t, mod, p["norm_a_w"], w_out_bf, rows_per_mod, mod_base)
    return x_new, (res[2] if want_final else None)


def kernel(x_prompt, x_sample, state_ssd, cache_k, cache_v, c, c_ctx, norm_w, w_ada, b_ada, w_in_e, w_out_e, conv_a_w, conv_a_b, dt_bias, a_log, d_skip, norm_a_w, conv_b_w, conv_b_b, hf_w1, hf_b1, hf_w2, hf_b2, hf_w3, hf_freq, hy_bias, w_in_o, w_out_o, rpb, final_norm_w):
    bp, lp, d = x_prompt.shape
    bs, ls, _ = x_sample.shape
    dm = N_HEADS * HD
    d_xbc = dm + 2 * N_GROUPS * N_STATE
    n_dt = 2 * N_HEADS

    cvecs = jnp.concatenate([c_ctx[None], c, jnp.zeros((SUBLANE - 1 - bs, d), F32)], axis=0)
    mods = _ada_mods(cvecs, w_ada, b_ada)

    xp = x_prompt.reshape(bp * lp, d)
    xs = x_sample.reshape(bs * ls, d)

    wi = w_in_e[0]
    o_dt = dm + d_xbc
    w_in0 = jnp.concatenate([wi[:, :o_dt], wi[:, o_dt + n_dt:], wi[:, o_dt:o_dt + n_dt],
                             jnp.zeros((d, LANE - n_dt), F32)], axis=1).astype(BF16)
    w_out0 = w_out_e[0].astype(BF16)
    p0 = dict(conv_a_w=conv_a_w[0], conv_a_b=conv_a_b[0], dt_bias=dt_bias[0], a_log=a_log[0], d_skip=d_skip[0],
              norm_a_w=norm_a_w[0].reshape(1, -1), conv_b_w=conv_b_w[0], conv_b_b=conv_b_b[0], hy_bias=hy_bias[0])
    mod0 = mods[0].reshape(SUBLANE, 1, 3 * d)
    nw0 = norm_w[0].reshape(1, d)
    hf = (hf_w1[0], hf_b1[0], hf_w2[0], hf_b2[0], hf_w3[0], hf_freq[0])
    klin_p = _hyena_filter_linear(lp, *hf)
    klin_s = _hyena_filter_linear(ls, *hf)
    xp, fin = _layer0(xp, bp, lp, mod0, bp * lp, 0, nw0, w_in0, w_out0, p0, None, True, klin_p)
    init_s = state_ssd[:, 0].reshape(bs, 2, dm, N_STATE)
    xs, _ = _layer0(xs, bs, ls, mod0, ls, 1, nw0, w_in0, w_out0, p0, init_s, False, klin_s)
    new_state_ssd = fin.reshape(bp, 1, 2, N_HEADS, HD, N_STATE)

    w_in1 = w_in_o[0].astype(BF16)
    w_out1 = w_out_o[0].astype(BF16)
    mod1 = mods[1].reshape(SUBLANE, 1, 3 * d)
    nw1 = norm_w[1].reshape(1, d)
    fw = final_norm_w.reshape(1, d)
    y_prompt, new_cache_k, new_cache_v = _ctx_layer(xp, bp, lp, mod1, nw1, w_in1, w_out1, fw)
    y_prompt = y_prompt.reshape(bp, lp, d)

    segs_s = ((0, d, TILED, BF16), (d, d, TILED, BF16), (0, d, CHAN, BF16), (3 * d, d, ROWS, BF16))
    wv_t = w_in_o[0][:, 2 * d:3 * d].T.astype(BF16)
    q_t, k_t, v_c, g = _proj_in(xs, mod1, nw1, w_in1, segs_s, ls, 1, w_t=wv_t)
    bias = _na_bias_tables(rpb[0])
    lc = cache_v.shape[3]
    cache_vt = jnp.swapaxes(cache_v[:, 0], 2, 3).reshape(bs, d // LANE, LANE, lc).astype(BF16)
    o_t = _na_attn(q_t, k_t, v_c, cache_k[:, 0], cache_vt, bias, bs, ls)
    y_sample = _proj_out1(xs, o_t, g, mod1, fw, w_out1, ls, 1).reshape(bs, ls, d)

    return (y_prompt, y_sample, new_state_ssd, new_cache_k, new_cache_v)
```

```python
import functools
import math

import jax
import jax.numpy as jnp
import numpy as np
from jax import lax
from jax.experimental import pallas as pl
from jax.experimental.pallas import tpu as pltpu

F32 = jnp.float32
BF16 = jnp.bfloat16

EPS = 1e-6
GRID_W = 64
WIN_H = 8
WIN_W = 16
HD = 64
N_HEADS = 16
N_STATE = 128
N_GROUPS = 2
CHUNK = 128
HY_EMB = 33
HY_BANDS = (HY_EMB - 1) // 2
HY_HID = 64
HY_TARGET = 1e-2
HY_DECAY_PCT_HI = 0.3
HY_DECAY_PCT_LO = 1.5

LANE = 128
SUBLANE = 8
VMEM_LIMIT = 56 * 1024 * 1024

ROW_TILE = 512
DFT_BLOCK = 512
HY_BATCH = 8
HY_BATCH_LONG = 2
NA_QROWS = 8
NA_QTILE = 4
NA_KCHUNK = 128
NA_BAND = NA_QTILE + WIN_H
NEG_INF = float("-inf")


def _bf(x):
    return x.astype(BF16)


def _dot(a, b):
    return jnp.dot(a, b, preferred_element_type=F32)


def _dot_nt(a, b):
    return lax.dot_general(a, b, (((1,), (1,)), ((), ())), preferred_element_type=F32)


def _split2(x):
    hi = _bf(x)
    lo = _bf(x - hi.astype(F32))
    return hi, lo


def _split3(x):
    hi = _bf(x)
    r = x - hi.astype(F32)
    mid = _bf(r)
    lo = _bf(r - mid.astype(F32))
    return hi, mid, lo


def _dot3(a, b):
    ah, al = _split2(a)
    bh, bl = _split2(b)
    return _dot(ah, bh) + (_dot(ah, bl) + _dot(al, bh))


def _dot_rhs_parts(a_exact, b, parts):
    pieces = _split3(b) if parts == 3 else _split2(b)
    acc = _dot(a_exact, pieces[0])
    for p in pieces[1:]:
        acc = acc + _dot(a_exact, p)
    return acc


def _silu(x):
    return x * jax.nn.sigmoid(x)


def _rms(x, g):
    ms = jnp.mean(x * x, axis=-1, keepdims=True)
    return x * lax.rsqrt(ms + EPS) * g


def _softplus(x):
    return jnp.maximum(x, 0.0) + jnp.log1p(jnp.exp(-jnp.abs(x)))


def _params(*sem):
    return pltpu.CompilerParams(dimension_semantics=sem, vmem_limit_bytes=VMEM_LIMIT)


def _mods_kernel(c_ref, w_ref, b_ref, o_ref):
    a = _silu(c_ref[...])
    o_ref[0] = _dot3(a, w_ref[0]) + b_ref[0]


def _ada_mods(cvecs, w_ada, b_ada):
    depth, d, n3 = w_ada.shape
    tn = n3 // 4
    return pl.pallas_call(
        _mods_kernel,
        grid=(depth, n3 // tn),
        in_specs=[pl.BlockSpec((SUBLANE, d), lambda l, j: (0, 0)),
                  pl.BlockSpec((1, d, tn), lambda l, j: (l, 0, j)),
                  pl.BlockSpec((1, 1, tn), lambda l, j: (l, 0, j))],
        out_specs=pl.BlockSpec((1, SUBLANE, tn), lambda l, j: (l, 0, j)),
        out_shape=jax.ShapeDtypeStruct((depth, SUBLANE, n3), F32),
        compiler_params=_params("arbitrary", "arbitrary"),
        name="ada_mods",
    )(cvecs, w_ada, b_ada.reshape(depth, 1, n3))


ROWS, TILED, CHAN = "rows", "tiled", "chan"


def _proj_in_kernel(x_ref, mod_ref, nw_ref, w_ref, *refs, segs, has_wt):
    wt_ref = refs[0] if has_wt else None
    out_refs = refs[1:] if has_wt else refs
    d = x_ref.shape[1]
    m = mod_ref[0]
    h = _rms(x_ref[...], nw_ref[...]) * (1.0 + m[:, d:2 * d]) + m[:, 0:d]
    hb = _bf(h)
    for (off, width, layout), o_ref in zip(segs, out_refs):
        if layout == CHAN:
            o_ref[...] = _dot_nt(wt_ref[off:off + width, :], hb).astype(o_ref.dtype)
            continue
        step = 4 * LANE
        for c0 in range(0, width, step):
            cw = min(step, width - c0)
            res = _dot(hb, w_ref[:, off + c0:off + c0 + cw])
            if layout == TILED:
                for t in range(cw // LANE):
                    o_ref[(c0 // LANE) + t] = res[:, t * LANE:(t + 1) * LANE].astype(o_ref.dtype)
            else:
                o_ref[:, c0:c0 + cw] = res.astype(o_ref.dtype)


def _proj_in(x2d, mod, norm_w, w_bf, segs, rows_per_mod, mod_base, w_t=None):
    m_rows, d = x2d.shape
    tm = ROW_TILE
    assert m_rows % tm == 0 and rows_per_mod % tm == 0
    out_shapes, out_specs = [], []
    for (_, width, layout, dt) in segs:
        if layout == TILED:
            out_shapes.append(jax.ShapeDtypeStruct((width // LANE, m_rows, LANE), dt))
            out_specs.append(pl.BlockSpec((width // LANE, tm, LANE), lambda i: (0, i, 0)))
        elif layout == CHAN:
            out_shapes.append(jax.ShapeDtypeStruct((width, m_rows), dt))
            out_specs.append(pl.BlockSpec((width, tm), lambda i: (0, i)))
        else:
            out_shapes.append(jax.ShapeDtypeStruct((m_rows, width), dt))
            out_specs.append(pl.BlockSpec((tm, width), lambda i: (i, 0)))
    has_wt = w_t is not None
    kern = functools.partial(_proj_in_kernel, segs=tuple((o, w, t) for (o, w, t, _) in segs), has_wt=has_wt)
    weights = [w_bf, w_t] if has_wt else [w_bf]
    return pl.pallas_call(
        kern,
        grid=(m_rows // tm,),
        in_specs=[pl.BlockSpec((tm, d), lambda i: (i, 0)),
                  pl.BlockSpec((1, 1, 3 * d), lambda i: (mod_base + (i * tm) // rows_per_mod, 0, 0)),
                  pl.BlockSpec((1, d), lambda i: (0, 0))]
                 + [pl.BlockSpec(w.shape, lambda i: (0, 0), pipeline_mode=pl.Buffered(1)) for w in weights],
        out_specs=out_specs,
        out_shape=out_shapes,
        compiler_params=_params("arbitrary"),
        name="proj_in",
    )(x2d, mod, norm_w, *weights)


def _proj_in0_kernel(x_ref, xp_ref, xn_ref, mod_ref, nw_ref, w_ref, cwa_ref, cba_ref, cwb_ref, cbb_ref,
                     z_ref, xs_ref, wv_ref, gate_ref, dt_ref, *, seq_len, cols):
    i = pl.program_id(0)
    tm, d = x_ref.shape
    o_z, o_xbc, o_u, o_g, o_dt = cols
    d_a, d_xbc, d_b = o_xbc - o_z, o_u - o_xbc, o_dt - o_g
    m = mod_ref[0]

    def modnorm(x):
        return _bf(_rms(x, nw_ref[...]) * (1.0 + m[:, d:2 * d]) + m[:, 0:d])

    hb = modnorm(x_ref[...])
    hh = modnorm(jnp.concatenate([xp_ref[...], xn_ref[...]], axis=0))
    keep_prev = ((i * tm) % seq_len != 0).astype(F32)
    keep_next = (((i + 1) * tm) % seq_len != 0).astype(F32)
    starts = list(range(0, tm, seq_len))
    ends = [min(s + seq_len, tm) - 1 for s in starts]

    def conv(off, width, cw_ref, cb_ref, coff):
        res = _dot(hb, w_ref[:, off:off + width])
        rh = _dot(hh, w_ref[:, off:off + width])
        sub = lax.broadcasted_iota(jnp.int32, (SUBLANE, width), 0)
        down = pltpu.roll(res, 1, 0)
        up = pltpu.roll(res, tm - 1, 0)
        dparts, uparts, pos = [], [], 0
        for s in starts:
            fill = rh[SUBLANE - 1:SUBLANE] * keep_prev if s == 0 else 0.0
            dparts += [down[pos:s], jnp.where(sub == 0, fill, down[s:s + SUBLANE])]
            pos = s + SUBLANE
        dparts.append(down[pos:tm])
        pos = 0
        for e in ends:
            fill = rh[SUBLANE:SUBLANE + 1] * keep_next if e == tm - 1 else 0.0
            uparts += [up[pos:e + 1 - SUBLANE], jnp.where(sub == SUBLANE - 1, fill, up[e + 1 - SUBLANE:e + 1])]
            pos = e + 1
        uparts.append(up[pos:tm])
        down = jnp.concatenate([p for p in dparts if p.shape[0]], axis=0)
        up = jnp.concatenate([p for p in uparts if p.shape[0]], axis=0)
        cw = cw_ref[:, coff:coff + width]
        return cb_ref[:, coff:coff + width] + down * cw[0:1] + res * cw[1:2] + up * cw[2:3]

    step = 4 * LANE
    for c0 in range(0, d_a, step):
        z_ref[:, c0:c0 + step] = _dot(hb, w_ref[:, o_z + c0:o_z + c0 + step]).astype(z_ref.dtype)
    for c0 in range(0, d_xbc, step):
        xs_ref[:, c0:c0 + step] = _silu(conv(o_xbc + c0, step, cwa_ref, cba_ref, c0)).astype(xs_ref.dtype)
    for c0 in range(0, d_b, step):
        x1 = conv(o_u + d_b + c0, step, cwb_ref, cbb_ref, d_b + c0)
        v = conv(o_u + 2 * d_b + c0, step, cwb_ref, cbb_ref, 2 * d_b + c0)
        wv = x1 * v
        x0 = conv(o_u + c0, step, cwb_ref, cbb_ref, c0)
        gate = x0 * _silu(_dot(hb, w_ref[:, o_g + c0:o_g + c0 + step]))
        for t in range(step // LANE):
            wv_ref[c0 // LANE + t] = wv[:, t * LANE:(t + 1) * LANE].astype(wv_ref.dtype)
            gate_ref[c0 // LANE + t] = gate[:, t * LANE:(t + 1) * LANE].astype(gate_ref.dtype)
    dt_ref[...] = _dot(hb, w_ref[:, o_dt:o_dt + LANE])


def _proj_in0(x2d, mod, norm_w, w_bf, cols, conv_a_w, conv_a_b, conv_b_w, conv_b_b, seq_len, rows_per_mod, mod_base):
    m_rows, d = x2d.shape
    tm = ROW_TILE
    assert m_rows % tm == 0 and rows_per_mod % tm == 0
    assert tm % seq_len == 0 or seq_len % tm == 0
    o_z, o_xbc, o_u, o_g, o_dt = cols
    d_a, d_xbc, d_b = o_xbc - o_z, o_u - o_xbc, o_dt - o_g
    nsub = m_rows // SUBLANE
    spt = tm // SUBLANE
    cba, cbb = conv_a_b.reshape(1, -1), conv_b_b.reshape(1, -1)

    def const(a):
        return pl.BlockSpec(a.shape, lambda i: (0, 0), pipeline_mode=pl.Buffered(1))

    def tiles(n):
        return pl.BlockSpec((n, tm, LANE), lambda i: (0, i, 0))

    return pl.pallas_call(
        functools.partial(_proj_in0_kernel, seq_len=seq_len, cols=cols),
        grid=(m_rows // tm,),
        in_specs=[pl.BlockSpec((tm, d), lambda i: (i, 0)),
                  pl.BlockSpec((SUBLANE, d), lambda i: (jnp.maximum(i * spt - 1, 0), 0)),
                  pl.BlockSpec((SUBLANE, d), lambda i: (jnp.minimum((i + 1) * spt, nsub - 1), 0)),
                  pl.BlockSpec((1, 1, 3 * d), lambda i: (mod_base + (i * tm) // rows_per_mod, 0, 0)),
                  const(norm_w), const(w_bf), const(conv_a_w), const(cba), const(conv_b_w), const(cbb)],
        out_specs=[pl.BlockSpec((tm, d_a), lambda i: (i, 0)), pl.BlockSpec((tm, d_xbc), lambda i: (i, 0)),
                   tiles(d_b // LANE), tiles(d_b // LANE), pl.BlockSpec((tm, LANE), lambda i: (i, 0))],
        out_shape=[jax.ShapeDtypeStruct((m_rows, d_a), BF16), jax.ShapeDtypeStruct((m_rows, d_xbc), BF16),
                   jax.ShapeDtypeStruct((d_b // LANE, m_rows, LANE), BF16),
                   jax.ShapeDtypeStruct((d_b // LANE, m_rows, LANE), BF16),
                   jax.ShapeDtypeStruct((m_rows, LANE), F32)],
        compiler_params=_params("arbitrary"),
        name="proj_in0",
    )(x2d, x2d, x2d, mod, norm_w, w_bf, conv_a_w, cba, conv_b_w, cbb)


def _ssd_kernel(*refs, has_init, want_final):
    it = iter(refs)
    xm = [next(it), next(it)]
    dtr = [next(it), next(it)]
    dtb_ref, alog_ref, dskip_ref, exp_ref = (next(it) for _ in range(4))
    init_ref = next(it) if has_init else None
    y_refs = [next(it), next(it)]
    fin_ref = next(it) if want_final else None
    s_ref = next(it)

    c = pl.program_id(1)
    nc = pl.num_programs(1)
    q = xm[0].shape[1]
    dm = N_HEADS * HD
    gw = dm // N_GROUPS
    hpg = N_HEADS // N_GROUPS

    @pl.when(c == 0)
    def _():
        if has_init:
            for d in range(2):
                for t in range(dm // LANE):
                    s_ref[d, :, t * LANE:(t + 1) * LANE] = init_ref[0, d, t * LANE:(t + 1) * LANE, :].T
        else:
            s_ref[...] = jnp.zeros(s_ref.shape, F32)

    row = lax.broadcasted_iota(jnp.int32, (q, q), 0)
    col = lax.broadcasted_iota(jnp.int32, (q, q), 1)
    lane = lax.broadcasted_iota(jnp.int32, (q, LANE), 1)
    left = lane < HD
    tri = [(row >= col), (row <= col)]

    for d in range(2):
        xa = xm[d][0, :, 0:dm].astype(F32)
        bm = xm[d][0, :, dm:dm + N_GROUPS * N_STATE]
        cm = xm[d][0, :, dm + N_GROUPS * N_STATE:dm + 2 * N_GROUPS * N_STATE]

        dt = _softplus(dtr[d][0] + dtb_ref[...])
        adt = dt * (-jnp.exp(alog_ref[...]))
        tmat = jnp.where(tri[d], 1.0, 0.0).astype(BF16)
        cs = _dot_rhs_parts(tmat, adt, 3)
        cs_t = cs.T
        dt_t = dt.T
        edge = cs[q - 1:q, :] if d == 0 else cs[0:1, :]
        ex = exp_ref[d]
        e_cs = _dot(_bf(jnp.exp(cs)), ex)
        w_st = _dot(_bf(jnp.exp(edge - cs) * dt), ex)
        xw = xa * w_st

        y_parts = []
        s_prev = s_ref[d]
        new_state = []
        for g in range(N_GROUPS):
            bg = bm[:, g * N_STATE:(g + 1) * N_STATE]
            cg = cm[:, g * N_STATE:(g + 1) * N_STATE]
            gmat = _dot_nt(_bf(cg), _bf(bg))
            y_off = _dot(_bf(cg), _bf(s_prev[:, g * gw:(g + 1) * gw]))
            new_state.append(_dot(_bf(bg.astype(F32).T), _bf(xw[:, g * gw:(g + 1) * gw])))
            for pr in range(hpg // 2):
                mh = []
                for j in range(2):
                    k = d * N_HEADS + g * hpg + 2 * pr + j
                    diff = cs[:, k:k + 1] - cs_t[k:k + 1, :]
                    lm = jnp.exp(jnp.where(tri[d], diff, NEG_INF))
                    mh.append(_bf(gmat * lm * dt_t[k:k + 1, :]))
                c0 = g * gw + pr * LANE
                xpair = xa[:, c0:c0 + LANE]
                rhs = jnp.concatenate([_bf(jnp.where(left, xpair, 0.0)), _bf(jnp.where(left, 0.0, xpair))], axis=0)
                y_d = _dot(jnp.concatenate(mh, axis=1), rhs)
                y_parts.append(y_d + y_off[:, pr * LANE:(pr + 1) * LANE] * e_cs[:, c0:c0 + LANE])
        y = jnp.concatenate(y_parts, axis=1) + xa * dskip_ref[d:d + 1, :]
        y_refs[d][0] = y.astype(y_refs[d].dtype)
        e_edge = e_cs[q - 1:q, :] if d == 0 else e_cs[0:1, :]
        s_ref[d] = s_prev * e_edge + jnp.concatenate(new_state, axis=1)

    if want_final:
        @pl.when(c == nc - 1)
        def _():
            for d in range(2):
                for t in range(dm // LANE):
                    fin_ref[0, d, t * LANE:(t + 1) * LANE, :] = s_ref[d, :, t * LANE:(t + 1) * LANE].T


def _ssd(xs, dt_raw, dt_bias, a_log, d_skip, init, want_final):
    b, l, nchan = xs.shape
    dm = N_HEADS * HD
    q = CHUNK
    nc = l // q

    def full(a):
        nd = a.ndim
        return pl.BlockSpec(a.shape, lambda bi, c: (0,) * nd)

    ex = np.zeros((2, LANE, dm), np.float32)
    for d in range(2):
        for h in range(N_HEADS):
            ex[d, d * N_HEADS + h, h * HD:(h + 1) * HD] = 1.0
    ex = jnp.asarray(ex, BF16)
    pad = LANE - 2 * N_HEADS
    dtb = jnp.pad(dt_bias.reshape(1, 2 * N_HEADS), ((0, 0), (0, pad)))
    alog = jnp.pad(a_log.reshape(1, 2 * N_HEADS), ((0, 0), (0, pad)))
    dsk = jnp.repeat(d_skip, HD, axis=1)

    args = [xs, xs, dt_raw, dt_raw, dtb, alog, dsk, ex]
    in_specs = [pl.BlockSpec((1, q, nchan), lambda bi, c: (bi, c, 0)),
                pl.BlockSpec((1, q, nchan), lambda bi, c: (bi, nc - 1 - c, 0)),
                pl.BlockSpec((1, q, LANE), lambda bi, c: (bi, c, 0)),
                pl.BlockSpec((1, q, LANE), lambda bi, c: (bi, nc - 1 - c, 0)),
                full(dtb), full(alog), full(dsk), full(ex)]
    has_init = init is not None
    if has_init:
        args.append(init)
        in_specs.append(pl.BlockSpec((1, 2, dm, N_STATE), lambda bi, c: (bi, 0, 0, 0)))
    out_shapes = [jax.ShapeDtypeStruct((b, l, dm), BF16), jax.ShapeDtypeStruct((b, l, dm), BF16)]
    out_specs = [pl.BlockSpec((1, q, dm), lambda bi, c: (bi, c, 0)),
                 pl.BlockSpec((1, q, dm), lambda bi, c: (bi, nc - 1 - c, 0))]
    if want_final:
        out_shapes.append(jax.ShapeDtypeStruct((b, 2, dm, N_STATE), F32))
        out_specs.append(pl.BlockSpec((1, 2, dm, N_STATE), lambda bi, c: (bi, 0, 0, 0)))
    return pl.pallas_call(
        functools.partial(_ssd_kernel, has_init=has_init, want_final=want_final),
        grid=(b, nc),
        in_specs=in_specs,
        out_specs=out_specs,
        out_shape=out_shapes,
        scratch_shapes=[pltpu.VMEM((2, N_STATE, dm), F32)],
        compiler_params=_params("arbitrary", "arbitrary"),
        name="ssd_scan",
    )(*args)


def _hyena_tables(l):
    pos = np.abs(np.arange(2 * l, dtype=np.float64) - l)
    t = pos / (l - 1)
    w = 2.0 * math.pi * pos / l
    f = np.linspace(1e-4, HY_BANDS - 1, HY_BANDS)
    feats = np.zeros((2 * l, LANE), np.float64)
    feats[:, 0] = t
    feats[:, 1:1 + HY_BANDS] = np.cos(f[None] * w[:, None])
    feats[:, 1 + HY_BANDS:1 + 2 * HY_BANDS] = -np.sin(f[None] * w[:, None])
    return jnp.asarray(feats, F32)


def _dft_tables(p):
    n = 2 * p
    f = np.arange(p, dtype=np.float64)[:, None] + 0.5
    e = np.arange(p, dtype=np.float64)[None]
    ang = 2.0 * math.pi * f * e / n
    fa = np.concatenate([np.cos(ang), -np.sin(ang)], axis=0)
    inv = np.concatenate([np.cos(ang.T), -np.sin(ang.T)], axis=1) * (2.0 / n)
    return tuple(jnp.asarray(m, F32).astype(BF16) for m in (fa, inv))


def _spectra_kernel(f_ref, w1_ref, b1_ref, w2_ref, b2_ref, w3_ref, fr_ref, ad_ref, fa_ref, o_ref, bprev):
    q = pl.program_id(0)
    p = f_ref.shape[0]
    half = p // 2
    feats = f_ref[...]
    pre1 = _dot3(feats, w1_ref[...])
    packed = jnp.concatenate([pre1[:half], pre1[half:]], axis=1)
    fr = fr_ref[...]
    h1 = jnp.sin(fr * (packed + b1_ref[...]))
    h2 = _bf(jnp.sin(fr * (_dot3(h1, w2_ref[...]) + b2_ref[...])))
    w3 = _bf(w3_ref[...])
    filt = jnp.concatenate([_dot(h2[:, :HY_HID], w3), _dot(h2[:, HY_HID:], w3)], axis=0)
    taps = _bf(filt * jnp.exp(-feats[:, 0:1] * ad_ref[...]))
    a = _dot(fa_ref[...], taps)

    @pl.when(q > 0)
    def _():
        g = a + bprev[...]
        for t in range(o_ref.shape[0]):
            o_ref[t, 0] = g[:, t * LANE:(t + 1) * LANE]

    odd = (lax.broadcasted_iota(jnp.int32, (p, a.shape[1]), 0) & 1) == 1
    a_re, a_im = a[0:p], a[p:]
    a_re0 = a_re - taps[0:1, :].astype(F32)
    bprev[0:p] = jnp.where(odd, a_im, -a_im)
    bprev[p:] = jnp.where(odd, -a_re0, a_re0)


def _hyena_spectra(l, p, fa, w1, b1, w2, b2, w3, freq):
    db = w3.shape[1] // 2
    nblk = 2 * l // p
    nct = db // LANE
    feats = _hyena_tables(l)
    w1p = jnp.pad(w1, ((0, LANE - HY_EMB), (0, 0)))
    zero = jnp.zeros_like(w2)
    w2bd = jnp.concatenate([jnp.concatenate([w2, zero], axis=1), jnp.concatenate([zero, w2], axis=1)], axis=0)
    deltas = np.linspace(math.log(HY_TARGET) / HY_DECAY_PCT_HI, math.log(HY_TARGET) / HY_DECAY_PCT_LO, db)
    absd = jnp.asarray(np.abs(deltas)[None], F32)
    b1r, b2r, frr = (jnp.tile(v.reshape(1, -1), (1, 2)) for v in (b1, b2, freq))

    def full(a):
        return pl.BlockSpec(a.shape, lambda q: (0, 0))

    return pl.pallas_call(
        _spectra_kernel,
        grid=(nblk,),
        in_specs=[pl.BlockSpec((p, LANE), lambda q: (q, 0)), full(w1p), full(b1r), full(w2bd), full(b2r),
                  pl.BlockSpec((HY_HID, db), lambda q: (0, jnp.where(q < nblk // 2, 1, 0))),
                  full(frr), full(absd), full(fa)],
        out_specs=pl.BlockSpec((nct, 1, 2 * p, LANE), lambda q: (0, jnp.maximum(q - 1, 0), 0, 0)),
        out_shape=jax.ShapeDtypeStruct((nct, nblk - 1, 2 * p, LANE), F32),
        scratch_shapes=[pltpu.VMEM((2 * p, db), F32)],
        compiler_params=_params("arbitrary"),
        name="hyena_spectra",
    )(feats, w1p, b1r, w2bd, b2r, w3, frr, absd, fa)


def _hyena_kernel(w_ref, gate_ref, gs_ref, hb_ref, fa_ref, iv_ref, o_ref, u_scr, y_scr, *, p):
    bt, l = w_ref.shape[1], w_ref.shape[2]
    nb = l // p
    fa = fa_ref[...]
    for j in range(nb):
        rhs = jnp.concatenate([w_ref[0, bb, j * p:(j + 1) * p, :] for bb in range(bt)], axis=1)
        u_scr[j] = _dot(fa, rhs)
    rt_rows = 64
    for i in range(nb):
        def body(rt, carry):
            r0 = pl.multiple_of(rt * rt_rows, rt_rows)
            for bb in range(bt):
                ls = slice(bb * LANE, (bb + 1) * LANE)
                acc_re = jnp.zeros((rt_rows, LANE), F32)
                acc_im = jnp.zeros((rt_rows, LANE), F32)
                for j in range(nb):
                    s = i - j + nb - 1
                    gre = gs_ref[0, s, pl.ds(r0, rt_rows), :]
                    gim = gs_ref[0, s, pl.ds(p + r0, rt_rows), :]
                    ure = u_scr[j, pl.ds(r0, rt_rows), ls]
                    uim = u_scr[j, pl.ds(p + r0, rt_rows), ls]
                    acc_re = acc_re + (gre * ure - gim * uim)
                    acc_im = acc_im + (gre * uim + gim * ure)
                y_scr[pl.ds(r0, rt_rows), ls] = acc_re
                y_scr[pl.ds(p + r0, rt_rows), ls] = acc_im
            return carry
        lax.fori_loop(0, p // rt_rows, body, 0)
        conv = _dot(iv_ref[...], _bf(y_scr[...]))
        sl = slice(i * p, (i + 1) * p)
        for bb in range(bt):
            wi = w_ref[0, bb, sl, :].astype(F32)
            o_ref[0, bb, sl, :] = (gate_ref[0, bb, sl, :].astype(F32)
                                   * (conv[:, bb * LANE:(bb + 1) * LANE] + wi * hb_ref[...])).astype(o_ref.dtype)


def _hyena(w_t, gate_t, filt_params, hy_bias, b, l):
    nct = w_t.shape[0]
    p = min(DFT_BLOCK, l)
    nb = l // p
    nseg = 2 * nb - 1
    bt = min(b, HY_BATCH if nb == 1 else HY_BATCH_LONG)
    fa, iv = _dft_tables(p)
    spectra = _hyena_spectra(l, p, fa, *filt_params)
    w4 = w_t.reshape(nct, b, l, LANE)
    g4 = gate_t.reshape(nct, b, l, LANE)
    hbr = hy_bias.reshape(1, -1)
    act = pl.BlockSpec((1, bt, l, LANE), lambda ct, bi: (ct, bi, 0, 0))

    def full(a):
        return pl.BlockSpec(a.shape, lambda ct, bi: (0, 0), pipeline_mode=pl.Buffered(1))

    out = pl.pallas_call(
        functools.partial(_hyena_kernel, p=p),
        grid=(nct, b // bt),
        in_specs=[act, act,
                  pl.BlockSpec((1, nseg, 2 * p, LANE), lambda ct, bi: (ct, 0, 0, 0), pipeline_mode=pl.Buffered(1)),
                  pl.BlockSpec((1, LANE), lambda ct, bi: (0, ct)),
                  full(fa), full(iv)],
        out_specs=act,
        out_shape=jax.ShapeDtypeStruct((nct, b, l, LANE), BF16),
        scratch_shapes=[pltpu.VMEM((nb, 2 * p, bt * LANE), F32), pltpu.VMEM((2 * p, bt * LANE), F32)],
        compiler_params=_params("arbitrary", "arbitrary"),
        name="hyena_conv",
    )(w4, g4, spectra, hbr, fa, iv)
    return out.reshape(nct, b * l, LANE)


def _proj_out0_kernel(x_ref, yf_ref, yb_ref, z_ref, yh_ref, mod_ref, naw_ref, w_ref, o_ref):
    d = x_ref.shape[1]
    ya = _rms((yf_ref[...].astype(F32) + yb_ref[...].astype(F32)) * _silu(z_ref[...].astype(F32)), naw_ref[...])
    yh = jnp.concatenate([yh_ref[t] for t in range(yh_ref.shape[0])], axis=1)
    da = ya.shape[1]
    acc = _dot(_bf(ya), w_ref[0:da, :]) + _dot(_bf(yh), w_ref[da:, :])
    gate = mod_ref[0][:, 2 * d:3 * d]
    o_ref[...] = x_ref[...] + gate * acc


def _proj_out0(x2d, y_f, y_b, z, yh_t, mod, norm_a_w, w_bf, rows_per_mod, mod_base):
    m_rows, d = x2d.shape
    tm = ROW_TILE
    assert m_rows % tm == 0 and rows_per_mod % tm == 0
    da = y_f.shape[1]
    nt = yh_t.shape[0]

    def rowspec(wd):
        return pl.BlockSpec((tm, wd), lambda i: (i, 0))

    return pl.pallas_call(
        _proj_out0_kernel,
        grid=(m_rows // tm,),
        in_specs=[rowspec(d), rowspec(da), rowspec(da), rowspec(da),
                  pl.BlockSpec((nt, tm, LANE), lambda i: (0, i, 0)),
                  pl.BlockSpec((1, 1, 3 * d), lambda i: (mod_base + (i * tm) // rows_per_mod, 0, 0)),
                  pl.BlockSpec((1, da), lambda i: (0, 0)),
                  pl.BlockSpec(w_bf.shape, lambda i: (0, 0), pipeline_mode=pl.Buffered(1))],
        out_specs=rowspec(d),
        out_shape=jax.ShapeDtypeStruct((m_rows, d), F32),
        compiler_params=_params("arbitrary"),
        name="proj_out0",
    )(x2d, y_f, y_b, z, yh_t, mod, norm_a_w, w_bf)


def _proj_out1_kernel(x_ref, o_ref_in, g_ref, mod_ref, fw_ref, w_ref, y_ref):
    d = x_ref.shape[1]
    o = jnp.concatenate([o_ref_in[t] for t in range(o_ref_in.shape[0])], axis=1).astype(F32)
    a = o * _silu(g_ref[...].astype(F32))
    acc = _dot(_bf(a), w_ref[...])
    gate = mod_ref[0][:, 2 * d:3 * d]
    y_ref[...] = _rms(x_ref[...] + gate * acc, fw_ref[...])


def _proj_out1(x2d, o_t, g, mod, final_w, w_bf, rows_per_mod, mod_base):
    m_rows, d = x2d.shape
    tm = ROW_TILE
    assert m_rows % tm == 0 and rows_per_mod % tm == 0
    nt = o_t.shape[0]
    return pl.pallas_call(
        _proj_out1_kernel,
        grid=(m_rows // tm,),
        in_specs=[pl.BlockSpec((tm, d), lambda i: (i, 0)),
                  pl.BlockSpec((nt, tm, LANE), lambda i: (0, i, 0)),
                  pl.BlockSpec((tm, g.shape[1]), lambda i: (i, 0)),
                  pl.BlockSpec((1, 1, 3 * d), lambda i: (mod_base + (i * tm) // rows_per_mod, 0, 0)),
                  pl.BlockSpec((1, d), lambda i: (0, 0)),
                  pl.BlockSpec(w_bf.shape, lambda i: (0, 0), pipeline_mode=pl.Buffered(1))],
        out_specs=pl.BlockSpec((tm, d), lambda i: (i, 0)),
        out_shape=jax.ShapeDtypeStruct((m_rows, d), F32),
        compiler_params=_params("arbitrary"),
        name="proj_out1",
    )(x2d, o_t, g, mod, final_w, w_bf)


def _ctx_layer_kernel(x_ref, mod_ref, nw_ref, wi_ref, wo_ref, fw_ref, y_ref, ck_ref, cv_ref):
    l, d = x_ref.shape
    m = mod_ref[0]
    x = x_ref[...]
    hb = _bf(_rms(x, nw_ref[...]) * (1.0 + m[:, d:2 * d]) + m[:, 0:d])
    qb = _bf(_dot(hb, wi_ref[:, 0:d]) * (HD ** -0.5))
    k = _dot(hb, wi_ref[:, d:2 * d])
    v = _dot(hb, wi_ref[:, 2 * d:3 * d])
    g = _dot(hb, wi_ref[:, 3 * d:4 * d])
    nh = d // HD
    scores, vbs = [], []
    for h in range(nh):
        sl = slice(h * HD, (h + 1) * HD)
        kh = k[:, sl]
        vh = v[:, sl]
        ck_ref[0, 0, h] = kh
        cv_ref[0, 0, h] = vh
        vbs.append(_bf(vh))
        scores.append(_dot_nt(qb[:, sl], _bf(kh)))
    s_all = jnp.concatenate(scores, axis=0)
    pexp = jnp.exp(s_all - jnp.max(s_all, axis=-1, keepdims=True))
    den = jnp.sum(pexp, axis=-1, keepdims=True)
    pb = _bf(pexp)
    outs = [_dot(pb[h * l:(h + 1) * l], vbs[h]) / den[h * l:(h + 1) * l] for h in range(nh)]
    a = jnp.concatenate(outs, axis=1) * _silu(g)
    acc = _dot(_bf(a), wo_ref[...])
    y_ref[...] = _rms(x + m[:, 2 * d:3 * d] * acc, fw_ref[...])


def _ctx_layer(x2d, b, l, mod, norm_w, w_in_bf, w_out_bf, final_w):
    d = x2d.shape[1]
    nh = d // HD
    cache_spec = pl.BlockSpec((1, 1, nh, l, HD), lambda bi: (bi, 0, 0, 0, 0))
    cache_shape = jax.ShapeDtypeStruct((b, 1, nh, l, HD), F32)

    def const(a):
        nd = a.ndim
        return pl.BlockSpec(a.shape, lambda bi: (0,) * nd, pipeline_mode=pl.Buffered(1))

    return pl.pallas_call(
        _ctx_layer_kernel,
        grid=(b,),
        in_specs=[pl.BlockSpec((l, d), lambda bi: (bi, 0)),
                  pl.BlockSpec((1, 1, 3 * d), lambda bi: (0, 0, 0)),
                  const(norm_w), const(w_in_bf), const(w_out_bf), const(final_w)],
        out_specs=[pl.BlockSpec((l, d), lambda bi: (bi, 0)), cache_spec, cache_spec],
        out_shape=[jax.ShapeDtypeStruct((b * l, d), F32), cache_shape, cache_shape],
        compiler_params=_params("arbitrary"),
        name="ctx_layer",
    )(x2d, mod, norm_w, w_in_bf, w_out_bf, final_w)


def _na_bias_kernel(rpb_ref, o_ref):
    h = pl.program_id(0)
    ndr = 2 * WIN_H - 1
    ndc = 2 * WIN_W - 1
    ck = lax.broadcasted_iota(jnp.int32, (GRID_W, LANE), 0)
    lane = lax.broadcasted_iota(jnp.int32, (GRID_W, LANE), 1)
    cq = lane & (GRID_W - 1)
    first = lane < GRID_W
    dc = jnp.clip(ck - cq + (WIN_W - 1), 0, ndc - 1)
    col0 = jnp.clip(cq - WIN_W // 2, 0, GRID_W - WIN_W)
    col_in = (ck >= col0) & (ck < col0 + WIN_W)
    dc_is = [(dc == e) & col_in for e in range(ndc)]
    tables = []
    for dr in range(ndr):
        t = jnp.full((GRID_W, LANE), NEG_INF, F32)
        for e in range(ndc):
            t = jnp.where(dc_is[e], rpb_ref[(h * ndr + dr) * ndc + e], t)
        tables.append(t)
    for ip in range(NA_BAND):
        for ap in range(NA_QTILE // 2):
            x = ip - 2 * ap + (WIN_H - 1) - NA_QROWS // 2
            o_ref[0, ip * GRID_W:(ip + 1) * GRID_W, ap * LANE:(ap + 1) * LANE] = jnp.where(
                first, tables[x], tables[x - 1])


def _na_bias_tables(rpb):
    nh = rpb.shape[0]
    shape = (NA_BAND * GRID_W, NA_QTILE * GRID_W)
    return pl.pallas_call(
        _na_bias_kernel,
        grid=(nh,),
        in_specs=[pl.BlockSpec(memory_space=pltpu.SMEM)],
        out_specs=pl.BlockSpec((1,) + shape, lambda h: (h, 0, 0)),
        out_shape=jax.ShapeDtypeStruct((nh,) + shape, F32),
        compiler_params=_params("arbitrary"),
        name="na_bias",
    )(rpb.reshape(-1))


def _na_kernel(q_ref, kp_ref, kc_ref, kn_ref, vp_ref, vc_ref, vn_ref, ck_ref, cv_ref, bias_ref, o_ref, mask_scr,
               s_scr, *, n_rows):
    rb = pl.program_id(1)
    qrows = NA_QROWS
    nq = qrows * GRID_W
    half = (qrows // 2) * GRID_W
    qt = NA_QTILE * GRID_W
    nband = NA_BAND * GRID_W
    q = q_ref[0, 0] * (HD ** -0.5)
    kloc = jnp.concatenate([kp_ref[0, 0][nq - half:nq], kc_ref[0, 0], kn_ref[0, 0][0:half]], axis=0)
    vloc = jnp.concatenate([vp_ref[:, nq - half:nq], vc_ref[...], vn_ref[:, 0:half]], axis=1)

    @pl.when(pl.program_id(2) == 0)
    def _():
        for t in range(qrows // NA_QTILE):
            i = t * NA_QTILE + lax.broadcasted_iota(jnp.int32, (nband, qt), 0) // GRID_W
            a = t * NA_QTILE + lax.broadcasted_iota(jnp.int32, (nband, qt), 1) // GRID_W
            r = rb * qrows + a
            kr = rb * qrows - qrows // 2 + i
            rs = jnp.clip(r - WIN_H // 2, 0, n_rows - WIN_H)
            mask_scr[t] = jnp.where((kr >= rs) & (kr < rs + WIN_H), 0.0, NEG_INF)

    rows = []
    for j in range(LANE // HD):
        sl = slice(j * HD, (j + 1) * HD)
        ckb = _bf(ck_ref[0, j])
        cvt = cv_ref[0, 0, sl, :]
        tiles = []
        nctx = ckb.shape[0]
        for t in range(qrows // NA_QTILE):
            k0 = t * NA_QTILE * GRID_W
            qh = q[t * qt:(t + 1) * qt, sl]
            m = jnp.full((1, qt), NEG_INF, F32)
            for c0 in range(0, nband + nctx, NA_KCHUNK):
                if c0 < nband:
                    rs_ = slice(c0, c0 + NA_KCHUNK)
                    s = _dot_nt(kloc[k0 + c0:k0 + c0 + NA_KCHUNK, sl], qh) + bias_ref[j, rs_, :] + mask_scr[t, rs_, :]
                else:
                    s = _dot_nt(ckb[c0 - nband:c0 - nband + NA_KCHUNK], qh)
                s_scr[c0:c0 + NA_KCHUNK, :] = s
                m = jnp.maximum(m, jnp.max(s, axis=0, keepdims=True))
            den = jnp.zeros((1, qt), F32)
            o = jnp.zeros((HD, qt), F32)
            for c0 in range(0, nband + nctx, NA_KCHUNK):
                pexp = jnp.exp(s_scr[c0:c0 + NA_KCHUNK, :] - m)
                den = den + jnp.sum(pexp, axis=0, keepdims=True)
                if c0 < nband:
                    vt = vloc[sl, k0 + c0:k0 + c0 + NA_KCHUNK]
                else:
                    vt = cvt[:, c0 - nband:c0 - nband + NA_KCHUNK]
                o = o + _dot(vt, _bf(pexp))
            tiles.append(o / den)
        rows.append(jnp.concatenate(tiles, axis=1))
    o_ref[0, 0] = jnp.concatenate(rows, axis=0).T.astype(o_ref.dtype)


def _na_attn(q_t, k_t, v_c, cache_k, cache_vt, bias, b, l):
    npair = q_t.shape[0]
    hpp = LANE // HD
    n_rows = l // GRID_W
    nrb = n_rows // NA_QROWS
    nq = NA_QROWS * GRID_W
    lc = cache_k.shape[2]
    q4, k4 = (a.reshape(npair, b, l, LANE) for a in (q_t, k_t))

    def prev_blk(rb):
        return jnp.maximum(rb - 1, 0)

    def next_blk(rb):
        return jnp.minimum(rb + 1, nrb - 1)

    def same_blk(rb):
        return rb

    def tok(f):
        return pl.BlockSpec((1, 1, nq, LANE), lambda bi, rb, hp: (hp, bi, f(rb), 0))

    def chan(f):
        return pl.BlockSpec((LANE, nq), lambda bi, rb, hp: (hp, bi * nrb + f(rb)))

    out = pl.pallas_call(
        functools.partial(_na_kernel, n_rows=n_rows),
        grid=(b, nrb, npair),
        in_specs=[tok(same_blk), tok(prev_blk), tok(same_blk), tok(next_blk),
                  chan(prev_blk), chan(same_blk), chan(next_blk),
                  pl.BlockSpec((1, hpp, lc, HD), lambda bi, rb, hp: (bi, hp, 0, 0)),
                  pl.BlockSpec((1, 1, LANE, lc), lambda bi, rb, hp: (bi, hp, 0, 0)),
                  pl.BlockSpec((hpp,) + bias.shape[1:], lambda bi, rb, hp: (hp, 0, 0))],
        out_specs=tok(same_blk),
        out_shape=jax.ShapeDtypeStruct((npair, b, l, LANE), BF16),
        scratch_shapes=[pltpu.VMEM((NA_QROWS // NA_QTILE,) + bias.shape[1:], F32),
                        pltpu.VMEM((bias.shape[1] + lc, bias.shape[2]), F32)],
        compiler_params=_params("arbitrary", "arbitrary", "arbitrary"),
        name="na_attn",
    )(q4, k4, k4, k4, v_c, v_c, v_c, cache_k, cache_vt, bias)
    return out.reshape(npair, b * l, LANE)


def _reorder_kernel(w_ref, o_ref, *, o_dt, n_dt):
    rows, n = w_ref.shape
    rest = n - o_dt - n_dt
    o_ref[:, 0:o_dt] = w_ref[:, 0:o_dt].astype(o_ref.dtype)
    o_ref[:, o_dt:o_dt + rest] = w_ref[:, o_dt + n_dt:n].astype(o_ref.dtype)
    tail = jnp.concatenate([w_ref[:, o_dt:o_dt + n_dt], jnp.zeros((rows, LANE - n_dt), F32)], axis=1)
    o_ref[:, o_dt + rest:o_dt + rest + LANE] = tail.astype(o_ref.dtype)


def _reorder_w_in(w, o_dt, n_dt):
    d, n = w.shape
    n_out = n - n_dt + LANE
    tr = LANE
    return pl.pallas_call(
        functools.partial(_reorder_kernel, o_dt=o_dt, n_dt=n_dt),
        grid=(d // tr,),
        in_specs=[pl.BlockSpec((tr, n), lambda i: (i, 0))],
        out_specs=pl.BlockSpec((tr, n_out), lambda i: (i, 0)),
        out_shape=jax.ShapeDtypeStruct((d, n_out), BF16),
        compiler_params=_params("arbitrary"),
        name="reorder_w_in",
    )(w)


def _layer0(x2d, b, l, mod, rows_per_mod, mod_base, norm_w, w_in_bf, w_out_bf, p, init, want_final, filt_params):
    dm = N_HEADS * HD
    d_xbc = dm + 2 * N_GROUPS * N_STATE
    d_b = p["hy_bias"].shape[0]
    cols = (0, dm, dm + d_xbc, dm + d_xbc + 3 * d_b, dm + d_xbc + 4 * d_b)
    z, xs, w_t, gate_t, dt_raw = _proj_in0(x2d, mod, norm_w, w_in_bf, cols, p["conv_a_w"], p["conv_a_b"],
                                           p["conv_b_w"], p["conv_b_b"], l, rows_per_mod, mod_base)
    res = _ssd(xs.reshape(b, l, d_xbc), dt_raw.reshape(b, l, LANE), p["dt_bias"], p["a_log"], p["d_skip"],
               init, want_final)
    y_f, y_b = res[0].reshape(b * l, dm), res[1].reshape(b * l, dm)
    yh_t = _hyena(w_t, gate_t, filt_params, p["hy_bias"], b, l)
    x_new = _proj_out0(x2d, y_f, y_b, z, yh_t, mod, p["norm_a_w"], w_out_bf, rows_per_mod, mod_base)
    return x_new, (res[2] if want_final else None)


def kernel(x_prompt, x_sample, state_ssd, cache_k, cache_v, c, c_ctx, norm_w, w_ada, b_ada, w_in_e, w_out_e, conv_a_w, conv_a_b, dt_bias, a_log, d_skip, norm_a_w, conv_b_w, conv_b_b, hf_w1, hf_b1, hf_w2, hf_b2, hf_w3, hf_freq, hy_bias, w_in_o, w_out_o, rpb, final_norm_w):
    bp, lp, d = x_prompt.shape
    bs, ls, _ = x_sample.shape
    dm = N_HEADS * HD
    d_xbc = dm + 2 * N_GROUPS * N_STATE
    n_dt = 2 * N_HEADS

    cvecs = jnp.concatenate([c_ctx[None], c, jnp.zeros((SUBLANE - 1 - bs, d), F32)], axis=0)
    mods = _ada_mods(cvecs, w_ada, b_ada)

    xp = x_prompt.reshape(bp * lp, d)
    xs = x_sample.reshape(bs * ls, d)

    w_in0 = _reorder_w_in(w_in_e[0], dm + d_xbc, n_dt)
    w_out0 = w_out_e[0].astype(BF16)
    p0 = dict(conv_a_w=conv_a_w[0], conv_a_b=conv_a_b[0], dt_bias=dt_bias[0], a_log=a_log[0], d_skip=d_skip[0],
              norm_a_w=norm_a_w[0].reshape(1, -1), conv_b_w=conv_b_w[0], conv_b_b=conv_b_b[0], hy_bias=hy_bias[0])
    mod0 = mods[0].reshape(SUBLANE, 1, 3 * d)
    nw0 = norm_w[0].reshape(1, d)
    hf = (hf_w1[0], hf_b1[0], hf_w2[0], hf_b2[0], hf_w3[0], hf_freq[0])
    xp, fin = _layer0(xp, bp, lp, mod0, bp * lp, 0, nw0, w_in0, w_out0, p0, None, True, hf)
    init_s = state_ssd[:, 0].reshape(bs, 2, dm, N_STATE)
    xs, _ = _layer0(xs, bs, ls, mod0, ls, 1, nw0, w_in0, w_out0, p0, init_s, False, hf)
    new_state_ssd = fin.reshape(bp, 1, 2, N_HEADS, HD, N_STATE)

    w_in1 = w_in_o[0].astype(BF16)
    w_out1 = w_out_o[0].astype(BF16)
    mod1 = mods[1].reshape(SUBLANE, 1, 3 * d)
    nw1 = norm_w[1].reshape(1, d)
    fw = final_norm_w.reshape(1, d)
    y_prompt, new_cache_k, new_cache_v = _ctx_layer(xp, bp, lp, mod1, nw1, w_in1, w_out1, fw)
    y_prompt = y_prompt.reshape(bp, lp, d)

    segs_s = ((0, d, TILED, BF16), (d, d, TILED, BF16), (0, d, CHAN, BF16), (3 * d, d, ROWS, BF16))
    wv_t = w_in_o[0][:, 2 * d:3 * d].T.astype(BF16)
    q_t, k_t, v_c, g = _proj_in(xs, mod1, nw1, w_in1, segs_s, ls, 1, w_t=wv_t)
    bias = _na_bias_tables(rpb[0])
    lc = cache_v.shape[3]
    cache_vt = jnp.swapaxes(cache_v[:, 0], 2, 3).reshape(bs, d // LANE, LANE, lc).astype(BF16)
    o_t = _na_attn(q_t, k_t, v_c, cache_k[:, 0], cache_vt, bias, bs, ls)
    y_sample = _proj_out1(xs, o_t, g, mod1, fw, w_out1, ls, 1).reshape(bs, ls, d)

    return (y_prompt, y_sample, new_state_ssd, new_cache_k, new_cache_v)
```

```python
import functools
import math

import jax
import jax.numpy as jnp
import numpy as np
from jax import lax
from jax.experimental import pallas as pl
from jax.experimental.pallas import tpu as pltpu

F32 = jnp.float32
BF16 = jnp.bfloat16

EPS = 1e-6
GRID_W = 64
WIN_H = 8
WIN_W = 16
HD = 64
N_HEADS = 16
N_STATE = 128
N_GROUPS = 2
CHUNK = 128
HY_EMB = 33
HY_BANDS = (HY_EMB - 1) // 2
HY_HID = 64
HY_TARGET = 1e-2
HY_DECAY_PCT_HI = 0.3
HY_DECAY_PCT_LO = 1.5

LANE = 128
SUBLANE = 8
VMEM_LIMIT = 56 * 1024 * 1024

ROW_TILE = 512
DFT_BLOCK = 512
HY_BATCH = 32
HY_BATCH_LONG = 2
NA_QROWS = 8
NA_QTILE = 4
NA_KCHUNK = 128
NA_BAND = NA_QTILE + WIN_H
NEG_INF = float("-inf")


def _bf(x):
    return x.astype(BF16)


def _dot(a, b):
    return jnp.dot(a, b, preferred_element_type=F32)


def _dot_nt(a, b):
    return lax.dot_general(a, b, (((1,), (1,)), ((), ())), preferred_element_type=F32)


def _split2(x):
    hi = _bf(x)
    lo = _bf(x - hi.astype(F32))
    return hi, lo


def _split3(x):
    hi = _bf(x)
    r = x - hi.astype(F32)
    mid = _bf(r)
    lo = _bf(r - mid.astype(F32))
    return hi, mid, lo


def _dot3(a, b):
    ah, al = _split2(a)
    bh, bl = _split2(b)
    return _dot(ah, bh) + (_dot(ah, bl) + _dot(al, bh))


def _dot_rhs_parts(a_exact, b, parts):
    pieces = _split3(b) if parts == 3 else _split2(b)
    acc = _dot(a_exact, pieces[0])
    for p in pieces[1:]:
        acc = acc + _dot(a_exact, p)
    return acc


def _silu(x):
    return x * jax.nn.sigmoid(x)


def _rms(x, g):
    ms = jnp.mean(x * x, axis=-1, keepdims=True)
    return x * lax.rsqrt(ms + EPS) * g


def _softplus(x):
    return jnp.maximum(x, 0.0) + jnp.log1p(jnp.exp(-jnp.abs(x)))


def _params(*sem):
    return pltpu.CompilerParams(dimension_semantics=sem, vmem_limit_bytes=VMEM_LIMIT)


def _mods_kernel(c_ref, w_ref, b_ref, o_ref):
    a = _silu(c_ref[...])
    o_ref[0] = _dot3(a, w_ref[0]) + b_ref[0]


def _ada_mods(cvecs, w_ada, b_ada):
    depth, d, n3 = w_ada.shape
    tn = n3 // 4
    return pl.pallas_call(
        _mods_kernel,
        grid=(depth, n3 // tn),
        in_specs=[pl.BlockSpec((SUBLANE, d), lambda l, j: (0, 0)),
                  pl.BlockSpec((1, d, tn), lambda l, j: (l, 0, j)),
                  pl.BlockSpec((1, 1, tn), lambda l, j: (l, 0, j))],
        out_specs=pl.BlockSpec((1, SUBLANE, tn), lambda l, j: (l, 0, j)),
        out_shape=jax.ShapeDtypeStruct((depth, SUBLANE, n3), F32),
        compiler_params=_params("arbitrary", "arbitrary"),
        name="ada_mods",
    )(cvecs, w_ada, b_ada.reshape(depth, 1, n3))


ROWS, TILED, CHAN = "rows", "tiled", "chan"


def _proj_in_kernel(x_ref, mod_ref, nw_ref, w_ref, *refs, segs, has_wt):
    wt_ref = refs[0] if has_wt else None
    out_refs = refs[1:] if has_wt else refs
    d = x_ref.shape[1]
    m = mod_ref[0]
    h = _rms(x_ref[...], nw_ref[...]) * (1.0 + m[:, d:2 * d]) + m[:, 0:d]
    hb = _bf(h)
    for (off, width, layout), o_ref in zip(segs, out_refs):
        if layout == CHAN:
            o_ref[...] = _dot_nt(wt_ref[off:off + width, :], hb).astype(o_ref.dtype)
            continue
        step = 4 * LANE
        for c0 in range(0, width, step):
            cw = min(step, width - c0)
            res = _dot(hb, w_ref[:, off + c0:off + c0 + cw])
            if layout == TILED:
                for t in range(cw // LANE):
                    o_ref[(c0 // LANE) + t] = res[:, t * LANE:(t + 1) * LANE].astype(o_ref.dtype)
            else:
                o_ref[:, c0:c0 + cw] = res.astype(o_ref.dtype)


def _proj_in(x2d, mod, norm_w, w_bf, segs, rows_per_mod, mod_base, w_t=None):
    m_rows, d = x2d.shape
    tm = ROW_TILE
    assert m_rows % tm == 0 and rows_per_mod % tm == 0
    out_shapes, out_specs = [], []
    for (_, width, layout, dt) in segs:
        if layout == TILED:
            out_shapes.append(jax.ShapeDtypeStruct((width // LANE, m_rows, LANE), dt))
            out_specs.append(pl.BlockSpec((width // LANE, tm, LANE), lambda i: (0, i, 0)))
        elif layout == CHAN:
            out_shapes.append(jax.ShapeDtypeStruct((width, m_rows), dt))
            out_specs.append(pl.BlockSpec((width, tm), lambda i: (0, i)))
        else:
            out_shapes.append(jax.ShapeDtypeStruct((m_rows, width), dt))
            out_specs.append(pl.BlockSpec((tm, width), lambda i: (i, 0)))
    has_wt = w_t is not None
    kern = functools.partial(_proj_in_kernel, segs=tuple((o, w, t) for (o, w, t, _) in segs), has_wt=has_wt)
    weights = [w_bf, w_t] if has_wt else [w_bf]
    return pl.pallas_call(
        kern,
        grid=(m_rows // tm,),
        in_specs=[pl.BlockSpec((tm, d), lambda i: (i, 0)),
                  pl.BlockSpec((1, 1, 3 * d), lambda i: (mod_base + (i * tm) // rows_per_mod, 0, 0)),
                  pl.BlockSpec((1, d), lambda i: (0, 0))]
                 + [pl.BlockSpec(w.shape, lambda i: (0, 0), pipeline_mode=pl.Buffered(1)) for w in weights],
        out_specs=out_specs,
        out_shape=out_shapes,
        compiler_params=_params("arbitrary"),
        name="proj_in",
    )(x2d, mod, norm_w, *weights)


def _conv3_rows(res, prev_row, next_row, seq_len, cw, cb):
    tm, width = res.shape
    starts = list(range(0, tm, seq_len))
    ends = [min(s + seq_len, tm) - 1 for s in starts]
    sub = lax.broadcasted_iota(jnp.int32, (SUBLANE, width), 0)
    down = pltpu.roll(res, 1, 0)
    up = pltpu.roll(res, tm - 1, 0)
    dparts, uparts, pos = [], [], 0
    for s in starts:
        fill = prev_row if s == 0 else 0.0
        dparts += [down[pos:s], jnp.where(sub == 0, fill, down[s:s + SUBLANE])]
        pos = s + SUBLANE
    dparts.append(down[pos:tm])
    pos = 0
    for e in ends:
        fill = next_row if e == tm - 1 else 0.0
        uparts += [up[pos:e + 1 - SUBLANE], jnp.where(sub == SUBLANE - 1, fill, up[e + 1 - SUBLANE:e + 1])]
        pos = e + 1
    uparts.append(up[pos:tm])
    down = jnp.concatenate([p for p in dparts if p.shape[0]], axis=0)
    up = jnp.concatenate([p for p in uparts if p.shape[0]], axis=0)
    return cb + down * cw[0:1] + res * cw[1:2] + up * cw[2:3]


def _proj_in0_kernel(x_ref, xp_ref, xn_ref, mod_ref, nw_ref, w_ref, cwa_ref, cba_ref, cwb_ref, cbb_ref,
                     z_ref, xs_ref, wv_ref, gate_ref, dt_ref, *, seq_len, cols):
    i = pl.program_id(0)
    tm, d = x_ref.shape
    o_z, o_xbc, o_u, o_g, o_dt = cols
    d_a, d_xbc, d_b = o_xbc - o_z, o_u - o_xbc, o_dt - o_g
    m = mod_ref[0]

    def modnorm(x):
        return _bf(_rms(x, nw_ref[...]) * (1.0 + m[:, d:2 * d]) + m[:, 0:d])

    hb = modnorm(x_ref[...])
    hh = modnorm(jnp.concatenate([xp_ref[...], xn_ref[...]], axis=0))
    keep_prev = ((i * tm) % seq_len != 0).astype(F32)
    keep_next = (((i + 1) * tm) % seq_len != 0).astype(F32)

    def conv(off, width, cw_ref, cb_ref, coff):
        res = _dot(hb, w_ref[:, off:off + width])
        rh = _dot(hh, w_ref[:, off:off + width])
        return _conv3_rows(res, rh[SUBLANE - 1:SUBLANE] * keep_prev, rh[SUBLANE:SUBLANE + 1] * keep_next, seq_len,
                           cw_ref[:, coff:coff + width], cb_ref[:, coff:coff + width])

    step = 4 * LANE
    for c0 in range(0, d_a, step):
        z_ref[:, c0:c0 + step] = _dot(hb, w_ref[:, o_z + c0:o_z + c0 + step]).astype(z_ref.dtype)
    for c0 in range(0, d_xbc, step):
        xs_ref[:, c0:c0 + step] = _silu(conv(o_xbc + c0, step, cwa_ref, cba_ref, c0)).astype(xs_ref.dtype)
    for c0 in range(0, d_b, step):
        x1 = conv(o_u + d_b + c0, step, cwb_ref, cbb_ref, d_b + c0)
        v = conv(o_u + 2 * d_b + c0, step, cwb_ref, cbb_ref, 2 * d_b + c0)
        wv = x1 * v
        x0 = conv(o_u + c0, step, cwb_ref, cbb_ref, c0)
        gate = x0 * _silu(_dot(hb, w_ref[:, o_g + c0:o_g + c0 + step]))
        for t in range(step // LANE):
            wv_ref[c0 // LANE + t] = wv[:, t * LANE:(t + 1) * LANE].astype(wv_ref.dtype)
            gate_ref[c0 // LANE + t] = gate[:, t * LANE:(t + 1) * LANE].astype(gate_ref.dtype)
    dt_ref[...] = _dot(hb, w_ref[:, o_dt:o_dt + LANE])


def _proj_in0(x2d, mod, norm_w, w_bf, cols, conv_a_w, conv_a_b, conv_b_w, conv_b_b, seq_len, rows_per_mod, mod_base):
    m_rows, d = x2d.shape
    tm = ROW_TILE
    assert m_rows % tm == 0 and rows_per_mod % tm == 0
    assert tm % seq_len == 0 or seq_len % tm == 0
    o_z, o_xbc, o_u, o_g, o_dt = cols
    d_a, d_xbc, d_b = o_xbc - o_z, o_u - o_xbc, o_dt - o_g
    nsub = m_rows // SUBLANE
    spt = tm // SUBLANE
    cba, cbb = conv_a_b.reshape(1, -1), conv_b_b.reshape(1, -1)

    def const(a):
        return pl.BlockSpec(a.shape, lambda i: (0, 0), pipeline_mode=pl.Buffered(1))

    def tiles(n):
        return pl.BlockSpec((n, tm, LANE), lambda i: (0, i, 0))

    return pl.pallas_call(
        functools.partial(_proj_in0_kernel, seq_len=seq_len, cols=cols),
        grid=(m_rows // tm,),
        in_specs=[pl.BlockSpec((tm, d), lambda i: (i, 0)),
                  pl.BlockSpec((SUBLANE, d), lambda i: (jnp.maximum(i * spt - 1, 0), 0)),
                  pl.BlockSpec((SUBLANE, d), lambda i: (jnp.minimum((i + 1) * spt, nsub - 1), 0)),
                  pl.BlockSpec((1, 1, 3 * d), lambda i: (mod_base + (i * tm) // rows_per_mod, 0, 0)),
                  const(norm_w), const(w_bf), const(conv_a_w), const(cba), const(conv_b_w), const(cbb)],
        out_specs=[pl.BlockSpec((tm, d_a), lambda i: (i, 0)), pl.BlockSpec((tm, d_xbc), lambda i: (i, 0)),
                   tiles(d_b // LANE), tiles(d_b // LANE), pl.BlockSpec((tm, LANE), lambda i: (i, 0))],
        out_shape=[jax.ShapeDtypeStruct((m_rows, d_a), BF16), jax.ShapeDtypeStruct((m_rows, d_xbc), BF16),
                   jax.ShapeDtypeStruct((d_b // LANE, m_rows, LANE), BF16),
                   jax.ShapeDtypeStruct((d_b // LANE, m_rows, LANE), BF16),
                   jax.ShapeDtypeStruct((m_rows, LANE), F32)],
        compiler_params=_params("arbitrary"),
        name="proj_in0",
    )(x2d, x2d, x2d, mod, norm_w, w_bf, conv_a_w, cba, conv_b_w, cbb)


def _ssd_chunk(d, xa, bm, cm, dt_raw, s_prev, dtb, alog, dskip, ex):
    q, dm = xa.shape
    gw = dm // N_GROUPS
    hpg = N_HEADS // N_GROUPS
    row = lax.broadcasted_iota(jnp.int32, (q, q), 0)
    col = lax.broadcasted_iota(jnp.int32, (q, q), 1)
    left = lax.broadcasted_iota(jnp.int32, (q, LANE), 1) < HD
    tri = (row >= col) if d == 0 else (row <= col)

    dt = _softplus(dt_raw + dtb)
    adt = dt * (-jnp.exp(alog))
    cs = _dot_rhs_parts(jnp.where(tri, 1.0, 0.0).astype(BF16), adt, 3)
    cs_t = cs.T
    dt_t = dt.T
    edge = cs[q - 1:q, :] if d == 0 else cs[0:1, :]
    e_cs = _dot(_bf(jnp.exp(cs)), ex)
    w_st = _dot(_bf(jnp.exp(edge - cs) * dt), ex)
    xw = xa * w_st

    y_parts = []
    new_state = []
    for g in range(N_GROUPS):
        bg = bm[:, g * N_STATE:(g + 1) * N_STATE]
        cg = cm[:, g * N_STATE:(g + 1) * N_STATE]
        gmat = _dot_nt(_bf(cg), _bf(bg))
        y_off = _dot(_bf(cg), _bf(s_prev[:, g * gw:(g + 1) * gw]))
        new_state.append(_dot(_bf(bg.astype(F32).T), _bf(xw[:, g * gw:(g + 1) * gw])))
        for pr in range(hpg // 2):
            mh = []
            for j in range(2):
                k = d * N_HEADS + g * hpg + 2 * pr + j
                diff = cs[:, k:k + 1] - cs_t[k:k + 1, :]
                lm = jnp.exp(jnp.where(tri, diff, NEG_INF))
                mh.append(_bf(gmat * lm * dt_t[k:k + 1, :]))
            c0 = g * gw + pr * LANE
            xpair = xa[:, c0:c0 + LANE]
            rhs = jnp.concatenate([_bf(jnp.where(left, xpair, 0.0)), _bf(jnp.where(left, 0.0, xpair))], axis=0)
            y_d = _dot(jnp.concatenate(mh, axis=1), rhs)
            y_parts.append(y_d + y_off[:, pr * LANE:(pr + 1) * LANE] * e_cs[:, c0:c0 + LANE])
    y = jnp.concatenate(y_parts, axis=1) + xa * dskip
    e_edge = e_cs[q - 1:q, :] if d == 0 else e_cs[0:1, :]
    return y, s_prev * e_edge + jnp.concatenate(new_state, axis=1)


def _ssd_kernel(*refs, has_init, want_final):
    it = iter(refs)
    xm = [next(it), next(it)]
    dtr = [next(it), next(it)]
    dtb_ref, alog_ref, dskip_ref, exp_ref = (next(it) for _ in range(4))
    init_ref = next(it) if has_init else None
    y_refs = [next(it), next(it)]
    fin_ref = next(it) if want_final else None
    s_ref = next(it)

    c = pl.program_id(1)
    nc = pl.num_programs(1)
    q = xm[0].shape[1]
    dm = N_HEADS * HD
    gw = dm // N_GROUPS
    hpg = N_HEADS // N_GROUPS

    @pl.when(c == 0)
    def _():
        if has_init:
            for d in range(2):
                for t in range(dm // LANE):
                    s_ref[d, :, t * LANE:(t + 1) * LANE] = init_ref[0, d, t * LANE:(t + 1) * LANE, :].T
        else:
            s_ref[...] = jnp.zeros(s_ref.shape, F32)

    for d in range(2):
        xa = xm[d][0, :, 0:dm].astype(F32)
        bm = xm[d][0, :, dm:dm + N_GROUPS * N_STATE]
        cm = xm[d][0, :, dm + N_GROUPS * N_STATE:dm + 2 * N_GROUPS * N_STATE]
        y, s_new = _ssd_chunk(d, xa, bm, cm, dtr[d][0], s_ref[d], dtb_ref[...], alog_ref[...],
                              dskip_ref[d:d + 1, :], exp_ref[d])
        y_refs[d][0] = y.astype(y_refs[d].dtype)
        s_ref[d] = s_new

    if want_final:
        @pl.when(c == nc - 1)
        def _():
            for d in range(2):
                for t in range(dm // LANE):
                    fin_ref[0, d, t * LANE:(t + 1) * LANE, :] = s_ref[d, :, t * LANE:(t + 1) * LANE].T


def _ssd(xs, dt_raw, dt_bias, a_log, d_skip, init, want_final):
    b, l, nchan = xs.shape
    dm = N_HEADS * HD
    q = CHUNK
    nc = l // q

    def full(a):
        nd = a.ndim
        return pl.BlockSpec(a.shape, lambda bi, c: (0,) * nd)

    ex = np.zeros((2, LANE, dm), np.float32)
    for d in range(2):
        for h in range(N_HEADS):
            ex[d, d * N_HEADS + h, h * HD:(h + 1) * HD] = 1.0
    ex = jnp.asarray(ex, BF16)
    pad = LANE - 2 * N_HEADS
    dtb = jnp.pad(dt_bias.reshape(1, 2 * N_HEADS), ((0, 0), (0, pad)))
    alog = jnp.pad(a_log.reshape(1, 2 * N_HEADS), ((0, 0), (0, pad)))
    dsk = jnp.repeat(d_skip, HD, axis=1)

    args = [xs, xs, dt_raw, dt_raw, dtb, alog, dsk, ex]
    in_specs = [pl.BlockSpec((1, q, nchan), lambda bi, c: (bi, c, 0)),
                pl.BlockSpec((1, q, nchan), lambda bi, c: (bi, nc - 1 - c, 0)),
                pl.BlockSpec((1, q, LANE), lambda bi, c: (bi, c, 0)),
                pl.BlockSpec((1, q, LANE), lambda bi, c: (bi, nc - 1 - c, 0)),
                full(dtb), full(alog), full(dsk), full(ex)]
    has_init = init is not None
    if has_init:
        args.append(init)
        in_specs.append(pl.BlockSpec((1, 2, dm, N_STATE), lambda bi, c: (bi, 0, 0, 0)))
    out_shapes = [jax.ShapeDtypeStruct((b, l, dm), BF16), jax.ShapeDtypeStruct((b, l, dm), BF16)]
    out_specs = [pl.BlockSpec((1, q, dm), lambda bi, c: (bi, c, 0)),
                 pl.BlockSpec((1, q, dm), lambda bi, c: (bi, nc - 1 - c, 0))]
    if want_final:
        out_shapes.append(jax.ShapeDtypeStruct((b, 2, dm, N_STATE), F32))
        out_specs.append(pl.BlockSpec((1, 2, dm, N_STATE), lambda bi, c: (bi, 0, 0, 0)))
    return pl.pallas_call(
        functools.partial(_ssd_kernel, has_init=has_init, want_final=want_final),
        grid=(b, nc),
        in_specs=in_specs,
        out_specs=out_specs,
        out_shape=out_shapes,
        scratch_shapes=[pltpu.VMEM((2, N_STATE, dm), F32)],
        compiler_params=_params("arbitrary", "arbitrary"),
        name="ssd_scan",
    )(*args)


def _hyena_tables(l):
    pos = np.abs(np.arange(2 * l, dtype=np.float64) - l)
    t = pos / (l - 1)
    w = 2.0 * math.pi * pos / l
    f = np.linspace(1e-4, HY_BANDS - 1, HY_BANDS)
    feats = np.zeros((2 * l, LANE), np.float64)
    feats[:, 0] = t
    feats[:, 1:1 + HY_BANDS] = np.cos(f[None] * w[:, None])
    feats[:, 1 + HY_BANDS:1 + 2 * HY_BANDS] = -np.sin(f[None] * w[:, None])
    return jnp.asarray(feats, F32)


def _dft_tables(p):
    n = 2 * p
    f = np.arange(p, dtype=np.float64)[:, None] + 0.5
    e = np.arange(p, dtype=np.float64)[None]
    ang = 2.0 * math.pi * f * e / n
    fa = np.concatenate([np.cos(ang), -np.sin(ang)], axis=0)
    inv = np.concatenate([np.cos(ang.T), -np.sin(ang.T)], axis=1) * (2.0 / n)
    return tuple(jnp.asarray(m, F32).astype(BF16) for m in (fa, inv))


def _spectra_kernel(f_ref, w1_ref, b1_ref, w2_ref, b2_ref, w3_ref, fr_ref, ad_ref, fa_ref, o_ref, bprev):
    q = pl.program_id(0)
    p = f_ref.shape[0]
    half = p // 2
    feats = f_ref[...]
    pre1 = _dot3(feats, w1_ref[...])
    packed = jnp.concatenate([pre1[:half], pre1[half:]], axis=1)
    fr = fr_ref[...]
    h1 = jnp.sin(fr * (packed + b1_ref[...]))
    h2 = _bf(jnp.sin(fr * (_dot3(h1, w2_ref[...]) + b2_ref[...])))
    w3 = _bf(w3_ref[...])
    filt = jnp.concatenate([_dot(h2[:, :HY_HID], w3), _dot(h2[:, HY_HID:], w3)], axis=0)
    taps = _bf(filt * jnp.exp(-feats[:, 0:1] * ad_ref[...]))
    a = _dot(fa_ref[...], taps)

    @pl.when(q > 0)
    def _():
        g = a + bprev[...]
        for t in range(o_ref.shape[0]):
            o_ref[t, 0] = g[:, t * LANE:(t + 1) * LANE]

    odd = (lax.broadcasted_iota(jnp.int32, (p, a.shape[1]), 0) & 1) == 1
    a_re, a_im = a[0:p], a[p:]
    a_re0 = a_re - taps[0:1, :].astype(F32)
    bprev[0:p] = jnp.where(odd, a_im, -a_im)
    bprev[p:] = jnp.where(odd, -a_re0, a_re0)


def _hyena_spectra(l, p, fa, w1, b1, w2, b2, w3, freq):
    db = w3.shape[1] // 2
    nblk = 2 * l // p
    nct = db // LANE
    feats = _hyena_tables(l)
    w1p = jnp.pad(w1, ((0, LANE - HY_EMB), (0, 0)))
    zero = jnp.zeros_like(w2)
    w2bd = jnp.concatenate([jnp.concatenate([w2, zero], axis=1), jnp.concatenate([zero, w2], axis=1)], axis=0)
    deltas = np.linspace(math.log(HY_TARGET) / HY_DECAY_PCT_HI, math.log(HY_TARGET) / HY_DECAY_PCT_LO, db)
    absd = jnp.asarray(np.abs(deltas)[None], F32)
    b1r, b2r, frr = (jnp.tile(v.reshape(1, -1), (1, 2)) for v in (b1, b2, freq))

    def full(a):
        return pl.BlockSpec(a.shape, lambda q: (0, 0))

    return pl.pallas_call(
        _spectra_kernel,
        grid=(nblk,),
        in_specs=[pl.BlockSpec((p, LANE), lambda q: (q, 0)), full(w1p), full(b1r), full(w2bd), full(b2r),
                  pl.BlockSpec((HY_HID, db), lambda q: (0, jnp.where(q < nblk // 2, 1, 0))),
                  full(frr), full(absd), full(fa)],
        out_specs=pl.BlockSpec((nct, 1, 2 * p, LANE), lambda q: (0, jnp.maximum(q - 1, 0), 0, 0)),
        out_shape=jax.ShapeDtypeStruct((nct, nblk - 1, 2 * p, LANE), F32),
        scratch_shapes=[pltpu.VMEM((2 * p, db), F32)],
        compiler_params=_params("arbitrary"),
        name="hyena_spectra",
    )(feats, w1p, b1r, w2bd, b2r, w3, frr, absd, fa)


def _hyena_kernel(w_ref, gate_ref, gs_ref, hb_ref, fa_ref, iv_ref, o_ref, u_scr, y_scr, *, p):
    bt, l = w_ref.shape[1], w_ref.shape[2]
    nb = l // p
    fa = fa_ref[...]
    for j in range(nb):
        rhs = jnp.concatenate([w_ref[0, bb, j * p:(j + 1) * p, :] for bb in range(bt)], axis=1)
        u_scr[j] = _dot(fa, rhs)
    rt_rows = 64
    for i in range(nb):
        def body(rt, carry):
            r0 = pl.multiple_of(rt * rt_rows, rt_rows)
            for bb in range(bt):
                ls = slice(bb * LANE, (bb + 1) * LANE)
                acc_re = jnp.zeros((rt_rows, LANE), F32)
                acc_im = jnp.zeros((rt_rows, LANE), F32)
                for j in range(nb):
                    s = i - j + nb - 1
                    gre = gs_ref[0, s, pl.ds(r0, rt_rows), :]
                    gim = gs_ref[0, s, pl.ds(p + r0, rt_rows), :]
                    ure = u_scr[j, pl.ds(r0, rt_rows), ls]
                    uim = u_scr[j, pl.ds(p + r0, rt_rows), ls]
                    acc_re = acc_re + (gre * ure - gim * uim)
                    acc_im = acc_im + (gre * uim + gim * ure)
                y_scr[pl.ds(r0, rt_rows), ls] = acc_re
                y_scr[pl.ds(p + r0, rt_rows), ls] = acc_im
            return carry
        lax.fori_loop(0, p // rt_rows, body, 0)
        conv = _dot(iv_ref[...], _bf(y_scr[...]))
        sl = slice(i * p, (i + 1) * p)
        for bb in range(bt):
            wi = w_ref[0, bb, sl, :].astype(F32)
            o_ref[0, bb, sl, :] = (gate_ref[0, bb, sl, :].astype(F32)
                                   * (conv[:, bb * LANE:(bb + 1) * LANE] + wi * hb_ref[...])).astype(o_ref.dtype)


def _hyena(w_t, gate_t, filt_params, hy_bias, b, l):
    nct = w_t.shape[0]
    p = min(DFT_BLOCK, l)
    nb = l // p
    nseg = 2 * nb - 1
    bt = min(b, HY_BATCH if nb == 1 else HY_BATCH_LONG)
    fa, iv = _dft_tables(p)
    spectra = _hyena_spectra(l, p, fa, *filt_params)
    w4 = w_t.reshape(nct, b, l, LANE)
    g4 = gate_t.reshape(nct, b, l, LANE)
    hbr = hy_bias.reshape(1, -1)
    act = pl.BlockSpec((1, bt, l, LANE), lambda ct, bi: (ct, bi, 0, 0))

    def full(a):
        return pl.BlockSpec(a.shape, lambda ct, bi: (0, 0), pipeline_mode=pl.Buffered(1))

    out = pl.pallas_call(
        functools.partial(_hyena_kernel, p=p),
        grid=(nct, b // bt),
        in_specs=[act, act,
                  pl.BlockSpec((1, nseg, 2 * p, LANE), lambda ct, bi: (ct, 0, 0, 0), pipeline_mode=pl.Buffered(1)),
                  pl.BlockSpec((1, LANE), lambda ct, bi: (0, ct)),
                  full(fa), full(iv)],
        out_specs=act,
        out_shape=jax.ShapeDtypeStruct((nct, b, l, LANE), BF16),
        scratch_shapes=[pltpu.VMEM((nb, 2 * p, bt * LANE), F32), pltpu.VMEM((2 * p, bt * LANE), F32)],
        compiler_params=_params("arbitrary", "arbitrary"),
        name="hyena_conv",
    )(w4, g4, spectra, hbr, fa, iv)
    return out.reshape(nct, b * l, LANE)


def _proj_out0_kernel(x_ref, yf_ref, yb_ref, z_ref, yh_ref, mod_ref, naw_ref, w_ref, o_ref):
    d = x_ref.shape[1]
    ya = _rms((yf_ref[...].astype(F32) + yb_ref[...].astype(F32)) * _silu(z_ref[...].astype(F32)), naw_ref[...])
    yh = jnp.concatenate([yh_ref[t] for t in range(yh_ref.shape[0])], axis=1)
    da = ya.shape[1]
    acc = _dot(_bf(ya), w_ref[0:da, :]) + _dot(_bf(yh), w_ref[da:, :])
    gate = mod_ref[0][:, 2 * d:3 * d]
    o_ref[...] = x_ref[...] + gate * acc


def _proj_out0(x2d, y_f, y_b, z, yh_t, mod, norm_a_w, w_bf, rows_per_mod, mod_base):
    m_rows, d = x2d.shape
    tm = ROW_TILE
    assert m_rows % tm == 0 and rows_per_mod % tm == 0
    da = y_f.shape[1]
    nt = yh_t.shape[0]

    def rowspec(wd):
        return pl.BlockSpec((tm, wd), lambda i: (i, 0))

    return pl.pallas_call(
        _proj_out0_kernel,
        grid=(m_rows // tm,),
        in_specs=[rowspec(d), rowspec(da), rowspec(da), rowspec(da),
                  pl.BlockSpec((nt, tm, LANE), lambda i: (0, i, 0)),
                  pl.BlockSpec((1, 1, 3 * d), lambda i: (mod_base + (i * tm) // rows_per_mod, 0, 0)),
                  pl.BlockSpec((1, da), lambda i: (0, 0)),
                  pl.BlockSpec(w_bf.shape, lambda i: (0, 0), pipeline_mode=pl.Buffered(1))],
        out_specs=rowspec(d),
        out_shape=jax.ShapeDtypeStruct((m_rows, d), F32),
        compiler_params=_params("arbitrary"),
        name="proj_out0",
    )(x2d, y_f, y_b, z, yh_t, mod, norm_a_w, w_bf)


def _proj_out1_kernel(x_ref, o_ref_in, g_ref, mod_ref, fw_ref, w_ref, y_ref):
    d = x_ref.shape[1]
    o = jnp.concatenate([o_ref_in[t] for t in range(o_ref_in.shape[0])], axis=1).astype(F32)
    a = o * _silu(g_ref[...].astype(F32))
    acc = _dot(_bf(a), w_ref[...])
    gate = mod_ref[0][:, 2 * d:3 * d]
    y_ref[...] = _rms(x_ref[...] + gate * acc, fw_ref[...])


def _proj_out1(x2d, o_t, g, mod, final_w, w_bf, rows_per_mod, mod_base):
    m_rows, d = x2d.shape
    tm = ROW_TILE
    assert m_rows % tm == 0 and rows_per_mod % tm == 0
    nt = o_t.shape[0]
    return pl.pallas_call(
        _proj_out1_kernel,
        grid=(m_rows // tm,),
        in_specs=[pl.BlockSpec((tm, d), lambda i: (i, 0)),
                  pl.BlockSpec((nt, tm, LANE), lambda i: (0, i, 0)),
                  pl.BlockSpec((tm, g.shape[1]), lambda i: (i, 0)),
                  pl.BlockSpec((1, 1, 3 * d), lambda i: (mod_base + (i * tm) // rows_per_mod, 0, 0)),
                  pl.BlockSpec((1, d), lambda i: (0, 0)),
                  pl.BlockSpec(w_bf.shape, lambda i: (0, 0), pipeline_mode=pl.Buffered(1))],
        out_specs=pl.BlockSpec((tm, d), lambda i: (i, 0)),
        out_shape=jax.ShapeDtypeStruct((m_rows, d), F32),
        compiler_params=_params("arbitrary"),
        name="proj_out1",
    )(x2d, o_t, g, mod, final_w, w_bf)


def _ctx_layer_kernel(x_ref, mod_ref, nw_ref, wi_ref, wo_ref, fw_ref, y_ref, ck_ref, cv_ref):
    l, d = x_ref.shape
    m = mod_ref[0]
    x = x_ref[...]
    hb = _bf(_rms(x, nw_ref[...]) * (1.0 + m[:, d:2 * d]) + m[:, 0:d])
    qb = _bf(_dot(hb, wi_ref[:, 0:d]) * (HD ** -0.5))
    k = _dot(hb, wi_ref[:, d:2 * d])
    v = _dot(hb, wi_ref[:, 2 * d:3 * d])
    g = _dot(hb, wi_ref[:, 3 * d:4 * d])
    nh = d // HD
    scores, vbs = [], []
    for h in range(nh):
        sl = slice(h * HD, (h + 1) * HD)
        kh = k[:, sl]
        vh = v[:, sl]
        ck_ref[0, 0, h] = kh
        cv_ref[0, 0, h] = vh
        vbs.append(_bf(vh))
        scores.append(_dot_nt(qb[:, sl], _bf(kh)))
    s_all = jnp.concatenate(scores, axis=0)
    pexp = jnp.exp(s_all - jnp.max(s_all, axis=-1, keepdims=True))
    den = jnp.sum(pexp, axis=-1, keepdims=True)
    pb = _bf(pexp)
    outs = [_dot(pb[h * l:(h + 1) * l], vbs[h]) / den[h * l:(h + 1) * l] for h in range(nh)]
    a = jnp.concatenate(outs, axis=1) * _silu(g)
    acc = _dot(_bf(a), wo_ref[...])
    y_ref[...] = _rms(x + m[:, 2 * d:3 * d] * acc, fw_ref[...])


def _ctx_layer(x2d, b, l, mod, norm_w, w_in_bf, w_out_bf, final_w):
    d = x2d.shape[1]
    nh = d // HD
    cache_spec = pl.BlockSpec((1, 1, nh, l, HD), lambda bi: (bi, 0, 0, 0, 0))
    cache_shape = jax.ShapeDtypeStruct((b, 1, nh, l, HD), F32)

    def const(a):
        nd = a.ndim
        return pl.BlockSpec(a.shape, lambda bi: (0,) * nd, pipeline_mode=pl.Buffered(1))

    return pl.pallas_call(
        _ctx_layer_kernel,
        grid=(b,),
        in_specs=[pl.BlockSpec((l, d), lambda bi: (bi, 0)),
                  pl.BlockSpec((1, 1, 3 * d), lambda bi: (0, 0, 0)),
                  const(norm_w), const(w_in_bf), const(w_out_bf), const(final_w)],
        out_specs=[pl.BlockSpec((l, d), lambda bi: (bi, 0)), cache_spec, cache_spec],
        out_shape=[jax.ShapeDtypeStruct((b * l, d), F32), cache_shape, cache_shape],
        compiler_params=_params("arbitrary"),
        name="ctx_layer",
    )(x2d, mod, norm_w, w_in_bf, w_out_bf, final_w)


def _na_bias_kernel(rpb_ref, o_ref):
    h = pl.program_id(0)
    ndr = 2 * WIN_H - 1
    ndc = 2 * WIN_W - 1
    ck = lax.broadcasted_iota(jnp.int32, (GRID_W, LANE), 0)
    lane = lax.broadcasted_iota(jnp.int32, (GRID_W, LANE), 1)
    cq = lane & (GRID_W - 1)
    first = lane < GRID_W
    dc = jnp.clip(ck - cq + (WIN_W - 1), 0, ndc - 1)
    col0 = jnp.clip(cq - WIN_W // 2, 0, GRID_W - WIN_W)
    col_in = (ck >= col0) & (ck < col0 + WIN_W)
    dc_is = [(dc == e) & col_in for e in range(ndc)]
    tables = []
    for dr in range(ndr):
        t = jnp.full((GRID_W, LANE), NEG_INF, F32)
        for e in range(ndc):
            t = jnp.where(dc_is[e], rpb_ref[(h * ndr + dr) * ndc + e], t)
        tables.append(t)
    for ip in range(NA_BAND):
        for ap in range(NA_QTILE // 2):
            x = ip - 2 * ap + (WIN_H - 1) - NA_QROWS // 2
            o_ref[0, ip * GRID_W:(ip + 1) * GRID_W, ap * LANE:(ap + 1) * LANE] = jnp.where(
                first, tables[x], tables[x - 1])


def _na_bias_tables(rpb):
    nh = rpb.shape[0]
    shape = (NA_BAND * GRID_W, NA_QTILE * GRID_W)
    return pl.pallas_call(
        _na_bias_kernel,
        grid=(nh,),
        in_specs=[pl.BlockSpec(memory_space=pltpu.SMEM)],
        out_specs=pl.BlockSpec((1,) + shape, lambda h: (h, 0, 0)),
        out_shape=jax.ShapeDtypeStruct((nh,) + shape, F32),
        compiler_params=_params("arbitrary"),
        name="na_bias",
    )(rpb.reshape(-1))


def _na_kernel(q_ref, kp_ref, kc_ref, kn_ref, vp_ref, vc_ref, vn_ref, ck_ref, cv_ref, bias_ref, o_ref, mask_scr,
               s_scr, *, n_rows):
    rb = pl.program_id(1)
    qrows = NA_QROWS
    nq = qrows * GRID_W
    half = (qrows // 2) * GRID_W
    qt = NA_QTILE * GRID_W
    nband = NA_BAND * GRID_W
    q = q_ref[0, 0] * (HD ** -0.5)
    kloc = jnp.concatenate([kp_ref[0, 0][nq - half:nq], kc_ref[0, 0], kn_ref[0, 0][0:half]], axis=0)
    vloc = jnp.concatenate([vp_ref[:, nq - half:nq], vc_ref[...], vn_ref[:, 0:half]], axis=1)

    @pl.when(pl.program_id(2) == 0)
    def _():
        for t in range(qrows // NA_QTILE):
            i = t * NA_QTILE + lax.broadcasted_iota(jnp.int32, (nband, qt), 0) // GRID_W
            a = t * NA_QTILE + lax.broadcasted_iota(jnp.int32, (nband, qt), 1) // GRID_W
            r = rb * qrows + a
            kr = rb * qrows - qrows // 2 + i
            rs = jnp.clip(r - WIN_H // 2, 0, n_rows - WIN_H)
            mask_scr[t] = jnp.where((kr >= rs) & (kr < rs + WIN_H), 0.0, NEG_INF)

    rows = []
    for j in range(LANE // HD):
        sl = slice(j * HD, (j + 1) * HD)
        ckb = _bf(ck_ref[0, j])
        cvt = cv_ref[0, 0, sl, :]
        tiles = []
        nctx = ckb.shape[0]
        for t in range(qrows // NA_QTILE):
            k0 = t * NA_QTILE * GRID_W
            qh = q[t * qt:(t + 1) * qt, sl]
            m = jnp.full((1, qt), NEG_INF, F32)
            for c0 in range(0, nband + nctx, NA_KCHUNK):
                if c0 < nband:
                    rs_ = slice(c0, c0 + NA_KCHUNK)
                    s = _dot_nt(kloc[k0 + c0:k0 + c0 + NA_KCHUNK, sl], qh) + bias_ref[j, rs_, :] + mask_scr[t, rs_, :]
                else:
                    s = _dot_nt(ckb[c0 - nband:c0 - nband + NA_KCHUNK], qh)
                s_scr[c0:c0 + NA_KCHUNK, :] = s
                m = jnp.maximum(m, jnp.max(s, axis=0, keepdims=True))
            den = jnp.zeros((1, qt), F32)
            o = jnp.zeros((HD, qt), F32)
            for c0 in range(0, nband + nctx, NA_KCHUNK):
                pexp = jnp.exp(s_scr[c0:c0 + NA_KCHUNK, :] - m)
                den = den + jnp.sum(pexp, axis=0, keepdims=True)
                if c0 < nband:
                    vt = vloc[sl, k0 + c0:k0 + c0 + NA_KCHUNK]
                else:
                    vt = cvt[:, c0 - nband:c0 - nband + NA_KCHUNK]
                o = o + _dot(vt, _bf(pexp))
            tiles.append(o / den)
        rows.append(jnp.concatenate(tiles, axis=1))
    o_ref[0, 0] = jnp.concatenate(rows, axis=0).T.astype(o_ref.dtype)


def _na_attn(q_t, k_t, v_c, cache_k, cache_vt, bias, b, l):
    npair = q_t.shape[0]
    hpp = LANE // HD
    n_rows = l // GRID_W
    nrb = n_rows // NA_QROWS
    nq = NA_QROWS * GRID_W
    lc = cache_k.shape[2]
    q4, k4 = (a.reshape(npair, b, l, LANE) for a in (q_t, k_t))

    def prev_blk(rb):
        return jnp.maximum(rb - 1, 0)

    def next_blk(rb):
        return jnp.minimum(rb + 1, nrb - 1)

    def same_blk(rb):
        return rb

    def tok(f):
        return pl.BlockSpec((1, 1, nq, LANE), lambda bi, rb, hp: (hp, bi, f(rb), 0))

    def chan(f):
        return pl.BlockSpec((LANE, nq), lambda bi, rb, hp: (hp, bi * nrb + f(rb)))

    out = pl.pallas_call(
        functools.partial(_na_kernel, n_rows=n_rows),
        grid=(b, nrb, npair),
        in_specs=[tok(same_blk), tok(prev_blk), tok(same_blk), tok(next_blk),
                  chan(prev_blk), chan(same_blk), chan(next_blk),
                  pl.BlockSpec((1, hpp, lc, HD), lambda bi, rb, hp: (bi, hp, 0, 0)),
                  pl.BlockSpec((1, 1, LANE, lc), lambda bi, rb, hp: (bi, hp, 0, 0)),
                  pl.BlockSpec((hpp,) + bias.shape[1:], lambda bi, rb, hp: (hp, 0, 0))],
        out_specs=tok(same_blk),
        out_shape=jax.ShapeDtypeStruct((npair, b, l, LANE), BF16),
        scratch_shapes=[pltpu.VMEM((NA_QROWS // NA_QTILE,) + bias.shape[1:], F32),
                        pltpu.VMEM((bias.shape[1] + lc, bias.shape[2]), F32)],
        compiler_params=_params("arbitrary", "arbitrary", "arbitrary"),
        name="na_attn",
    )(q4, k4, k4, k4, v_c, v_c, v_c, cache_k, cache_vt, bias)
    return out.reshape(npair, b * l, LANE)


def _ctx_layer0_kernel(x_ref, mod_ref, nw_ref, w_ref, cwa_ref, cba_ref, cwb_ref, cbb_ref, dtb_ref, alog_ref,
                       dskip_ref, ex_ref, naw_ref, wo_ref, gs_ref, hyb_ref, fa_ref, iv_ref, o_ref, fin_ref,
                       *, seq_len, cols):
    tm, d = x_ref.shape
    o_z, o_xbc, o_u, o_g, o_dt = cols
    d_a, d_xbc, d_b = o_xbc - o_z, o_u - o_xbc, o_dt - o_g
    nseq, q, p = tm // seq_len, CHUNK, seq_len
    nc = seq_len // q
    step = 4 * LANE
    m = mod_ref[0]
    x = x_ref[...]
    hb = _bf(_rms(x, nw_ref[...]) * (1.0 + m[:, d:2 * d]) + m[:, 0:d])

    def conv(off, cw_ref, cb_ref, coff):
        return _conv3_rows(_dot(hb, w_ref[:, off:off + step]), 0.0, 0.0, seq_len,
                           cw_ref[:, coff:coff + step], cb_ref[:, coff:coff + step])

    xs = jnp.concatenate([_silu(conv(o_xbc + c0, cwa_ref, cba_ref, c0)) for c0 in range(0, d_xbc, step)], axis=1)
    dt_raw = _dot(hb, w_ref[:, o_dt:o_dt + LANE])
    xa = xs[:, 0:d_a]
    bm = _bf(xs[:, d_a:d_a + N_GROUPS * N_STATE])
    cm = _bf(xs[:, d_a + N_GROUPS * N_STATE:d_xbc])
    dtb, alog = dtb_ref[...], alog_ref[...]
    y_rows = []
    for s in range(nseq):
        state = [jnp.zeros((N_STATE, d_a), F32), jnp.zeros((N_STATE, d_a), F32)]
        y_chunks = [None] * nc
        for direction, order in ((0, range(nc)), (1, reversed(range(nc)))):
            for c in order:
                rows = slice(s * seq_len + c * q, s * seq_len + (c + 1) * q)
                y, state[direction] = _ssd_chunk(direction, xa[rows], bm[rows], cm[rows], dt_raw[rows],
                                                 state[direction], dtb, alog,
                                                 dskip_ref[direction:direction + 1, :], ex_ref[direction])
                y_chunks[c] = y if y_chunks[c] is None else y_chunks[c] + y
        for direction in range(2):
            for t in range(d_a // LANE):
                fin_ref[s, direction, t * LANE:(t + 1) * LANE, :] = state[direction][:, t * LANE:(t + 1) * LANE].T
        y_rows += y_chunks
    z = jnp.concatenate([_dot(hb, w_ref[:, o_z + c0:o_z + c0 + step]) for c0 in range(0, d_a, step)], axis=1)
    ya = _rms(jnp.concatenate(y_rows, axis=0) * _silu(z), naw_ref[...])

    yh_cols = []
    for c0 in range(0, d_b, step):
        wv = conv(o_u + d_b + c0, cwb_ref, cbb_ref, d_b + c0) * conv(o_u + 2 * d_b + c0, cwb_ref, cbb_ref, 2 * d_b + c0)
        gate = conv(o_u + c0, cwb_ref, cbb_ref, c0) * _silu(_dot(hb, w_ref[:, o_g + c0:o_g + c0 + step]))
        spec = jnp.concatenate([gs_ref[c0 // LANE + t, 0] for t in range(step // LANE)], axis=1)
        g_re, g_im = spec[0:p], spec[p:]
        outs = []
        for s in range(nseq):
            rows = slice(s * seq_len, (s + 1) * seq_len)
            u = _dot(fa_ref[...], _bf(wv[rows]))
            u_re, u_im = u[0:p], u[p:]
            prod = jnp.concatenate([g_re * u_re - g_im * u_im, g_re * u_im + g_im * u_re], axis=0)
            lc = _dot(iv_ref[...], _bf(prod))
            outs.append(gate[rows] * (lc + wv[rows] * hyb_ref[:, c0:c0 + step]))
        yh_cols.append(jnp.concatenate(outs, axis=0))
    yh = jnp.concatenate(yh_cols, axis=1)

    acc = _dot(_bf(ya), wo_ref[0:d_a, :]) + _dot(_bf(yh), wo_ref[d_a:, :])
    o_ref[...] = x + m[:, 2 * d:3 * d] * acc


def _ctx_layer0(x2d, b, l, mod, norm_w, w_in_bf, cols, p, w_out_bf, filt_params):
    m_rows, d = x2d.shape
    tm = ROW_TILE
    assert m_rows % tm == 0 and tm % l == 0 and l <= DFT_BLOCK and l % CHUNK == 0
    nseq = tm // l
    dm = N_HEADS * HD
    fa, iv = _dft_tables(l)
    spectra = _hyena_spectra(l, l, fa, *filt_params)
    ex = np.zeros((2, LANE, dm), np.float32)
    for direction in range(2):
        for h in range(N_HEADS):
            ex[direction, direction * N_HEADS + h, h * HD:(h + 1) * HD] = 1.0
    ex = jnp.asarray(ex, BF16)
    pad = LANE - 2 * N_HEADS
    dtb = jnp.pad(p["dt_bias"].reshape(1, 2 * N_HEADS), ((0, 0), (0, pad)))
    alog = jnp.pad(p["a_log"].reshape(1, 2 * N_HEADS), ((0, 0), (0, pad)))
    dsk = jnp.repeat(p["d_skip"], HD, axis=1)
    consts = [norm_w, w_in_bf, p["conv_a_w"], p["conv_a_b"].reshape(1, -1), p["conv_b_w"], p["conv_b_b"].reshape(1, -1),
              dtb, alog, dsk, ex, p["norm_a_w"], w_out_bf, spectra, p["hy_bias"].reshape(1, -1), fa, iv]

    def const(a):
        nd = a.ndim
        return pl.BlockSpec(a.shape, lambda i: (0,) * nd, pipeline_mode=pl.Buffered(1))

    return pl.pallas_call(
        functools.partial(_ctx_layer0_kernel, seq_len=l, cols=cols),
        grid=(m_rows // tm,),
        in_specs=[pl.BlockSpec((tm, d), lambda i: (i, 0)), pl.BlockSpec((1, 1, 3 * d), lambda i: (0, 0, 0))]
                 + [const(a) for a in consts],
        out_specs=[pl.BlockSpec((tm, d), lambda i: (i, 0)),
                   pl.BlockSpec((nseq, 2, dm, N_STATE), lambda i: (i, 0, 0, 0))],
        out_shape=[jax.ShapeDtypeStruct((m_rows, d), F32), jax.ShapeDtypeStruct((b, 2, dm, N_STATE), F32)],
        compiler_params=_params("arbitrary"),
        name="ctx_layer0",
    )(x2d, mod, *consts)


def _reorder_kernel(w_ref, o_ref, *, o_dt, n_dt):
    rows, n = w_ref.shape
    rest = n - o_dt - n_dt
    o_ref[:, 0:o_dt] = w_ref[:, 0:o_dt].astype(o_ref.dtype)
    o_ref[:, o_dt:o_dt + rest] = w_ref[:, o_dt + n_dt:n].astype(o_ref.dtype)
    tail = jnp.concatenate([w_ref[:, o_dt:o_dt + n_dt], jnp.zeros((rows, LANE - n_dt), F32)], axis=1)
    o_ref[:, o_dt + rest:o_dt + rest + LANE] = tail.astype(o_ref.dtype)


def _reorder_w_in(w, o_dt, n_dt):
    d, n = w.shape
    n_out = n - n_dt + LANE
    tr = LANE
    return pl.pallas_call(
        functools.partial(_reorder_kernel, o_dt=o_dt, n_dt=n_dt),
        grid=(d // tr,),
        in_specs=[pl.BlockSpec((tr, n), lambda i: (i, 0))],
        out_specs=pl.BlockSpec((tr, n_out), lambda i: (i, 0)),
        out_shape=jax.ShapeDtypeStruct((d, n_out), BF16),
        compiler_params=_params("arbitrary"),
        name="reorder_w_in",
    )(w)


def _layer0_cols(d_b):
    dm = N_HEADS * HD
    d_xbc = dm + 2 * N_GROUPS * N_STATE
    return (0, dm, dm + d_xbc, dm + d_xbc + 3 * d_b, dm + d_xbc + 4 * d_b)


def _layer0(x2d, b, l, mod, rows_per_mod, mod_base, norm_w, w_in_bf, w_out_bf, p, init, want_final, filt_params):
    dm = N_HEADS * HD
    d_xbc = dm + 2 * N_GROUPS * N_STATE
    cols = _layer0_cols(p["hy_bias"].shape[0])
    z, xs, w_t, gate_t, dt_raw = _proj_in0(x2d, mod, norm_w, w_in_bf, cols, p["conv_a_w"], p["conv_a_b"],
                                           p["conv_b_w"], p["conv_b_b"], l, rows_per_mod, mod_base)
    res = _ssd(xs.reshape(b, l, d_xbc), dt_raw.reshape(b, l, LANE), p["dt_bias"], p["a_log"], p["d_skip"],
               init, want_final)
    y_f, y_b = res[0].reshape(b * l, dm), res[1].reshape(b * l, dm)
    yh_t = _hyena(w_t, gate_t, filt_params, p["hy_bias"], b, l)
    x_new = _proj_out0(x2d, y_f, y_b, z, yh_t, mod, p["norm_a_w"], w_out_bf, rows_per_mod, mod_base)
    return x_new, (res[2] if want_final else None)


def kernel(x_prompt, x_sample, state_ssd, cache_k, cache_v, c, c_ctx, norm_w, w_ada, b_ada, w_in_e, w_out_e, conv_a_w, conv_a_b, dt_bias, a_log, d_skip, norm_a_w, conv_b_w, conv_b_b, hf_w1, hf_b1, hf_w2, hf_b2, hf_w3, hf_freq, hy_bias, w_in_o, w_out_o, rpb, final_norm_w):
    bp, lp, d = x_prompt.shape
    bs, ls, _ = x_sample.shape
    dm = N_HEADS * HD
    d_xbc = dm + 2 * N_GROUPS * N_STATE
    n_dt = 2 * N_HEADS

    cvecs = jnp.concatenate([c_ctx[None], c, jnp.zeros((SUBLANE - 1 - bs, d), F32)], axis=0)
    mods = _ada_mods(cvecs, w_ada, b_ada)

    xp = x_prompt.reshape(bp * lp, d)
    xs = x_sample.reshape(bs * ls, d)

    w_in0 = _reorder_w_in(w_in_e[0], dm + d_xbc, n_dt)
    w_out0 = w_out_e[0].astype(BF16)
    p0 = dict(conv_a_w=conv_a_w[0], conv_a_b=conv_a_b[0], dt_bias=dt_bias[0], a_log=a_log[0], d_skip=d_skip[0],
              norm_a_w=norm_a_w[0].reshape(1, -1), conv_b_w=conv_b_w[0], conv_b_b=conv_b_b[0], hy_bias=hy_bias[0])
    mod0 = mods[0].reshape(SUBLANE, 1, 3 * d)
    nw0 = norm_w[0].reshape(1, d)
    hf = (hf_w1[0], hf_b1[0], hf_w2[0], hf_b2[0], hf_w3[0], hf_freq[0])
    xp, fin = _ctx_layer0(xp, bp, lp, mod0, nw0, w_in0, _layer0_cols(hy_bias.shape[1]), p0, w_out0, hf)
    init_s = state_ssd[:, 0].reshape(bs, 2, dm, N_STATE)
    xs, _ = _layer0(xs, bs, ls, mod0, ls, 1, nw0, w_in0, w_out0, p0, init_s, False, hf)
    new_state_ssd = fin.reshape(bp, 1, 2, N_HEADS, HD, N_STATE)

    w_in1 = w_in_o[0].astype(BF16)
    w_out1 = w_out_o[0].astype(BF16)
    mod1 = mods[1].reshape(SUBLANE, 1, 3 * d)
    nw1 = norm_w[1].reshape(1, d)
    fw = final_norm_w.reshape(1, d)
    y_prompt, new_cache_k, new_cache_v = _ctx_layer(xp, bp, lp, mod1, nw1, w_in1, w_out1, fw)
    y_prompt = y_prompt.reshape(bp, lp, d)

    segs_s = ((0, d, TILED, BF16), (d, d, TILED, BF16), (0, d, CHAN, BF16), (3 * d, d, ROWS, BF16))
    wv_t = w_in_o[0][:, 2 * d:3 * d].T.astype(BF16)
    q_t, k_t, v_c, g = _proj_in(xs, mod1, nw1, w_in1, segs_s, ls, 1, w_t=wv_t)
    bias = _na_bias_tables(rpb[0])
    lc = cache_v.shape[3]
    cache_vt = jnp.swapaxes(cache_v[:, 0], 2, 3).reshape(bs, d // LANE, LANE, lc).astype(BF16)
    o_t = _na_attn(q_t, k_t, v_c, cache_k[:, 0], cache_vt, bias, bs, ls)
    y_sample = _proj_out1(xs, o_t, g, mod1, fw, w_out1, ls, 1).reshape(bs, ls, d)

    return (y_prompt, y_sample, new_state_ssd, new_cache_k, new_cache_v)
```

```python
import functools
import math

import jax
import jax.numpy as jnp
import numpy as np
from jax import lax
from jax.experimental import pallas as pl
from jax.experimental.pallas import tpu as pltpu

F32 = jnp.float32
BF16 = jnp.bfloat16

EPS = 1e-6
GRID_W = 64
WIN_H = 8
WIN_W = 16
HD = 64
N_HEADS = 16
N_STATE = 128
N_GROUPS = 2
CHUNK = 128
HY_EMB = 33
HY_BANDS = (HY_EMB - 1) // 2
HY_HID = 64
HY_TARGET = 1e-2
HY_DECAY_PCT_HI = 0.3
HY_DECAY_PCT_LO = 1.5

LANE = 128
SUBLANE = 8
VMEM_LIMIT = 56 * 1024 * 1024

ROW_TILE = 512
DFT_BLOCK = 512
HY_BATCH = 32
HY_BATCH_LONG = 2
NA_QROWS = 8
NA_QTILE = 4
NA_KCHUNK = 128
NA_BAND = NA_QTILE + WIN_H
NEG_INF = float("-inf")


def _bf(x):
    return x.astype(BF16)


def _dot(a, b):
    return jnp.dot(a, b, preferred_element_type=F32)


def _dot_nt(a, b):
    return lax.dot_general(a, b, (((1,), (1,)), ((), ())), preferred_element_type=F32)


def _split2(x):
    hi = _bf(x)
    lo = _bf(x - hi.astype(F32))
    return hi, lo


def _split3(x):
    hi = _bf(x)
    r = x - hi.astype(F32)
    mid = _bf(r)
    lo = _bf(r - mid.astype(F32))
    return hi, mid, lo


def _dot3(a, b):
    ah, al = _split2(a)
    bh, bl = _split2(b)
    return _dot(ah, bh) + (_dot(ah, bl) + _dot(al, bh))


def _dot_rhs_parts(a_exact, b, parts):
    pieces = _split3(b) if parts == 3 else _split2(b)
    acc = _dot(a_exact, pieces[0])
    for p in pieces[1:]:
        acc = acc + _dot(a_exact, p)
    return acc


def _silu(x):
    return x * jax.nn.sigmoid(x)


def _rms(x, g):
    ms = jnp.mean(x * x, axis=-1, keepdims=True)
    return x * lax.rsqrt(ms + EPS) * g


def _softplus(x):
    return jnp.maximum(x, 0.0) + jnp.log1p(jnp.exp(-jnp.abs(x)))


def _params(*sem):
    return pltpu.CompilerParams(dimension_semantics=sem, vmem_limit_bytes=VMEM_LIMIT)


def _mods_kernel(c_ref, w_ref, b_ref, o_ref):
    a = _silu(c_ref[...])
    o_ref[0] = _dot3(a, w_ref[0]) + b_ref[0]


def _ada_mods(cvecs, w_ada, b_ada):
    depth, d, n3 = w_ada.shape
    tn = n3 // 4
    return pl.pallas_call(
        _mods_kernel,
        grid=(depth, n3 // tn),
        in_specs=[pl.BlockSpec((SUBLANE, d), lambda l, j: (0, 0)),
                  pl.BlockSpec((1, d, tn), lambda l, j: (l, 0, j)),
                  pl.BlockSpec((1, 1, tn), lambda l, j: (l, 0, j))],
        out_specs=pl.BlockSpec((1, SUBLANE, tn), lambda l, j: (l, 0, j)),
        out_shape=jax.ShapeDtypeStruct((depth, SUBLANE, n3), F32),
        compiler_params=_params("arbitrary", "arbitrary"),
        name="ada_mods",
    )(cvecs, w_ada, b_ada.reshape(depth, 1, n3))


ROWS, TILED, CHAN = "rows", "tiled", "chan"


def _proj_in_kernel(x_ref, mod_ref, nw_ref, w_ref, *refs, segs, has_wt):
    wt_ref = refs[0] if has_wt else None
    out_refs = refs[1:] if has_wt else refs
    d = x_ref.shape[1]
    m = mod_ref[0]
    h = _rms(x_ref[...], nw_ref[...]) * (1.0 + m[:, d:2 * d]) + m[:, 0:d]
    hb = _bf(h)
    for (off, width, layout), o_ref in zip(segs, out_refs):
        if layout == CHAN:
            o_ref[...] = _dot_nt(wt_ref[off:off + width, :], hb).astype(o_ref.dtype)
            continue
        step = 2 * LANE
        for c0 in range(0, width, step):
            cw = min(step, width - c0)
            res = _dot(hb, w_ref[:, off + c0:off + c0 + cw])
            if layout == TILED:
                for t in range(cw // LANE):
                    o_ref[(c0 // LANE) + t] = res[:, t * LANE:(t + 1) * LANE].astype(o_ref.dtype)
            else:
                o_ref[:, c0:c0 + cw] = res.astype(o_ref.dtype)


def _proj_in(x2d, mod, norm_w, w_bf, segs, rows_per_mod, mod_base, w_t=None):
    m_rows, d = x2d.shape
    tm = ROW_TILE
    assert m_rows % tm == 0 and rows_per_mod % tm == 0
    out_shapes, out_specs = [], []
    for (_, width, layout, dt) in segs:
        if layout == TILED:
            out_shapes.append(jax.ShapeDtypeStruct((width // LANE, m_rows, LANE), dt))
            out_specs.append(pl.BlockSpec((width // LANE, tm, LANE), lambda i: (0, i, 0)))
        elif layout == CHAN:
            out_shapes.append(jax.ShapeDtypeStruct((width, m_rows), dt))
            out_specs.append(pl.BlockSpec((width, tm), lambda i: (0, i)))
        else:
            out_shapes.append(jax.ShapeDtypeStruct((m_rows, width), dt))
            out_specs.append(pl.BlockSpec((tm, width), lambda i: (i, 0)))
    has_wt = w_t is not None
    kern = functools.partial(_proj_in_kernel, segs=tuple((o, w, t) for (o, w, t, _) in segs), has_wt=has_wt)
    weights = [w_bf, w_t] if has_wt else [w_bf]
    return pl.pallas_call(
        kern,
        grid=(m_rows // tm,),
        in_specs=[pl.BlockSpec((tm, d), lambda i: (i, 0)),
                  pl.BlockSpec((1, 1, 3 * d), lambda i: (mod_base + (i * tm) // rows_per_mod, 0, 0)),
                  pl.BlockSpec((1, d), lambda i: (0, 0))]
                 + [pl.BlockSpec(w.shape, lambda i: (0, 0), pipeline_mode=pl.Buffered(1)) for w in weights],
        out_specs=out_specs,
        out_shape=out_shapes,
        compiler_params=_params("arbitrary"),
        name="proj_in",
    )(x2d, mod, norm_w, *weights)


def _conv3_rows(res, prev_row, next_row, seq_len, cw, cb):
    tm, width = res.shape
    starts = list(range(0, tm, seq_len))
    ends = [min(s + seq_len, tm) - 1 for s in starts]
    sub = lax.broadcasted_iota(jnp.int32, (SUBLANE, width), 0)
    down = pltpu.roll(res, 1, 0)
    up = pltpu.roll(res, tm - 1, 0)
    dparts, uparts, pos = [], [], 0
    for s in starts:
        fill = prev_row if s == 0 else 0.0
        dparts += [down[pos:s], jnp.where(sub == 0, fill, down[s:s + SUBLANE])]
        pos = s + SUBLANE
    dparts.append(down[pos:tm])
    pos = 0
    for e in ends:
        fill = next_row if e == tm - 1 else 0.0
        uparts += [up[pos:e + 1 - SUBLANE], jnp.where(sub == SUBLANE - 1, fill, up[e + 1 - SUBLANE:e + 1])]
        pos = e + 1
    uparts.append(up[pos:tm])
    down = jnp.concatenate([p for p in dparts if p.shape[0]], axis=0)
    up = jnp.concatenate([p for p in uparts if p.shape[0]], axis=0)
    return cb + down * cw[0:1] + res * cw[1:2] + up * cw[2:3]


def _proj_in0_kernel(x_ref, xp_ref, xn_ref, mod_ref, nw_ref, w_ref, cwa_ref, cba_ref, cwb_ref, cbb_ref,
                     z_ref, xs_ref, wv_ref, gate_ref, dt_ref, *, seq_len, cols):
    i = pl.program_id(0)
    tm, d = x_ref.shape
    o_z, o_xbc, o_u, o_g, o_dt = cols
    d_a, d_xbc, d_b = o_xbc - o_z, o_u - o_xbc, o_dt - o_g
    m = mod_ref[0]

    def modnorm(x):
        return _bf(_rms(x, nw_ref[...]) * (1.0 + m[:, d:2 * d]) + m[:, 0:d])

    hb = modnorm(x_ref[...])
    hh = modnorm(jnp.concatenate([xp_ref[...], xn_ref[...]], axis=0))
    keep_prev = ((i * tm) % seq_len != 0).astype(F32)
    keep_next = (((i + 1) * tm) % seq_len != 0).astype(F32)

    def conv(off, width, cw_ref, cb_ref, coff):
        res = _dot(hb, w_ref[:, off:off + width])
        rh = _dot(hh, w_ref[:, off:off + width])
        return _conv3_rows(res, rh[SUBLANE - 1:SUBLANE] * keep_prev, rh[SUBLANE:SUBLANE + 1] * keep_next, seq_len,
                           cw_ref[:, coff:coff + width], cb_ref[:, coff:coff + width])

    step = 2 * LANE
    for c0 in range(0, d_a, step):
        z_ref[:, c0:c0 + step] = _dot(hb, w_ref[:, o_z + c0:o_z + c0 + step]).astype(z_ref.dtype)
    for c0 in range(0, d_xbc, step):
        xs_ref[:, c0:c0 + step] = _silu(conv(o_xbc + c0, step, cwa_ref, cba_ref, c0)).astype(xs_ref.dtype)
    for c0 in range(0, d_b, step):
        x1 = conv(o_u + d_b + c0, step, cwb_ref, cbb_ref, d_b + c0)
        v = conv(o_u + 2 * d_b + c0, step, cwb_ref, cbb_ref, 2 * d_b + c0)
        wv = x1 * v
        x0 = conv(o_u + c0, step, cwb_ref, cbb_ref, c0)
        gate = x0 * _silu(_dot(hb, w_ref[:, o_g + c0:o_g + c0 + step]))
        for t in range(step // LANE):
            wv_ref[c0 // LANE + t] = wv[:, t * LANE:(t + 1) * LANE].astype(wv_ref.dtype)
            gate_ref[c0 // LANE + t] = gate[:, t * LANE:(t + 1) * LANE].astype(gate_ref.dtype)
    dt_ref[...] = _dot(hb, w_ref[:, o_dt:o_dt + LANE])


def _proj_in0(x2d, mod, norm_w, w_bf, cols, conv_a_w, conv_a_b, conv_b_w, conv_b_b, seq_len, rows_per_mod, mod_base):
    m_rows, d = x2d.shape
    tm = ROW_TILE
    assert m_rows % tm == 0 and rows_per_mod % tm == 0
    assert tm % seq_len == 0 or seq_len % tm == 0
    o_z, o_xbc, o_u, o_g, o_dt = cols
    d_a, d_xbc, d_b = o_xbc - o_z, o_u - o_xbc, o_dt - o_g
    nsub = m_rows // SUBLANE
    spt = tm // SUBLANE
    cba, cbb = conv_a_b.reshape(1, -1), conv_b_b.reshape(1, -1)

    def const(a):
        return pl.BlockSpec(a.shape, lambda i: (0, 0), pipeline_mode=pl.Buffered(1))

    def tiles(n):
        return pl.BlockSpec((n, tm, LANE), lambda i: (0, i, 0))

    return pl.pallas_call(
        functools.partial(_proj_in0_kernel, seq_len=seq_len, cols=cols),
        grid=(m_rows // tm,),
        in_specs=[pl.BlockSpec((tm, d), lambda i: (i, 0)),
                  pl.BlockSpec((SUBLANE, d), lambda i: (jnp.maximum(i * spt - 1, 0), 0)),
                  pl.BlockSpec((SUBLANE, d), lambda i: (jnp.minimum((i + 1) * spt, nsub - 1), 0)),
                  pl.BlockSpec((1, 1, 3 * d), lambda i: (mod_base + (i * tm) // rows_per_mod, 0, 0)),
                  const(norm_w), const(w_bf), const(conv_a_w), const(cba), const(conv_b_w), const(cbb)],
        out_specs=[pl.BlockSpec((tm, d_a), lambda i: (i, 0)), pl.BlockSpec((tm, d_xbc), lambda i: (i, 0)),
                   tiles(d_b // LANE), tiles(d_b // LANE), pl.BlockSpec((tm, LANE), lambda i: (i, 0))],
        out_shape=[jax.ShapeDtypeStruct((m_rows, d_a), BF16), jax.ShapeDtypeStruct((m_rows, d_xbc), BF16),
                   jax.ShapeDtypeStruct((d_b // LANE, m_rows, LANE), BF16),
                   jax.ShapeDtypeStruct((d_b // LANE, m_rows, LANE), BF16),
                   jax.ShapeDtypeStruct((m_rows, LANE), F32)],
        compiler_params=_params("arbitrary"),
        name="proj_in0",
    )(x2d, x2d, x2d, mod, norm_w, w_bf, conv_a_w, cba, conv_b_w, cbb)


def _ssd_chunk(d, xa, bm, cm, dt_raw, s_prev, dtb, alog, dskip, ex):
    q, dm = xa.shape
    gw = dm // N_GROUPS
    hpg = N_HEADS // N_GROUPS
    row = lax.broadcasted_iota(jnp.int32, (q, q), 0)
    col = lax.broadcasted_iota(jnp.int32, (q, q), 1)
    left = lax.broadcasted_iota(jnp.int32, (q, LANE), 1) < HD
    tri = (row >= col) if d == 0 else (row <= col)

    dt = _softplus(dt_raw + dtb)
    adt = dt * (-jnp.exp(alog))
    cs = _dot_rhs_parts(jnp.where(tri, 1.0, 0.0).astype(BF16), adt, 3)
    cs_t = cs.T
    dt_t = dt.T
    edge = cs[q - 1:q, :] if d == 0 else cs[0:1, :]
    e_cs = _dot(_bf(jnp.exp(cs)), ex)
    w_st = _dot(_bf(jnp.exp(edge - cs) * dt), ex)
    xw = xa * w_st

    y_parts = []
    new_state = []
    for g in range(N_GROUPS):
        bg = bm[:, g * N_STATE:(g + 1) * N_STATE]
        cg = cm[:, g * N_STATE:(g + 1) * N_STATE]
        gmat = _dot_nt(_bf(cg), _bf(bg))
        y_off = _dot(_bf(cg), _bf(s_prev[:, g * gw:(g + 1) * gw]))
        new_state.append(_dot(_bf(bg.astype(F32).T), _bf(xw[:, g * gw:(g + 1) * gw])))
        for pr in range(hpg // 2):
            mh = []
            for j in range(2):
                k = d * N_HEADS + g * hpg + 2 * pr + j
                diff = cs[:, k:k + 1] - cs_t[k:k + 1, :]
                lm = jnp.exp(jnp.where(tri, diff, NEG_INF))
                mh.append(_bf(gmat * lm * dt_t[k:k + 1, :]))
            c0 = g * gw + pr * LANE
            xpair = xa[:, c0:c0 + LANE]
            rhs = jnp.concatenate([_bf(jnp.where(left, xpair, 0.0)), _bf(jnp.where(left, 0.0, xpair))], axis=0)
            y_d = _dot(jnp.concatenate(mh, axis=1), rhs)
            y_parts.append(y_d + y_off[:, pr * LANE:(pr + 1) * LANE] * e_cs[:, c0:c0 + LANE])
    y = jnp.concatenate(y_parts, axis=1) + xa * dskip
    e_edge = e_cs[q - 1:q, :] if d == 0 else e_cs[0:1, :]
    return y, s_prev * e_edge + jnp.concatenate(new_state, axis=1)


def _ssd_kernel(*refs, has_init, want_final):
    it = iter(refs)
    xm = [next(it), next(it)]
    dtr = [next(it), next(it)]
    dtb_ref, alog_ref, dskip_ref, exp_ref = (next(it) for _ in range(4))
    init_ref = next(it) if has_init else None
    y_refs = [next(it), next(it)]
    fin_ref = next(it) if want_final else None
    s_ref = next(it)

    c = pl.program_id(1)
    nc = pl.num_programs(1)
    q = xm[0].shape[1]
    dm = N_HEADS * HD
    gw = dm // N_GROUPS
    hpg = N_HEADS // N_GROUPS

    @pl.when(c == 0)
    def _():
        if has_init:
            for d in range(2):
                for t in range(dm // LANE):
                    s_ref[d, :, t * LANE:(t + 1) * LANE] = init_ref[0, d, t * LANE:(t + 1) * LANE, :].T
        else:
            s_ref[...] = jnp.zeros(s_ref.shape, F32)

    for d in range(2):
        xa = xm[d][0, :, 0:dm].astype(F32)
        bm = xm[d][0, :, dm:dm + N_GROUPS * N_STATE]
        cm = xm[d][0, :, dm + N_GROUPS * N_STATE:dm + 2 * N_GROUPS * N_STATE]
        y, s_new = _ssd_chunk(d, xa, bm, cm, dtr[d][0], s_ref[d], dtb_ref[...], alog_ref[...],
                              dskip_ref[d:d + 1, :], exp_ref[d])
        y_refs[d][0] = y.astype(y_refs[d].dtype)
        s_ref[d] = s_new

    if want_final:
        @pl.when(c == nc - 1)
        def _():
            for d in range(2):
                for t in range(dm // LANE):
                    fin_ref[0, d, t * LANE:(t + 1) * LANE, :] = s_ref[d, :, t * LANE:(t + 1) * LANE].T


def _ssd(xs, dt_raw, dt_bias, a_log, d_skip, init, want_final):
    b, l, nchan = xs.shape
    dm = N_HEADS * HD
    q = CHUNK
    nc = l // q

    def full(a):
        nd = a.ndim
        return pl.BlockSpec(a.shape, lambda bi, c: (0,) * nd)

    ex = np.zeros((2, LANE, dm), np.float32)
    for d in range(2):
        for h in range(N_HEADS):
            ex[d, d * N_HEADS + h, h * HD:(h + 1) * HD] = 1.0
    ex = jnp.asarray(ex, BF16)
    pad = LANE - 2 * N_HEADS
    dtb = jnp.pad(dt_bias.reshape(1, 2 * N_HEADS), ((0, 0), (0, pad)))
    alog = jnp.pad(a_log.reshape(1, 2 * N_HEADS), ((0, 0), (0, pad)))
    dsk = jnp.repeat(d_skip, HD, axis=1)

    args = [xs, xs, dt_raw, dt_raw, dtb, alog, dsk, ex]
    in_specs = [pl.BlockSpec((1, q, nchan), lambda bi, c: (bi, c, 0)),
                pl.BlockSpec((1, q, nchan), lambda bi, c: (bi, nc - 1 - c, 0)),
                pl.BlockSpec((1, q, LANE), lambda bi, c: (bi, c, 0)),
                pl.BlockSpec((1, q, LANE), lambda bi, c: (bi, nc - 1 - c, 0)),
                full(dtb), full(alog), full(dsk), full(ex)]
    has_init = init is not None
    if has_init:
        args.append(init)
        in_specs.append(pl.BlockSpec((1, 2, dm, N_STATE), lambda bi, c: (bi, 0, 0, 0)))
    out_shapes = [jax.ShapeDtypeStruct((b, l, dm), BF16), jax.ShapeDtypeStruct((b, l, dm), BF16)]
    out_specs = [pl.BlockSpec((1, q, dm), lambda bi, c: (bi, c, 0)),
                 pl.BlockSpec((1, q, dm), lambda bi, c: (bi, nc - 1 - c, 0))]
    if want_final:
        out_shapes.append(jax.ShapeDtypeStruct((b, 2, dm, N_STATE), F32))
        out_specs.append(pl.BlockSpec((1, 2, dm, N_STATE), lambda bi, c: (bi, 0, 0, 0)))
    return pl.pallas_call(
        functools.partial(_ssd_kernel, has_init=has_init, want_final=want_final),
        grid=(b, nc),
        in_specs=in_specs,
        out_specs=out_specs,
        out_shape=out_shapes,
        scratch_shapes=[pltpu.VMEM((2, N_STATE, dm), F32)],
        compiler_params=_params("arbitrary", "arbitrary"),
        name="ssd_scan",
    )(*args)


def _hyena_tables(l):
    pos = np.abs(np.arange(2 * l, dtype=np.float64) - l)
    t = pos / (l - 1)
    w = 2.0 * math.pi * pos / l
    f = np.linspace(1e-4, HY_BANDS - 1, HY_BANDS)
    feats = np.zeros((2 * l, LANE), np.float64)
    feats[:, 0] = t
    feats[:, 1:1 + HY_BANDS] = np.cos(f[None] * w[:, None])
    feats[:, 1 + HY_BANDS:1 + 2 * HY_BANDS] = -np.sin(f[None] * w[:, None])
    return jnp.asarray(feats, F32)


def _dft_tables(p):
    n = 2 * p
    f = np.arange(p, dtype=np.float64)[:, None] + 0.5
    e = np.arange(p, dtype=np.float64)[None]
    ang = 2.0 * math.pi * f * e / n
    fa = np.concatenate([np.cos(ang), -np.sin(ang)], axis=0)
    inv = np.concatenate([np.cos(ang.T), -np.sin(ang.T)], axis=1) * (2.0 / n)
    return tuple(jnp.asarray(m, F32).astype(BF16) for m in (fa, inv))


def _spectra_kernel(f_ref, w1_ref, b1_ref, w2_ref, b2_ref, w3_ref, fr_ref, ad_ref, fa_ref, o_ref, bprev):
    q = pl.program_id(0)
    p = f_ref.shape[0]
    half = p // 2
    feats = f_ref[...]
    pre1 = _dot3(feats, w1_ref[...])
    packed = jnp.concatenate([pre1[:half], pre1[half:]], axis=1)
    fr = fr_ref[...]
    h1 = jnp.sin(fr * (packed + b1_ref[...]))
    h2 = _bf(jnp.sin(fr * (_dot3(h1, w2_ref[...]) + b2_ref[...])))
    w3 = _bf(w3_ref[...])
    filt = jnp.concatenate([_dot(h2[:, :HY_HID], w3), _dot(h2[:, HY_HID:], w3)], axis=0)
    taps = _bf(filt * jnp.exp(-feats[:, 0:1] * ad_ref[...]))
    a = _dot(fa_ref[...], taps)

    @pl.when(q > 0)
    def _():
        g = a + bprev[...]
        for t in range(o_ref.shape[0]):
            o_ref[t, 0] = g[:, t * LANE:(t + 1) * LANE]

    odd = (lax.broadcasted_iota(jnp.int32, (p, a.shape[1]), 0) & 1) == 1
    a_re, a_im = a[0:p], a[p:]
    a_re0 = a_re - taps[0:1, :].astype(F32)
    bprev[0:p] = jnp.where(odd, a_im, -a_im)
    bprev[p:] = jnp.where(odd, -a_re0, a_re0)


def _hyena_spectra(l, p, fa, w1, b1, w2, b2, w3, freq):
    db = w3.shape[1] // 2
    nblk = 2 * l // p
    nct = db // LANE
    feats = _hyena_tables(l)
    w1p = jnp.pad(w1, ((0, LANE - HY_EMB), (0, 0)))
    zero = jnp.zeros_like(w2)
    w2bd = jnp.concatenate([jnp.concatenate([w2, zero], axis=1), jnp.concatenate([zero, w2], axis=1)], axis=0)
    deltas = np.linspace(math.log(HY_TARGET) / HY_DECAY_PCT_HI, math.log(HY_TARGET) / HY_DECAY_PCT_LO, db)
    absd = jnp.asarray(np.abs(deltas)[None], F32)
    b1r, b2r, frr = (jnp.tile(v.reshape(1, -1), (1, 2)) for v in (b1, b2, freq))

    def full(a):
        return pl.BlockSpec(a.shape, lambda q: (0, 0))

    return pl.pallas_call(
        _spectra_kernel,
        grid=(nblk,),
        in_specs=[pl.BlockSpec((p, LANE), lambda q: (q, 0)), full(w1p), full(b1r), full(w2bd), full(b2r),
                  pl.BlockSpec((HY_HID, db), lambda q: (0, jnp.where(q < nblk // 2, 1, 0))),
                  full(frr), full(absd), full(fa)],
        out_specs=pl.BlockSpec((nct, 1, 2 * p, LANE), lambda q: (0, jnp.maximum(q - 1, 0), 0, 0)),
        out_shape=jax.ShapeDtypeStruct((nct, nblk - 1, 2 * p, LANE), F32),
        scratch_shapes=[pltpu.VMEM((2 * p, db), F32)],
        compiler_params=_params("arbitrary"),
        name="hyena_spectra",
    )(feats, w1p, b1r, w2bd, b2r, w3, frr, absd, fa)


def _hyena_kernel(w_ref, gate_ref, gs_ref, hb_ref, fa_ref, iv_ref, o_ref, u_scr, y_scr, *, p):
    bt, l = w_ref.shape[1], w_ref.shape[2]
    nb = l // p
    fa = fa_ref[...]
    for j in range(nb):
        rhs = jnp.concatenate([w_ref[0, bb, j * p:(j + 1) * p, :] for bb in range(bt)], axis=1)
        u_scr[j] = _dot(fa, rhs)
    rt_rows = 64
    for i in range(nb):
        def body(rt, carry):
            r0 = pl.multiple_of(rt * rt_rows, rt_rows)
            for bb in range(bt):
                ls = slice(bb * LANE, (bb + 1) * LANE)
                acc_re = jnp.zeros((rt_rows, LANE), F32)
                acc_im = jnp.zeros((rt_rows, LANE), F32)
                for j in range(nb):
                    s = i - j + nb - 1
                    gre = gs_ref[0, s, pl.ds(r0, rt_rows), :]
                    gim = gs_ref[0, s, pl.ds(p + r0, rt_rows), :]
                    ure = u_scr[j, pl.ds(r0, rt_rows), ls]
                    uim = u_scr[j, pl.ds(p + r0, rt_rows), ls]
                    acc_re = acc_re + (gre * ure - gim * uim)
                    acc_im = acc_im + (gre * uim + gim * ure)
                y_scr[pl.ds(r0, rt_rows), ls] = acc_re
                y_scr[pl.ds(p + r0, rt_rows), ls] = acc_im
            return carry
        lax.fori_loop(0, p // rt_rows, body, 0)
        conv = _dot(iv_ref[...], _bf(y_scr[...]))
        sl = slice(i * p, (i + 1) * p)
        for bb in range(bt):
            wi = w_ref[0, bb, sl, :].astype(F32)
            o_ref[0, bb, sl, :] = (gate_ref[0, bb, sl, :].astype(F32)
                                   * (conv[:, bb * LANE:(bb + 1) * LANE] + wi * hb_ref[...])).astype(o_ref.dtype)


def _hyena(w_t, gate_t, filt_params, hy_bias, b, l):
    nct = w_t.shape[0]
    p = min(DFT_BLOCK, l)
    nb = l // p
    nseg = 2 * nb - 1
    bt = min(b, HY_BATCH if nb == 1 else HY_BATCH_LONG)
    fa, iv = _dft_tables(p)
    spectra = _hyena_spectra(l, p, fa, *filt_params)
    w4 = w_t.reshape(nct, b, l, LANE)
    g4 = gate_t.reshape(nct, b, l, LANE)
    hbr = hy_bias.reshape(1, -1)
    act = pl.BlockSpec((1, bt, l, LANE), lambda ct, bi: (ct, bi, 0, 0))

    def full(a):
        return pl.BlockSpec(a.shape, lambda ct, bi: (0, 0), pipeline_mode=pl.Buffered(1))

    out = pl.pallas_call(
        functools.partial(_hyena_kernel, p=p),
        grid=(nct, b // bt),
        in_specs=[act, act,
                  pl.BlockSpec((1, nseg, 2 * p, LANE), lambda ct, bi: (ct, 0, 0, 0), pipeline_mode=pl.Buffered(1)),
                  pl.BlockSpec((1, LANE), lambda ct, bi: (0, ct)),
                  full(fa), full(iv)],
        out_specs=act,
        out_shape=jax.ShapeDtypeStruct((nct, b, l, LANE), BF16),
        scratch_shapes=[pltpu.VMEM((nb, 2 * p, bt * LANE), F32), pltpu.VMEM((2 * p, bt * LANE), F32)],
        compiler_params=_params("arbitrary", "arbitrary"),
        name="hyena_conv",
    )(w4, g4, spectra, hbr, fa, iv)
    return out.reshape(nct, b * l, LANE)


def _proj_out0_kernel(x_ref, yf_ref, yb_ref, z_ref, yh_ref, mod_ref, naw_ref, w_ref, o_ref):
    d = x_ref.shape[1]
    ya = _rms((yf_ref[...].astype(F32) + yb_ref[...].astype(F32)) * _silu(z_ref[...].astype(F32)), naw_ref[...])
    yh = jnp.concatenate([yh_ref[t] for t in range(yh_ref.shape[0])], axis=1)
    da = ya.shape[1]
    acc = _dot(_bf(ya), w_ref[0:da, :]) + _dot(_bf(yh), w_ref[da:, :])
    gate = mod_ref[0][:, 2 * d:3 * d]
    o_ref[...] = x_ref[...] + gate * acc


def _proj_out0(x2d, y_f, y_b, z, yh_t, mod, norm_a_w, w_bf, rows_per_mod, mod_base):
    m_rows, d = x2d.shape
    tm = ROW_TILE
    assert m_rows % tm == 0 and rows_per_mod % tm == 0
    da = y_f.shape[1]
    nt = yh_t.shape[0]

    def rowspec(wd):
        return pl.BlockSpec((tm, wd), lambda i: (i, 0))

    return pl.pallas_call(
        _proj_out0_kernel,
        grid=(m_rows // tm,),
        in_specs=[rowspec(d), rowspec(da), rowspec(da), rowspec(da),
                  pl.BlockSpec((nt, tm, LANE), lambda i: (0, i, 0)),
                  pl.BlockSpec((1, 1, 3 * d), lambda i: (mod_base + (i * tm) // rows_per_mod, 0, 0)),
                  pl.BlockSpec((1, da), lambda i: (0, 0)),
                  pl.BlockSpec(w_bf.shape, lambda i: (0, 0), pipeline_mode=pl.Buffered(1))],
        out_specs=rowspec(d),
        out_shape=jax.ShapeDtypeStruct((m_rows, d), F32),
        compiler_params=_params("arbitrary"),
        name="proj_out0",
    )(x2d, y_f, y_b, z, yh_t, mod, norm_a_w, w_bf)


def _proj_out1_kernel(x_ref, o_ref_in, g_ref, mod_ref, fw_ref, w_ref, y_ref):
    d = x_ref.shape[1]
    o = jnp.concatenate([o_ref_in[t] for t in range(o_ref_in.shape[0])], axis=1).astype(F32)
    a = o * _silu(g_ref[...].astype(F32))
    acc = _dot(_bf(a), w_ref[...])
    gate = mod_ref[0][:, 2 * d:3 * d]
    y_ref[...] = _rms(x_ref[...] + gate * acc, fw_ref[...])


def _proj_out1(x2d, o_t, g, mod, final_w, w_bf, rows_per_mod, mod_base):
    m_rows, d = x2d.shape
    tm = ROW_TILE
    assert m_rows % tm == 0 and rows_per_mod % tm == 0
    nt = o_t.shape[0]
    return pl.pallas_call(
        _proj_out1_kernel,
        grid=(m_rows // tm,),
        in_specs=[pl.BlockSpec((tm, d), lambda i: (i, 0)),
                  pl.BlockSpec((nt, tm, LANE), lambda i: (0, i, 0)),
                  pl.BlockSpec((tm, g.shape[1]), lambda i: (i, 0)),
                  pl.BlockSpec((1, 1, 3 * d), lambda i: (mod_base + (i * tm) // rows_per_mod, 0, 0)),
                  pl.BlockSpec((1, d), lambda i: (0, 0)),
                  pl.BlockSpec(w_bf.shape, lambda i: (0, 0), pipeline_mode=pl.Buffered(1))],
        out_specs=pl.BlockSpec((tm, d), lambda i: (i, 0)),
        out_shape=jax.ShapeDtypeStruct((m_rows, d), F32),
        compiler_params=_params("arbitrary"),
        name="proj_out1",
    )(x2d, o_t, g, mod, final_w, w_bf)


def _ctx_layer_kernel(x_ref, mod_ref, nw_ref, wi_ref, wo_ref, fw_ref, y_ref, ck_ref, cv_ref):
    tm, d = x_ref.shape
    nseq = ck_ref.shape[0]
    l = tm // nseq
    m = mod_ref[0]
    x = x_ref[...]
    hb = _bf(_rms(x, nw_ref[...]) * (1.0 + m[:, d:2 * d]) + m[:, 0:d])
    qb = _bf(_dot(hb, wi_ref[:, 0:d]) * (HD ** -0.5))
    k = _dot(hb, wi_ref[:, d:2 * d])
    v = _dot(hb, wi_ref[:, 2 * d:3 * d])
    g = _dot(hb, wi_ref[:, 3 * d:4 * d])
    nh = d // HD
    o_rows = []
    for s in range(nseq):
        rows = slice(s * l, (s + 1) * l)
        scores, vbs = [], []
        for h in range(nh):
            sl = slice(h * HD, (h + 1) * HD)
            kh = k[rows, sl]
            vh = v[rows, sl]
            ck_ref[s, 0, h] = kh
            cv_ref[s, 0, h] = vh
            vbs.append(_bf(vh))
            scores.append(_dot_nt(qb[rows, sl], _bf(kh)))
        s_all = jnp.concatenate(scores, axis=0)
        pexp = jnp.exp(s_all - jnp.max(s_all, axis=-1, keepdims=True))
        den = jnp.sum(pexp, axis=-1, keepdims=True)
        pb = _bf(pexp)
        o_rows.append(jnp.concatenate(
            [_dot(pb[h * l:(h + 1) * l], vbs[h]) / den[h * l:(h + 1) * l] for h in range(nh)], axis=1))
    a = jnp.concatenate(o_rows, axis=0) * _silu(g)
    acc = _dot(_bf(a), wo_ref[...])
    y_ref[...] = _rms(x + m[:, 2 * d:3 * d] * acc, fw_ref[...])


def _ctx_layer(x2d, b, l, mod, norm_w, w_in_bf, w_out_bf, final_w):
    m_rows, d = x2d.shape
    tm = ROW_TILE
    assert m_rows % tm == 0 and tm % l == 0
    nseq = tm // l
    nh = d // HD
    cache_spec = pl.BlockSpec((nseq, 1, nh, l, HD), lambda i: (i, 0, 0, 0, 0))
    cache_shape = jax.ShapeDtypeStruct((b, 1, nh, l, HD), F32)

    def const(a):
        nd = a.ndim
        return pl.BlockSpec(a.shape, lambda i: (0,) * nd, pipeline_mode=pl.Buffered(1))

    return pl.pallas_call(
        _ctx_layer_kernel,
        grid=(m_rows // tm,),
        in_specs=[pl.BlockSpec((tm, d), lambda i: (i, 0)),
                  pl.BlockSpec((1, 1, 3 * d), lambda i: (0, 0, 0)),
                  const(norm_w), const(w_in_bf), const(w_out_bf), const(final_w)],
        out_specs=[pl.BlockSpec((tm, d), lambda i: (i, 0)), cache_spec, cache_spec],
        out_shape=[jax.ShapeDtypeStruct((m_rows, d), F32), cache_shape, cache_shape],
        compiler_params=_params("arbitrary"),
        name="ctx_layer",
    )(x2d, mod, norm_w, w_in_bf, w_out_bf, final_w)


def _na_bias_kernel(rpb_ref, o_ref):
    h = pl.program_id(0)
    ndr = 2 * WIN_H - 1
    ndc = 2 * WIN_W - 1
    ck = lax.broadcasted_iota(jnp.int32, (GRID_W, LANE), 0)
    lane = lax.broadcasted_iota(jnp.int32, (GRID_W, LANE), 1)
    cq = lane & (GRID_W - 1)
    first = lane < GRID_W
    dc = jnp.clip(ck - cq + (WIN_W - 1), 0, ndc - 1)
    col0 = jnp.clip(cq - WIN_W // 2, 0, GRID_W - WIN_W)
    col_in = (ck >= col0) & (ck < col0 + WIN_W)
    dc_is = [(dc == e) & col_in for e in range(ndc)]
    tables = []
    for dr in range(ndr):
        t = jnp.full((GRID_W, LANE), NEG_INF, F32)
        for e in range(ndc):
            t = jnp.where(dc_is[e], rpb_ref[(h * ndr + dr) * ndc + e], t)
        tables.append(t)
    for ip in range(NA_BAND):
        for ap in range(NA_QTILE // 2):
            x = ip - 2 * ap + (WIN_H - 1) - NA_QROWS // 2
            o_ref[0, ip * GRID_W:(ip + 1) * GRID_W, ap * LANE:(ap + 1) * LANE] = jnp.where(
                first, tables[x], tables[x - 1])


def _na_bias_tables(rpb):
    nh = rpb.shape[0]
    shape = (NA_BAND * GRID_W, NA_QTILE * GRID_W)
    return pl.pallas_call(
        _na_bias_kernel,
        grid=(nh,),
        in_specs=[pl.BlockSpec(memory_space=pltpu.SMEM)],
        out_specs=pl.BlockSpec((1,) + shape, lambda h: (h, 0, 0)),
        out_shape=jax.ShapeDtypeStruct((nh,) + shape, F32),
        compiler_params=_params("arbitrary"),
        name="na_bias",
    )(rpb.reshape(-1))


def _na_kernel(q_ref, kp_ref, kc_ref, kn_ref, vp_ref, vc_ref, vn_ref, ck_ref, cv_ref, bias_ref, o_ref, mask_scr,
               s_scr, *, n_rows):
    rb = pl.program_id(1)
    qrows = NA_QROWS
    nq = qrows * GRID_W
    half = (qrows // 2) * GRID_W
    qt = NA_QTILE * GRID_W
    nband = NA_BAND * GRID_W
    q = q_ref[0, 0] * (HD ** -0.5)
    kloc = jnp.concatenate([kp_ref[0, 0][nq - half:nq], kc_ref[0, 0], kn_ref[0, 0][0:half]], axis=0)
    vloc = jnp.concatenate([vp_ref[:, nq - half:nq], vc_ref[...], vn_ref[:, 0:half]], axis=1)

    @pl.when(pl.program_id(2) == 0)
    def _():
        for t in range(qrows // NA_QTILE):
            i = t * NA_QTILE + lax.broadcasted_iota(jnp.int32, (nband, qt), 0) // GRID_W
            a = t * NA_QTILE + lax.broadcasted_iota(jnp.int32, (nband, qt), 1) // GRID_W
            r = rb * qrows + a
            kr = rb * qrows - qrows // 2 + i
            rs = jnp.clip(r - WIN_H // 2, 0, n_rows - WIN_H)
            mask_scr[t] = jnp.where((kr >= rs) & (kr < rs + WIN_H), 0.0, NEG_INF)

    rows = []
    for j in range(LANE // HD):
        sl = slice(j * HD, (j + 1) * HD)
        ckb = _bf(ck_ref[0, j])
        cvt = cv_ref[0, 0, sl, :]
        tiles = []
        nctx = ckb.shape[0]
        for t in range(qrows // NA_QTILE):
            k0 = t * NA_QTILE * GRID_W
            qh = q[t * qt:(t + 1) * qt, sl]
            m = jnp.full((1, qt), NEG_INF, F32)
            for c0 in range(0, nband + nctx, NA_KCHUNK):
                if c0 < nband:
                    rs_ = slice(c0, c0 + NA_KCHUNK)
                    s = _dot_nt(kloc[k0 + c0:k0 + c0 + NA_KCHUNK, sl], qh) + bias_ref[j, rs_, :] + mask_scr[t, rs_, :]
                else:
                    s = _dot_nt(ckb[c0 - nband:c0 - nband + NA_KCHUNK], qh)
                s_scr[c0:c0 + NA_KCHUNK, :] = s
                m = jnp.maximum(m, jnp.max(s, axis=0, keepdims=True))
            den = jnp.zeros((1, qt), F32)
            o = jnp.zeros((HD, qt), F32)
            for c0 in range(0, nband + nctx, NA_KCHUNK):
                pexp = jnp.exp(s_scr[c0:c0 + NA_KCHUNK, :] - m)
                den = den + jnp.sum(pexp, axis=0, keepdims=True)
                if c0 < nband:
                    vt = vloc[sl, k0 + c0:k0 + c0 + NA_KCHUNK]
                else:
                    vt = cvt[:, c0 - nband:c0 - nband + NA_KCHUNK]
                o = o + _dot(vt, _bf(pexp))
            tiles.append(o / den)
        rows.append(jnp.concatenate(tiles, axis=1))
    o_ref[0, 0] = jnp.concatenate(rows, axis=0).T.astype(o_ref.dtype)


def _na_attn(q_t, k_t, v_c, cache_k, cache_vt, bias, b, l):
    npair = q_t.shape[0]
    hpp = LANE // HD
    n_rows = l // GRID_W
    nrb = n_rows // NA_QROWS
    nq = NA_QROWS * GRID_W
    lc = cache_k.shape[2]
    q4, k4 = (a.reshape(npair, b, l, LANE) for a in (q_t, k_t))

    def prev_blk(rb):
        return jnp.maximum(rb - 1, 0)

    def next_blk(rb):
        return jnp.minimum(rb + 1, nrb - 1)

    def same_blk(rb):
        return rb

    def tok(f):
        return pl.BlockSpec((1, 1, nq, LANE), lambda bi, rb, hp: (hp, bi, f(rb), 0))

    def chan(f):
        return pl.BlockSpec((LANE, nq), lambda bi, rb, hp: (hp, bi * nrb + f(rb)))

    out = pl.pallas_call(
        functools.partial(_na_kernel, n_rows=n_rows),
        grid=(b, nrb, npair),
        in_specs=[tok(same_blk), tok(prev_blk), tok(same_blk), tok(next_blk),
                  chan(prev_blk), chan(same_blk), chan(next_blk),
                  pl.BlockSpec((1, hpp, lc, HD), lambda bi, rb, hp: (bi, hp, 0, 0)),
                  pl.BlockSpec((1, 1, LANE, lc), lambda bi, rb, hp: (bi, hp, 0, 0)),
                  pl.BlockSpec((hpp,) + bias.shape[1:], lambda bi, rb, hp: (hp, 0, 0))],
        out_specs=tok(same_blk),
        out_shape=jax.ShapeDtypeStruct((npair, b, l, LANE), BF16),
        scratch_shapes=[pltpu.VMEM((NA_QROWS // NA_QTILE,) + bias.shape[1:], F32),
                        pltpu.VMEM((bias.shape[1] + lc, bias.shape[2]), F32)],
        compiler_params=_params("arbitrary", "arbitrary", "arbitrary"),
        name="na_attn",
    )(q4, k4, k4, k4, v_c, v_c, v_c, cache_k, cache_vt, bias)
    return out.reshape(npair, b * l, LANE)


def _ctx_layer0_kernel(x_ref, mod_ref, nw_ref, w_ref, cwa_ref, cba_ref, cwb_ref, cbb_ref, dtb_ref, alog_ref,
                       dskip_ref, ex_ref, naw_ref, wo_ref, gs_ref, hyb_ref, fa_ref, iv_ref, o_ref, fin_ref,
                       *, seq_len, cols):
    tm, d = x_ref.shape
    o_z, o_xbc, o_u, o_g, o_dt = cols
    d_a, d_xbc, d_b = o_xbc - o_z, o_u - o_xbc, o_dt - o_g
    nseq, q, p = tm // seq_len, CHUNK, seq_len
    nc = seq_len // q
    step = 2 * LANE
    m = mod_ref[0]
    x = x_ref[...]
    hb = _bf(_rms(x, nw_ref[...]) * (1.0 + m[:, d:2 * d]) + m[:, 0:d])

    def conv(off, cw_ref, cb_ref, coff):
        return _conv3_rows(_dot(hb, w_ref[:, off:off + step]), 0.0, 0.0, seq_len,
                           cw_ref[:, coff:coff + step], cb_ref[:, coff:coff + step])

    xs = jnp.concatenate([_silu(conv(o_xbc + c0, cwa_ref, cba_ref, c0)) for c0 in range(0, d_xbc, step)], axis=1)
    dt_raw = _dot(hb, w_ref[:, o_dt:o_dt + LANE])
    xa = xs[:, 0:d_a]
    bm = _bf(xs[:, d_a:d_a + N_GROUPS * N_STATE])
    cm = _bf(xs[:, d_a + N_GROUPS * N_STATE:d_xbc])
    dtb, alog = dtb_ref[...], alog_ref[...]
    y_rows = []
    for s in range(nseq):
        state = [jnp.zeros((N_STATE, d_a), F32), jnp.zeros((N_STATE, d_a), F32)]
        y_chunks = [None] * nc
        for direction, order in ((0, range(nc)), (1, reversed(range(nc)))):
            for c in order:
                rows = slice(s * seq_len + c * q, s * seq_len + (c + 1) * q)
                y, state[direction] = _ssd_chunk(direction, xa[rows], bm[rows], cm[rows], dt_raw[rows],
                                                 state[direction], dtb, alog,
                                                 dskip_ref[direction:direction + 1, :], ex_ref[direction])
                y_chunks[c] = y if y_chunks[c] is None else y_chunks[c] + y
        for direction in range(2):
            for t in range(d_a // LANE):
                fin_ref[s, direction, t * LANE:(t + 1) * LANE, :] = state[direction][:, t * LANE:(t + 1) * LANE].T
        y_rows += y_chunks
    z = jnp.concatenate([_dot(hb, w_ref[:, o_z + c0:o_z + c0 + step]) for c0 in range(0, d_a, step)], axis=1)
    ya = _rms(jnp.concatenate(y_rows, axis=0) * _silu(z), naw_ref[...])

    yh_cols = []
    for c0 in range(0, d_b, step):
        wv = conv(o_u + d_b + c0, cwb_ref, cbb_ref, d_b + c0) * conv(o_u + 2 * d_b + c0, cwb_ref, cbb_ref, 2 * d_b + c0)
        gate = conv(o_u + c0, cwb_ref, cbb_ref, c0) * _silu(_dot(hb, w_ref[:, o_g + c0:o_g + c0 + step]))
        spec = jnp.concatenate([gs_ref[c0 // LANE + t, 0] for t in range(step // LANE)], axis=1)
        g_re, g_im = spec[0:p], spec[p:]
        outs = []
        for s in range(nseq):
            rows = slice(s * seq_len, (s + 1) * seq_len)
            u = _dot(fa_ref[...], _bf(wv[rows]))
            u_re, u_im = u[0:p], u[p:]
            prod = jnp.concatenate([g_re * u_re - g_im * u_im, g_re * u_im + g_im * u_re], axis=0)
            lc = _dot(iv_ref[...], _bf(prod))
            outs.append(gate[rows] * (lc + wv[rows] * hyb_ref[:, c0:c0 + step]))
        yh_cols.append(jnp.concatenate(outs, axis=0))
    yh = jnp.concatenate(yh_cols, axis=1)

    acc = _dot(_bf(ya), wo_ref[0:d_a, :]) + _dot(_bf(yh), wo_ref[d_a:, :])
    o_ref[...] = x + m[:, 2 * d:3 * d] * acc


def _ctx_layer0(x2d, b, l, mod, norm_w, w_in_bf, cols, p, w_out_bf, filt_params):
    m_rows, d = x2d.shape
    tm = ROW_TILE
    assert m_rows % tm == 0 and tm % l == 0 and l <= DFT_BLOCK and l % CHUNK == 0
    nseq = tm // l
    dm = N_HEADS * HD
    fa, iv = _dft_tables(l)
    spectra = _hyena_spectra(l, l, fa, *filt_params)
    ex = np.zeros((2, LANE, dm), np.float32)
    for direction in range(2):
        for h in range(N_HEADS):
            ex[direction, direction * N_HEADS + h, h * HD:(h + 1) * HD] = 1.0
    ex = jnp.asarray(ex, BF16)
    pad = LANE - 2 * N_HEADS
    dtb = jnp.pad(p["dt_bias"].reshape(1, 2 * N_HEADS), ((0, 0), (0, pad)))
    alog = jnp.pad(p["a_log"].reshape(1, 2 * N_HEADS), ((0, 0), (0, pad)))
    dsk = jnp.repeat(p["d_skip"], HD, axis=1)
    consts = [norm_w, w_in_bf, p["conv_a_w"], p["conv_a_b"].reshape(1, -1), p["conv_b_w"], p["conv_b_b"].reshape(1, -1),
              dtb, alog, dsk, ex, p["norm_a_w"], w_out_bf, spectra, p["hy_bias"].reshape(1, -1), fa, iv]

    def const(a):
        nd = a.ndim
        return pl.BlockSpec(a.shape, lambda i: (0,) * nd, pipeline_mode=pl.Buffered(1))

    return pl.pallas_call(
        functools.partial(_ctx_layer0_kernel, seq_len=l, cols=cols),
        grid=(m_rows // tm,),
        in_specs=[pl.BlockSpec((tm, d), lambda i: (i, 0)), pl.BlockSpec((1, 1, 3 * d), lambda i: (0, 0, 0))]
                 + [const(a) for a in consts],
        out_specs=[pl.BlockSpec((tm, d), lambda i: (i, 0)),
                   pl.BlockSpec((nseq, 2, dm, N_STATE), lambda i: (i, 0, 0, 0))],
        out_shape=[jax.ShapeDtypeStruct((m_rows, d), F32), jax.ShapeDtypeStruct((b, 2, dm, N_STATE), F32)],
        compiler_params=_params("arbitrary"),
        name="ctx_layer0",
    )(x2d, mod, *consts)


def _reorder_kernel(w_ref, o_ref, *, o_dt, n_dt):
    rows, n = w_ref.shape
    rest = n - o_dt - n_dt
    o_ref[:, 0:o_dt] = w_ref[:, 0:o_dt].astype(o_ref.dtype)
    o_ref[:, o_dt:o_dt + rest] = w_ref[:, o_dt + n_dt:n].astype(o_ref.dtype)
    tail = jnp.concatenate([w_ref[:, o_dt:o_dt + n_dt], jnp.zeros((rows, LANE - n_dt), F32)], axis=1)
    o_ref[:, o_dt + rest:o_dt + rest + LANE] = tail.astype(o_ref.dtype)


def _reorder_w_in(w, o_dt, n_dt):
    d, n = w.shape
    n_out = n - n_dt + LANE
    tr = LANE
    return pl.pallas_call(
        functools.partial(_reorder_kernel, o_dt=o_dt, n_dt=n_dt),
        grid=(d // tr,),
        in_specs=[pl.BlockSpec((tr, n), lambda i: (i, 0))],
        out_specs=pl.BlockSpec((tr, n_out), lambda i: (i, 0)),
        out_shape=jax.ShapeDtypeStruct((d, n_out), BF16),
        compiler_params=_params("arbitrary"),
        name="reorder_w_in",
    )(w)


def _layer0_cols(d_b):
    dm = N_HEADS * HD
    d_xbc = dm + 2 * N_GROUPS * N_STATE
    return (0, dm, dm + d_xbc, dm + d_xbc + 3 * d_b, dm + d_xbc + 4 * d_b)


def _layer0(x2d, b, l, mod, rows_per_mod, mod_base, norm_w, w_in_bf, w_out_bf, p, init, want_final, filt_params):
    dm = N_HEADS * HD
    d_xbc = dm + 2 * N_GROUPS * N_STATE
    cols = _layer0_cols(p["hy_bias"].shape[0])
    z, xs, w_t, gate_t, dt_raw = _proj_in0(x2d, mod, norm_w, w_in_bf, cols, p["conv_a_w"], p["conv_a_b"],
                                           p["conv_b_w"], p["conv_b_b"], l, rows_per_mod, mod_base)
    res = _ssd(xs.reshape(b, l, d_xbc), dt_raw.reshape(b, l, LANE), p["dt_bias"], p["a_log"], p["d_skip"],
               init, want_final)
    y_f, y_b = res[0].reshape(b * l, dm), res[1].reshape(b * l, dm)
    yh_t = _hyena(w_t, gate_t, filt_params, p["hy_bias"], b, l)
    x_new = _proj_out0(x2d, y_f, y_b, z, yh_t, mod, p["norm_a_w"], w_out_bf, rows_per_mod, mod_base)
    return x_new, (res[2] if want_final else None)


def kernel(x_prompt, x_sample, state_ssd, cache_k, cache_v, c, c_ctx, norm_w, w_ada, b_ada, w_in_e, w_out_e, conv_a_w, conv_a_b, dt_bias, a_log, d_skip, norm_a_w, conv_b_w, conv_b_b, hf_w1, hf_b1, hf_w2, hf_b2, hf_w3, hf_freq, hy_bias, w_in_o, w_out_o, rpb, final_norm_w):
    bp, lp, d = x_prompt.shape
    bs, ls, _ = x_sample.shape
    dm = N_HEADS * HD
    d_xbc = dm + 2 * N_GROUPS * N_STATE
    n_dt = 2 * N_HEADS

    cvecs = jnp.concatenate([c_ctx[None], c, jnp.zeros((SUBLANE - 1 - bs, d), F32)], axis=0)
    mods = _ada_mods(cvecs, w_ada, b_ada)

    xp = x_prompt.reshape(bp * lp, d)
    xs = x_sample.reshape(bs * ls, d)

    w_in0 = _reorder_w_in(w_in_e[0], dm + d_xbc, n_dt)
    w_out0 = w_out_e[0].astype(BF16)
    p0 = dict(conv_a_w=conv_a_w[0], conv_a_b=conv_a_b[0], dt_bias=dt_bias[0], a_log=a_log[0], d_skip=d_skip[0],
              norm_a_w=norm_a_w[0].reshape(1, -1), conv_b_w=conv_b_w[0], conv_b_b=conv_b_b[0], hy_bias=hy_bias[0])
    mod0 = mods[0].reshape(SUBLANE, 1, 3 * d)
    nw0 = norm_w[0].reshape(1, d)
    hf = (hf_w1[0], hf_b1[0], hf_w2[0], hf_b2[0], hf_w3[0], hf_freq[0])
    xp, fin = _ctx_layer0(xp, bp, lp, mod0, nw0, w_in0, _layer0_cols(hy_bias.shape[1]), p0, w_out0, hf)
    init_s = state_ssd[:, 0].reshape(bs, 2, dm, N_STATE)
    xs, _ = _layer0(xs, bs, ls, mod0, ls, 1, nw0, w_in0, w_out0, p0, init_s, False, hf)
    new_state_ssd = fin.reshape(bp, 1, 2, N_HEADS, HD, N_STATE)

    w_in1 = w_in_o[0].astype(BF16)
    w_out1 = w_out_o[0].astype(BF16)
    mod1 = mods[1].reshape(SUBLANE, 1, 3 * d)
    nw1 = norm_w[1].reshape(1, d)
    fw = final_norm_w.reshape(1, d)
    y_prompt, new_cache_k, new_cache_v = _ctx_layer(xp, bp, lp, mod1, nw1, w_in1, w_out1, fw)
    y_prompt = y_prompt.reshape(bp, lp, d)

    segs_s = ((0, d, TILED, BF16), (d, d, TILED, BF16), (0, d, CHAN, BF16), (3 * d, d, ROWS, BF16))
    wv_t = w_in_o[0][:, 2 * d:3 * d].T.astype(BF16)
    q_t, k_t, v_c, g = _proj_in(xs, mod1, nw1, w_in1, segs_s, ls, 1, w_t=wv_t)
    bias = _na_bias_tables(rpb[0])
    lc = cache_v.shape[3]
    cache_vt = jnp.swapaxes(cache_v[:, 0], 2, 3).reshape(bs, d // LANE, LANE, lc).astype(BF16)
    o_t = _na_attn(q_t, k_t, v_c, cache_k[:, 0], cache_vt, bias, bs, ls)
    y_sample = _proj_out1(xs, o_t, g, mod1, fw, w_out1, ls, 1).reshape(bs, ls, d)

    return (y_prompt, y_sample, new_state_ssd, new_cache_k, new_cache_v)
```

```python
import functools
import math

import jax
import jax.numpy as jnp
import numpy as np
from jax import lax
from jax.experimental import pallas as pl
from jax.experimental.pallas import tpu as pltpu

F32 = jnp.float32
BF16 = jnp.bfloat16

EPS = 1e-6
GRID_W = 64
WIN_H = 8
WIN_W = 16
HD = 64
N_HEADS = 16
N_STATE = 128
N_GROUPS = 2
CHUNK = 128
HY_EMB = 33
HY_BANDS = (HY_EMB - 1) // 2
HY_HID = 64
HY_TARGET = 1e-2
HY_DECAY_PCT_HI = 0.3
HY_DECAY_PCT_LO = 1.5

LANE = 128
SUBLANE = 8
VMEM_LIMIT = 56 * 1024 * 1024

ROW_TILE = 512
DFT_BLOCK = 512
HY_BATCH = 32
HY_BATCH_LONG = 2
NA_QROWS = 8
NA_QTILE = 4
NA_KCHUNK = 128
NA_BAND = NA_QTILE + WIN_H
NEG_INF = float("-inf")


def _bf(x):
    return x.astype(BF16)


def _dot(a, b):
    return jnp.dot(a, b, preferred_element_type=F32)


def _dot_nt(a, b):
    return lax.dot_general(a, b, (((1,), (1,)), ((), ())), preferred_element_type=F32)


def _split2(x):
    hi = _bf(x)
    lo = _bf(x - hi.astype(F32))
    return hi, lo


def _split3(x):
    hi = _bf(x)
    r = x - hi.astype(F32)
    mid = _bf(r)
    lo = _bf(r - mid.astype(F32))
    return hi, mid, lo


def _dot3(a, b):
    ah, al = _split2(a)
    bh, bl = _split2(b)
    return _dot(ah, bh) + (_dot(ah, bl) + _dot(al, bh))


def _dot_rhs_parts(a_exact, b, parts):
    pieces = _split3(b) if parts == 3 else _split2(b)
    acc = _dot(a_exact, pieces[0])
    for p in pieces[1:]:
        acc = acc + _dot(a_exact, p)
    return acc


def _silu(x):
    return x * jax.nn.sigmoid(x)


def _rms(x, g):
    ms = jnp.mean(x * x, axis=-1, keepdims=True)
    return x * lax.rsqrt(ms + EPS) * g


def _softplus(x):
    return jnp.maximum(x, 0.0) + jnp.log1p(jnp.exp(-jnp.abs(x)))


def _params(*sem):
    return pltpu.CompilerParams(dimension_semantics=sem, vmem_limit_bytes=VMEM_LIMIT)


def _mods_kernel(c_ref, w_ref, b_ref, o_ref):
    a = _silu(c_ref[...])
    o_ref[0] = _dot3(a, w_ref[0]) + b_ref[0]


def _ada_mods(cvecs, w_ada, b_ada):
    depth, d, n3 = w_ada.shape
    tn = n3 // 4
    return pl.pallas_call(
        _mods_kernel,
        grid=(depth, n3 // tn),
        in_specs=[pl.BlockSpec((SUBLANE, d), lambda l, j: (0, 0)),
                  pl.BlockSpec((1, d, tn), lambda l, j: (l, 0, j)),
                  pl.BlockSpec((1, 1, tn), lambda l, j: (l, 0, j))],
        out_specs=pl.BlockSpec((1, SUBLANE, tn), lambda l, j: (l, 0, j)),
        out_shape=jax.ShapeDtypeStruct((depth, SUBLANE, n3), F32),
        compiler_params=_params("arbitrary", "arbitrary"),
        name="ada_mods",
    )(cvecs, w_ada, b_ada.reshape(depth, 1, n3))


ROWS, TILED, CHAN = "rows", "tiled", "chan"


def _proj_in_kernel(x_ref, mod_ref, nw_ref, w_ref, *refs, segs, has_wt):
    wt_ref = refs[0] if has_wt else None
    out_refs = refs[1:] if has_wt else refs
    d = x_ref.shape[1]
    m = mod_ref[0]
    h = _rms(x_ref[...], nw_ref[...]) * (1.0 + m[:, d:2 * d]) + m[:, 0:d]
    hb = _bf(h)
    for (off, width, layout), o_ref in zip(segs, out_refs):
        if layout == CHAN:
            o_ref[...] = _dot_nt(wt_ref[off:off + width, :], hb).astype(o_ref.dtype)
            continue
        step = 2 * LANE
        for c0 in range(0, width, step):
            cw = min(step, width - c0)
            res = _dot(hb, w_ref[:, off + c0:off + c0 + cw])
            if layout == TILED:
                for t in range(cw // LANE):
                    o_ref[(c0 // LANE) + t] = res[:, t * LANE:(t + 1) * LANE].astype(o_ref.dtype)
            else:
                o_ref[:, c0:c0 + cw] = res.astype(o_ref.dtype)


def _proj_in(x2d, mod, norm_w, w_bf, segs, rows_per_mod, mod_base, w_t=None):
    m_rows, d = x2d.shape
    tm = ROW_TILE
    assert m_rows % tm == 0 and rows_per_mod % tm == 0
    out_shapes, out_specs = [], []
    for (_, width, layout, dt) in segs:
        if layout == TILED:
            out_shapes.append(jax.ShapeDtypeStruct((width // LANE, m_rows, LANE), dt))
            out_specs.append(pl.BlockSpec((width // LANE, tm, LANE), lambda i: (0, i, 0)))
        elif layout == CHAN:
            out_shapes.append(jax.ShapeDtypeStruct((width, m_rows), dt))
            out_specs.append(pl.BlockSpec((width, tm), lambda i: (0, i)))
        else:
            out_shapes.append(jax.ShapeDtypeStruct((m_rows, width), dt))
            out_specs.append(pl.BlockSpec((tm, width), lambda i: (i, 0)))
    has_wt = w_t is not None
    kern = functools.partial(_proj_in_kernel, segs=tuple((o, w, t) for (o, w, t, _) in segs), has_wt=has_wt)
    weights = [w_bf, w_t] if has_wt else [w_bf]
    return pl.pallas_call(
        kern,
        grid=(m_rows // tm,),
        in_specs=[pl.BlockSpec((tm, d), lambda i: (i, 0)),
                  pl.BlockSpec((1, 1, 3 * d), lambda i: (mod_base + (i * tm) // rows_per_mod, 0, 0)),
                  pl.BlockSpec((1, d), lambda i: (0, 0))]
                 + [pl.BlockSpec(w.shape, lambda i: (0, 0), pipeline_mode=pl.Buffered(1)) for w in weights],
        out_specs=out_specs,
        out_shape=out_shapes,
        compiler_params=_params("arbitrary"),
        name="proj_in",
    )(x2d, mod, norm_w, *weights)


def _conv3_rows(res, prev_row, next_row, seq_len, cw, cb):
    tm, width = res.shape
    starts = list(range(0, tm, seq_len))
    ends = [min(s + seq_len, tm) - 1 for s in starts]
    sub = lax.broadcasted_iota(jnp.int32, (SUBLANE, width), 0)
    down = pltpu.roll(res, 1, 0)
    up = pltpu.roll(res, tm - 1, 0)
    dparts, uparts, pos = [], [], 0
    for s in starts:
        fill = prev_row if s == 0 else 0.0
        dparts += [down[pos:s], jnp.where(sub == 0, fill, down[s:s + SUBLANE])]
        pos = s + SUBLANE
    dparts.append(down[pos:tm])
    pos = 0
    for e in ends:
        fill = next_row if e == tm - 1 else 0.0
        uparts += [up[pos:e + 1 - SUBLANE], jnp.where(sub == SUBLANE - 1, fill, up[e + 1 - SUBLANE:e + 1])]
        pos = e + 1
    uparts.append(up[pos:tm])
    down = jnp.concatenate([p for p in dparts if p.shape[0]], axis=0)
    up = jnp.concatenate([p for p in uparts if p.shape[0]], axis=0)
    return cb + down * cw[0:1] + res * cw[1:2] + up * cw[2:3]


def _proj_in0_kernel(x_ref, xp_ref, xn_ref, mod_ref, nw_ref, w_ref, cwa_ref, cba_ref, cwb_ref, cbb_ref,
                     dtb_ref, alog_ref, dskip_ref, ex_ref, init_ref,
                     z_ref, xs_ref, wv_ref, gate_ref, dt_ref, yf_ref, s_scr, *, seq_len, cols):
    i = pl.program_id(0)
    tm, d = x_ref.shape
    o_z, o_xbc, o_u, o_g, o_dt = cols
    d_a, d_xbc, d_b = o_xbc - o_z, o_u - o_xbc, o_dt - o_g
    m = mod_ref[0]

    def modnorm(x):
        return _bf(_rms(x, nw_ref[...]) * (1.0 + m[:, d:2 * d]) + m[:, 0:d])

    hb = modnorm(x_ref[...])
    hh = modnorm(jnp.concatenate([xp_ref[...], xn_ref[...]], axis=0))
    keep_prev = ((i * tm) % seq_len != 0).astype(F32)
    keep_next = (((i + 1) * tm) % seq_len != 0).astype(F32)

    def conv(off, width, cw_ref, cb_ref, coff):
        res = _dot(hb, w_ref[:, off:off + width])
        rh = _dot(hh, w_ref[:, off:off + width])
        return _conv3_rows(res, rh[SUBLANE - 1:SUBLANE] * keep_prev, rh[SUBLANE:SUBLANE + 1] * keep_next, seq_len,
                           cw_ref[:, coff:coff + width], cb_ref[:, coff:coff + width])

    step = 2 * LANE
    for c0 in range(0, d_a, step):
        z_ref[:, c0:c0 + step] = _dot(hb, w_ref[:, o_z + c0:o_z + c0 + step]).astype(z_ref.dtype)
    for c0 in range(0, d_xbc, step):
        xs_ref[:, c0:c0 + step] = _silu(conv(o_xbc + c0, step, cwa_ref, cba_ref, c0)).astype(xs_ref.dtype)
    for c0 in range(0, d_b, step):
        x1 = conv(o_u + d_b + c0, step, cwb_ref, cbb_ref, d_b + c0)
        v = conv(o_u + 2 * d_b + c0, step, cwb_ref, cbb_ref, 2 * d_b + c0)
        wv = x1 * v
        x0 = conv(o_u + c0, step, cwb_ref, cbb_ref, c0)
        gate = x0 * _silu(_dot(hb, w_ref[:, o_g + c0:o_g + c0 + step]))
        for t in range(step // LANE):
            wv_ref[c0 // LANE + t] = wv[:, t * LANE:(t + 1) * LANE].astype(wv_ref.dtype)
            gate_ref[c0 // LANE + t] = gate[:, t * LANE:(t + 1) * LANE].astype(gate_ref.dtype)
    dt_raw = _dot(hb, w_ref[:, o_dt:o_dt + LANE])
    dt_ref[...] = dt_raw
    y_f = _ssd_tile(0, xs_ref[...], dt_raw, s_scr, init_ref, (i * tm) % seq_len == 0,
                    dtb_ref[...], alog_ref[...], dskip_ref[0:1, :], ex_ref[0])
    yf_ref[...] = y_f.astype(yf_ref.dtype)


def _proj_in0(x2d, mod, norm_w, w_bf, cols, conv_a_w, conv_a_b, conv_b_w, conv_b_b, ssd_consts, init, seq_len,
              rows_per_mod, mod_base):
    m_rows, d = x2d.shape
    tm = ROW_TILE
    assert m_rows % tm == 0 and rows_per_mod % tm == 0 and seq_len % tm == 0 and tm % CHUNK == 0
    o_z, o_xbc, o_u, o_g, o_dt = cols
    d_a, d_xbc, d_b = o_xbc - o_z, o_u - o_xbc, o_dt - o_g
    nsub = m_rows // SUBLANE
    spt = tm // SUBLANE
    cba, cbb = conv_a_b.reshape(1, -1), conv_b_b.reshape(1, -1)
    consts = [norm_w, w_bf, conv_a_w, cba, conv_b_w, cbb] + list(ssd_consts)

    def const(a):
        nd = a.ndim
        return pl.BlockSpec(a.shape, lambda i: (0,) * nd, pipeline_mode=pl.Buffered(1))

    def tiles(n):
        return pl.BlockSpec((n, tm, LANE), lambda i: (0, i, 0))

    def rows(wd):
        return pl.BlockSpec((tm, wd), lambda i: (i, 0))

    return pl.pallas_call(
        functools.partial(_proj_in0_kernel, seq_len=seq_len, cols=cols),
        grid=(m_rows // tm,),
        in_specs=[rows(d),
                  pl.BlockSpec((SUBLANE, d), lambda i: (jnp.maximum(i * spt - 1, 0), 0)),
                  pl.BlockSpec((SUBLANE, d), lambda i: (jnp.minimum((i + 1) * spt, nsub - 1), 0)),
                  pl.BlockSpec((1, 1, 3 * d), lambda i: (mod_base + (i * tm) // rows_per_mod, 0, 0))]
                 + [const(a) for a in consts]
                 + [pl.BlockSpec((1, 1, d_a, N_STATE), lambda i: ((i * tm) // seq_len, 0, 0, 0))],
        out_specs=[rows(d_a), rows(d_xbc), tiles(d_b // LANE), tiles(d_b // LANE), rows(LANE), rows(d_a)],
        out_shape=[jax.ShapeDtypeStruct((m_rows, d_a), BF16), jax.ShapeDtypeStruct((m_rows, d_xbc), BF16),
                   jax.ShapeDtypeStruct((d_b // LANE, m_rows, LANE), BF16),
                   jax.ShapeDtypeStruct((d_b // LANE, m_rows, LANE), BF16),
                   jax.ShapeDtypeStruct((m_rows, LANE), F32), jax.ShapeDtypeStruct((m_rows, d_a), BF16)],
        scratch_shapes=[pltpu.VMEM((N_STATE, d_a), F32)],
        compiler_params=_params("arbitrary"),
        name="proj_in0",
    )(x2d, x2d, x2d, mod, *consts, init)


def _ssd_chunk(d, xa, bm, cm, dt_raw, s_prev, dtb, alog, dskip, ex):
    q, dm = xa.shape
    gw = dm // N_GROUPS
    hpg = N_HEADS // N_GROUPS
    row = lax.broadcasted_iota(jnp.int32, (q, q), 0)
    col = lax.broadcasted_iota(jnp.int32, (q, q), 1)
    left = lax.broadcasted_iota(jnp.int32, (q, LANE), 1) < HD
    tri = (row >= col) if d == 0 else (row <= col)

    dt = _softplus(dt_raw + dtb)
    adt = dt * (-jnp.exp(alog))
    cs = _dot_rhs_parts(jnp.where(tri, 1.0, 0.0).astype(BF16), adt, 3)
    cs_t = cs.T
    dt_t = dt.T
    edge = cs[q - 1:q, :] if d == 0 else cs[0:1, :]
    e_cs = _dot(_bf(jnp.exp(cs)), ex)
    w_st = _dot(_bf(jnp.exp(edge - cs) * dt), ex)
    xw = xa * w_st

    y_parts = []
    new_state = []
    for g in range(N_GROUPS):
        bg = bm[:, g * N_STATE:(g + 1) * N_STATE]
        cg = cm[:, g * N_STATE:(g + 1) * N_STATE]
        gmat = _dot_nt(_bf(cg), _bf(bg))
        y_off = _dot(_bf(cg), _bf(s_prev[:, g * gw:(g + 1) * gw]))
        new_state.append(_dot(_bf(bg.astype(F32).T), _bf(xw[:, g * gw:(g + 1) * gw])))
        for pr in range(hpg // 2):
            mh = []
            for j in range(2):
                k = d * N_HEADS + g * hpg + 2 * pr + j
                diff = cs[:, k:k + 1] - cs_t[k:k + 1, :]
                lm = jnp.exp(jnp.where(tri, diff, NEG_INF))
                mh.append(_bf(gmat * lm * dt_t[k:k + 1, :]))
            c0 = g * gw + pr * LANE
            xpair = xa[:, c0:c0 + LANE]
            rhs = jnp.concatenate([_bf(jnp.where(left, xpair, 0.0)), _bf(jnp.where(left, 0.0, xpair))], axis=0)
            y_d = _dot(jnp.concatenate(mh, axis=1), rhs)
            y_parts.append(y_d + y_off[:, pr * LANE:(pr + 1) * LANE] * e_cs[:, c0:c0 + LANE])
    y = jnp.concatenate(y_parts, axis=1) + xa * dskip
    e_edge = e_cs[q - 1:q, :] if d == 0 else e_cs[0:1, :]
    return y, s_prev * e_edge + jnp.concatenate(new_state, axis=1)


def _ssd_tile(direction, xs, dt_raw, s_scr, init_ref, at_seq_edge, dtb, alog, dskip, ex):
    rows, dm = xs.shape[0], N_HEADS * HD

    @pl.when(at_seq_edge)
    def _():
        for t in range(dm // LANE):
            s_scr[:, t * LANE:(t + 1) * LANE] = init_ref[0, 0, t * LANE:(t + 1) * LANE, :].T

    nck = rows // CHUNK
    ys = [None] * nck
    state = s_scr[...]
    for c in (range(nck) if direction == 0 else reversed(range(nck))):
        r = slice(c * CHUNK, (c + 1) * CHUNK)
        ys[c], state = _ssd_chunk(direction, xs[r, 0:dm].astype(F32), xs[r, dm:dm + N_GROUPS * N_STATE],
                                  xs[r, dm + N_GROUPS * N_STATE:], dt_raw[r], state, dtb, alog, dskip, ex)
    s_scr[...] = state
    return jnp.concatenate(ys, axis=0)


def _ssd_consts(dt_bias, a_log, d_skip):
    dm = N_HEADS * HD
    ex = np.zeros((2, LANE, dm), np.float32)
    for d in range(2):
        for h in range(N_HEADS):
            ex[d, d * N_HEADS + h, h * HD:(h + 1) * HD] = 1.0
    pad = LANE - 2 * N_HEADS
    dtb = jnp.pad(dt_bias.reshape(1, 2 * N_HEADS), ((0, 0), (0, pad)))
    alog = jnp.pad(a_log.reshape(1, 2 * N_HEADS), ((0, 0), (0, pad)))
    return dtb, alog, jnp.repeat(d_skip, HD, axis=1), jnp.asarray(ex, BF16)


def _hyena_tables(l):
    pos = np.abs(np.arange(2 * l, dtype=np.float64) - l)
    t = pos / (l - 1)
    w = 2.0 * math.pi * pos / l
    f = np.linspace(1e-4, HY_BANDS - 1, HY_BANDS)
    feats = np.zeros((2 * l, LANE), np.float64)
    feats[:, 0] = t
    feats[:, 1:1 + HY_BANDS] = np.cos(f[None] * w[:, None])
    feats[:, 1 + HY_BANDS:1 + 2 * HY_BANDS] = -np.sin(f[None] * w[:, None])
    return jnp.asarray(feats, F32)


def _dft_tables(p):
    n = 2 * p
    f = np.arange(p, dtype=np.float64)[:, None] + 0.5
    e = np.arange(p, dtype=np.float64)[None]
    ang = 2.0 * math.pi * f * e / n
    fa = np.concatenate([np.cos(ang), -np.sin(ang)], axis=0)
    inv = np.concatenate([np.cos(ang.T), -np.sin(ang.T)], axis=1) * (2.0 / n)
    return tuple(jnp.asarray(m, F32).astype(BF16) for m in (fa, inv))


def _spectra_kernel(f_ref, w1_ref, b1_ref, w2_ref, b2_ref, w3_ref, fr_ref, ad_ref, fa_ref, o_ref, bprev):
    q = pl.program_id(0)
    p = f_ref.shape[0]
    half = p // 2
    feats = f_ref[...]
    pre1 = _dot3(feats, w1_ref[...])
    packed = jnp.concatenate([pre1[:half], pre1[half:]], axis=1)
    fr = fr_ref[...]
    h1 = jnp.sin(fr * (packed + b1_ref[...]))
    h2 = _bf(jnp.sin(fr * (_dot3(h1, w2_ref[...]) + b2_ref[...])))
    w3 = _bf(w3_ref[...])
    filt = jnp.concatenate([_dot(h2[:, :HY_HID], w3), _dot(h2[:, HY_HID:], w3)], axis=0)
    taps = _bf(filt * jnp.exp(-feats[:, 0:1] * ad_ref[...]))
    a = _dot(fa_ref[...], taps)

    @pl.when(q > 0)
    def _():
        g = a + bprev[...]
        for t in range(o_ref.shape[0]):
            o_ref[t, 0] = g[:, t * LANE:(t + 1) * LANE]

    odd = (lax.broadcasted_iota(jnp.int32, (p, a.shape[1]), 0) & 1) == 1
    a_re, a_im = a[0:p], a[p:]
    a_re0 = a_re - taps[0:1, :].astype(F32)
    bprev[0:p] = jnp.where(odd, a_im, -a_im)
    bprev[p:] = jnp.where(odd, -a_re0, a_re0)


def _hyena_spectra(l, p, fa, w1, b1, w2, b2, w3, freq):
    db = w3.shape[1] // 2
    nblk = 2 * l // p
    nct = db // LANE
    feats = _hyena_tables(l)
    w1p = jnp.pad(w1, ((0, LANE - HY_EMB), (0, 0)))
    zero = jnp.zeros_like(w2)
    w2bd = jnp.concatenate([jnp.concatenate([w2, zero], axis=1), jnp.concatenate([zero, w2], axis=1)], axis=0)
    deltas = np.linspace(math.log(HY_TARGET) / HY_DECAY_PCT_HI, math.log(HY_TARGET) / HY_DECAY_PCT_LO, db)
    absd = jnp.asarray(np.abs(deltas)[None], F32)
    b1r, b2r, frr = (jnp.tile(v.reshape(1, -1), (1, 2)) for v in (b1, b2, freq))

    def full(a):
        return pl.BlockSpec(a.shape, lambda q: (0, 0))

    return pl.pallas_call(
        _spectra_kernel,
        grid=(nblk,),
        in_specs=[pl.BlockSpec((p, LANE), lambda q: (q, 0)), full(w1p), full(b1r), full(w2bd), full(b2r),
                  pl.BlockSpec((HY_HID, db), lambda q: (0, jnp.where(q < nblk // 2, 1, 0))),
                  full(frr), full(absd), full(fa)],
        out_specs=pl.BlockSpec((nct, 1, 2 * p, LANE), lambda q: (0, jnp.maximum(q - 1, 0), 0, 0)),
        out_shape=jax.ShapeDtypeStruct((nct, nblk - 1, 2 * p, LANE), F32),
        scratch_shapes=[pltpu.VMEM((2 * p, db), F32)],
        compiler_params=_params("arbitrary"),
        name="hyena_spectra",
    )(feats, w1p, b1r, w2bd, b2r, w3, frr, absd, fa)


def _hyena_kernel(w_ref, gate_ref, gs_ref, hb_ref, fa_ref, iv_ref, o_ref, u_scr, y_scr, *, p):
    bt, l = w_ref.shape[1], w_ref.shape[2]
    nb = l // p
    fa = fa_ref[...]
    for j in range(nb):
        rhs = jnp.concatenate([w_ref[0, bb, j * p:(j + 1) * p, :] for bb in range(bt)], axis=1)
        u_scr[j] = _dot(fa, rhs)
    rt_rows = 64
    for i in range(nb):
        def body(rt, carry):
            r0 = pl.multiple_of(rt * rt_rows, rt_rows)
            for bb in range(bt):
                ls = slice(bb * LANE, (bb + 1) * LANE)
                acc_re = jnp.zeros((rt_rows, LANE), F32)
                acc_im = jnp.zeros((rt_rows, LANE), F32)
                for j in range(nb):
                    s = i - j + nb - 1
                    gre = gs_ref[0, s, pl.ds(r0, rt_rows), :]
                    gim = gs_ref[0, s, pl.ds(p + r0, rt_rows), :]
                    ure = u_scr[j, pl.ds(r0, rt_rows), ls]
                    uim = u_scr[j, pl.ds(p + r0, rt_rows), ls]
                    acc_re = acc_re + (gre * ure - gim * uim)
                    acc_im = acc_im + (gre * uim + gim * ure)
                y_scr[pl.ds(r0, rt_rows), ls] = acc_re
                y_scr[pl.ds(p + r0, rt_rows), ls] = acc_im
            return carry
        lax.fori_loop(0, p // rt_rows, body, 0)
        conv = _dot(iv_ref[...], _bf(y_scr[...]))
        sl = slice(i * p, (i + 1) * p)
        for bb in range(bt):
            wi = w_ref[0, bb, sl, :].astype(F32)
            o_ref[0, bb, sl, :] = (gate_ref[0, bb, sl, :].astype(F32)
                                   * (conv[:, bb * LANE:(bb + 1) * LANE] + wi * hb_ref[...])).astype(o_ref.dtype)


def _hyena(w_t, gate_t, filt_params, hy_bias, b, l):
    nct = w_t.shape[0]
    p = min(DFT_BLOCK, l)
    nb = l // p
    nseg = 2 * nb - 1
    bt = min(b, HY_BATCH if nb == 1 else HY_BATCH_LONG)
    fa, iv = _dft_tables(p)
    spectra = _hyena_spectra(l, p, fa, *filt_params)
    w4 = w_t.reshape(nct, b, l, LANE)
    g4 = gate_t.reshape(nct, b, l, LANE)
    hbr = hy_bias.reshape(1, -1)
    act = pl.BlockSpec((1, bt, l, LANE), lambda ct, bi: (ct, bi, 0, 0))

    def full(a):
        return pl.BlockSpec(a.shape, lambda ct, bi: (0, 0), pipeline_mode=pl.Buffered(1))

    out = pl.pallas_call(
        functools.partial(_hyena_kernel, p=p),
        grid=(nct, b // bt),
        in_specs=[act, act,
                  pl.BlockSpec((1, nseg, 2 * p, LANE), lambda ct, bi: (ct, 0, 0, 0), pipeline_mode=pl.Buffered(1)),
                  pl.BlockSpec((1, LANE), lambda ct, bi: (0, ct)),
                  full(fa), full(iv)],
        out_specs=act,
        out_shape=jax.ShapeDtypeStruct((nct, b, l, LANE), BF16),
        scratch_shapes=[pltpu.VMEM((nb, 2 * p, bt * LANE), F32), pltpu.VMEM((2 * p, bt * LANE), F32)],
        compiler_params=_params("arbitrary", "arbitrary"),
        name="hyena_conv",
    )(w4, g4, spectra, hbr, fa, iv)
    return out.reshape(nct, b * l, LANE)


def _proj_out0_kernel(x_ref, yf_ref, xs_ref, dt_ref, z_ref, yh_ref, mod_ref, naw_ref, w_ref, dtb_ref, alog_ref,
                      dskip_ref, ex_ref, init_ref, o_ref, s_scr, *, seq_len, n_tiles):
    j = n_tiles - 1 - pl.program_id(0)
    tm, d = x_ref.shape
    y_b = _ssd_tile(1, xs_ref[...], dt_ref[...], s_scr, init_ref, ((j + 1) * tm) % seq_len == 0,
                    dtb_ref[...], alog_ref[...], dskip_ref[1:2, :], ex_ref[1])
    ya = _rms((yf_ref[...].astype(F32) + y_b) * _silu(z_ref[...].astype(F32)), naw_ref[...])
    yh = jnp.concatenate([yh_ref[t] for t in range(yh_ref.shape[0])], axis=1)
    da = ya.shape[1]
    acc = _dot(_bf(ya), w_ref[0:da, :]) + _dot(_bf(yh), w_ref[da:, :])
    gate = mod_ref[0][:, 2 * d:3 * d]
    o_ref[...] = x_ref[...] + gate * acc


def _proj_out0(x2d, y_f, xs, dt_raw, z, yh_t, mod, norm_a_w, w_bf, ssd_consts, init, seq_len, rows_per_mod, mod_base):
    m_rows, d = x2d.shape
    tm = ROW_TILE
    assert m_rows % tm == 0 and rows_per_mod % tm == 0 and seq_len % tm == 0 and tm % CHUNK == 0
    da = y_f.shape[1]
    nt = yh_t.shape[0]
    n_tiles = m_rows // tm
    consts = [norm_a_w, w_bf] + list(ssd_consts)

    def rowspec(wd):
        return pl.BlockSpec((tm, wd), lambda i: (n_tiles - 1 - i, 0))

    def const(a):
        nd = a.ndim
        return pl.BlockSpec(a.shape, lambda i: (0,) * nd, pipeline_mode=pl.Buffered(1))

    return pl.pallas_call(
        functools.partial(_proj_out0_kernel, seq_len=seq_len, n_tiles=n_tiles),
        grid=(n_tiles,),
        in_specs=[rowspec(d), rowspec(da), rowspec(xs.shape[1]), rowspec(LANE), rowspec(da),
                  pl.BlockSpec((nt, tm, LANE), lambda i: (0, n_tiles - 1 - i, 0)),
                  pl.BlockSpec((1, 1, 3 * d),
                               lambda i: (mod_base + ((n_tiles - 1 - i) * tm) // rows_per_mod, 0, 0))]
                 + [const(a) for a in consts]
                 + [pl.BlockSpec((1, 1, da, N_STATE), lambda i: (((n_tiles - 1 - i) * tm) // seq_len, 1, 0, 0))],
        out_specs=rowspec(d),
        out_shape=jax.ShapeDtypeStruct((m_rows, d), F32),
        scratch_shapes=[pltpu.VMEM((N_STATE, da), F32)],
        compiler_params=_params("arbitrary"),
        name="proj_out0",
    )(x2d, y_f, xs, dt_raw, z, yh_t, mod, *consts, init)


def _proj_out1_kernel(x_ref, o_ref_in, g_ref, mod_ref, fw_ref, w_ref, y_ref):
    d = x_ref.shape[1]
    o = jnp.concatenate([o_ref_in[t] for t in range(o_ref_in.shape[0])], axis=1).astype(F32)
    a = o * _silu(g_ref[...].astype(F32))
    acc = _dot(_bf(a), w_ref[...])
    gate = mod_ref[0][:, 2 * d:3 * d]
    y_ref[...] = _rms(x_ref[...] + gate * acc, fw_ref[...])


def _proj_out1(x2d, o_t, g, mod, final_w, w_bf, rows_per_mod, mod_base):
    m_rows, d = x2d.shape
    tm = ROW_TILE
    assert m_rows % tm == 0 and rows_per_mod % tm == 0
    nt = o_t.shape[0]
    return pl.pallas_call(
        _proj_out1_kernel,
        grid=(m_rows // tm,),
        in_specs=[pl.BlockSpec((tm, d), lambda i: (i, 0)),
                  pl.BlockSpec((nt, tm, LANE), lambda i: (0, i, 0)),
                  pl.BlockSpec((tm, g.shape[1]), lambda i: (i, 0)),
                  pl.BlockSpec((1, 1, 3 * d), lambda i: (mod_base + (i * tm) // rows_per_mod, 0, 0)),
                  pl.BlockSpec((1, d), lambda i: (0, 0)),
                  pl.BlockSpec(w_bf.shape, lambda i: (0, 0), pipeline_mode=pl.Buffered(1))],
        out_specs=pl.BlockSpec((tm, d), lambda i: (i, 0)),
        out_shape=jax.ShapeDtypeStruct((m_rows, d), F32),
        compiler_params=_params("arbitrary"),
        name="proj_out1",
    )(x2d, o_t, g, mod, final_w, w_bf)


def _ctx_layer_kernel(x_ref, mod_ref, nw_ref, wi_ref, wo_ref, fw_ref, y_ref, ck_ref, cv_ref):
    tm, d = x_ref.shape
    nseq = ck_ref.shape[0]
    l = tm // nseq
    m = mod_ref[0]
    x = x_ref[...]
    hb = _bf(_rms(x, nw_ref[...]) * (1.0 + m[:, d:2 * d]) + m[:, 0:d])
    qb = _bf(_dot(hb, wi_ref[:, 0:d]) * (HD ** -0.5))
    k = _dot(hb, wi_ref[:, d:2 * d])
    v = _dot(hb, wi_ref[:, 2 * d:3 * d])
    g = _dot(hb, wi_ref[:, 3 * d:4 * d])
    nh = d // HD
    o_rows = []
    for s in range(nseq):
        rows = slice(s * l, (s + 1) * l)
        scores, vbs = [], []
        for h in range(nh):
            sl = slice(h * HD, (h + 1) * HD)
            kh = k[rows, sl]
            vh = v[rows, sl]
            ck_ref[s, 0, h] = kh
            cv_ref[s, 0, h] = vh
            vbs.append(_bf(vh))
            scores.append(_dot_nt(qb[rows, sl], _bf(kh)))
        s_all = jnp.concatenate(scores, axis=0)
        pexp = jnp.exp(s_all - jnp.max(s_all, axis=-1, keepdims=True))
        den = jnp.sum(pexp, axis=-1, keepdims=True)
        pb = _bf(pexp)
        o_rows.append(jnp.concatenate(
            [_dot(pb[h * l:(h + 1) * l], vbs[h]) / den[h * l:(h + 1) * l] for h in range(nh)], axis=1))
    a = jnp.concatenate(o_rows, axis=0) * _silu(g)
    acc = _dot(_bf(a), wo_ref[...])
    y_ref[...] = _rms(x + m[:, 2 * d:3 * d] * acc, fw_ref[...])


def _ctx_layer(x2d, b, l, mod, norm_w, w_in_bf, w_out_bf, final_w):
    m_rows, d = x2d.shape
    tm = ROW_TILE
    assert m_rows % tm == 0 and tm % l == 0
    nseq = tm // l
    nh = d // HD
    cache_spec = pl.BlockSpec((nseq, 1, nh, l, HD), lambda i: (i, 0, 0, 0, 0))
    cache_shape = jax.ShapeDtypeStruct((b, 1, nh, l, HD), F32)

    def const(a):
        nd = a.ndim
        return pl.BlockSpec(a.shape, lambda i: (0,) * nd, pipeline_mode=pl.Buffered(1))

    return pl.pallas_call(
        _ctx_layer_kernel,
        grid=(m_rows // tm,),
        in_specs=[pl.BlockSpec((tm, d), lambda i: (i, 0)),
                  pl.BlockSpec((1, 1, 3 * d), lambda i: (0, 0, 0)),
                  const(norm_w), const(w_in_bf), const(w_out_bf), const(final_w)],
        out_specs=[pl.BlockSpec((tm, d), lambda i: (i, 0)), cache_spec, cache_spec],
        out_shape=[jax.ShapeDtypeStruct((m_rows, d), F32), cache_shape, cache_shape],
        compiler_params=_params("arbitrary"),
        name="ctx_layer",
    )(x2d, mod, norm_w, w_in_bf, w_out_bf, final_w)


def _na_bias_kernel(rpb_ref, o_ref):
    h = pl.program_id(0)
    ndr = 2 * WIN_H - 1
    ndc = 2 * WIN_W - 1
    ck = lax.broadcasted_iota(jnp.int32, (GRID_W, LANE), 0)
    lane = lax.broadcasted_iota(jnp.int32, (GRID_W, LANE), 1)
    cq = lane & (GRID_W - 1)
    first = lane < GRID_W
    dc = jnp.clip(ck - cq + (WIN_W - 1), 0, ndc - 1)
    col0 = jnp.clip(cq - WIN_W // 2, 0, GRID_W - WIN_W)
    col_in = (ck >= col0) & (ck < col0 + WIN_W)
    dc_is = [(dc == e) & col_in for e in range(ndc)]
    tables = []
    for dr in range(ndr):
        t = jnp.full((GRID_W, LANE), NEG_INF, F32)
        for e in range(ndc):
            t = jnp.where(dc_is[e], rpb_ref[(h * ndr + dr) * ndc + e], t)
        tables.append(t)
    for ip in range(NA_BAND):
        for ap in range(NA_QTILE // 2):
            x = ip - 2 * ap + (WIN_H - 1) - NA_QROWS // 2
            o_ref[0, ip * GRID_W:(ip + 1) * GRID_W, ap * LANE:(ap + 1) * LANE] = jnp.where(
                first, tables[x], tables[x - 1])


def _na_bias_tables(rpb):
    nh = rpb.shape[0]
    shape = (NA_BAND * GRID_W, NA_QTILE * GRID_W)
    return pl.pallas_call(
        _na_bias_kernel,
        grid=(nh,),
        in_specs=[pl.BlockSpec(memory_space=pltpu.SMEM)],
        out_specs=pl.BlockSpec((1,) + shape, lambda h: (h, 0, 0)),
        out_shape=jax.ShapeDtypeStruct((nh,) + shape, F32),
        compiler_params=_params("arbitrary"),
        name="na_bias",
    )(rpb.reshape(-1))


def _na_kernel(q_ref, kp_ref, kc_ref, kn_ref, vp_ref, vc_ref, vn_ref, ck_ref, cv_ref, bias_ref, o_ref, mask_scr,
               s_scr, *, n_rows):
    rb = pl.program_id(1)
    qrows = NA_QROWS
    nq = qrows * GRID_W
    half = (qrows // 2) * GRID_W
    qt = NA_QTILE * GRID_W
    nband = NA_BAND * GRID_W
    q = q_ref[0, 0] * (HD ** -0.5)
    kloc = jnp.concatenate([kp_ref[0, 0][nq - half:nq], kc_ref[0, 0], kn_ref[0, 0][0:half]], axis=0)
    vloc = jnp.concatenate([vp_ref[:, nq - half:nq], vc_ref[...], vn_ref[:, 0:half]], axis=1)

    @pl.when(pl.program_id(2) == 0)
    def _():
        for t in range(qrows // NA_QTILE):
            i = t * NA_QTILE + lax.broadcasted_iota(jnp.int32, (nband, qt), 0) // GRID_W
            a = t * NA_QTILE + lax.broadcasted_iota(jnp.int32, (nband, qt), 1) // GRID_W
            r = rb * qrows + a
            kr = rb * qrows - qrows // 2 + i
            rs = jnp.clip(r - WIN_H // 2, 0, n_rows - WIN_H)
            mask_scr[t] = jnp.where((kr >= rs) & (kr < rs + WIN_H), 0.0, NEG_INF)

    rows = []
    for j in range(LANE // HD):
        sl = slice(j * HD, (j + 1) * HD)
        ckb = _bf(ck_ref[0, j])
        cvt = cv_ref[0, 0, sl, :]
        tiles = []
        nctx = ckb.shape[0]
        for t in range(qrows // NA_QTILE):
            k0 = t * NA_QTILE * GRID_W
            qh = q[t * qt:(t + 1) * qt, sl]
            m = jnp.full((1, qt), NEG_INF, F32)
            for c0 in range(0, nband + nctx, NA_KCHUNK):
                if c0 < nband:
                    rs_ = slice(c0, c0 + NA_KCHUNK)
                    s = _dot_nt(kloc[k0 + c0:k0 + c0 + NA_KCHUNK, sl], qh) + bias_ref[j, rs_, :] + mask_scr[t, rs_, :]
                else:
                    s = _dot_nt(ckb[c0 - nband:c0 - nband + NA_KCHUNK], qh)
                s_scr[c0:c0 + NA_KCHUNK, :] = s
                m = jnp.maximum(m, jnp.max(s, axis=0, keepdims=True))
            den = jnp.zeros((1, qt), F32)
            o = jnp.zeros((HD, qt), F32)
            for c0 in range(0, nband + nctx, NA_KCHUNK):
                pexp = jnp.exp(s_scr[c0:c0 + NA_KCHUNK, :] - m)
                den = den + jnp.sum(pexp, axis=0, keepdims=True)
                if c0 < nband:
                    vt = vloc[sl, k0 + c0:k0 + c0 + NA_KCHUNK]
                else:
                    vt = cvt[:, c0 - nband:c0 - nband + NA_KCHUNK]
                o = o + _dot(vt, _bf(pexp))
            tiles.append(o / den)
        rows.append(jnp.concatenate(tiles, axis=1))
    o_ref[0, 0] = jnp.concatenate(rows, axis=0).T.astype(o_ref.dtype)


def _na_attn(q_t, k_t, v_c, cache_k, cache_vt, bias, b, l):
    npair = q_t.shape[0]
    hpp = LANE // HD
    n_rows = l // GRID_W
    nrb = n_rows // NA_QROWS
    nq = NA_QROWS * GRID_W
    lc = cache_k.shape[2]
    q4, k4 = (a.reshape(npair, b, l, LANE) for a in (q_t, k_t))

    def prev_blk(rb):
        return jnp.maximum(rb - 1, 0)

    def next_blk(rb):
        return jnp.minimum(rb + 1, nrb - 1)

    def same_blk(rb):
        return rb

    def tok(f):
        return pl.BlockSpec((1, 1, nq, LANE), lambda bi, rb, hp: (hp, bi, f(rb), 0))

    def chan(f):
        return pl.BlockSpec((LANE, nq), lambda bi, rb, hp: (hp, bi * nrb + f(rb)))

    out = pl.pallas_call(
        functools.partial(_na_kernel, n_rows=n_rows),
        grid=(b, nrb, npair),
        in_specs=[tok(same_blk), tok(prev_blk), tok(same_blk), tok(next_blk),
                  chan(prev_blk), chan(same_blk), chan(next_blk),
                  pl.BlockSpec((1, hpp, lc, HD), lambda bi, rb, hp: (bi, hp, 0, 0)),
                  pl.BlockSpec((1, 1, LANE, lc), lambda bi, rb, hp: (bi, hp, 0, 0)),
                  pl.BlockSpec((hpp,) + bias.shape[1:], lambda bi, rb, hp: (hp, 0, 0))],
        out_specs=tok(same_blk),
        out_shape=jax.ShapeDtypeStruct((npair, b, l, LANE), BF16),
        scratch_shapes=[pltpu.VMEM((NA_QROWS // NA_QTILE,) + bias.shape[1:], F32),
                        pltpu.VMEM((bias.shape[1] + lc, bias.shape[2]), F32)],
        compiler_params=_params("arbitrary", "arbitrary", "arbitrary"),
        name="na_attn",
    )(q4, k4, k4, k4, v_c, v_c, v_c, cache_k, cache_vt, bias)
    return out.reshape(npair, b * l, LANE)


def _ctx_layer0_kernel(x_ref, mod_ref, nw_ref, w_ref, cwa_ref, cba_ref, cwb_ref, cbb_ref, dtb_ref, alog_ref,
                       dskip_ref, ex_ref, naw_ref, wo_ref, gs_ref, hyb_ref, fa_ref, iv_ref, o_ref, fin_ref,
                       *, seq_len, cols):
    tm, d = x_ref.shape
    o_z, o_xbc, o_u, o_g, o_dt = cols
    d_a, d_xbc, d_b = o_xbc - o_z, o_u - o_xbc, o_dt - o_g
    nseq, q, p = tm // seq_len, CHUNK, seq_len
    nc = seq_len // q
    step = 2 * LANE
    m = mod_ref[0]
    x = x_ref[...]
    hb = _bf(_rms(x, nw_ref[...]) * (1.0 + m[:, d:2 * d]) + m[:, 0:d])

    def conv(off, cw_ref, cb_ref, coff):
        return _conv3_rows(_dot(hb, w_ref[:, off:off + step]), 0.0, 0.0, seq_len,
                           cw_ref[:, coff:coff + step], cb_ref[:, coff:coff + step])

    xs = jnp.concatenate([_silu(conv(o_xbc + c0, cwa_ref, cba_ref, c0)) for c0 in range(0, d_xbc, step)], axis=1)
    dt_raw = _dot(hb, w_ref[:, o_dt:o_dt + LANE])
    xa = xs[:, 0:d_a]
    bm = _bf(xs[:, d_a:d_a + N_GROUPS * N_STATE])
    cm = _bf(xs[:, d_a + N_GROUPS * N_STATE:d_xbc])
    dtb, alog = dtb_ref[...], alog_ref[...]
    y_rows = []
    for s in range(nseq):
        state = [jnp.zeros((N_STATE, d_a), F32), jnp.zeros((N_STATE, d_a), F32)]
        y_chunks = [None] * nc
        for direction, order in ((0, range(nc)), (1, reversed(range(nc)))):
            for c in order:
                rows = slice(s * seq_len + c * q, s * seq_len + (c + 1) * q)
                y, state[direction] = _ssd_chunk(direction, xa[rows], bm[rows], cm[rows], dt_raw[rows],
                                                 state[direction], dtb, alog,
                                                 dskip_ref[direction:direction + 1, :], ex_ref[direction])
                y_chunks[c] = y if y_chunks[c] is None else y_chunks[c] + y
        for direction in range(2):
            for t in range(d_a // LANE):
                fin_ref[s, direction, t * LANE:(t + 1) * LANE, :] = state[direction][:, t * LANE:(t + 1) * LANE].T
        y_rows += y_chunks
    z = jnp.concatenate([_dot(hb, w_ref[:, o_z + c0:o_z + c0 + step]) for c0 in range(0, d_a, step)], axis=1)
    ya = _rms(jnp.concatenate(y_rows, axis=0) * _silu(z), naw_ref[...])

    yh_cols = []
    for c0 in range(0, d_b, step):
        wv = conv(o_u + d_b + c0, cwb_ref, cbb_ref, d_b + c0) * conv(o_u + 2 * d_b + c0, cwb_ref, cbb_ref, 2 * d_b + c0)
        gate = conv(o_u + c0, cwb_ref, cbb_ref, c0) * _silu(_dot(hb, w_ref[:, o_g + c0:o_g + c0 + step]))
        spec = jnp.concatenate([gs_ref[c0 // LANE + t, 0] for t in range(step // LANE)], axis=1)
        g_re, g_im = spec[0:p], spec[p:]
        outs = []
        for s in range(nseq):
            rows = slice(s * seq_len, (s + 1) * seq_len)
            u = _dot(fa_ref[...], _bf(wv[rows]))
            u_re, u_im = u[0:p], u[p:]
            prod = jnp.concatenate([g_re * u_re - g_im * u_im, g_re * u_im + g_im * u_re], axis=0)
            lc = _dot(iv_ref[...], _bf(prod))
            outs.append(gate[rows] * (lc + wv[rows] * hyb_ref[:, c0:c0 + step]))
        yh_cols.append(jnp.concatenate(outs, axis=0))
    yh = jnp.concatenate(yh_cols, axis=1)

    acc = _dot(_bf(ya), wo_ref[0:d_a, :]) + _dot(_bf(yh), wo_ref[d_a:, :])
    o_ref[...] = x + m[:, 2 * d:3 * d] * acc


def _ctx_layer0(x2d, b, l, mod, norm_w, w_in_bf, cols, p, w_out_bf, filt_params):
    m_rows, d = x2d.shape
    tm = ROW_TILE
    assert m_rows % tm == 0 and tm % l == 0 and l <= DFT_BLOCK and l % CHUNK == 0
    nseq = tm // l
    dm = N_HEADS * HD
    fa, iv = _dft_tables(l)
    spectra = _hyena_spectra(l, l, fa, *filt_params)
    consts = ([norm_w, w_in_bf, p["conv_a_w"], p["conv_a_b"].reshape(1, -1), p["conv_b_w"], p["conv_b_b"].reshape(1, -1)]
              + list(_ssd_consts(p["dt_bias"], p["a_log"], p["d_skip"]))
              + [p["norm_a_w"], w_out_bf, spectra, p["hy_bias"].reshape(1, -1), fa, iv])

    def const(a):
        nd = a.ndim
        return pl.BlockSpec(a.shape, lambda i: (0,) * nd, pipeline_mode=pl.Buffered(1))

    return pl.pallas_call(
        functools.partial(_ctx_layer0_kernel, seq_len=l, cols=cols),
        grid=(m_rows // tm,),
        in_specs=[pl.BlockSpec((tm, d), lambda i: (i, 0)), pl.BlockSpec((1, 1, 3 * d), lambda i: (0, 0, 0))]
                 + [const(a) for a in consts],
        out_specs=[pl.BlockSpec((tm, d), lambda i: (i, 0)),
                   pl.BlockSpec((nseq, 2, dm, N_STATE), lambda i: (i, 0, 0, 0))],
        out_shape=[jax.ShapeDtypeStruct((m_rows, d), F32), jax.ShapeDtypeStruct((b, 2, dm, N_STATE), F32)],
        compiler_params=_params("arbitrary"),
        name="ctx_layer0",
    )(x2d, mod, *consts)


def _reorder_kernel(w_ref, o_ref, *, o_dt, n_dt):
    rows, n = w_ref.shape
    rest = n - o_dt - n_dt
    o_ref[:, 0:o_dt] = w_ref[:, 0:o_dt].astype(o_ref.dtype)
    o_ref[:, o_dt:o_dt + rest] = w_ref[:, o_dt + n_dt:n].astype(o_ref.dtype)
    tail = jnp.concatenate([w_ref[:, o_dt:o_dt + n_dt], jnp.zeros((rows, LANE - n_dt), F32)], axis=1)
    o_ref[:, o_dt + rest:o_dt + rest + LANE] = tail.astype(o_ref.dtype)


def _reorder_w_in(w, o_dt, n_dt):
    d, n = w.shape
    n_out = n - n_dt + LANE
    tr = LANE
    return pl.pallas_call(
        functools.partial(_reorder_kernel, o_dt=o_dt, n_dt=n_dt),
        grid=(d // tr,),
        in_specs=[pl.BlockSpec((tr, n), lambda i: (i, 0))],
        out_specs=pl.BlockSpec((tr, n_out), lambda i: (i, 0)),
        out_shape=jax.ShapeDtypeStruct((d, n_out), BF16),
        compiler_params=_params("arbitrary"),
        name="reorder_w_in",
    )(w)


def _layer0_cols(d_b):
    dm = N_HEADS * HD
    d_xbc = dm + 2 * N_GROUPS * N_STATE
    return (0, dm, dm + d_xbc, dm + d_xbc + 3 * d_b, dm + d_xbc + 4 * d_b)


def _latent_layer0(x2d, b, l, mod, mod_base, norm_w, w_in_bf, w_out_bf, p, init, filt_params):
    cols = _layer0_cols(p["hy_bias"].shape[0])
    consts = _ssd_consts(p["dt_bias"], p["a_log"], p["d_skip"])
    z, xs, w_t, gate_t, dt_raw, y_f = _proj_in0(x2d, mod, norm_w, w_in_bf, cols, p["conv_a_w"], p["conv_a_b"],
                                                p["conv_b_w"], p["conv_b_b"], consts, init, l, l, mod_base)
    yh_t = _hyena(w_t, gate_t, filt_params, p["hy_bias"], b, l)
    return _proj_out0(x2d, y_f, xs, dt_raw, z, yh_t, mod, p["norm_a_w"], w_out_bf, consts, init, l, l, mod_base)


def kernel(x_prompt, x_sample, state_ssd, cache_k, cache_v, c, c_ctx, norm_w, w_ada, b_ada, w_in_e, w_out_e, conv_a_w, conv_a_b, dt_bias, a_log, d_skip, norm_a_w, conv_b_w, conv_b_b, hf_w1, hf_b1, hf_w2, hf_b2, hf_w3, hf_freq, hy_bias, w_in_o, w_out_o, rpb, final_norm_w):
    bp, lp, d = x_prompt.shape
    bs, ls, _ = x_sample.shape
    dm = N_HEADS * HD
    d_xbc = dm + 2 * N_GROUPS * N_STATE
    n_dt = 2 * N_HEADS

    cvecs = jnp.concatenate([c_ctx[None], c, jnp.zeros((SUBLANE - 1 - bs, d), F32)], axis=0)
    mods = _ada_mods(cvecs, w_ada, b_ada)

    xp = x_prompt.reshape(bp * lp, d)
    xs = x_sample.reshape(bs * ls, d)

    w_in0 = _reorder_w_in(w_in_e[0], dm + d_xbc, n_dt)
    w_out0 = w_out_e[0].astype(BF16)
    p0 = dict(conv_a_w=conv_a_w[0], conv_a_b=conv_a_b[0], dt_bias=dt_bias[0], a_log=a_log[0], d_skip=d_skip[0],
              norm_a_w=norm_a_w[0].reshape(1, -1), conv_b_w=conv_b_w[0], conv_b_b=conv_b_b[0], hy_bias=hy_bias[0])
    mod0 = mods[0].reshape(SUBLANE, 1, 3 * d)
    nw0 = norm_w[0].reshape(1, d)
    hf = (hf_w1[0], hf_b1[0], hf_w2[0], hf_b2[0], hf_w3[0], hf_freq[0])
    xp, fin = _ctx_layer0(xp, bp, lp, mod0, nw0, w_in0, _layer0_cols(hy_bias.shape[1]), p0, w_out0, hf)
    init_s = state_ssd[:, 0].reshape(bs, 2, dm, N_STATE)
    xs = _latent_layer0(xs, bs, ls, mod0, 1, nw0, w_in0, w_out0, p0, init_s, hf)
    new_state_ssd = fin.reshape(bp, 1, 2, N_HEADS, HD, N_STATE)

    w_in1 = w_in_o[0].astype(BF16)
    w_out1 = w_out_o[0].astype(BF16)
    mod1 = mods[1].reshape(SUBLANE, 1, 3 * d)
    nw1 = norm_w[1].reshape(1, d)
    fw = final_norm_w.reshape(1, d)
    y_prompt, new_cache_k, new_cache_v = _ctx_layer(xp, bp, lp, mod1, nw1, w_in1, w_out1, fw)
    y_prompt = y_prompt.reshape(bp, lp, d)

    segs_s = ((0, d, TILED, BF16), (d, d, TILED, BF16), (0, d, CHAN, BF16), (3 * d, d, ROWS, BF16))
    wv_t = w_in_o[0][:, 2 * d:3 * d].T.astype(BF16)
    q_t, k_t, v_c, g = _proj_in(xs, mod1, nw1, w_in1, segs_s, ls, 1, w_t=wv_t)
    bias = _na_bias_tables(rpb[0])
    lc = cache_v.shape[3]
    cache_vt = jnp.swapaxes(cache_v[:, 0], 2, 3).reshape(bs, d // LANE, LANE, lc).astype(BF16)
    o_t = _na_attn(q_t, k_t, v_c, cache_k[:, 0], cache_vt, bias, bs, ls)
    y_sample = _proj_out1(xs, o_t, g, mod1, fw, w_out1, ls, 1).reshape(bs, ls, d)

    return (y_prompt, y_sample, new_state_ssd, new_cache_k, new_cache_v)
```

```python
import functools
import math

import jax
import jax.numpy as jnp
import numpy as np
from jax import lax
from jax.experimental import pallas as pl
from jax.experimental.pallas import tpu as pltpu

F32 = jnp.float32
BF16 = jnp.bfloat16

EPS = 1e-6
GRID_W = 64
WIN_H = 8
WIN_W = 16
HD = 64
N_HEADS = 16
N_STATE = 128
N_GROUPS = 2
CHUNK = 128
HY_EMB = 33
HY_BANDS = (HY_EMB - 1) // 2
HY_HID = 64
HY_TARGET = 1e-2
HY_DECAY_PCT_HI = 0.3
HY_DECAY_PCT_LO = 1.5

LANE = 128
SUBLANE = 8
VMEM_LIMIT = 56 * 1024 * 1024

ROW_TILE = 512
DFT_BLOCK = 512
HY_BATCH = 32
HY_BATCH_LONG = 2
NA_QROWS = 8
NA_QTILE = 4
NA_KCHUNK = 128
NA_BAND = NA_QTILE + WIN_H
NEG_INF = float("-inf")


def _bf(x):
    return x.astype(BF16)


def _dot(a, b):
    return jnp.dot(a, b, preferred_element_type=F32)


def _dot_nt(a, b):
    return lax.dot_general(a, b, (((1,), (1,)), ((), ())), preferred_element_type=F32)


def _split2(x):
    hi = _bf(x)
    lo = _bf(x - hi.astype(F32))
    return hi, lo


def _split3(x):
    hi = _bf(x)
    r = x - hi.astype(F32)
    mid = _bf(r)
    lo = _bf(r - mid.astype(F32))
    return hi, mid, lo


def _dot3(a, b):
    ah, al = _split2(a)
    bh, bl = _split2(b)
    return _dot(ah, bh) + (_dot(ah, bl) + _dot(al, bh))


def _dot_rhs_parts(a_exact, b, parts):
    pieces = _split3(b) if parts == 3 else _split2(b)
    acc = _dot(a_exact, pieces[0])
    for p in pieces[1:]:
        acc = acc + _dot(a_exact, p)
    return acc


def _silu(x):
    return x * jax.nn.sigmoid(x)


def _rms(x, g):
    ms = jnp.mean(x * x, axis=-1, keepdims=True)
    return x * lax.rsqrt(ms + EPS) * g


def _softplus(x):
    return jnp.maximum(x, 0.0) + jnp.log1p(jnp.exp(-jnp.abs(x)))


def _params(*sem):
    return pltpu.CompilerParams(dimension_semantics=sem, vmem_limit_bytes=VMEM_LIMIT)


def _mods_kernel(c_ref, w_ref, b_ref, o_ref):
    a = _silu(c_ref[...])
    o_ref[0] = _dot3(a, w_ref[0]) + b_ref[0]


def _ada_mods(cvecs, w_ada, b_ada):
    depth, d, n3 = w_ada.shape
    tn = n3 // 4
    return pl.pallas_call(
        _mods_kernel,
        grid=(depth, n3 // tn),
        in_specs=[pl.BlockSpec((SUBLANE, d), lambda l, j: (0, 0)),
                  pl.BlockSpec((1, d, tn), lambda l, j: (l, 0, j)),
                  pl.BlockSpec((1, 1, tn), lambda l, j: (l, 0, j))],
        out_specs=pl.BlockSpec((1, SUBLANE, tn), lambda l, j: (l, 0, j)),
        out_shape=jax.ShapeDtypeStruct((depth, SUBLANE, n3), F32),
        compiler_params=_params("arbitrary", "arbitrary"),
        name="ada_mods",
    )(cvecs, w_ada, b_ada.reshape(depth, 1, n3))


def _conv3_rows(res, prev_row, next_row, seq_len, cw, cb):
    tm, width = res.shape
    starts = list(range(0, tm, seq_len))
    ends = [min(s + seq_len, tm) - 1 for s in starts]
    sub = lax.broadcasted_iota(jnp.int32, (SUBLANE, width), 0)
    down = pltpu.roll(res, 1, 0)
    up = pltpu.roll(res, tm - 1, 0)
    dparts, uparts, pos = [], [], 0
    for s in starts:
        fill = prev_row if s == 0 else 0.0
        dparts += [down[pos:s], jnp.where(sub == 0, fill, down[s:s + SUBLANE])]
        pos = s + SUBLANE
    dparts.append(down[pos:tm])
    pos = 0
    for e in ends:
        fill = next_row if e == tm - 1 else 0.0
        uparts += [up[pos:e + 1 - SUBLANE], jnp.where(sub == SUBLANE - 1, fill, up[e + 1 - SUBLANE:e + 1])]
        pos = e + 1
    uparts.append(up[pos:tm])
    down = jnp.concatenate([p for p in dparts if p.shape[0]], axis=0)
    up = jnp.concatenate([p for p in uparts if p.shape[0]], axis=0)
    return cb + down * cw[0:1] + res * cw[1:2] + up * cw[2:3]


def _proj_in0_kernel(x_ref, xp_ref, xn_ref, mod_ref, nw_ref, w_ref, cwa_ref, cba_ref, cwb_ref, cbb_ref,
                     dtb_ref, alog_ref, dskip_ref, ex_ref, init_ref,
                     z_ref, xs_ref, wv_ref, gate_ref, dt_ref, yf_ref, s_scr, *, seq_len, cols):
    i = pl.program_id(0)
    tm, d = x_ref.shape
    o_z, o_xbc, o_u, o_g, o_dt = cols
    d_a, d_xbc, d_b = o_xbc - o_z, o_u - o_xbc, o_dt - o_g
    m = mod_ref[0]

    def modnorm(x):
        return _bf(_rms(x, nw_ref[...]) * (1.0 + m[:, d:2 * d]) + m[:, 0:d])

    hb = modnorm(x_ref[...])
    hh = modnorm(jnp.concatenate([xp_ref[...], xn_ref[...]], axis=0))
    keep_prev = ((i * tm) % seq_len != 0).astype(F32)
    keep_next = (((i + 1) * tm) % seq_len != 0).astype(F32)

    def conv(off, width, cw_ref, cb_ref, coff):
        res = _dot(hb, w_ref[:, off:off + width])
        rh = _dot(hh, w_ref[:, off:off + width])
        return _conv3_rows(res, rh[SUBLANE - 1:SUBLANE] * keep_prev, rh[SUBLANE:SUBLANE + 1] * keep_next, seq_len,
                           cw_ref[:, coff:coff + width], cb_ref[:, coff:coff + width])

    step = 2 * LANE
    for c0 in range(0, d_a, step):
        z_ref[:, c0:c0 + step] = _dot(hb, w_ref[:, o_z + c0:o_z + c0 + step]).astype(z_ref.dtype)
    for c0 in range(0, d_xbc, step):
        xs_ref[:, c0:c0 + step] = _silu(conv(o_xbc + c0, step, cwa_ref, cba_ref, c0)).astype(xs_ref.dtype)
    for c0 in range(0, d_b, step):
        x1 = conv(o_u + d_b + c0, step, cwb_ref, cbb_ref, d_b + c0)
        v = conv(o_u + 2 * d_b + c0, step, cwb_ref, cbb_ref, 2 * d_b + c0)
        wv = x1 * v
        x0 = conv(o_u + c0, step, cwb_ref, cbb_ref, c0)
        gate = x0 * _silu(_dot(hb, w_ref[:, o_g + c0:o_g + c0 + step]))
        for t in range(step // LANE):
            wv_ref[c0 // LANE + t] = wv[:, t * LANE:(t + 1) * LANE].astype(wv_ref.dtype)
            gate_ref[c0 // LANE + t] = gate[:, t * LANE:(t + 1) * LANE].astype(gate_ref.dtype)
    dt_raw = _dot(hb, w_ref[:, o_dt:o_dt + LANE])
    dt_ref[...] = dt_raw
    y_f = _ssd_tile(0, xs_ref[...], dt_raw, s_scr, init_ref, (i * tm) % seq_len == 0,
                    dtb_ref[...], alog_ref[...], dskip_ref[0:1, :], ex_ref[0])
    yf_ref[...] = y_f.astype(yf_ref.dtype)


def _proj_in0(x2d, mod, norm_w, w_bf, cols, conv_a_w, conv_a_b, conv_b_w, conv_b_b, ssd_consts, init, seq_len,
              rows_per_mod, mod_base):
    m_rows, d = x2d.shape
    tm = ROW_TILE
    assert m_rows % tm == 0 and rows_per_mod % tm == 0 and seq_len % tm == 0 and tm % CHUNK == 0
    o_z, o_xbc, o_u, o_g, o_dt = cols
    d_a, d_xbc, d_b = o_xbc - o_z, o_u - o_xbc, o_dt - o_g
    nsub = m_rows // SUBLANE
    spt = tm // SUBLANE
    cba, cbb = conv_a_b.reshape(1, -1), conv_b_b.reshape(1, -1)
    consts = [norm_w, w_bf, conv_a_w, cba, conv_b_w, cbb] + list(ssd_consts)

    def const(a):
        nd = a.ndim
        return pl.BlockSpec(a.shape, lambda i: (0,) * nd, pipeline_mode=pl.Buffered(1))

    def tiles(n):
        return pl.BlockSpec((n, tm, LANE), lambda i: (0, i, 0))

    def rows(wd):
        return pl.BlockSpec((tm, wd), lambda i: (i, 0))

    return pl.pallas_call(
        functools.partial(_proj_in0_kernel, seq_len=seq_len, cols=cols),
        grid=(m_rows // tm,),
        in_specs=[rows(d),
                  pl.BlockSpec((SUBLANE, d), lambda i: (jnp.maximum(i * spt - 1, 0), 0)),
                  pl.BlockSpec((SUBLANE, d), lambda i: (jnp.minimum((i + 1) * spt, nsub - 1), 0)),
                  pl.BlockSpec((1, 1, 3 * d), lambda i: (mod_base + (i * tm) // rows_per_mod, 0, 0))]
                 + [const(a) for a in consts]
                 + [pl.BlockSpec((1, 1, d_a, N_STATE), lambda i: ((i * tm) // seq_len, 0, 0, 0))],
        out_specs=[rows(d_a), rows(d_xbc), tiles(d_b // LANE), tiles(d_b // LANE), rows(LANE), rows(d_a)],
        out_shape=[jax.ShapeDtypeStruct((m_rows, d_a), BF16), jax.ShapeDtypeStruct((m_rows, d_xbc), BF16),
                   jax.ShapeDtypeStruct((d_b // LANE, m_rows, LANE), BF16),
                   jax.ShapeDtypeStruct((d_b // LANE, m_rows, LANE), BF16),
                   jax.ShapeDtypeStruct((m_rows, LANE), F32), jax.ShapeDtypeStruct((m_rows, d_a), BF16)],
        scratch_shapes=[pltpu.VMEM((N_STATE, d_a), F32)],
        compiler_params=_params("arbitrary"),
        name="proj_in0",
    )(x2d, x2d, x2d, mod, *consts, init)


def _ssd_chunk(d, xa, bm, cm, dt_raw, s_prev, dtb, alog, dskip, ex):
    q, dm = xa.shape
    gw = dm // N_GROUPS
    hpg = N_HEADS // N_GROUPS
    row = lax.broadcasted_iota(jnp.int32, (q, q), 0)
    col = lax.broadcasted_iota(jnp.int32, (q, q), 1)
    left = lax.broadcasted_iota(jnp.int32, (q, LANE), 1) < HD
    tri = (row >= col) if d == 0 else (row <= col)

    dt = _softplus(dt_raw + dtb)
    adt = dt * (-jnp.exp(alog))
    cs = _dot_rhs_parts(jnp.where(tri, 1.0, 0.0).astype(BF16), adt, 3)
    cs_t = cs.T
    dt_t = dt.T
    edge = cs[q - 1:q, :] if d == 0 else cs[0:1, :]
    e_cs = _dot(_bf(jnp.exp(cs)), ex)
    w_st = _dot(_bf(jnp.exp(edge - cs) * dt), ex)
    xw = xa * w_st

    y_parts = []
    new_state = []
    for g in range(N_GROUPS):
        bg = bm[:, g * N_STATE:(g + 1) * N_STATE]
        cg = cm[:, g * N_STATE:(g + 1) * N_STATE]
        gmat = _dot_nt(_bf(cg), _bf(bg))
        y_off = _dot(_bf(cg), _bf(s_prev[:, g * gw:(g + 1) * gw]))
        new_state.append(_dot(_bf(bg.astype(F32).T), _bf(xw[:, g * gw:(g + 1) * gw])))
        for pr in range(hpg // 2):
            mh = []
            for j in range(2):
                k = d * N_HEADS + g * hpg + 2 * pr + j
                diff = cs[:, k:k + 1] - cs_t[k:k + 1, :]
                lm = jnp.exp(jnp.where(tri, diff, NEG_INF))
                mh.append(_bf(gmat * lm * dt_t[k:k + 1, :]))
            c0 = g * gw + pr * LANE
            xpair = xa[:, c0:c0 + LANE]
            rhs = jnp.concatenate([_bf(jnp.where(left, xpair, 0.0)), _bf(jnp.where(left, 0.0, xpair))], axis=0)
            y_d = _dot(jnp.concatenate(mh, axis=1), rhs)
            y_parts.append(y_d + y_off[:, pr * LANE:(pr + 1) * LANE] * e_cs[:, c0:c0 + LANE])
    y = jnp.concatenate(y_parts, axis=1) + xa * dskip
    e_edge = e_cs[q - 1:q, :] if d == 0 else e_cs[0:1, :]
    return y, s_prev * e_edge + jnp.concatenate(new_state, axis=1)


def _ssd_tile(direction, xs, dt_raw, s_scr, init_ref, at_seq_edge, dtb, alog, dskip, ex):
    rows, dm = xs.shape[0], N_HEADS * HD

    @pl.when(at_seq_edge)
    def _():
        for t in range(dm // LANE):
            s_scr[:, t * LANE:(t + 1) * LANE] = init_ref[0, 0, t * LANE:(t + 1) * LANE, :].T

    nck = rows // CHUNK
    ys = [None] * nck
    state = s_scr[...]
    for c in (range(nck) if direction == 0 else reversed(range(nck))):
        r = slice(c * CHUNK, (c + 1) * CHUNK)
        ys[c], state = _ssd_chunk(direction, xs[r, 0:dm].astype(F32), xs[r, dm:dm + N_GROUPS * N_STATE],
                                  xs[r, dm + N_GROUPS * N_STATE:], dt_raw[r], state, dtb, alog, dskip, ex)
    s_scr[...] = state
    return jnp.concatenate(ys, axis=0)


def _ssd_consts(dt_bias, a_log, d_skip):
    dm = N_HEADS * HD
    ex = np.zeros((2, LANE, dm), np.float32)
    for d in range(2):
        for h in range(N_HEADS):
            ex[d, d * N_HEADS + h, h * HD:(h + 1) * HD] = 1.0
    pad = LANE - 2 * N_HEADS
    dtb = jnp.pad(dt_bias.reshape(1, 2 * N_HEADS), ((0, 0), (0, pad)))
    alog = jnp.pad(a_log.reshape(1, 2 * N_HEADS), ((0, 0), (0, pad)))
    return dtb, alog, jnp.repeat(d_skip, HD, axis=1), jnp.asarray(ex, BF16)


def _hyena_tables(l):
    pos = np.abs(np.arange(2 * l, dtype=np.float64) - l)
    t = pos / (l - 1)
    w = 2.0 * math.pi * pos / l
    f = np.linspace(1e-4, HY_BANDS - 1, HY_BANDS)
    feats = np.zeros((2 * l, LANE), np.float64)
    feats[:, 0] = t
    feats[:, 1:1 + HY_BANDS] = np.cos(f[None] * w[:, None])
    feats[:, 1 + HY_BANDS:1 + 2 * HY_BANDS] = -np.sin(f[None] * w[:, None])
    return jnp.asarray(feats, F32)


def _dft_tables(p):
    n = 2 * p
    f = np.arange(p, dtype=np.float64)[:, None] + 0.5
    e = np.arange(p, dtype=np.float64)[None]
    ang = 2.0 * math.pi * f * e / n
    fa = np.concatenate([np.cos(ang), -np.sin(ang)], axis=0)
    inv = np.concatenate([np.cos(ang.T), -np.sin(ang.T)], axis=1) * (2.0 / n)
    return tuple(jnp.asarray(m, F32).astype(BF16) for m in (fa, inv))


def _spectra_kernel(f_ref, w1_ref, b1_ref, w2_ref, b2_ref, w3_ref, fr_ref, ad_ref, fa_ref, o_ref, bprev):
    q = pl.program_id(0)
    p = f_ref.shape[0]
    half = p // 2
    feats = f_ref[...]
    pre1 = _dot3(feats, w1_ref[...])
    packed = jnp.concatenate([pre1[:half], pre1[half:]], axis=1)
    fr = fr_ref[...]
    h1 = jnp.sin(fr * (packed + b1_ref[...]))
    h2 = _bf(jnp.sin(fr * (_dot3(h1, w2_ref[...]) + b2_ref[...])))
    w3 = _bf(w3_ref[...])
    filt = jnp.concatenate([_dot(h2[:, :HY_HID], w3), _dot(h2[:, HY_HID:], w3)], axis=0)
    taps = _bf(filt * jnp.exp(-feats[:, 0:1] * ad_ref[...]))
    a = _dot(fa_ref[...], taps)

    @pl.when(q > 0)
    def _():
        g = a + bprev[...]
        for t in range(o_ref.shape[0]):
            o_ref[t, 0] = g[:, t * LANE:(t + 1) * LANE]

    odd = (lax.broadcasted_iota(jnp.int32, (p, a.shape[1]), 0) & 1) == 1
    a_re, a_im = a[0:p], a[p:]
    a_re0 = a_re - taps[0:1, :].astype(F32)
    bprev[0:p] = jnp.where(odd, a_im, -a_im)
    bprev[p:] = jnp.where(odd, -a_re0, a_re0)


def _hyena_spectra(l, p, fa, w1, b1, w2, b2, w3, freq):
    db = w3.shape[1] // 2
    nblk = 2 * l // p
    nct = db // LANE
    feats = _hyena_tables(l)
    w1p = jnp.pad(w1, ((0, LANE - HY_EMB), (0, 0)))
    zero = jnp.zeros_like(w2)
    w2bd = jnp.concatenate([jnp.concatenate([w2, zero], axis=1), jnp.concatenate([zero, w2], axis=1)], axis=0)
    deltas = np.linspace(math.log(HY_TARGET) / HY_DECAY_PCT_HI, math.log(HY_TARGET) / HY_DECAY_PCT_LO, db)
    absd = jnp.asarray(np.abs(deltas)[None], F32)
    b1r, b2r, frr = (jnp.tile(v.reshape(1, -1), (1, 2)) for v in (b1, b2, freq))

    def full(a):
        return pl.BlockSpec(a.shape, lambda q: (0, 0))

    return pl.pallas_call(
        _spectra_kernel,
        grid=(nblk,),
        in_specs=[pl.BlockSpec((p, LANE), lambda q: (q, 0)), full(w1p), full(b1r), full(w2bd), full(b2r),
                  pl.BlockSpec((HY_HID, db), lambda q: (0, jnp.where(q < nblk // 2, 1, 0))),
                  full(frr), full(absd), full(fa)],
        out_specs=pl.BlockSpec((nct, 1, 2 * p, LANE), lambda q: (0, jnp.maximum(q - 1, 0), 0, 0)),
        out_shape=jax.ShapeDtypeStruct((nct, nblk - 1, 2 * p, LANE), F32),
        scratch_shapes=[pltpu.VMEM((2 * p, db), F32)],
        compiler_params=_params("arbitrary"),
        name="hyena_spectra",
    )(feats, w1p, b1r, w2bd, b2r, w3, frr, absd, fa)


def _hyena_kernel(w_ref, gate_ref, gs_ref, hb_ref, fa_ref, iv_ref, o_ref, u_scr, y_scr, *, p):
    bt, l = w_ref.shape[1], w_ref.shape[2]
    nb = l // p
    fa = fa_ref[...]
    for j in range(nb):
        rhs = jnp.concatenate([w_ref[0, bb, j * p:(j + 1) * p, :] for bb in range(bt)], axis=1)
        u_scr[j] = _dot(fa, rhs)
    rt_rows = 64
    for i in range(nb):
        def body(rt, carry):
            r0 = pl.multiple_of(rt * rt_rows, rt_rows)
            for bb in range(bt):
                ls = slice(bb * LANE, (bb + 1) * LANE)
                acc_re = jnp.zeros((rt_rows, LANE), F32)
                acc_im = jnp.zeros((rt_rows, LANE), F32)
                for j in range(nb):
                    s = i - j + nb - 1
                    gre = gs_ref[0, s, pl.ds(r0, rt_rows), :]
                    gim = gs_ref[0, s, pl.ds(p + r0, rt_rows), :]
                    ure = u_scr[j, pl.ds(r0, rt_rows), ls]
                    uim = u_scr[j, pl.ds(p + r0, rt_rows), ls]
                    acc_re = acc_re + (gre * ure - gim * uim)
                    acc_im = acc_im + (gre * uim + gim * ure)
                y_scr[pl.ds(r0, rt_rows), ls] = acc_re
                y_scr[pl.ds(p + r0, rt_rows), ls] = acc_im
            return carry
        lax.fori_loop(0, p // rt_rows, body, 0)
        conv = _dot(iv_ref[...], _bf(y_scr[...]))
        sl = slice(i * p, (i + 1) * p)
        for bb in range(bt):
            wi = w_ref[0, bb, sl, :].astype(F32)
            o_ref[0, bb, sl, :] = (gate_ref[0, bb, sl, :].astype(F32)
                                   * (conv[:, bb * LANE:(bb + 1) * LANE] + wi * hb_ref[...])).astype(o_ref.dtype)


def _hyena(w_t, gate_t, filt_params, hy_bias, b, l):
    nct = w_t.shape[0]
    p = min(DFT_BLOCK, l)
    nb = l // p
    nseg = 2 * nb - 1
    bt = min(b, HY_BATCH if nb == 1 else HY_BATCH_LONG)
    fa, iv = _dft_tables(p)
    spectra = _hyena_spectra(l, p, fa, *filt_params)
    w4 = w_t.reshape(nct, b, l, LANE)
    g4 = gate_t.reshape(nct, b, l, LANE)
    hbr = hy_bias.reshape(1, -1)
    act = pl.BlockSpec((1, bt, l, LANE), lambda ct, bi: (ct, bi, 0, 0))

    def full(a):
        return pl.BlockSpec(a.shape, lambda ct, bi: (0, 0), pipeline_mode=pl.Buffered(1))

    out = pl.pallas_call(
        functools.partial(_hyena_kernel, p=p),
        grid=(nct, b // bt),
        in_specs=[act, act,
                  pl.BlockSpec((1, nseg, 2 * p, LANE), lambda ct, bi: (ct, 0, 0, 0), pipeline_mode=pl.Buffered(1)),
                  pl.BlockSpec((1, LANE), lambda ct, bi: (0, ct)),
                  full(fa), full(iv)],
        out_specs=act,
        out_shape=jax.ShapeDtypeStruct((nct, b, l, LANE), BF16),
        scratch_shapes=[pltpu.VMEM((nb, 2 * p, bt * LANE), F32), pltpu.VMEM((2 * p, bt * LANE), F32)],
        compiler_params=_params("arbitrary", "arbitrary"),
        name="hyena_conv",
    )(w4, g4, spectra, hbr, fa, iv)
    return out.reshape(nct, b * l, LANE)


def _proj_out0_kernel(x_ref, yf_ref, xs_ref, dt_ref, z_ref, yh_ref, mod_ref, mod1_ref, naw_ref, w_ref, dtb_ref,
                      alog_ref, dskip_ref, ex_ref, nw1_ref, w1_ref, wvt_ref, init_ref,
                      o_ref, q_ref, k_ref, v_ref, g_ref, s_scr, *, seq_len, n_tiles):
    j = n_tiles - 1 - pl.program_id(0)
    tm, d = x_ref.shape
    y_b = _ssd_tile(1, xs_ref[...], dt_ref[...], s_scr, init_ref, ((j + 1) * tm) % seq_len == 0,
                    dtb_ref[...], alog_ref[...], dskip_ref[1:2, :], ex_ref[1])
    ya = _rms((yf_ref[...].astype(F32) + y_b) * _silu(z_ref[...].astype(F32)), naw_ref[...])
    yh = jnp.concatenate([yh_ref[t] for t in range(yh_ref.shape[0])], axis=1)
    da = ya.shape[1]
    acc = _dot(_bf(ya), w_ref[0:da, :]) + _dot(_bf(yh), w_ref[da:, :])
    x_new = x_ref[...] + mod_ref[0][:, 2 * d:3 * d] * acc
    o_ref[...] = x_new

    m1 = mod1_ref[0]
    h1 = _bf(_rms(x_new, nw1_ref[...]) * (1.0 + m1[:, d:2 * d]) + m1[:, 0:d])
    step = 2 * LANE
    for off, t_ref in ((0, q_ref), (d, k_ref)):
        for c0 in range(0, d, step):
            res = _dot(h1, w1_ref[:, off + c0:off + c0 + step])
            for t in range(step // LANE):
                t_ref[c0 // LANE + t] = res[:, t * LANE:(t + 1) * LANE].astype(t_ref.dtype)
    v_ref[...] = _dot_nt(wvt_ref[...], h1).astype(v_ref.dtype)
    for c0 in range(0, d, step):
        g_ref[:, c0:c0 + step] = _dot(h1, w1_ref[:, 3 * d + c0:3 * d + c0 + step]).astype(g_ref.dtype)


def _proj_out0(x2d, y_f, xs, dt_raw, z, yh_t, mod, mod1, norm_a_w, w_bf, ssd_consts, norm_w1, w_in1_bf, wv_t, init,
               seq_len, rows_per_mod, mod_base):
    m_rows, d = x2d.shape
    tm = ROW_TILE
    assert m_rows % tm == 0 and rows_per_mod % tm == 0 and seq_len % tm == 0 and tm % CHUNK == 0
    da = y_f.shape[1]
    nt = yh_t.shape[0]
    n_tiles = m_rows // tm
    consts = [norm_a_w, w_bf] + list(ssd_consts) + [norm_w1, w_in1_bf, wv_t]

    def rowspec(wd):
        return pl.BlockSpec((tm, wd), lambda i: (n_tiles - 1 - i, 0))

    def tiles(n):
        return pl.BlockSpec((n, tm, LANE), lambda i: (0, n_tiles - 1 - i, 0))

    def modspec():
        return pl.BlockSpec((1, 1, 3 * d), lambda i: (mod_base + ((n_tiles - 1 - i) * tm) // rows_per_mod, 0, 0))

    def const(a):
        nd = a.ndim
        return pl.BlockSpec(a.shape, lambda i: (0,) * nd, pipeline_mode=pl.Buffered(1))

    return pl.pallas_call(
        functools.partial(_proj_out0_kernel, seq_len=seq_len, n_tiles=n_tiles),
        grid=(n_tiles,),
        in_specs=[rowspec(d), rowspec(da), rowspec(xs.shape[1]), rowspec(LANE), rowspec(da), tiles(nt),
                  modspec(), modspec()]
                 + [const(a) for a in consts]
                 + [pl.BlockSpec((1, 1, da, N_STATE), lambda i: (((n_tiles - 1 - i) * tm) // seq_len, 1, 0, 0))],
        out_specs=[rowspec(d), tiles(d // LANE), tiles(d // LANE),
                   pl.BlockSpec((d, tm), lambda i: (0, n_tiles - 1 - i)), rowspec(d)],
        out_shape=[jax.ShapeDtypeStruct((m_rows, d), F32),
                   jax.ShapeDtypeStruct((d // LANE, m_rows, LANE), BF16),
                   jax.ShapeDtypeStruct((d // LANE, m_rows, LANE), BF16),
                   jax.ShapeDtypeStruct((d, m_rows), BF16), jax.ShapeDtypeStruct((m_rows, d), BF16)],
        scratch_shapes=[pltpu.VMEM((N_STATE, da), F32)],
        compiler_params=_params("arbitrary"),
        name="proj_out0",
    )(x2d, y_f, xs, dt_raw, z, yh_t, mod, mod1, *consts, init)


def _proj_out1_kernel(x_ref, o_ref_in, g_ref, mod_ref, fw_ref, w_ref, y_ref):
    d = x_ref.shape[1]
    o = jnp.concatenate([o_ref_in[t] for t in range(o_ref_in.shape[0])], axis=1).astype(F32)
    a = o * _silu(g_ref[...].astype(F32))
    acc = _dot(_bf(a), w_ref[...])
    gate = mod_ref[0][:, 2 * d:3 * d]
    y_ref[...] = _rms(x_ref[...] + gate * acc, fw_ref[...])


def _proj_out1(x2d, o_t, g, mod, final_w, w_bf, rows_per_mod, mod_base):
    m_rows, d = x2d.shape
    tm = ROW_TILE
    assert m_rows % tm == 0 and rows_per_mod % tm == 0
    nt = o_t.shape[0]
    return pl.pallas_call(
        _proj_out1_kernel,
        grid=(m_rows // tm,),
        in_specs=[pl.BlockSpec((tm, d), lambda i: (i, 0)),
                  pl.BlockSpec((nt, tm, LANE), lambda i: (0, i, 0)),
                  pl.BlockSpec((tm, g.shape[1]), lambda i: (i, 0)),
                  pl.BlockSpec((1, 1, 3 * d), lambda i: (mod_base + (i * tm) // rows_per_mod, 0, 0)),
                  pl.BlockSpec((1, d), lambda i: (0, 0)),
                  pl.BlockSpec(w_bf.shape, lambda i: (0, 0), pipeline_mode=pl.Buffered(1))],
        out_specs=pl.BlockSpec((tm, d), lambda i: (i, 0)),
        out_shape=jax.ShapeDtypeStruct((m_rows, d), F32),
        compiler_params=_params("arbitrary"),
        name="proj_out1",
    )(x2d, o_t, g, mod, final_w, w_bf)


def _ctx_layer_kernel(x_ref, mod_ref, nw_ref, wi_ref, wo_ref, fw_ref, y_ref, ck_ref, cv_ref):
    tm, d = x_ref.shape
    nseq = ck_ref.shape[0]
    l = tm // nseq
    m = mod_ref[0]
    x = x_ref[...]
    hb = _bf(_rms(x, nw_ref[...]) * (1.0 + m[:, d:2 * d]) + m[:, 0:d])
    qb = _bf(_dot(hb, wi_ref[:, 0:d]) * (HD ** -0.5))
    k = _dot(hb, wi_ref[:, d:2 * d])
    v = _dot(hb, wi_ref[:, 2 * d:3 * d])
    g = _dot(hb, wi_ref[:, 3 * d:4 * d])
    nh = d // HD
    o_rows = []
    for s in range(nseq):
        rows = slice(s * l, (s + 1) * l)
        scores, vbs = [], []
        for h in range(nh):
            sl = slice(h * HD, (h + 1) * HD)
            kh = k[rows, sl]
            vh = v[rows, sl]
            ck_ref[s, 0, h] = kh
            cv_ref[s, 0, h] = vh
            vbs.append(_bf(vh))
            scores.append(_dot_nt(qb[rows, sl], _bf(kh)))
        s_all = jnp.concatenate(scores, axis=0)
        pexp = jnp.exp(s_all - jnp.max(s_all, axis=-1, keepdims=True))
        den = jnp.sum(pexp, axis=-1, keepdims=True)
        pb = _bf(pexp)
        o_rows.append(jnp.concatenate(
            [_dot(pb[h * l:(h + 1) * l], vbs[h]) / den[h * l:(h + 1) * l] for h in range(nh)], axis=1))
    a = jnp.concatenate(o_rows, axis=0) * _silu(g)
    acc = _dot(_bf(a), wo_ref[...])
    y_ref[...] = _rms(x + m[:, 2 * d:3 * d] * acc, fw_ref[...])


def _ctx_layer(x2d, b, l, mod, norm_w, w_in_bf, w_out_bf, final_w):
    m_rows, d = x2d.shape
    tm = ROW_TILE
    assert m_rows % tm == 0 and tm % l == 0
    nseq = tm // l
    nh = d // HD
    cache_spec = pl.BlockSpec((nseq, 1, nh, l, HD), lambda i: (i, 0, 0, 0, 0))
    cache_shape = jax.ShapeDtypeStruct((b, 1, nh, l, HD), F32)

    def const(a):
        nd = a.ndim
        return pl.BlockSpec(a.shape, lambda i: (0,) * nd, pipeline_mode=pl.Buffered(1))

    return pl.pallas_call(
        _ctx_layer_kernel,
        grid=(m_rows // tm,),
        in_specs=[pl.BlockSpec((tm, d), lambda i: (i, 0)),
                  pl.BlockSpec((1, 1, 3 * d), lambda i: (0, 0, 0)),
                  const(norm_w), const(w_in_bf), const(w_out_bf), const(final_w)],
        out_specs=[pl.BlockSpec((tm, d), lambda i: (i, 0)), cache_spec, cache_spec],
        out_shape=[jax.ShapeDtypeStruct((m_rows, d), F32), cache_shape, cache_shape],
        compiler_params=_params("arbitrary"),
        name="ctx_layer",
    )(x2d, mod, norm_w, w_in_bf, w_out_bf, final_w)


def _na_bias_kernel(rpb_ref, o_ref):
    h = pl.program_id(0)
    ndr = 2 * WIN_H - 1
    ndc = 2 * WIN_W - 1
    ck = lax.broadcasted_iota(jnp.int32, (GRID_W, LANE), 0)
    lane = lax.broadcasted_iota(jnp.int32, (GRID_W, LANE), 1)
    cq = lane & (GRID_W - 1)
    first = lane < GRID_W
    dc = jnp.clip(ck - cq + (WIN_W - 1), 0, ndc - 1)
    col0 = jnp.clip(cq - WIN_W // 2, 0, GRID_W - WIN_W)
    col_in = (ck >= col0) & (ck < col0 + WIN_W)
    dc_is = [(dc == e) & col_in for e in range(ndc)]
    tables = []
    for dr in range(ndr):
        t = jnp.full((GRID_W, LANE), NEG_INF, F32)
        for e in range(ndc):
            t = jnp.where(dc_is[e], rpb_ref[(h * ndr + dr) * ndc + e], t)
        tables.append(t)
    for ip in range(NA_BAND):
        for ap in range(NA_QTILE // 2):
            x = ip - 2 * ap + (WIN_H - 1) - NA_QROWS // 2
            o_ref[0, ip * GRID_W:(ip + 1) * GRID_W, ap * LANE:(ap + 1) * LANE] = jnp.where(
                first, tables[x], tables[x - 1])


def _na_bias_tables(rpb):
    nh = rpb.shape[0]
    shape = (NA_BAND * GRID_W, NA_QTILE * GRID_W)
    return pl.pallas_call(
        _na_bias_kernel,
        grid=(nh,),
        in_specs=[pl.BlockSpec(memory_space=pltpu.SMEM)],
        out_specs=pl.BlockSpec((1,) + shape, lambda h: (h, 0, 0)),
        out_shape=jax.ShapeDtypeStruct((nh,) + shape, F32),
        compiler_params=_params("arbitrary"),
        name="na_bias",
    )(rpb.reshape(-1))


def _na_kernel(q_ref, kp_ref, kc_ref, kn_ref, vp_ref, vc_ref, vn_ref, ck_ref, cv_ref, bias_ref, o_ref, mask_scr,
               s_scr, *, n_rows):
    rb = pl.program_id(1)
    qrows = NA_QROWS
    nq = qrows * GRID_W
    half = (qrows // 2) * GRID_W
    qt = NA_QTILE * GRID_W
    nband = NA_BAND * GRID_W
    q = q_ref[0, 0] * (HD ** -0.5)
    kloc = jnp.concatenate([kp_ref[0, 0][nq - half:nq], kc_ref[0, 0], kn_ref[0, 0][0:half]], axis=0)
    vloc = jnp.concatenate([vp_ref[:, nq - half:nq], vc_ref[...], vn_ref[:, 0:half]], axis=1)

    @pl.when(pl.program_id(2) == 0)
    def _():
        for t in range(qrows // NA_QTILE):
            i = t * NA_QTILE + lax.broadcasted_iota(jnp.int32, (nband, qt), 0) // GRID_W
            a = t * NA_QTILE + lax.broadcasted_iota(jnp.int32, (nband, qt), 1) // GRID_W
            r = rb * qrows + a
            kr = rb * qrows - qrows // 2 + i
            rs = jnp.clip(r - WIN_H // 2, 0, n_rows - WIN_H)
            mask_scr[t] = jnp.where((kr >= rs) & (kr < rs + WIN_H), 0.0, NEG_INF)

    rows = []
    for j in range(LANE // HD):
        sl = slice(j * HD, (j + 1) * HD)
        ckb = _bf(ck_ref[0, j])
        cvt = cv_ref[0, 0, sl, :]
        tiles = []
        nctx = ckb.shape[0]
        for t in range(qrows // NA_QTILE):
            k0 = t * NA_QTILE * GRID_W
            qh = q[t * qt:(t + 1) * qt, sl]
            m = jnp.full((1, qt), NEG_INF, F32)
            for c0 in range(0, nband + nctx, NA_KCHUNK):
                if c0 < nband:
                    rs_ = slice(c0, c0 + NA_KCHUNK)
                    s = _dot_nt(kloc[k0 + c0:k0 + c0 + NA_KCHUNK, sl], qh) + bias_ref[j, rs_, :] + mask_scr[t, rs_, :]
                else:
                    s = _dot_nt(ckb[c0 - nband:c0 - nband + NA_KCHUNK], qh)
                s_scr[c0:c0 + NA_KCHUNK, :] = s
                m = jnp.maximum(m, jnp.max(s, axis=0, keepdims=True))
            den = jnp.zeros((1, qt), F32)
            o = jnp.zeros((HD, qt), F32)
            for c0 in range(0, nband + nctx, NA_KCHUNK):
                pexp = jnp.exp(s_scr[c0:c0 + NA_KCHUNK, :] - m)
                den = den + jnp.sum(pexp, axis=0, keepdims=True)
                if c0 < nband:
                    vt = vloc[sl, k0 + c0:k0 + c0 + NA_KCHUNK]
                else:
                    vt = cvt[:, c0 - nband:c0 - nband + NA_KCHUNK]
                o = o + _dot(vt, _bf(pexp))
            tiles.append(o / den)
        rows.append(jnp.concatenate(tiles, axis=1))
    o_ref[0, 0] = jnp.concatenate(rows, axis=0).T.astype(o_ref.dtype)


def _na_attn(q_t, k_t, v_c, cache_k, cache_vt, bias, b, l):
    npair = q_t.shape[0]
    hpp = LANE // HD
    n_rows = l // GRID_W
    nrb = n_rows // NA_QROWS
    nq = NA_QROWS * GRID_W
    lc = cache_k.shape[2]
    q4, k4 = (a.reshape(npair, b, l, LANE) for a in (q_t, k_t))

    def prev_blk(rb):
        return jnp.maximum(rb - 1, 0)

    def next_blk(rb):
        return jnp.minimum(rb + 1, nrb - 1)

    def same_blk(rb):
        return rb

    def tok(f):
        return pl.BlockSpec((1, 1, nq, LANE), lambda bi, rb, hp: (hp, bi, f(rb), 0))

    def chan(f):
        return pl.BlockSpec((LANE, nq), lambda bi, rb, hp: (hp, bi * nrb + f(rb)))

    out = pl.pallas_call(
        functools.partial(_na_kernel, n_rows=n_rows),
        grid=(b, nrb, npair),
        in_specs=[tok(same_blk), tok(prev_blk), tok(same_blk), tok(next_blk),
                  chan(prev_blk), chan(same_blk), chan(next_blk),
                  pl.BlockSpec((1, hpp, lc, HD), lambda bi, rb, hp: (bi, hp, 0, 0)),
                  pl.BlockSpec((1, 1, LANE, lc), lambda bi, rb, hp: (bi, hp, 0, 0)),
                  pl.BlockSpec((hpp,) + bias.shape[1:], lambda bi, rb, hp: (hp, 0, 0))],
        out_specs=tok(same_blk),
        out_shape=jax.ShapeDtypeStruct((npair, b, l, LANE), BF16),
        scratch_shapes=[pltpu.VMEM((NA_QROWS // NA_QTILE,) + bias.shape[1:], F32),
                        pltpu.VMEM((bias.shape[1] + lc, bias.shape[2]), F32)],
        compiler_params=_params("arbitrary", "arbitrary", "arbitrary"),
        name="na_attn",
    )(q4, k4, k4, k4, v_c, v_c, v_c, cache_k, cache_vt, bias)
    return out.reshape(npair, b * l, LANE)


def _ctx_layer0_kernel(x_ref, mod_ref, nw_ref, w_ref, cwa_ref, cba_ref, cwb_ref, cbb_ref, dtb_ref, alog_ref,
                       dskip_ref, ex_ref, naw_ref, wo_ref, gs_ref, hyb_ref, fa_ref, iv_ref, o_ref, fin_ref,
                       *, seq_len, cols):
    tm, d = x_ref.shape
    o_z, o_xbc, o_u, o_g, o_dt = cols
    d_a, d_xbc, d_b = o_xbc - o_z, o_u - o_xbc, o_dt - o_g
    nseq, q, p = tm // seq_len, CHUNK, seq_len
    nc = seq_len // q
    step = 2 * LANE
    m = mod_ref[0]
    x = x_ref[...]
    hb = _bf(_rms(x, nw_ref[...]) * (1.0 + m[:, d:2 * d]) + m[:, 0:d])

    def conv(off, cw_ref, cb_ref, coff):
        return _conv3_rows(_dot(hb, w_ref[:, off:off + step]), 0.0, 0.0, seq_len,
                           cw_ref[:, coff:coff + step], cb_ref[:, coff:coff + step])

    xs = jnp.concatenate([_silu(conv(o_xbc + c0, cwa_ref, cba_ref, c0)) for c0 in range(0, d_xbc, step)], axis=1)
    dt_raw = _dot(hb, w_ref[:, o_dt:o_dt + LANE])
    xa = xs[:, 0:d_a]
    bm = _bf(xs[:, d_a:d_a + N_GROUPS * N_STATE])
    cm = _bf(xs[:, d_a + N_GROUPS * N_STATE:d_xbc])
    dtb, alog = dtb_ref[...], alog_ref[...]
    y_rows = []
    for s in range(nseq):
        state = [jnp.zeros((N_STATE, d_a), F32), jnp.zeros((N_STATE, d_a), F32)]
        y_chunks = [None] * nc
        for direction, order in ((0, range(nc)), (1, reversed(range(nc)))):
            for c in order:
                rows = slice(s * seq_len + c * q, s * seq_len + (c + 1) * q)
                y, state[direction] = _ssd_chunk(direction, xa[rows], bm[rows], cm[rows], dt_raw[rows],
                                                 state[direction], dtb, alog,
                                                 dskip_ref[direction:direction + 1, :], ex_ref[direction])
                y_chunks[c] = y if y_chunks[c] is None else y_chunks[c] + y
        for direction in range(2):
            for t in range(d_a // LANE):
                fin_ref[s, direction, t * LANE:(t + 1) * LANE, :] = state[direction][:, t * LANE:(t + 1) * LANE].T
        y_rows += y_chunks
    z = jnp.concatenate([_dot(hb, w_ref[:, o_z + c0:o_z + c0 + step]) for c0 in range(0, d_a, step)], axis=1)
    ya = _rms(jnp.concatenate(y_rows, axis=0) * _silu(z), naw_ref[...])

    yh_cols = []
    for c0 in range(0, d_b, step):
        wv = conv(o_u + d_b + c0, cwb_ref, cbb_ref, d_b + c0) * conv(o_u + 2 * d_b + c0, cwb_ref, cbb_ref, 2 * d_b + c0)
        gate = conv(o_u + c0, cwb_ref, cbb_ref, c0) * _silu(_dot(hb, w_ref[:, o_g + c0:o_g + c0 + step]))
        spec = jnp.concatenate([gs_ref[c0 // LANE + t, 0] for t in range(step // LANE)], axis=1)
        g_re, g_im = spec[0:p], spec[p:]
        outs = []
        for s in range(nseq):
            rows = slice(s * seq_len, (s + 1) * seq_len)
            u = _dot(fa_ref[...], _bf(wv[rows]))
            u_re, u_im = u[0:p], u[p:]
            prod = jnp.concatenate([g_re * u_re - g_im * u_im, g_re * u_im + g_im * u_re], axis=0)
            lc = _dot(iv_ref[...], _bf(prod))
            outs.append(gate[rows] * (lc + wv[rows] * hyb_ref[:, c0:c0 + step]))
        yh_cols.append(jnp.concatenate(outs, axis=0))
    yh = jnp.concatenate(yh_cols, axis=1)

    acc = _dot(_bf(ya), wo_ref[0:d_a, :]) + _dot(_bf(yh), wo_ref[d_a:, :])
    o_ref[...] = x + m[:, 2 * d:3 * d] * acc


def _ctx_layer0(x2d, b, l, mod, norm_w, w_in_bf, cols, p, w_out_bf, filt_params):
    m_rows, d = x2d.shape
    tm = ROW_TILE
    assert m_rows % tm == 0 and tm % l == 0 and l <= DFT_BLOCK and l % CHUNK == 0
    nseq = tm // l
    dm = N_HEADS * HD
    fa, iv = _dft_tables(l)
    spectra = _hyena_spectra(l, l, fa, *filt_params)
    consts = ([norm_w, w_in_bf, p["conv_a_w"], p["conv_a_b"].reshape(1, -1), p["conv_b_w"], p["conv_b_b"].reshape(1, -1)]
              + list(_ssd_consts(p["dt_bias"], p["a_log"], p["d_skip"]))
              + [p["norm_a_w"], w_out_bf, spectra, p["hy_bias"].reshape(1, -1), fa, iv])

    def const(a):
        nd = a.ndim
        return pl.BlockSpec(a.shape, lambda i: (0,) * nd, pipeline_mode=pl.Buffered(1))

    return pl.pallas_call(
        functools.partial(_ctx_layer0_kernel, seq_len=l, cols=cols),
        grid=(m_rows // tm,),
        in_specs=[pl.BlockSpec((tm, d), lambda i: (i, 0)), pl.BlockSpec((1, 1, 3 * d), lambda i: (0, 0, 0))]
                 + [const(a) for a in consts],
        out_specs=[pl.BlockSpec((tm, d), lambda i: (i, 0)),
                   pl.BlockSpec((nseq, 2, dm, N_STATE), lambda i: (i, 0, 0, 0))],
        out_shape=[jax.ShapeDtypeStruct((m_rows, d), F32), jax.ShapeDtypeStruct((b, 2, dm, N_STATE), F32)],
        compiler_params=_params("arbitrary"),
        name="ctx_layer0",
    )(x2d, mod, *consts)


def _reorder_kernel(w_ref, o_ref, *, o_dt, n_dt):
    rows, n = w_ref.shape
    rest = n - o_dt - n_dt
    o_ref[:, 0:o_dt] = w_ref[:, 0:o_dt].astype(o_ref.dtype)
    o_ref[:, o_dt:o_dt + rest] = w_ref[:, o_dt + n_dt:n].astype(o_ref.dtype)
    tail = jnp.concatenate([w_ref[:, o_dt:o_dt + n_dt], jnp.zeros((rows, LANE - n_dt), F32)], axis=1)
    o_ref[:, o_dt + rest:o_dt + rest + LANE] = tail.astype(o_ref.dtype)


def _reorder_w_in(w, o_dt, n_dt):
    d, n = w.shape
    n_out = n - n_dt + LANE
    tr = LANE
    return pl.pallas_call(
        functools.partial(_reorder_kernel, o_dt=o_dt, n_dt=n_dt),
        grid=(d // tr,),
        in_specs=[pl.BlockSpec((tr, n), lambda i: (i, 0))],
        out_specs=pl.BlockSpec((tr, n_out), lambda i: (i, 0)),
        out_shape=jax.ShapeDtypeStruct((d, n_out), BF16),
        compiler_params=_params("arbitrary"),
        name="reorder_w_in",
    )(w)


def _layer0_cols(d_b):
    dm = N_HEADS * HD
    d_xbc = dm + 2 * N_GROUPS * N_STATE
    return (0, dm, dm + d_xbc, dm + d_xbc + 3 * d_b, dm + d_xbc + 4 * d_b)


def _latent_layer0(x2d, b, l, mod, mod1, mod_base, norm_w, w_in_bf, w_out_bf, p, init, filt_params, norm_w1,
                   w_in1_bf, wv_t):
    cols = _layer0_cols(p["hy_bias"].shape[0])
    consts = _ssd_consts(p["dt_bias"], p["a_log"], p["d_skip"])
    z, xs, w_t, gate_t, dt_raw, y_f = _proj_in0(x2d, mod, norm_w, w_in_bf, cols, p["conv_a_w"], p["conv_a_b"],
                                                p["conv_b_w"], p["conv_b_b"], consts, init, l, l, mod_base)
    yh_t = _hyena(w_t, gate_t, filt_params, p["hy_bias"], b, l)
    return _proj_out0(x2d, y_f, xs, dt_raw, z, yh_t, mod, mod1, p["norm_a_w"], w_out_bf, consts, norm_w1, w_in1_bf,
                      wv_t, init, l, l, mod_base)


def kernel(x_prompt, x_sample, state_ssd, cache_k, cache_v, c, c_ctx, norm_w, w_ada, b_ada, w_in_e, w_out_e, conv_a_w, conv_a_b, dt_bias, a_log, d_skip, norm_a_w, conv_b_w, conv_b_b, hf_w1, hf_b1, hf_w2, hf_b2, hf_w3, hf_freq, hy_bias, w_in_o, w_out_o, rpb, final_norm_w):
    bp, lp, d = x_prompt.shape
    bs, ls, _ = x_sample.shape
    dm = N_HEADS * HD
    d_xbc = dm + 2 * N_GROUPS * N_STATE
    n_dt = 2 * N_HEADS

    cvecs = jnp.concatenate([c_ctx[None], c, jnp.zeros((SUBLANE - 1 - bs, d), F32)], axis=0)
    mods = _ada_mods(cvecs, w_ada, b_ada)

    xp = x_prompt.reshape(bp * lp, d)
    xs = x_sample.reshape(bs * ls, d)

    w_in0 = _reorder_w_in(w_in_e[0], dm + d_xbc, n_dt)
    w_out0 = w_out_e[0].astype(BF16)
    p0 = dict(conv_a_w=conv_a_w[0], conv_a_b=conv_a_b[0], dt_bias=dt_bias[0], a_log=a_log[0], d_skip=d_skip[0],
              norm_a_w=norm_a_w[0].reshape(1, -1), conv_b_w=conv_b_w[0], conv_b_b=conv_b_b[0], hy_bias=hy_bias[0])
    mod0 = mods[0].reshape(SUBLANE, 1, 3 * d)
    nw0 = norm_w[0].reshape(1, d)
    hf = (hf_w1[0], hf_b1[0], hf_w2[0], hf_b2[0], hf_w3[0], hf_freq[0])
    w_in1 = w_in_o[0].astype(BF16)
    w_out1 = w_out_o[0].astype(BF16)
    wv_t = w_in_o[0][:, 2 * d:3 * d].T.astype(BF16)
    mod1 = mods[1].reshape(SUBLANE, 1, 3 * d)
    nw1 = norm_w[1].reshape(1, d)
    xp, fin = _ctx_layer0(xp, bp, lp, mod0, nw0, w_in0, _layer0_cols(hy_bias.shape[1]), p0, w_out0, hf)
    init_s = state_ssd[:, 0].reshape(bs, 2, dm, N_STATE)
    xs, q_t, k_t, v_c, g = _latent_layer0(xs, bs, ls, mod0, mod1, 1, nw0, w_in0, w_out0, p0, init_s, hf, nw1, w_in1,
                                          wv_t)
    new_state_ssd = fin.reshape(bp, 1, 2, N_HEADS, HD, N_STATE)

    fw = final_norm_w.reshape(1, d)
    y_prompt, new_cache_k, new_cache_v = _ctx_layer(xp, bp, lp, mod1, nw1, w_in1, w_out1, fw)
    y_prompt = y_prompt.reshape(bp, lp, d)

    bias = _na_bias_tables(rpb[0])
    lc = cache_v.shape[3]
    cache_vt = jnp.swapaxes(cache_v[:, 0], 2, 3).reshape(bs, d // LANE, LANE, lc).astype(BF16)
    o_t = _na_attn(q_t, k_t, v_c, cache_k[:, 0], cache_vt, bias, bs, ls)
    y_sample = _proj_out1(xs, o_t, g, mod1, fw, w_out1, ls, 1).reshape(bs, ls, d)

    return (y_prompt, y_sample, new_state_ssd, new_cache_k, new_cache_v)
```

```python
import functools
import math

import jax
import jax.numpy as jnp
import numpy as np
from jax import lax
from jax.experimental import pallas as pl
from jax.experimental.pallas import tpu as pltpu

F32 = jnp.float32
BF16 = jnp.bfloat16

EPS = 1e-6
GRID_W = 64
WIN_H = 8
WIN_W = 16
HD = 64
N_HEADS = 16
N_STATE = 128
N_GROUPS = 2
CHUNK = 128
HY_EMB = 33
HY_BANDS = (HY_EMB - 1) // 2
HY_HID = 64
HY_TARGET = 1e-2
HY_DECAY_PCT_HI = 0.3
HY_DECAY_PCT_LO = 1.5

LANE = 128
SUBLANE = 8
VMEM_LIMIT = 56 * 1024 * 1024

ROW_TILE = 512
DFT_BLOCK = 512
HY_BATCH = 32
HY_BATCH_LONG = 2
NA_QROWS = 8
NA_QTILE = 4
NA_KCHUNK = 128
NA_BAND = NA_QTILE + WIN_H
NEG_INF = float("-inf")


def _bf(x):
    return x.astype(BF16)


def _dot(a, b):
    return jnp.dot(a, b, preferred_element_type=F32)


def _dot_nt(a, b):
    return lax.dot_general(a, b, (((1,), (1,)), ((), ())), preferred_element_type=F32)


def _split2(x):
    hi = _bf(x)
    lo = _bf(x - hi.astype(F32))
    return hi, lo


def _split3(x):
    hi = _bf(x)
    r = x - hi.astype(F32)
    mid = _bf(r)
    lo = _bf(r - mid.astype(F32))
    return hi, mid, lo


def _dot3(a, b):
    ah, al = _split2(a)
    bh, bl = _split2(b)
    return _dot(ah, bh) + (_dot(ah, bl) + _dot(al, bh))


def _dot_rhs_parts(a_exact, b, parts):
    pieces = _split3(b) if parts == 3 else _split2(b)
    acc = _dot(a_exact, pieces[0])
    for p in pieces[1:]:
        acc = acc + _dot(a_exact, p)
    return acc


def _silu(x):
    return x * jax.nn.sigmoid(x)


def _rms(x, g):
    ms = jnp.mean(x * x, axis=-1, keepdims=True)
    return x * lax.rsqrt(ms + EPS) * g


def _softplus(x):
    return jnp.maximum(x, 0.0) + jnp.log1p(jnp.exp(-jnp.abs(x)))


def _params(*sem):
    return pltpu.CompilerParams(dimension_semantics=sem, vmem_limit_bytes=VMEM_LIMIT)


def _mods_kernel(c_ref, w_ref, b_ref, o_ref):
    a = _silu(c_ref[...])
    o_ref[0] = _dot3(a, w_ref[0]) + b_ref[0]


def _ada_mods(cvecs, w_ada, b_ada):
    depth, d, n3 = w_ada.shape
    tn = n3 // 4
    return pl.pallas_call(
        _mods_kernel,
        grid=(depth, n3 // tn),
        in_specs=[pl.BlockSpec((SUBLANE, d), lambda l, j: (0, 0)),
                  pl.BlockSpec((1, d, tn), lambda l, j: (l, 0, j)),
                  pl.BlockSpec((1, 1, tn), lambda l, j: (l, 0, j))],
        out_specs=pl.BlockSpec((1, SUBLANE, tn), lambda l, j: (l, 0, j)),
        out_shape=jax.ShapeDtypeStruct((depth, SUBLANE, n3), F32),
        compiler_params=_params("arbitrary", "arbitrary"),
        name="ada_mods",
    )(cvecs, w_ada, b_ada.reshape(depth, 1, n3))


def _conv3_rows(res, prev_row, next_row, seq_len, cw, cb):
    tm, width = res.shape
    starts = list(range(0, tm, seq_len))
    ends = [min(s + seq_len, tm) - 1 for s in starts]
    sub = lax.broadcasted_iota(jnp.int32, (SUBLANE, width), 0)
    down = pltpu.roll(res, 1, 0)
    up = pltpu.roll(res, tm - 1, 0)
    dparts, uparts, pos = [], [], 0
    for s in starts:
        fill = prev_row if s == 0 else 0.0
        dparts += [down[pos:s], jnp.where(sub == 0, fill, down[s:s + SUBLANE])]
        pos = s + SUBLANE
    dparts.append(down[pos:tm])
    pos = 0
    for e in ends:
        fill = next_row if e == tm - 1 else 0.0
        uparts += [up[pos:e + 1 - SUBLANE], jnp.where(sub == SUBLANE - 1, fill, up[e + 1 - SUBLANE:e + 1])]
        pos = e + 1
    uparts.append(up[pos:tm])
    down = jnp.concatenate([p for p in dparts if p.shape[0]], axis=0)
    up = jnp.concatenate([p for p in uparts if p.shape[0]], axis=0)
    return cb + down * cw[0:1] + res * cw[1:2] + up * cw[2:3]


def _proj_in0_kernel(x_ref, xp_ref, xn_ref, mod_ref, nw_ref, w_ref, cwa_ref, cba_ref, cwb_ref, cbb_ref,
                     dtb_ref, alog_ref, dskip_ref, ex_ref, init_ref,
                     z_ref, xs_ref, wv_ref, gate_ref, dt_ref, yf_ref, s_scr, *, seq_len, cols):
    i = pl.program_id(0)
    tm, d = x_ref.shape
    o_z, o_xbc, o_u, o_g, o_dt = cols
    d_a, d_xbc, d_b = o_xbc - o_z, o_u - o_xbc, o_dt - o_g
    m = mod_ref[0]

    def modnorm(x):
        return _bf(_rms(x, nw_ref[...]) * (1.0 + m[:, d:2 * d]) + m[:, 0:d])

    hb = modnorm(x_ref[...])
    hh = modnorm(jnp.concatenate([xp_ref[...], xn_ref[...]], axis=0))
    keep_prev = ((i * tm) % seq_len != 0).astype(F32)
    keep_next = (((i + 1) * tm) % seq_len != 0).astype(F32)

    def conv(off, width, cw_ref, cb_ref, coff):
        res = _dot(hb, w_ref[:, off:off + width])
        rh = _dot(hh, w_ref[:, off:off + width])
        return _conv3_rows(res, rh[SUBLANE - 1:SUBLANE] * keep_prev, rh[SUBLANE:SUBLANE + 1] * keep_next, seq_len,
                           cw_ref[:, coff:coff + width], cb_ref[:, coff:coff + width])

    step = 2 * LANE
    for c0 in range(0, d_a, step):
        z_ref[:, c0:c0 + step] = _dot(hb, w_ref[:, o_z + c0:o_z + c0 + step]).astype(z_ref.dtype)
    for c0 in range(0, d_xbc, step):
        xs_ref[:, c0:c0 + step] = _silu(conv(o_xbc + c0, step, cwa_ref, cba_ref, c0)).astype(xs_ref.dtype)
    for c0 in range(0, d_b, step):
        x1 = conv(o_u + d_b + c0, step, cwb_ref, cbb_ref, d_b + c0)
        v = conv(o_u + 2 * d_b + c0, step, cwb_ref, cbb_ref, 2 * d_b + c0)
        wv = x1 * v
        x0 = conv(o_u + c0, step, cwb_ref, cbb_ref, c0)
        gate = x0 * _silu(_dot(hb, w_ref[:, o_g + c0:o_g + c0 + step]))
        for t in range(step // LANE):
            wv_ref[c0 // LANE + t] = wv[:, t * LANE:(t + 1) * LANE].astype(wv_ref.dtype)
            gate_ref[c0 // LANE + t] = gate[:, t * LANE:(t + 1) * LANE].astype(gate_ref.dtype)
    dt_raw = _dot(hb, w_ref[:, o_dt:o_dt + LANE])
    dt_ref[...] = dt_raw
    y_f = _ssd_tile(0, xs_ref[...], dt_raw, s_scr, init_ref, (i * tm) % seq_len == 0,
                    dtb_ref[...], alog_ref[...], dskip_ref[0:1, :], ex_ref[0])
    yf_ref[...] = y_f.astype(yf_ref.dtype)


def _proj_in0(x2d, mod, norm_w, w_bf, cols, conv_a_w, conv_a_b, conv_b_w, conv_b_b, ssd_consts, init, seq_len,
              rows_per_mod, mod_base):
    m_rows, d = x2d.shape
    tm = ROW_TILE
    assert m_rows % tm == 0 and rows_per_mod % tm == 0 and seq_len % tm == 0 and tm % CHUNK == 0
    o_z, o_xbc, o_u, o_g, o_dt = cols
    d_a, d_xbc, d_b = o_xbc - o_z, o_u - o_xbc, o_dt - o_g
    nsub = m_rows // SUBLANE
    spt = tm // SUBLANE
    cba, cbb = conv_a_b.reshape(1, -1), conv_b_b.reshape(1, -1)
    consts = [norm_w, w_bf, conv_a_w, cba, conv_b_w, cbb] + list(ssd_consts)

    def const(a):
        nd = a.ndim
        return pl.BlockSpec(a.shape, lambda i: (0,) * nd, pipeline_mode=pl.Buffered(1))

    def tiles(n):
        return pl.BlockSpec((n, tm, LANE), lambda i: (0, i, 0))

    def rows(wd):
        return pl.BlockSpec((tm, wd), lambda i: (i, 0))

    return pl.pallas_call(
        functools.partial(_proj_in0_kernel, seq_len=seq_len, cols=cols),
        grid=(m_rows // tm,),
        in_specs=[rows(d),
                  pl.BlockSpec((SUBLANE, d), lambda i: (jnp.maximum(i * spt - 1, 0), 0)),
                  pl.BlockSpec((SUBLANE, d), lambda i: (jnp.minimum((i + 1) * spt, nsub - 1), 0)),
                  pl.BlockSpec((1, 1, 3 * d), lambda i: (mod_base + (i * tm) // rows_per_mod, 0, 0))]
                 + [const(a) for a in consts]
                 + [pl.BlockSpec((1, 1, d_a, N_STATE), lambda i: ((i * tm) // seq_len, 0, 0, 0))],
        out_specs=[rows(d_a), rows(d_xbc), tiles(d_b // LANE), tiles(d_b // LANE), rows(LANE), rows(d_a)],
        out_shape=[jax.ShapeDtypeStruct((m_rows, d_a), BF16), jax.ShapeDtypeStruct((m_rows, d_xbc), BF16),
                   jax.ShapeDtypeStruct((d_b // LANE, m_rows, LANE), BF16),
                   jax.ShapeDtypeStruct((d_b // LANE, m_rows, LANE), BF16),
                   jax.ShapeDtypeStruct((m_rows, LANE), F32), jax.ShapeDtypeStruct((m_rows, d_a), BF16)],
        scratch_shapes=[pltpu.VMEM((N_STATE, d_a), F32)],
        compiler_params=_params("arbitrary"),
        name="proj_in0",
    )(x2d, x2d, x2d, mod, *consts, init)


def _ssd_chunk(d, xa, bm, cm, dt_raw, s_prev, dtb, alog, dskip, ex):
    q, dm = xa.shape
    gw = dm // N_GROUPS
    hpg = N_HEADS // N_GROUPS
    row = lax.broadcasted_iota(jnp.int32, (q, q), 0)
    col = lax.broadcasted_iota(jnp.int32, (q, q), 1)
    left = lax.broadcasted_iota(jnp.int32, (q, LANE), 1) < HD
    tri = (row >= col) if d == 0 else (row <= col)

    dt = _softplus(dt_raw + dtb)
    adt = dt * (-jnp.exp(alog))
    cs = _dot_rhs_parts(jnp.where(tri, 1.0, 0.0).astype(BF16), adt, 3)
    cs_t = cs.T
    dt_t = dt.T
    edge = cs[q - 1:q, :] if d == 0 else cs[0:1, :]
    e_cs = _dot(_bf(jnp.exp(cs)), ex)
    w_st = _dot(_bf(jnp.exp(edge - cs) * dt), ex)
    xw = xa * w_st

    y_parts = []
    new_state = []
    for g in range(N_GROUPS):
        bg = bm[:, g * N_STATE:(g + 1) * N_STATE]
        cg = cm[:, g * N_STATE:(g + 1) * N_STATE]
        gmat = _dot_nt(_bf(cg), _bf(bg))
        y_off = _dot(_bf(cg), _bf(s_prev[:, g * gw:(g + 1) * gw]))
        new_state.append(_dot(_bf(bg.astype(F32).T), _bf(xw[:, g * gw:(g + 1) * gw])))
        for pr in range(hpg // 2):
            mh = []
            for j in range(2):
                k = d * N_HEADS + g * hpg + 2 * pr + j
                diff = cs[:, k:k + 1] - cs_t[k:k + 1, :]
                lm = jnp.exp(jnp.where(tri, diff, NEG_INF))
                mh.append(_bf(gmat * lm * dt_t[k:k + 1, :]))
            c0 = g * gw + pr * LANE
            xpair = xa[:, c0:c0 + LANE]
            rhs = jnp.concatenate([_bf(jnp.where(left, xpair, 0.0)), _bf(jnp.where(left, 0.0, xpair))], axis=0)
            y_d = _dot(jnp.concatenate(mh, axis=1), rhs)
            y_parts.append(y_d + y_off[:, pr * LANE:(pr + 1) * LANE] * e_cs[:, c0:c0 + LANE])
    y = jnp.concatenate(y_parts, axis=1) + xa * dskip
    e_edge = e_cs[q - 1:q, :] if d == 0 else e_cs[0:1, :]
    return y, s_prev * e_edge + jnp.concatenate(new_state, axis=1)


def _ssd_tile(direction, xs, dt_raw, s_scr, init_ref, at_seq_edge, dtb, alog, dskip, ex):
    rows, dm = xs.shape[0], N_HEADS * HD

    @pl.when(at_seq_edge)
    def _():
        for t in range(dm // LANE):
            s_scr[:, t * LANE:(t + 1) * LANE] = init_ref[0, 0, t * LANE:(t + 1) * LANE, :].T

    nck = rows // CHUNK
    ys = [None] * nck
    state = s_scr[...]
    for c in (range(nck) if direction == 0 else reversed(range(nck))):
        r = slice(c * CHUNK, (c + 1) * CHUNK)
        ys[c], state = _ssd_chunk(direction, xs[r, 0:dm].astype(F32), xs[r, dm:dm + N_GROUPS * N_STATE],
                                  xs[r, dm + N_GROUPS * N_STATE:], dt_raw[r], state, dtb, alog, dskip, ex)
    s_scr[...] = state
    return jnp.concatenate(ys, axis=0)


def _ssd_consts(dt_bias, a_log, d_skip):
    dm = N_HEADS * HD
    ex = np.zeros((2, LANE, dm), np.float32)
    for d in range(2):
        for h in range(N_HEADS):
            ex[d, d * N_HEADS + h, h * HD:(h + 1) * HD] = 1.0
    pad = LANE - 2 * N_HEADS
    dtb = jnp.pad(dt_bias.reshape(1, 2 * N_HEADS), ((0, 0), (0, pad)))
    alog = jnp.pad(a_log.reshape(1, 2 * N_HEADS), ((0, 0), (0, pad)))
    return dtb, alog, jnp.repeat(d_skip, HD, axis=1), jnp.asarray(ex, BF16)


def _hyena_tables(l):
    pos = np.abs(np.arange(2 * l, dtype=np.float64) - l)
    t = pos / (l - 1)
    w = 2.0 * math.pi * pos / l
    f = np.linspace(1e-4, HY_BANDS - 1, HY_BANDS)
    feats = np.zeros((2 * l, LANE), np.float64)
    feats[:, 0] = t
    feats[:, 1:1 + HY_BANDS] = np.cos(f[None] * w[:, None])
    feats[:, 1 + HY_BANDS:1 + 2 * HY_BANDS] = -np.sin(f[None] * w[:, None])
    return jnp.asarray(feats, F32)


def _dft_tables(p):
    n = 2 * p
    f = np.arange(p, dtype=np.float64)[:, None] + 0.5
    e = np.arange(p, dtype=np.float64)[None]
    ang = 2.0 * math.pi * f * e / n
    fa = np.concatenate([np.cos(ang), -np.sin(ang)], axis=0)
    inv = np.concatenate([np.cos(ang.T), -np.sin(ang.T)], axis=1) * (2.0 / n)
    return tuple(jnp.asarray(m, F32).astype(BF16) for m in (fa, inv))


def _spectra_kernel(f_ref, w1_ref, b1_ref, w2_ref, b2_ref, w3_ref, fr_ref, ad_ref, fa_ref, o_ref, bprev):
    q = pl.program_id(0)
    p = f_ref.shape[0]
    half = p // 2
    feats = f_ref[...]
    pre1 = _dot3(feats, w1_ref[...])
    packed = jnp.concatenate([pre1[:half], pre1[half:]], axis=1)
    fr = fr_ref[...]
    h1 = jnp.sin(fr * (packed + b1_ref[...]))
    h2 = _bf(jnp.sin(fr * (_dot3(h1, w2_ref[...]) + b2_ref[...])))
    w3 = _bf(w3_ref[...])
    filt = jnp.concatenate([_dot(h2[:, :HY_HID], w3), _dot(h2[:, HY_HID:], w3)], axis=0)
    taps = _bf(filt * jnp.exp(-feats[:, 0:1] * ad_ref[...]))
    a = _dot(fa_ref[...], taps)

    @pl.when(q > 0)
    def _():
        g = a + bprev[...]
        for t in range(o_ref.shape[0]):
            o_ref[t, 0] = g[:, t * LANE:(t + 1) * LANE]

    odd = (lax.broadcasted_iota(jnp.int32, (p, a.shape[1]), 0) & 1) == 1
    a_re, a_im = a[0:p], a[p:]
    a_re0 = a_re - taps[0:1, :].astype(F32)
    bprev[0:p] = jnp.where(odd, a_im, -a_im)
    bprev[p:] = jnp.where(odd, -a_re0, a_re0)


def _hyena_spectra(l, p, fa, w1, b1, w2, b2, w3, freq):
    db = w3.shape[1] // 2
    nblk = 2 * l // p
    nct = db // LANE
    feats = _hyena_tables(l)
    w1p = jnp.pad(w1, ((0, LANE - HY_EMB), (0, 0)))
    zero = jnp.zeros_like(w2)
    w2bd = jnp.concatenate([jnp.concatenate([w2, zero], axis=1), jnp.concatenate([zero, w2], axis=1)], axis=0)
    deltas = np.linspace(math.log(HY_TARGET) / HY_DECAY_PCT_HI, math.log(HY_TARGET) / HY_DECAY_PCT_LO, db)
    absd = jnp.asarray(np.abs(deltas)[None], F32)
    b1r, b2r, frr = (jnp.tile(v.reshape(1, -1), (1, 2)) for v in (b1, b2, freq))

    def full(a):
        return pl.BlockSpec(a.shape, lambda q: (0, 0))

    return pl.pallas_call(
        _spectra_kernel,
        grid=(nblk,),
        in_specs=[pl.BlockSpec((p, LANE), lambda q: (q, 0)), full(w1p), full(b1r), full(w2bd), full(b2r),
                  pl.BlockSpec((HY_HID, db), lambda q: (0, jnp.where(q < nblk // 2, 1, 0))),
                  full(frr), full(absd), full(fa)],
        out_specs=pl.BlockSpec((nct, 1, 2 * p, LANE), lambda q: (0, jnp.maximum(q - 1, 0), 0, 0)),
        out_shape=jax.ShapeDtypeStruct((nct, nblk - 1, 2 * p, LANE), F32),
        scratch_shapes=[pltpu.VMEM((2 * p, db), F32)],
        compiler_params=_params("arbitrary"),
        name="hyena_spectra",
    )(feats, w1p, b1r, w2bd, b2r, w3, frr, absd, fa)


def _hyena_kernel(w_ref, gate_ref, gs_ref, hb_ref, fa_ref, iv_ref, o_ref, u_scr, y_scr, *, p):
    bt, l = w_ref.shape[1], w_ref.shape[2]
    nb = l // p
    fa = fa_ref[...]
    for j in range(nb):
        rhs = jnp.concatenate([w_ref[0, bb, j * p:(j + 1) * p, :] for bb in range(bt)], axis=1)
        u_scr[j] = _dot(fa, rhs)
    rt_rows = 64
    for i in range(nb):
        def body(rt, carry):
            r0 = pl.multiple_of(rt * rt_rows, rt_rows)
            for bb in range(bt):
                ls = slice(bb * LANE, (bb + 1) * LANE)
                acc_re = jnp.zeros((rt_rows, LANE), F32)
                acc_im = jnp.zeros((rt_rows, LANE), F32)
                for j in range(nb):
                    s = i - j + nb - 1
                    gre = gs_ref[0, s, pl.ds(r0, rt_rows), :]
                    gim = gs_ref[0, s, pl.ds(p + r0, rt_rows), :]
                    ure = u_scr[j, pl.ds(r0, rt_rows), ls]
                    uim = u_scr[j, pl.ds(p + r0, rt_rows), ls]
                    acc_re = acc_re + (gre * ure - gim * uim)
                    acc_im = acc_im + (gre * uim + gim * ure)
                y_scr[pl.ds(r0, rt_rows), ls] = acc_re
                y_scr[pl.ds(p + r0, rt_rows), ls] = acc_im
            return carry
        lax.fori_loop(0, p // rt_rows, body, 0)
        conv = _dot(iv_ref[...], _bf(y_scr[...]))
        sl = slice(i * p, (i + 1) * p)
        for bb in range(bt):
            wi = w_ref[0, bb, sl, :].astype(F32)
            o_ref[0, bb, sl, :] = (gate_ref[0, bb, sl, :].astype(F32)
                                   * (conv[:, bb * LANE:(bb + 1) * LANE] + wi * hb_ref[...])).astype(o_ref.dtype)


def _hyena(w_t, gate_t, filt_params, hy_bias, b, l):
    nct = w_t.shape[0]
    p = min(DFT_BLOCK, l)
    nb = l // p
    nseg = 2 * nb - 1
    bt = min(b, HY_BATCH if nb == 1 else HY_BATCH_LONG)
    fa, iv = _dft_tables(p)
    spectra = _hyena_spectra(l, p, fa, *filt_params)
    w4 = w_t.reshape(nct, b, l, LANE)
    g4 = gate_t.reshape(nct, b, l, LANE)
    hbr = hy_bias.reshape(1, -1)
    act = pl.BlockSpec((1, bt, l, LANE), lambda ct, bi: (ct, bi, 0, 0))

    def full(a):
        return pl.BlockSpec(a.shape, lambda ct, bi: (0, 0), pipeline_mode=pl.Buffered(1))

    out = pl.pallas_call(
        functools.partial(_hyena_kernel, p=p),
        grid=(nct, b // bt),
        in_specs=[act, act,
                  pl.BlockSpec((1, nseg, 2 * p, LANE), lambda ct, bi: (ct, 0, 0, 0), pipeline_mode=pl.Buffered(1)),
                  pl.BlockSpec((1, LANE), lambda ct, bi: (0, ct)),
                  full(fa), full(iv)],
        out_specs=act,
        out_shape=jax.ShapeDtypeStruct((nct, b, l, LANE), BF16),
        scratch_shapes=[pltpu.VMEM((nb, 2 * p, bt * LANE), F32), pltpu.VMEM((2 * p, bt * LANE), F32)],
        compiler_params=_params("arbitrary", "arbitrary"),
        name="hyena_conv",
    )(w4, g4, spectra, hbr, fa, iv)
    return out.reshape(nct, b * l, LANE)


def _proj_out0_kernel(x_ref, yf_ref, xs_ref, dt_ref, z_ref, yh_ref, mod_ref, mod1_ref, naw_ref, w_ref, dtb_ref,
                      alog_ref, dskip_ref, ex_ref, nw1_ref, w1_ref, wvt_ref, init_ref,
                      o_ref, q_ref, k_ref, v_ref, g_ref, s_scr, *, seq_len, n_tiles):
    j = n_tiles - 1 - pl.program_id(0)
    tm, d = x_ref.shape
    y_b = _ssd_tile(1, xs_ref[...], dt_ref[...], s_scr, init_ref, ((j + 1) * tm) % seq_len == 0,
                    dtb_ref[...], alog_ref[...], dskip_ref[1:2, :], ex_ref[1])
    ya = _rms((yf_ref[...].astype(F32) + y_b) * _silu(z_ref[...].astype(F32)), naw_ref[...])
    yh = jnp.concatenate([yh_ref[t] for t in range(yh_ref.shape[0])], axis=1)
    da = ya.shape[1]
    acc = _dot(_bf(ya), w_ref[0:da, :]) + _dot(_bf(yh), w_ref[da:, :])
    x_new = x_ref[...] + mod_ref[0][:, 2 * d:3 * d] * acc
    o_ref[...] = x_new

    m1 = mod1_ref[0]
    h1 = _bf(_rms(x_new, nw1_ref[...]) * (1.0 + m1[:, d:2 * d]) + m1[:, 0:d])
    step = 2 * LANE
    for off, t_ref in ((0, q_ref), (d, k_ref)):
        for c0 in range(0, d, step):
            res = _dot(h1, w1_ref[:, off + c0:off + c0 + step])
            for t in range(step // LANE):
                t_ref[c0 // LANE + t] = res[:, t * LANE:(t + 1) * LANE].astype(t_ref.dtype)
    v_ref[...] = _dot_nt(wvt_ref[...], h1).astype(v_ref.dtype)
    for c0 in range(0, d, step):
        g_ref[:, c0:c0 + step] = _dot(h1, w1_ref[:, 3 * d + c0:3 * d + c0 + step]).astype(g_ref.dtype)


def _proj_out0(x2d, y_f, xs, dt_raw, z, yh_t, mod, mod1, norm_a_w, w_bf, ssd_consts, norm_w1, w_in1_bf, wv_t, init,
               seq_len, rows_per_mod, mod_base):
    m_rows, d = x2d.shape
    tm = ROW_TILE
    assert m_rows % tm == 0 and rows_per_mod % tm == 0 and seq_len % tm == 0 and tm % CHUNK == 0
    da = y_f.shape[1]
    nt = yh_t.shape[0]
    n_tiles = m_rows // tm
    consts = [norm_a_w, w_bf] + list(ssd_consts) + [norm_w1, w_in1_bf, wv_t]

    def rowspec(wd):
        return pl.BlockSpec((tm, wd), lambda i: (n_tiles - 1 - i, 0))

    def tiles(n):
        return pl.BlockSpec((n, tm, LANE), lambda i: (0, n_tiles - 1 - i, 0))

    def modspec():
        return pl.BlockSpec((1, 1, 3 * d), lambda i: (mod_base + ((n_tiles - 1 - i) * tm) // rows_per_mod, 0, 0))

    def const(a):
        nd = a.ndim
        return pl.BlockSpec(a.shape, lambda i: (0,) * nd, pipeline_mode=pl.Buffered(1))

    return pl.pallas_call(
        functools.partial(_proj_out0_kernel, seq_len=seq_len, n_tiles=n_tiles),
        grid=(n_tiles,),
        in_specs=[rowspec(d), rowspec(da), rowspec(xs.shape[1]), rowspec(LANE), rowspec(da), tiles(nt),
                  modspec(), modspec()]
                 + [const(a) for a in consts]
                 + [pl.BlockSpec((1, 1, da, N_STATE), lambda i: (((n_tiles - 1 - i) * tm) // seq_len, 1, 0, 0))],
        out_specs=[rowspec(d), tiles(d // LANE), tiles(d // LANE),
                   pl.BlockSpec((d, tm), lambda i: (0, n_tiles - 1 - i)), rowspec(d)],
        out_shape=[jax.ShapeDtypeStruct((m_rows, d), F32),
                   jax.ShapeDtypeStruct((d // LANE, m_rows, LANE), BF16),
                   jax.ShapeDtypeStruct((d // LANE, m_rows, LANE), BF16),
                   jax.ShapeDtypeStruct((d, m_rows), BF16), jax.ShapeDtypeStruct((m_rows, d), BF16)],
        scratch_shapes=[pltpu.VMEM((N_STATE, da), F32)],
        compiler_params=_params("arbitrary"),
        name="proj_out0",
    )(x2d, y_f, xs, dt_raw, z, yh_t, mod, mod1, *consts, init)


def _proj_out1_kernel(x_ref, o_ref_in, g_ref, mod_ref, fw_ref, w_ref, y_ref):
    d = x_ref.shape[1]
    o = jnp.concatenate([o_ref_in[t] for t in range(o_ref_in.shape[0])], axis=1).astype(F32)
    a = o * _silu(g_ref[...].astype(F32))
    acc = _dot(_bf(a), w_ref[...])
    gate = mod_ref[0][:, 2 * d:3 * d]
    y_ref[...] = _rms(x_ref[...] + gate * acc, fw_ref[...])


def _proj_out1(x2d, o_t, g, mod, final_w, w_bf, rows_per_mod, mod_base):
    m_rows, d = x2d.shape
    tm = ROW_TILE
    assert m_rows % tm == 0 and rows_per_mod % tm == 0
    nt = o_t.shape[0]
    return pl.pallas_call(
        _proj_out1_kernel,
        grid=(m_rows // tm,),
        in_specs=[pl.BlockSpec((tm, d), lambda i: (i, 0)),
                  pl.BlockSpec((nt, tm, LANE), lambda i: (0, i, 0)),
                  pl.BlockSpec((tm, g.shape[1]), lambda i: (i, 0)),
                  pl.BlockSpec((1, 1, 3 * d), lambda i: (mod_base + (i * tm) // rows_per_mod, 0, 0)),
                  pl.BlockSpec((1, d), lambda i: (0, 0)),
                  pl.BlockSpec(w_bf.shape, lambda i: (0, 0), pipeline_mode=pl.Buffered(1))],
        out_specs=pl.BlockSpec((tm, d), lambda i: (i, 0)),
        out_shape=jax.ShapeDtypeStruct((m_rows, d), F32),
        compiler_params=_params("arbitrary"),
        name="proj_out1",
    )(x2d, o_t, g, mod, final_w, w_bf)


def _ctx_layer_kernel(x_ref, mod_ref, nw_ref, wi_ref, wo_ref, fw_ref, y_ref, ck_ref, cv_ref):
    tm, d = x_ref.shape
    nseq = ck_ref.shape[0]
    l = tm // nseq
    m = mod_ref[0]
    x = x_ref[...]
    hb = _bf(_rms(x, nw_ref[...]) * (1.0 + m[:, d:2 * d]) + m[:, 0:d])
    qb = _bf(_dot(hb, wi_ref[:, 0:d]) * (HD ** -0.5))
    k = _dot(hb, wi_ref[:, d:2 * d])
    v = _dot(hb, wi_ref[:, 2 * d:3 * d])
    g = _dot(hb, wi_ref[:, 3 * d:4 * d])
    nh = d // HD
    o_rows = []
    for s in range(nseq):
        rows = slice(s * l, (s + 1) * l)
        scores, vbs = [], []
        for h in range(nh):
            sl = slice(h * HD, (h + 1) * HD)
            kh = k[rows, sl]
            vh = v[rows, sl]
            ck_ref[s, 0, h] = kh
            cv_ref[s, 0, h] = vh
            vbs.append(_bf(vh))
            scores.append(_dot_nt(_bf(kh), qb[rows, sl]))
        s_all = jnp.concatenate(scores, axis=1)
        pexp = jnp.exp(s_all - jnp.max(s_all, axis=0, keepdims=True))
        den = jnp.sum(pexp, axis=0, keepdims=True)
        pb = _bf(pexp)
        o_t = jnp.concatenate(
            [lax.dot_general(vbs[h], pb[:, h * l:(h + 1) * l], (((0,), (0,)), ((), ())),
                             preferred_element_type=F32) / den[:, h * l:(h + 1) * l] for h in range(nh)], axis=0)
        o_rows.append(o_t.T)
    a = jnp.concatenate(o_rows, axis=0) * _silu(g)
    acc = _dot(_bf(a), wo_ref[...])
    y_ref[...] = _rms(x + m[:, 2 * d:3 * d] * acc, fw_ref[...])


def _ctx_layer(x2d, b, l, mod, norm_w, w_in_bf, w_out_bf, final_w):
    m_rows, d = x2d.shape
    tm = ROW_TILE
    assert m_rows % tm == 0 and tm % l == 0
    nseq = tm // l
    nh = d // HD
    cache_spec = pl.BlockSpec((nseq, 1, nh, l, HD), lambda i: (i, 0, 0, 0, 0))
    cache_shape = jax.ShapeDtypeStruct((b, 1, nh, l, HD), F32)

    def const(a):
        nd = a.ndim
        return pl.BlockSpec(a.shape, lambda i: (0,) * nd, pipeline_mode=pl.Buffered(1))

    return pl.pallas_call(
        _ctx_layer_kernel,
        grid=(m_rows // tm,),
        in_specs=[pl.BlockSpec((tm, d), lambda i: (i, 0)),
                  pl.BlockSpec((1, 1, 3 * d), lambda i: (0, 0, 0)),
                  const(norm_w), const(w_in_bf), const(w_out_bf), const(final_w)],
        out_specs=[pl.BlockSpec((tm, d), lambda i: (i, 0)), cache_spec, cache_spec],
        out_shape=[jax.ShapeDtypeStruct((m_rows, d), F32), cache_shape, cache_shape],
        compiler_params=_params("arbitrary"),
        name="ctx_layer",
    )(x2d, mod, norm_w, w_in_bf, w_out_bf, final_w)


def _na_bias_kernel(rpb_ref, o_ref):
    h = pl.program_id(0)
    ndr = 2 * WIN_H - 1
    ndc = 2 * WIN_W - 1
    ck = lax.broadcasted_iota(jnp.int32, (GRID_W, LANE), 0)
    lane = lax.broadcasted_iota(jnp.int32, (GRID_W, LANE), 1)
    cq = lane & (GRID_W - 1)
    first = lane < GRID_W
    dc = jnp.clip(ck - cq + (WIN_W - 1), 0, ndc - 1)
    col0 = jnp.clip(cq - WIN_W // 2, 0, GRID_W - WIN_W)
    col_in = (ck >= col0) & (ck < col0 + WIN_W)
    dc_is = [(dc == e) & col_in for e in range(ndc)]
    tables = []
    for dr in range(ndr):
        t = jnp.full((GRID_W, LANE), NEG_INF, F32)
        for e in range(ndc):
            t = jnp.where(dc_is[e], rpb_ref[(h * ndr + dr) * ndc + e], t)
        tables.append(t)
    for ip in range(NA_BAND):
        for ap in range(NA_QTILE // 2):
            x = ip - 2 * ap + (WIN_H - 1) - NA_QROWS // 2
            o_ref[0, ip * GRID_W:(ip + 1) * GRID_W, ap * LANE:(ap + 1) * LANE] = jnp.where(
                first, tables[x], tables[x - 1])


def _na_bias_tables(rpb):
    nh = rpb.shape[0]
    shape = (NA_BAND * GRID_W, NA_QTILE * GRID_W)
    return pl.pallas_call(
        _na_bias_kernel,
        grid=(nh,),
        in_specs=[pl.BlockSpec(memory_space=pltpu.SMEM)],
        out_specs=pl.BlockSpec((1,) + shape, lambda h: (h, 0, 0)),
        out_shape=jax.ShapeDtypeStruct((nh,) + shape, F32),
        compiler_params=_params("arbitrary"),
        name="na_bias",
    )(rpb.reshape(-1))


def _na_kernel(q_ref, kp_ref, kc_ref, kn_ref, vp_ref, vc_ref, vn_ref, ck_ref, cv_ref, bias_ref, o_ref, mask_scr,
               s_scr, *, n_rows):
    rb = pl.program_id(1)
    qrows = NA_QROWS
    nq = qrows * GRID_W
    half = (qrows // 2) * GRID_W
    qt = NA_QTILE * GRID_W
    nband = NA_BAND * GRID_W
    q = q_ref[0, 0] * (HD ** -0.5)
    kloc = jnp.concatenate([kp_ref[0, 0][nq - half:nq], kc_ref[0, 0], kn_ref[0, 0][0:half]], axis=0)
    vloc = jnp.concatenate([vp_ref[:, nq - half:nq], vc_ref[...], vn_ref[:, 0:half]], axis=1)

    @pl.when(pl.program_id(2) == 0)
    def _():
        for t in range(qrows // NA_QTILE):
            i = t * NA_QTILE + lax.broadcasted_iota(jnp.int32, (nband, qt), 0) // GRID_W
            a = t * NA_QTILE + lax.broadcasted_iota(jnp.int32, (nband, qt), 1) // GRID_W
            r = rb * qrows + a
            kr = rb * qrows - qrows // 2 + i
            rs = jnp.clip(r - WIN_H // 2, 0, n_rows - WIN_H)
            mask_scr[t] = jnp.where((kr >= rs) & (kr < rs + WIN_H), 0.0, NEG_INF)

    rows = []
    for j in range(LANE // HD):
        sl = slice(j * HD, (j + 1) * HD)
        ckb = _bf(ck_ref[0, j])
        cvt = cv_ref[0, 0, sl, :]
        tiles = []
        nctx = ckb.shape[0]
        for t in range(qrows // NA_QTILE):
            k0 = t * NA_QTILE * GRID_W
            qh = q[t * qt:(t + 1) * qt, sl]
            m = jnp.full((1, qt), NEG_INF, F32)
            for c0 in range(0, nband + nctx, NA_KCHUNK):
                if c0 < nband:
                    rs_ = slice(c0, c0 + NA_KCHUNK)
                    s = _dot_nt(kloc[k0 + c0:k0 + c0 + NA_KCHUNK, sl], qh) + bias_ref[j, rs_, :] + mask_scr[t, rs_, :]
                else:
                    s = _dot_nt(ckb[c0 - nband:c0 - nband + NA_KCHUNK], qh)
                s_scr[c0:c0 + NA_KCHUNK, :] = s
                m = jnp.maximum(m, jnp.max(s, axis=0, keepdims=True))
            den = jnp.zeros((1, qt), F32)
            o = jnp.zeros((HD, qt), F32)
            for c0 in range(0, nband + nctx, NA_KCHUNK):
                pexp = jnp.exp(s_scr[c0:c0 + NA_KCHUNK, :] - m)
                den = den + jnp.sum(pexp, axis=0, keepdims=True)
                if c0 < nband:
                    vt = vloc[sl, k0 + c0:k0 + c0 + NA_KCHUNK]
                else:
                    vt = cvt[:, c0 - nband:c0 - nband + NA_KCHUNK]
                o = o + _dot(vt, _bf(pexp))
            tiles.append(o / den)
        rows.append(jnp.concatenate(tiles, axis=1))
    o_ref[0, 0] = jnp.concatenate(rows, axis=0).T.astype(o_ref.dtype)


def _na_attn(q_t, k_t, v_c, cache_k, cache_vt, bias, b, l):
    npair = q_t.shape[0]
    hpp = LANE // HD
    n_rows = l // GRID_W
    nrb = n_rows // NA_QROWS
    nq = NA_QROWS * GRID_W
    lc = cache_k.shape[2]
    q4, k4 = (a.reshape(npair, b, l, LANE) for a in (q_t, k_t))

    def prev_blk(rb):
        return jnp.maximum(rb - 1, 0)

    def next_blk(rb):
        return jnp.minimum(rb + 1, nrb - 1)

    def same_blk(rb):
        return rb

    def tok(f):
        return pl.BlockSpec((1, 1, nq, LANE), lambda bi, rb, hp: (hp, bi, f(rb), 0))

    def chan(f):
        return pl.BlockSpec((LANE, nq), lambda bi, rb, hp: (hp, bi * nrb + f(rb)))

    out = pl.pallas_call(
        functools.partial(_na_kernel, n_rows=n_rows),
        grid=(b, nrb, npair),
        in_specs=[tok(same_blk), tok(prev_blk), tok(same_blk), tok(next_blk),
                  chan(prev_blk), chan(same_blk), chan(next_blk),
                  pl.BlockSpec((1, hpp, lc, HD), lambda bi, rb, hp: (bi, hp, 0, 0)),
                  pl.BlockSpec((1, 1, LANE, lc), lambda bi, rb, hp: (bi, hp, 0, 0)),
                  pl.BlockSpec((hpp,) + bias.shape[1:], lambda bi, rb, hp: (hp, 0, 0))],
        out_specs=tok(same_blk),
        out_shape=jax.ShapeDtypeStruct((npair, b, l, LANE), BF16),
        scratch_shapes=[pltpu.VMEM((NA_QROWS // NA_QTILE,) + bias.shape[1:], F32),
                        pltpu.VMEM((bias.shape[1] + lc, bias.shape[2]), F32)],
        compiler_params=_params("arbitrary", "arbitrary", "arbitrary"),
        name="na_attn",
    )(q4, k4, k4, k4, v_c, v_c, v_c, cache_k, cache_vt, bias)
    return out.reshape(npair, b * l, LANE)


def _ctx_layer0_kernel(x_ref, mod_ref, nw_ref, w_ref, cwa_ref, cba_ref, cwb_ref, cbb_ref, dtb_ref, alog_ref,
                       dskip_ref, ex_ref, naw_ref, wo_ref, gs_ref, hyb_ref, fa_ref, iv_ref, o_ref, fin_ref,
                       *, seq_len, cols):
    tm, d = x_ref.shape
    o_z, o_xbc, o_u, o_g, o_dt = cols
    d_a, d_xbc, d_b = o_xbc - o_z, o_u - o_xbc, o_dt - o_g
    nseq, q, p = tm // seq_len, CHUNK, seq_len
    nc = seq_len // q
    step = 2 * LANE
    m = mod_ref[0]
    x = x_ref[...]
    hb = _bf(_rms(x, nw_ref[...]) * (1.0 + m[:, d:2 * d]) + m[:, 0:d])

    def conv(off, cw_ref, cb_ref, coff):
        return _conv3_rows(_dot(hb, w_ref[:, off:off + step]), 0.0, 0.0, seq_len,
                           cw_ref[:, coff:coff + step], cb_ref[:, coff:coff + step])

    xs = jnp.concatenate([_silu(conv(o_xbc + c0, cwa_ref, cba_ref, c0)) for c0 in range(0, d_xbc, step)], axis=1)
    dt_raw = _dot(hb, w_ref[:, o_dt:o_dt + LANE])
    xa = xs[:, 0:d_a]
    bm = _bf(xs[:, d_a:d_a + N_GROUPS * N_STATE])
    cm = _bf(xs[:, d_a + N_GROUPS * N_STATE:d_xbc])
    dtb, alog = dtb_ref[...], alog_ref[...]
    y_rows = []
    for s in range(nseq):
        state = [jnp.zeros((N_STATE, d_a), F32), jnp.zeros((N_STATE, d_a), F32)]
        y_chunks = [None] * nc
        for direction, order in ((0, range(nc)), (1, reversed(range(nc)))):
            for c in order:
                rows = slice(s * seq_len + c * q, s * seq_len + (c + 1) * q)
                y, state[direction] = _ssd_chunk(direction, xa[rows], bm[rows], cm[rows], dt_raw[rows],
                                                 state[direction], dtb, alog,
                                                 dskip_ref[direction:direction + 1, :], ex_ref[direction])
                y_chunks[c] = y if y_chunks[c] is None else y_chunks[c] + y
        for direction in range(2):
            for t in range(d_a // LANE):
                fin_ref[s, direction, t * LANE:(t + 1) * LANE, :] = state[direction][:, t * LANE:(t + 1) * LANE].T
        y_rows += y_chunks
    z = jnp.concatenate([_dot(hb, w_ref[:, o_z + c0:o_z + c0 + step]) for c0 in range(0, d_a, step)], axis=1)
    ya = _rms(jnp.concatenate(y_rows, axis=0) * _silu(z), naw_ref[...])

    yh_cols = []
    for c0 in range(0, d_b, step):
        wv = conv(o_u + d_b + c0, cwb_ref, cbb_ref, d_b + c0) * conv(o_u + 2 * d_b + c0, cwb_ref, cbb_ref, 2 * d_b + c0)
        gate = conv(o_u + c0, cwb_ref, cbb_ref, c0) * _silu(_dot(hb, w_ref[:, o_g + c0:o_g + c0 + step]))
        spec = jnp.concatenate([gs_ref[c0 // LANE + t, 0] for t in range(step // LANE)], axis=1)
        g_re, g_im = spec[0:p], spec[p:]
        outs = []
        for s in range(nseq):
            rows = slice(s * seq_len, (s + 1) * seq_len)
            u = _dot(fa_ref[...], _bf(wv[rows]))
            u_re, u_im = u[0:p], u[p:]
            prod = jnp.concatenate([g_re * u_re - g_im * u_im, g_re * u_im + g_im * u_re], axis=0)
            lc = _dot(iv_ref[...], _bf(prod))
            outs.append(gate[rows] * (lc + wv[rows] * hyb_ref[:, c0:c0 + step]))
        yh_cols.append(jnp.concatenate(outs, axis=0))
    yh = jnp.concatenate(yh_cols, axis=1)

    acc = _dot(_bf(ya), wo_ref[0:d_a, :]) + _dot(_bf(yh), wo_ref[d_a:, :])
    o_ref[...] = x + m[:, 2 * d:3 * d] * acc


def _ctx_layer0(x2d, b, l, mod, norm_w, w_in_bf, cols, p, w_out_bf, filt_params):
    m_rows, d = x2d.shape
    tm = ROW_TILE
    assert m_rows % tm == 0 and tm % l == 0 and l <= DFT_BLOCK and l % CHUNK == 0
    nseq = tm // l
    dm = N_HEADS * HD
    fa, iv = _dft_tables(l)
    spectra = _hyena_spectra(l, l, fa, *filt_params)
    consts = ([norm_w, w_in_bf, p["conv_a_w"], p["conv_a_b"].reshape(1, -1), p["conv_b_w"], p["conv_b_b"].reshape(1, -1)]
              + list(_ssd_consts(p["dt_bias"], p["a_log"], p["d_skip"]))
              + [p["norm_a_w"], w_out_bf, spectra, p["hy_bias"].reshape(1, -1), fa, iv])

    def const(a):
        nd = a.ndim
        return pl.BlockSpec(a.shape, lambda i: (0,) * nd, pipeline_mode=pl.Buffered(1))

    return pl.pallas_call(
        functools.partial(_ctx_layer0_kernel, seq_len=l, cols=cols),
        grid=(m_rows // tm,),
        in_specs=[pl.BlockSpec((tm, d), lambda i: (i, 0)), pl.BlockSpec((1, 1, 3 * d), lambda i: (0, 0, 0))]
                 + [const(a) for a in consts],
        out_specs=[pl.BlockSpec((tm, d), lambda i: (i, 0)),
                   pl.BlockSpec((nseq, 2, dm, N_STATE), lambda i: (i, 0, 0, 0))],
        out_shape=[jax.ShapeDtypeStruct((m_rows, d), F32), jax.ShapeDtypeStruct((b, 2, dm, N_STATE), F32)],
        compiler_params=_params("arbitrary"),
        name="ctx_layer0",
    )(x2d, mod, *consts)


def _reorder_kernel(w_ref, o_ref, *, o_dt, n_dt):
    rows, n = w_ref.shape
    rest = n - o_dt - n_dt
    o_ref[:, 0:o_dt] = w_ref[:, 0:o_dt].astype(o_ref.dtype)
    o_ref[:, o_dt:o_dt + rest] = w_ref[:, o_dt + n_dt:n].astype(o_ref.dtype)
    tail = jnp.concatenate([w_ref[:, o_dt:o_dt + n_dt], jnp.zeros((rows, LANE - n_dt), F32)], axis=1)
    o_ref[:, o_dt + rest:o_dt + rest + LANE] = tail.astype(o_ref.dtype)


def _reorder_w_in(w, o_dt, n_dt):
    d, n = w.shape
    n_out = n - n_dt + LANE
    tr = LANE
    return pl.pallas_call(
        functools.partial(_reorder_kernel, o_dt=o_dt, n_dt=n_dt),
        grid=(d // tr,),
        in_specs=[pl.BlockSpec((tr, n), lambda i: (i, 0))],
        out_specs=pl.BlockSpec((tr, n_out), lambda i: (i, 0)),
        out_shape=jax.ShapeDtypeStruct((d, n_out), BF16),
        compiler_params=_params("arbitrary"),
        name="reorder_w_in",
    )(w)


def _layer0_cols(d_b):
    dm = N_HEADS * HD
    d_xbc = dm + 2 * N_GROUPS * N_STATE
    return (0, dm, dm + d_xbc, dm + d_xbc + 3 * d_b, dm + d_xbc + 4 * d_b)


def _latent_layer0(x2d, b, l, mod, mod1, mod_base, norm_w, w_in_bf, w_out_bf, p, init, filt_params, norm_w1,
                   w_in1_bf, wv_t):
    cols = _layer0_cols(p["hy_bias"].shape[0])
    consts = _ssd_consts(p["dt_bias"], p["a_log"], p["d_skip"])
    z, xs, w_t, gate_t, dt_raw, y_f = _proj_in0(x2d, mod, norm_w, w_in_bf, cols, p["conv_a_w"], p["conv_a_b"],
                                                p["conv_b_w"], p["conv_b_b"], consts, init, l, l, mod_base)
    yh_t = _hyena(w_t, gate_t, filt_params, p["hy_bias"], b, l)
    return _proj_out0(x2d, y_f, xs, dt_raw, z, yh_t, mod, mod1, p["norm_a_w"], w_out_bf, consts, norm_w1, w_in1_bf,
                      wv_t, init, l, l, mod_base)


def kernel(x_prompt, x_sample, state_ssd, cache_k, cache_v, c, c_ctx, norm_w, w_ada, b_ada, w_in_e, w_out_e, conv_a_w, conv_a_b, dt_bias, a_log, d_skip, norm_a_w, conv_b_w, conv_b_b, hf_w1, hf_b1, hf_w2, hf_b2, hf_w3, hf_freq, hy_bias, w_in_o, w_out_o, rpb, final_norm_w):
    bp, lp, d = x_prompt.shape
    bs, ls, _ = x_sample.shape
    dm = N_HEADS * HD
    d_xbc = dm + 2 * N_GROUPS * N_STATE
    n_dt = 2 * N_HEADS

    cvecs = jnp.concatenate([c_ctx[None], c, jnp.zeros((SUBLANE - 1 - bs, d), F32)], axis=0)
    mods = _ada_mods(cvecs, w_ada, b_ada)

    xp = x_prompt.reshape(bp * lp, d)
    xs = x_sample.reshape(bs * ls, d)

    w_in0 = _reorder_w_in(w_in_e[0], dm + d_xbc, n_dt)
    w_out0 = w_out_e[0].astype(BF16)
    p0 = dict(conv_a_w=conv_a_w[0], conv_a_b=conv_a_b[0], dt_bias=dt_bias[0], a_log=a_log[0], d_skip=d_skip[0],
              norm_a_w=norm_a_w[0].reshape(1, -1), conv_b_w=conv_b_w[0], conv_b_b=conv_b_b[0], hy_bias=hy_bias[0])
    mod0 = mods[0].reshape(SUBLANE, 1, 3 * d)
    nw0 = norm_w[0].reshape(1, d)
    hf = (hf_w1[0], hf_b1[0], hf_w2[0], hf_b2[0], hf_w3[0], hf_freq[0])
    w_in1 = w_in_o[0].astype(BF16)
    w_out1 = w_out_o[0].astype(BF16)
    wv_t = w_in_o[0][:, 2 * d:3 * d].T.astype(BF16)
    mod1 = mods[1].reshape(SUBLANE, 1, 3 * d)
    nw1 = norm_w[1].reshape(1, d)
    xp, fin = _ctx_layer0(xp, bp, lp, mod0, nw0, w_in0, _layer0_cols(hy_bias.shape[1]), p0, w_out0, hf)
    init_s = state_ssd[:, 0].reshape(bs, 2, dm, N_STATE)
    xs, q_t, k_t, v_c, g = _latent_layer0(xs, bs, ls, mod0, mod1, 1, nw0, w_in0, w_out0, p0, init_s, hf, nw1, w_in1,
                                          wv_t)
    new_state_ssd = fin.reshape(bp, 1, 2, N_HEADS, HD, N_STATE)

    fw = final_norm_w.reshape(1, d)
    y_prompt, new_cache_k, new_cache_v = _ctx_layer(xp, bp, lp, mod1, nw1, w_in1, w_out1, fw)
    y_prompt = y_prompt.reshape(bp, lp, d)

    bias = _na_bias_tables(rpb[0])
    lc = cache_v.shape[3]
    cache_vt = jnp.swapaxes(cache_v[:, 0], 2, 3).reshape(bs, d // LANE, LANE, lc).astype(BF16)
    o_t = _na_attn(q_t, k_t, v_c, cache_k[:, 0], cache_vt, bias, bs, ls)
    y_sample = _proj_out1(xs, o_t, g, mod1, fw, w_out1, ls, 1).reshape(bs, ls, d)

    return (y_prompt, y_sample, new_state_ssd, new_cache_k, new_cache_v)
```

```python
import functools
import math

import jax
import jax.numpy as jnp
import numpy as np
from jax import lax
from jax.experimental import pallas as pl
from jax.experimental.pallas import tpu as pltpu

F32 = jnp.float32
BF16 = jnp.bfloat16

EPS = 1e-6
GRID_W = 64
WIN_H = 8
WIN_W = 16
HD = 64
N_HEADS = 16
N_STATE = 128
N_GROUPS = 2
CHUNK = 128
HY_EMB = 33
HY_BANDS = (HY_EMB - 1) // 2
HY_HID = 64
HY_TARGET = 1e-2
HY_DECAY_PCT_HI = 0.3
HY_DECAY_PCT_LO = 1.5

LANE = 128
SUBLANE = 8
VMEM_LIMIT = 56 * 1024 * 1024

ROW_TILE = 512
DFT_BLOCK = 512
HY_BATCH = 32
HY_BATCH_LONG = 2
NA_QROWS = 8
NA_QTILE = 4
NA_KCHUNK = 128
NA_BAND = NA_QTILE + WIN_H
NEG_INF = float("-inf")


def _bf(x):
    return x.astype(BF16)


def _dot(a, b):
    return jnp.dot(a, b, preferred_element_type=F32)


def _dot_nt(a, b):
    return lax.dot_general(a, b, (((1,), (1,)), ((), ())), preferred_element_type=F32)


def _split2(x):
    hi = _bf(x)
    lo = _bf(x - hi.astype(F32))
    return hi, lo


def _split3(x):
    hi = _bf(x)
    r = x - hi.astype(F32)
    mid = _bf(r)
    lo = _bf(r - mid.astype(F32))
    return hi, mid, lo


def _dot3(a, b):
    ah, al = _split2(a)
    bh, bl = _split2(b)
    return _dot(ah, bh) + (_dot(ah, bl) + _dot(al, bh))


def _dot_rhs_parts(a_exact, b, parts):
    pieces = _split3(b) if parts == 3 else _split2(b)
    acc = _dot(a_exact, pieces[0])
    for p in pieces[1:]:
        acc = acc + _dot(a_exact, p)
    return acc


def _silu(x):
    return x * jax.nn.sigmoid(x)


def _rms(x, g):
    ms = jnp.mean(x * x, axis=-1, keepdims=True)
    return x * lax.rsqrt(ms + EPS) * g


def _softplus(x):
    return jnp.maximum(x, 0.0) + jnp.log1p(jnp.exp(-jnp.abs(x)))


def _params(*sem):
    return pltpu.CompilerParams(dimension_semantics=sem, vmem_limit_bytes=VMEM_LIMIT)


def _mods_kernel(c_ref, w_ref, b_ref, o_ref):
    a = _silu(c_ref[...])
    o_ref[0] = _dot3(a, w_ref[0]) + b_ref[0]


def _ada_mods(cvecs, w_ada, b_ada):
    depth, d, n3 = w_ada.shape
    tn = n3 // 4
    return pl.pallas_call(
        _mods_kernel,
        grid=(depth, n3 // tn),
        in_specs=[pl.BlockSpec((SUBLANE, d), lambda l, j: (0, 0)),
                  pl.BlockSpec((1, d, tn), lambda l, j: (l, 0, j)),
                  pl.BlockSpec((1, 1, tn), lambda l, j: (l, 0, j))],
        out_specs=pl.BlockSpec((1, SUBLANE, tn), lambda l, j: (l, 0, j)),
        out_shape=jax.ShapeDtypeStruct((depth, SUBLANE, n3), F32),
        compiler_params=_params("arbitrary", "arbitrary"),
        name="ada_mods",
    )(cvecs, w_ada, b_ada.reshape(depth, 1, n3))


def _conv3_rows(res, prev_row, next_row, seq_len, cw, cb):
    tm, width = res.shape
    starts = list(range(0, tm, seq_len))
    ends = [min(s + seq_len, tm) - 1 for s in starts]
    sub = lax.broadcasted_iota(jnp.int32, (SUBLANE, width), 0)
    down = pltpu.roll(res, 1, 0)
    up = pltpu.roll(res, tm - 1, 0)
    dparts, uparts, pos = [], [], 0
    for s in starts:
        fill = prev_row if s == 0 else 0.0
        dparts += [down[pos:s], jnp.where(sub == 0, fill, down[s:s + SUBLANE])]
        pos = s + SUBLANE
    dparts.append(down[pos:tm])
    pos = 0
    for e in ends:
        fill = next_row if e == tm - 1 else 0.0
        uparts += [up[pos:e + 1 - SUBLANE], jnp.where(sub == SUBLANE - 1, fill, up[e + 1 - SUBLANE:e + 1])]
        pos = e + 1
    uparts.append(up[pos:tm])
    down = jnp.concatenate([p for p in dparts if p.shape[0]], axis=0)
    up = jnp.concatenate([p for p in uparts if p.shape[0]], axis=0)
    return cb + down * cw[0:1] + res * cw[1:2] + up * cw[2:3]


def _proj_in0_kernel(x_ref, xp_ref, xn_ref, mod_ref, nw_ref, w_ref, cwa_ref, cba_ref, cwb_ref, cbb_ref,
                     dtb_ref, alog_ref, dskip_ref, ex_ref, init_ref,
                     z_ref, xs_ref, wv_ref, gate_ref, dt_ref, yf_ref, s_scr, *, seq_len, cols):
    i = pl.program_id(0)
    tm, d = x_ref.shape
    o_z, o_xbc, o_u, o_g, o_dt = cols
    d_a, d_xbc, d_b = o_xbc - o_z, o_u - o_xbc, o_dt - o_g
    m = mod_ref[0]

    def modnorm(x):
        return _bf(_rms(x, nw_ref[...]) * (1.0 + m[:, d:2 * d]) + m[:, 0:d])

    hb = modnorm(x_ref[...])
    hh = modnorm(jnp.concatenate([xp_ref[...], xn_ref[...]], axis=0))
    keep_prev = ((i * tm) % seq_len != 0).astype(F32)
    keep_next = (((i + 1) * tm) % seq_len != 0).astype(F32)

    def conv(off, width, cw_ref, cb_ref, coff):
        res = _dot(hb, w_ref[:, off:off + width])
        rh = _dot(hh, w_ref[:, off:off + width])
        return _conv3_rows(res, rh[SUBLANE - 1:SUBLANE] * keep_prev, rh[SUBLANE:SUBLANE + 1] * keep_next, seq_len,
                           cw_ref[:, coff:coff + width], cb_ref[:, coff:coff + width])

    step = 2 * LANE
    for c0 in range(0, d_a, step):
        z_ref[:, c0:c0 + step] = _dot(hb, w_ref[:, o_z + c0:o_z + c0 + step]).astype(z_ref.dtype)
    for c0 in range(0, d_xbc, step):
        xs_ref[:, c0:c0 + step] = _silu(conv(o_xbc + c0, step, cwa_ref, cba_ref, c0)).astype(xs_ref.dtype)
    for c0 in range(0, d_b, step):
        x1 = conv(o_u + d_b + c0, step, cwb_ref, cbb_ref, d_b + c0)
        v = conv(o_u + 2 * d_b + c0, step, cwb_ref, cbb_ref, 2 * d_b + c0)
        wv = x1 * v
        x0 = conv(o_u + c0, step, cwb_ref, cbb_ref, c0)
        gate = x0 * _silu(_dot(hb, w_ref[:, o_g + c0:o_g + c0 + step]))
        for t in range(step // LANE):
            wv_ref[c0 // LANE + t] = wv[:, t * LANE:(t + 1) * LANE].astype(wv_ref.dtype)
            gate_ref[c0 // LANE + t] = gate[:, t * LANE:(t + 1) * LANE].astype(gate_ref.dtype)
    dt_raw = _dot(hb, w_ref[:, o_dt:o_dt + LANE])
    dt_ref[...] = dt_raw
    y_f = _ssd_tile(0, xs_ref[...], dt_raw, s_scr, init_ref, (i * tm) % seq_len == 0,
                    dtb_ref[...], alog_ref[...], dskip_ref[0:1, :], ex_ref[0])
    yf_ref[...] = y_f.astype(yf_ref.dtype)


def _proj_in0(x2d, mod, norm_w, w_bf, cols, conv_a_w, conv_a_b, conv_b_w, conv_b_b, ssd_consts, init, seq_len,
              rows_per_mod, mod_base):
    m_rows, d = x2d.shape
    tm = ROW_TILE
    assert m_rows % tm == 0 and rows_per_mod % tm == 0 and seq_len % tm == 0 and tm % CHUNK == 0
    o_z, o_xbc, o_u, o_g, o_dt = cols
    d_a, d_xbc, d_b = o_xbc - o_z, o_u - o_xbc, o_dt - o_g
    nsub = m_rows // SUBLANE
    spt = tm // SUBLANE
    cba, cbb = conv_a_b.reshape(1, -1), conv_b_b.reshape(1, -1)
    consts = [norm_w, w_bf, conv_a_w, cba, conv_b_w, cbb] + list(ssd_consts)

    def const(a):
        nd = a.ndim
        return pl.BlockSpec(a.shape, lambda i: (0,) * nd, pipeline_mode=pl.Buffered(1))

    def tiles(n):
        return pl.BlockSpec((n, tm, LANE), lambda i: (0, i, 0))

    def rows(wd):
        return pl.BlockSpec((tm, wd), lambda i: (i, 0))

    return pl.pallas_call(
        functools.partial(_proj_in0_kernel, seq_len=seq_len, cols=cols),
        grid=(m_rows // tm,),
        in_specs=[rows(d),
                  pl.BlockSpec((SUBLANE, d), lambda i: (jnp.maximum(i * spt - 1, 0), 0)),
                  pl.BlockSpec((SUBLANE, d), lambda i: (jnp.minimum((i + 1) * spt, nsub - 1), 0)),
                  pl.BlockSpec((1, 1, 3 * d), lambda i: (mod_base + (i * tm) // rows_per_mod, 0, 0))]
                 + [const(a) for a in consts]
                 + [pl.BlockSpec((1, 1, d_a, N_STATE), lambda i: ((i * tm) // seq_len, 0, 0, 0))],
        out_specs=[rows(d_a), rows(d_xbc), tiles(d_b // LANE), tiles(d_b // LANE), rows(LANE), rows(d_a)],
        out_shape=[jax.ShapeDtypeStruct((m_rows, d_a), BF16), jax.ShapeDtypeStruct((m_rows, d_xbc), BF16),
                   jax.ShapeDtypeStruct((d_b // LANE, m_rows, LANE), BF16),
                   jax.ShapeDtypeStruct((d_b // LANE, m_rows, LANE), BF16),
                   jax.ShapeDtypeStruct((m_rows, LANE), F32), jax.ShapeDtypeStruct((m_rows, d_a), BF16)],
        scratch_shapes=[pltpu.VMEM((N_STATE, d_a), F32)],
        compiler_params=_params("arbitrary"),
        name="proj_in0",
    )(x2d, x2d, x2d, mod, *consts, init)


def _ssd_chunk(d, xa, bm, cm, dt_raw, s_prev, dtb, alog, dskip, ex):
    q, dm = xa.shape
    gw = dm // N_GROUPS
    hpg = N_HEADS // N_GROUPS
    row = lax.broadcasted_iota(jnp.int32, (q, q), 0)
    col = lax.broadcasted_iota(jnp.int32, (q, q), 1)
    left = lax.broadcasted_iota(jnp.int32, (q, LANE), 1) < HD
    tri = (row >= col) if d == 0 else (row <= col)

    dt = _softplus(dt_raw + dtb)
    adt = dt * (-jnp.exp(alog))
    cs = _dot_rhs_parts(jnp.where(tri, 1.0, 0.0).astype(BF16), adt, 3)
    cs_t = cs.T
    dt_t = dt.T
    edge = cs[q - 1:q, :] if d == 0 else cs[0:1, :]
    e_cs = _dot(_bf(jnp.exp(cs)), ex)
    w_st = _dot(_bf(jnp.exp(edge - cs) * dt), ex)
    xw = xa * w_st

    y_parts = []
    new_state = []
    for g in range(N_GROUPS):
        bg = bm[:, g * N_STATE:(g + 1) * N_STATE]
        cg = cm[:, g * N_STATE:(g + 1) * N_STATE]
        gmat = _dot_nt(_bf(cg), _bf(bg))
        y_off = _dot(_bf(cg), _bf(s_prev[:, g * gw:(g + 1) * gw]))
        new_state.append(_dot(_bf(bg.astype(F32).T), _bf(xw[:, g * gw:(g + 1) * gw])))
        for pr in range(hpg // 2):
            mh = []
            for j in range(2):
                k = d * N_HEADS + g * hpg + 2 * pr + j
                diff = cs[:, k:k + 1] - cs_t[k:k + 1, :]
                lm = jnp.exp(jnp.where(tri, diff, NEG_INF))
                mh.append(_bf(gmat * lm * dt_t[k:k + 1, :]))
            c0 = g * gw + pr * LANE
            xpair = xa[:, c0:c0 + LANE]
            rhs = jnp.concatenate([_bf(jnp.where(left, xpair, 0.0)), _bf(jnp.where(left, 0.0, xpair))], axis=0)
            y_d = _dot(jnp.concatenate(mh, axis=1), rhs)
            y_parts.append(y_d + y_off[:, pr * LANE:(pr + 1) * LANE] * e_cs[:, c0:c0 + LANE])
    y = jnp.concatenate(y_parts, axis=1) + xa * dskip
    e_edge = e_cs[q - 1:q, :] if d == 0 else e_cs[0:1, :]
    return y, s_prev * e_edge + jnp.concatenate(new_state, axis=1)


def _ssd_tile(direction, xs, dt_raw, s_scr, init_ref, at_seq_edge, dtb, alog, dskip, ex):
    rows, dm = xs.shape[0], N_HEADS * HD

    @pl.when(at_seq_edge)
    def _():
        for t in range(dm // LANE):
            s_scr[:, t * LANE:(t + 1) * LANE] = init_ref[0, 0, t * LANE:(t + 1) * LANE, :].T

    nck = rows // CHUNK
    ys = [None] * nck
    state = s_scr[...]
    for c in (range(nck) if direction == 0 else reversed(range(nck))):
        r = slice(c * CHUNK, (c + 1) * CHUNK)
        ys[c], state = _ssd_chunk(direction, xs[r, 0:dm].astype(F32), xs[r, dm:dm + N_GROUPS * N_STATE],
                                  xs[r, dm + N_GROUPS * N_STATE:], dt_raw[r], state, dtb, alog, dskip, ex)
    s_scr[...] = state
    return jnp.concatenate(ys, axis=0)


def _ssd_consts(dt_bias, a_log, d_skip):
    dm = N_HEADS * HD
    ex = np.zeros((2, LANE, dm), np.float32)
    for d in range(2):
        for h in range(N_HEADS):
            ex[d, d * N_HEADS + h, h * HD:(h + 1) * HD] = 1.0
    pad = LANE - 2 * N_HEADS
    dtb = jnp.pad(dt_bias.reshape(1, 2 * N_HEADS), ((0, 0), (0, pad)))
    alog = jnp.pad(a_log.reshape(1, 2 * N_HEADS), ((0, 0), (0, pad)))
    return dtb, alog, jnp.repeat(d_skip, HD, axis=1), jnp.asarray(ex, BF16)


def _hyena_tables(l):
    pos = np.abs(np.arange(2 * l, dtype=np.float64) - l)
    t = pos / (l - 1)
    w = 2.0 * math.pi * pos / l
    f = np.linspace(1e-4, HY_BANDS - 1, HY_BANDS)
    feats = np.zeros((2 * l, LANE), np.float64)
    feats[:, 0] = t
    feats[:, 1:1 + HY_BANDS] = np.cos(f[None] * w[:, None])
    feats[:, 1 + HY_BANDS:1 + 2 * HY_BANDS] = -np.sin(f[None] * w[:, None])
    return jnp.asarray(feats, F32)


def _dft_tables(p):
    n = 2 * p
    f = np.arange(p, dtype=np.float64)[:, None] + 0.5
    e = np.arange(p, dtype=np.float64)[None]
    ang = 2.0 * math.pi * f * e / n
    fa = np.concatenate([np.cos(ang), -np.sin(ang)], axis=0)
    inv = np.concatenate([np.cos(ang.T), -np.sin(ang.T)], axis=1) * (2.0 / n)
    return tuple(jnp.asarray(m, F32).astype(BF16) for m in (fa, inv))


def _spectra_kernel(f_ref, w1_ref, b1_ref, w2_ref, b2_ref, w3_ref, fr_ref, ad_ref, fa_ref, o_ref, bprev):
    q = pl.program_id(0)
    p = f_ref.shape[0]
    half = p // 2
    feats = f_ref[...]
    pre1 = _dot3(feats, w1_ref[...])
    packed = jnp.concatenate([pre1[:half], pre1[half:]], axis=1)
    fr = fr_ref[...]
    h1 = jnp.sin(fr * (packed + b1_ref[...]))
    h2 = _bf(jnp.sin(fr * (_dot3(h1, w2_ref[...]) + b2_ref[...])))
    w3 = _bf(w3_ref[...])
    filt = jnp.concatenate([_dot(h2[:, :HY_HID], w3), _dot(h2[:, HY_HID:], w3)], axis=0)
    taps = _bf(filt * jnp.exp(-feats[:, 0:1] * ad_ref[...]))
    a = _dot(fa_ref[...], taps)

    @pl.when(q > 0)
    def _():
        g = a + bprev[...]
        for t in range(o_ref.shape[0]):
            o_ref[t, 0] = g[:, t * LANE:(t + 1) * LANE]

    odd = (lax.broadcasted_iota(jnp.int32, (p, a.shape[1]), 0) & 1) == 1
    a_re, a_im = a[0:p], a[p:]
    a_re0 = a_re - taps[0:1, :].astype(F32)
    bprev[0:p] = jnp.where(odd, a_im, -a_im)
    bprev[p:] = jnp.where(odd, -a_re0, a_re0)


def _hyena_spectra(l, p, fa, w1, b1, w2, b2, w3, freq):
    db = w3.shape[1] // 2
    nblk = 2 * l // p
    nct = db // LANE
    feats = _hyena_tables(l)
    w1p = jnp.pad(w1, ((0, LANE - HY_EMB), (0, 0)))
    zero = jnp.zeros_like(w2)
    w2bd = jnp.concatenate([jnp.concatenate([w2, zero], axis=1), jnp.concatenate([zero, w2], axis=1)], axis=0)
    deltas = np.linspace(math.log(HY_TARGET) / HY_DECAY_PCT_HI, math.log(HY_TARGET) / HY_DECAY_PCT_LO, db)
    absd = jnp.asarray(np.abs(deltas)[None], F32)
    b1r, b2r, frr = (jnp.tile(v.reshape(1, -1), (1, 2)) for v in (b1, b2, freq))

    def full(a):
        return pl.BlockSpec(a.shape, lambda q: (0, 0))

    return pl.pallas_call(
        _spectra_kernel,
        grid=(nblk,),
        in_specs=[pl.BlockSpec((p, LANE), lambda q: (q, 0)), full(w1p), full(b1r), full(w2bd), full(b2r),
                  pl.BlockSpec((HY_HID, db), lambda q: (0, jnp.where(q < nblk // 2, 1, 0))),
                  full(frr), full(absd), full(fa)],
        out_specs=pl.BlockSpec((nct, 1, 2 * p, LANE), lambda q: (0, jnp.maximum(q - 1, 0), 0, 0)),
        out_shape=jax.ShapeDtypeStruct((nct, nblk - 1, 2 * p, LANE), F32),
        scratch_shapes=[pltpu.VMEM((2 * p, db), F32)],
        compiler_params=_params("arbitrary"),
        name="hyena_spectra",
    )(feats, w1p, b1r, w2bd, b2r, w3, frr, absd, fa)


def _hyena_kernel(w_ref, gate_ref, gs_ref, hb_ref, fa_ref, iv_ref, o_ref, u_scr, y_scr, *, p):
    bt, l = w_ref.shape[1], w_ref.shape[2]
    nb = l // p
    fa = fa_ref[...]
    for j in range(nb):
        rhs = jnp.concatenate([w_ref[0, bb, j * p:(j + 1) * p, :] for bb in range(bt)], axis=1)
        u_scr[j] = _dot(fa, rhs)
    rt_rows = 64
    for i in range(nb):
        def body(rt, carry):
            r0 = pl.multiple_of(rt * rt_rows, rt_rows)
            for bb in range(bt):
                ls = slice(bb * LANE, (bb + 1) * LANE)
                acc_re = jnp.zeros((rt_rows, LANE), F32)
                acc_im = jnp.zeros((rt_rows, LANE), F32)
                for j in range(nb):
                    s = i - j + nb - 1
                    gre = gs_ref[0, s, pl.ds(r0, rt_rows), :]
                    gim = gs_ref[0, s, pl.ds(p + r0, rt_rows), :]
                    ure = u_scr[j, pl.ds(r0, rt_rows), ls]
                    uim = u_scr[j, pl.ds(p + r0, rt_rows), ls]
                    acc_re = acc_re + (gre * ure - gim * uim)
                    acc_im = acc_im + (gre * uim + gim * ure)
                y_scr[pl.ds(r0, rt_rows), ls] = acc_re
                y_scr[pl.ds(p + r0, rt_rows), ls] = acc_im
            return carry
        lax.fori_loop(0, p // rt_rows, body, 0)
        conv = _dot(iv_ref[...], _bf(y_scr[...]))
        sl = slice(i * p, (i + 1) * p)
        for bb in range(bt):
            wi = w_ref[0, bb, sl, :].astype(F32)
            o_ref[0, bb, sl, :] = (gate_ref[0, bb, sl, :].astype(F32)
                                   * (conv[:, bb * LANE:(bb + 1) * LANE] + wi * hb_ref[...])).astype(o_ref.dtype)


def _hyena(w_t, gate_t, filt_params, hy_bias, b, l):
    nct = w_t.shape[0]
    p = min(DFT_BLOCK, l)
    nb = l // p
    nseg = 2 * nb - 1
    bt = min(b, HY_BATCH if nb == 1 else HY_BATCH_LONG)
    fa, iv = _dft_tables(p)
    spectra = _hyena_spectra(l, p, fa, *filt_params)
    w4 = w_t.reshape(nct, b, l, LANE)
    g4 = gate_t.reshape(nct, b, l, LANE)
    hbr = hy_bias.reshape(1, -1)
    act = pl.BlockSpec((1, bt, l, LANE), lambda ct, bi: (ct, bi, 0, 0))

    def full(a):
        return pl.BlockSpec(a.shape, lambda ct, bi: (0, 0), pipeline_mode=pl.Buffered(1))

    out = pl.pallas_call(
        functools.partial(_hyena_kernel, p=p),
        grid=(nct, b // bt),
        in_specs=[act, act,
                  pl.BlockSpec((1, nseg, 2 * p, LANE), lambda ct, bi: (ct, 0, 0, 0), pipeline_mode=pl.Buffered(1)),
                  pl.BlockSpec((1, LANE), lambda ct, bi: (0, ct)),
                  full(fa), full(iv)],
        out_specs=act,
        out_shape=jax.ShapeDtypeStruct((nct, b, l, LANE), BF16),
        scratch_shapes=[pltpu.VMEM((nb, 2 * p, bt * LANE), F32), pltpu.VMEM((2 * p, bt * LANE), F32)],
        compiler_params=_params("arbitrary", "arbitrary"),
        name="hyena_conv",
    )(w4, g4, spectra, hbr, fa, iv)
    return out.reshape(nct, b * l, LANE)


def _proj_out0_kernel(x_ref, yf_ref, xs_ref, dt_ref, z_ref, yh_ref, mod_ref, mod1_ref, naw_ref, w_ref, dtb_ref,
                      alog_ref, dskip_ref, ex_ref, nw1_ref, w1_ref, wvt_ref, init_ref,
                      o_ref, q_ref, k_ref, v_ref, g_ref, s_scr, *, seq_len, n_tiles):
    j = n_tiles - 1 - pl.program_id(0)
    tm, d = x_ref.shape
    y_b = _ssd_tile(1, xs_ref[...], dt_ref[...], s_scr, init_ref, ((j + 1) * tm) % seq_len == 0,
                    dtb_ref[...], alog_ref[...], dskip_ref[1:2, :], ex_ref[1])
    ya = _rms((yf_ref[...].astype(F32) + y_b) * _silu(z_ref[...].astype(F32)), naw_ref[...])
    yh = jnp.concatenate([yh_ref[t] for t in range(yh_ref.shape[0])], axis=1)
    da = ya.shape[1]
    acc = _dot(_bf(ya), w_ref[0:da, :]) + _dot(_bf(yh), w_ref[da:, :])
    x_new = x_ref[...] + mod_ref[0][:, 2 * d:3 * d] * acc
    o_ref[...] = x_new

    m1 = mod1_ref[0]
    h1 = _bf(_rms(x_new, nw1_ref[...]) * (1.0 + m1[:, d:2 * d]) + m1[:, 0:d])
    step = 2 * LANE
    for off, t_ref in ((0, q_ref), (d, k_ref)):
        for c0 in range(0, d, step):
            res = _dot(h1, w1_ref[:, off + c0:off + c0 + step])
            for t in range(step // LANE):
                t_ref[c0 // LANE + t] = res[:, t * LANE:(t + 1) * LANE].astype(t_ref.dtype)
    v_ref[...] = _dot_nt(wvt_ref[...], h1).astype(v_ref.dtype)
    for c0 in range(0, d, step):
        g_ref[:, c0:c0 + step] = _dot(h1, w1_ref[:, 3 * d + c0:3 * d + c0 + step]).astype(g_ref.dtype)


def _proj_out0(x2d, y_f, xs, dt_raw, z, yh_t, mod, mod1, norm_a_w, w_bf, ssd_consts, norm_w1, w_in1_bf, wv_t, init,
               seq_len, rows_per_mod, mod_base):
    m_rows, d = x2d.shape
    tm = ROW_TILE
    assert m_rows % tm == 0 and rows_per_mod % tm == 0 and seq_len % tm == 0 and tm % CHUNK == 0
    da = y_f.shape[1]
    nt = yh_t.shape[0]
    n_tiles = m_rows // tm
    consts = [norm_a_w, w_bf] + list(ssd_consts) + [norm_w1, w_in1_bf, wv_t]

    def rowspec(wd):
        return pl.BlockSpec((tm, wd), lambda i: (n_tiles - 1 - i, 0))

    def tiles(n):
        return pl.BlockSpec((n, tm, LANE), lambda i: (0, n_tiles - 1 - i, 0))

    def modspec():
        return pl.BlockSpec((1, 1, 3 * d), lambda i: (mod_base + ((n_tiles - 1 - i) * tm) // rows_per_mod, 0, 0))

    def const(a):
        nd = a.ndim
        return pl.BlockSpec(a.shape, lambda i: (0,) * nd, pipeline_mode=pl.Buffered(1))

    return pl.pallas_call(
        functools.partial(_proj_out0_kernel, seq_len=seq_len, n_tiles=n_tiles),
        grid=(n_tiles,),
        in_specs=[rowspec(d), rowspec(da), rowspec(xs.shape[1]), rowspec(LANE), rowspec(da), tiles(nt),
                  modspec(), modspec()]
                 + [const(a) for a in consts]
                 + [pl.BlockSpec((1, 1, da, N_STATE), lambda i: (((n_tiles - 1 - i) * tm) // seq_len, 1, 0, 0))],
        out_specs=[rowspec(d), tiles(d // LANE), tiles(d // LANE),
                   pl.BlockSpec((d, tm), lambda i: (0, n_tiles - 1 - i)), rowspec(d)],
        out_shape=[jax.ShapeDtypeStruct((m_rows, d), F32),
                   jax.ShapeDtypeStruct((d // LANE, m_rows, LANE), BF16),
                   jax.ShapeDtypeStruct((d // LANE, m_rows, LANE), BF16),
                   jax.ShapeDtypeStruct((d, m_rows), BF16), jax.ShapeDtypeStruct((m_rows, d), BF16)],
        scratch_shapes=[pltpu.VMEM((N_STATE, da), F32)],
        compiler_params=_params("arbitrary"),
        name="proj_out0",
    )(x2d, y_f, xs, dt_raw, z, yh_t, mod, mod1, *consts, init)


def _proj_out1_kernel(x_ref, o_ref_in, g_ref, mod_ref, fw_ref, w_ref, y_ref):
    d = x_ref.shape[1]
    o = jnp.concatenate([o_ref_in[t] for t in range(o_ref_in.shape[0])], axis=1).astype(F32)
    a = o * _silu(g_ref[...].astype(F32))
    acc = _dot(_bf(a), w_ref[...])
    gate = mod_ref[0][:, 2 * d:3 * d]
    y_ref[...] = _rms(x_ref[...] + gate * acc, fw_ref[...])


def _proj_out1(x2d, o_t, g, mod, final_w, w_bf, rows_per_mod, mod_base):
    m_rows, d = x2d.shape
    tm = ROW_TILE
    assert m_rows % tm == 0 and rows_per_mod % tm == 0
    nt = o_t.shape[0]
    return pl.pallas_call(
        _proj_out1_kernel,
        grid=(m_rows // tm,),
        in_specs=[pl.BlockSpec((tm, d), lambda i: (i, 0)),
                  pl.BlockSpec((nt, tm, LANE), lambda i: (0, i, 0)),
                  pl.BlockSpec((tm, g.shape[1]), lambda i: (i, 0)),
                  pl.BlockSpec((1, 1, 3 * d), lambda i: (mod_base + (i * tm) // rows_per_mod, 0, 0)),
                  pl.BlockSpec((1, d), lambda i: (0, 0)),
                  pl.BlockSpec(w_bf.shape, lambda i: (0, 0), pipeline_mode=pl.Buffered(1))],
        out_specs=pl.BlockSpec((tm, d), lambda i: (i, 0)),
        out_shape=jax.ShapeDtypeStruct((m_rows, d), F32),
        compiler_params=_params("arbitrary"),
        name="proj_out1",
    )(x2d, o_t, g, mod, final_w, w_bf)


def _ctx_layer_kernel(x_ref, mod_ref, nw_ref, wi_ref, wo_ref, fw_ref, y_ref, ck_ref, cv_ref):
    tm, d = x_ref.shape
    nseq = ck_ref.shape[0]
    l = tm // nseq
    m = mod_ref[0]
    x = x_ref[...]
    hb = _bf(_rms(x, nw_ref[...]) * (1.0 + m[:, d:2 * d]) + m[:, 0:d])
    qb = _bf(_dot(hb, wi_ref[:, 0:d]) * (HD ** -0.5))
    k = _dot(hb, wi_ref[:, d:2 * d])
    v = _dot(hb, wi_ref[:, 2 * d:3 * d])
    g = _dot(hb, wi_ref[:, 3 * d:4 * d])
    nh = d // HD
    o_rows = []
    for s in range(nseq):
        rows = slice(s * l, (s + 1) * l)
        scores, vbs = [], []
        for h in range(nh):
            sl = slice(h * HD, (h + 1) * HD)
            kh = k[rows, sl]
            vh = v[rows, sl]
            ck_ref[s, 0, h] = kh
            cv_ref[s, 0, h] = vh
            vbs.append(_bf(vh))
            scores.append(_dot_nt(_bf(kh), qb[rows, sl]))
        s_all = jnp.concatenate(scores, axis=1)
        pexp = jnp.exp(s_all - jnp.max(s_all, axis=0, keepdims=True))
        den = jnp.sum(pexp, axis=0, keepdims=True)
        pb = _bf(pexp)
        o_t = jnp.concatenate(
            [lax.dot_general(vbs[h], pb[:, h * l:(h + 1) * l], (((0,), (0,)), ((), ())),
                             preferred_element_type=F32) / den[:, h * l:(h + 1) * l] for h in range(nh)], axis=0)
        o_rows.append(o_t.T)
    a = jnp.concatenate(o_rows, axis=0) * _silu(g)
    acc = _dot(_bf(a), wo_ref[...])
    y_ref[...] = _rms(x + m[:, 2 * d:3 * d] * acc, fw_ref[...])


def _ctx_layer(x2d, b, l, mod, norm_w, w_in_bf, w_out_bf, final_w):
    m_rows, d = x2d.shape
    tm = ROW_TILE
    assert m_rows % tm == 0 and tm % l == 0
    nseq = tm // l
    nh = d // HD
    cache_spec = pl.BlockSpec((nseq, 1, nh, l, HD), lambda i: (i, 0, 0, 0, 0))
    cache_shape = jax.ShapeDtypeStruct((b, 1, nh, l, HD), F32)

    def const(a):
        nd = a.ndim
        return pl.BlockSpec(a.shape, lambda i: (0,) * nd, pipeline_mode=pl.Buffered(1))

    return pl.pallas_call(
        _ctx_layer_kernel,
        grid=(m_rows // tm,),
        in_specs=[pl.BlockSpec((tm, d), lambda i: (i, 0)),
                  pl.BlockSpec((1, 1, 3 * d), lambda i: (0, 0, 0)),
                  const(norm_w), const(w_in_bf), const(w_out_bf), const(final_w)],
        out_specs=[pl.BlockSpec((tm, d), lambda i: (i, 0)), cache_spec, cache_spec],
        out_shape=[jax.ShapeDtypeStruct((m_rows, d), F32), cache_shape, cache_shape],
        compiler_params=_params("arbitrary"),
        name="ctx_layer",
    )(x2d, mod, norm_w, w_in_bf, w_out_bf, final_w)


def _na_bias_kernel(rpb_ref, o_ref):
    h = pl.program_id(0)
    ndr = 2 * WIN_H - 1
    ndc = 2 * WIN_W - 1
    ck = lax.broadcasted_iota(jnp.int32, (GRID_W, LANE), 0)
    lane = lax.broadcasted_iota(jnp.int32, (GRID_W, LANE), 1)
    cq = lane & (GRID_W - 1)
    first = lane < GRID_W
    dc = jnp.clip(ck - cq + (WIN_W - 1), 0, ndc - 1)
    col0 = jnp.clip(cq - WIN_W // 2, 0, GRID_W - WIN_W)
    col_in = (ck >= col0) & (ck < col0 + WIN_W)
    dc_is = [(dc == e) & col_in for e in range(ndc)]
    tables = []
    for dr in range(ndr):
        t = jnp.full((GRID_W, LANE), NEG_INF, F32)
        for e in range(ndc):
            t = jnp.where(dc_is[e], rpb_ref[(h * ndr + dr) * ndc + e], t)
        tables.append(t)
    for ip in range(NA_BAND):
        for ap in range(NA_QTILE // 2):
            x = ip - 2 * ap + (WIN_H - 1) - NA_QROWS // 2
            o_ref[0, ip * GRID_W:(ip + 1) * GRID_W, ap * LANE:(ap + 1) * LANE] = jnp.where(
                first, tables[x], tables[x - 1])


def _na_bias_tables(rpb):
    nh = rpb.shape[0]
    shape = (NA_BAND * GRID_W, NA_QTILE * GRID_W)
    return pl.pallas_call(
        _na_bias_kernel,
        grid=(nh,),
        in_specs=[pl.BlockSpec(memory_space=pltpu.SMEM)],
        out_specs=pl.BlockSpec((1,) + shape, lambda h: (h, 0, 0)),
        out_shape=jax.ShapeDtypeStruct((nh,) + shape, F32),
        compiler_params=_params("arbitrary"),
        name="na_bias",
    )(rpb.reshape(-1))


def _na_kernel(q_ref, kp_ref, kc_ref, kn_ref, vp_ref, vc_ref, vn_ref, ck_ref, cv_ref, bias_ref, o_ref, mask_scr,
               s_scr, *, n_rows):
    rb = pl.program_id(1)
    qrows = NA_QROWS
    nq = qrows * GRID_W
    half = (qrows // 2) * GRID_W
    qt = NA_QTILE * GRID_W
    nband = NA_BAND * GRID_W

    def local_keys(g0):
        if g0 < half:
            return kp_ref, vp_ref, nq - half + g0
        if g0 < half + nq:
            return kc_ref, vc_ref, g0 - half
        return kn_ref, vn_ref, g0 - half - nq

    @pl.when(pl.program_id(2) == 0)
    def _():
        for t in range(qrows // NA_QTILE):
            i = t * NA_QTILE + lax.broadcasted_iota(jnp.int32, (nband, qt), 0) // GRID_W
            a = t * NA_QTILE + lax.broadcasted_iota(jnp.int32, (nband, qt), 1) // GRID_W
            r = rb * qrows + a
            kr = rb * qrows - qrows // 2 + i
            rs = jnp.clip(r - WIN_H // 2, 0, n_rows - WIN_H)
            mask_scr[t] = jnp.where((kr >= rs) & (kr < rs + WIN_H), 0.0, NEG_INF)

    rows = []
    for j in range(LANE // HD):
        sl = slice(j * HD, (j + 1) * HD)
        ckb = _bf(ck_ref[0, j])
        cvt = cv_ref[0, 0, sl, :]
        tiles = []
        nctx = ckb.shape[0]
        for t in range(qrows // NA_QTILE):
            k0 = t * NA_QTILE * GRID_W
            qh = q_ref[0, 0, t * qt:(t + 1) * qt, sl] * (HD ** -0.5)
            sc = s_scr.at[j * (qrows // NA_QTILE) + t]
            m = jnp.full((1, qt), NEG_INF, F32)
            for c0 in range(0, nband + nctx, NA_KCHUNK):
                if c0 < nband:
                    rs_ = slice(c0, c0 + NA_KCHUNK)
                    k_ref, _, off = local_keys(k0 + c0)
                    s = _dot_nt(k_ref[0, 0, off:off + NA_KCHUNK, sl], qh) + bias_ref[j, rs_, :] + mask_scr[t, rs_, :]
                else:
                    s = _dot_nt(ckb[c0 - nband:c0 - nband + NA_KCHUNK], qh)
                sc[c0:c0 + NA_KCHUNK, :] = s
                m = jnp.maximum(m, jnp.max(s, axis=0, keepdims=True))
            den = jnp.zeros((1, qt), F32)
            o = jnp.zeros((HD, qt), F32)
            for c0 in range(0, nband + nctx, NA_KCHUNK):
                pexp = jnp.exp(sc[c0:c0 + NA_KCHUNK, :] - m)
                den = den + jnp.sum(pexp, axis=0, keepdims=True)
                if c0 < nband:
                    _, v_ref, off = local_keys(k0 + c0)
                    vt = v_ref[sl, off:off + NA_KCHUNK]
                else:
                    vt = cvt[:, c0 - nband:c0 - nband + NA_KCHUNK]
                o = o + _dot(vt, _bf(pexp))
            tiles.append(o / den)
        rows.append(jnp.concatenate(tiles, axis=1))
    o_ref[0, 0] = jnp.concatenate(rows, axis=0).T.astype(o_ref.dtype)


def _na_attn(q_t, k_t, v_c, cache_k, cache_vt, bias, b, l):
    npair = q_t.shape[0]
    hpp = LANE // HD
    n_rows = l // GRID_W
    nrb = n_rows // NA_QROWS
    nq = NA_QROWS * GRID_W
    lc = cache_k.shape[2]
    q4, k4 = (a.reshape(npair, b, l, LANE) for a in (q_t, k_t))

    def prev_blk(rb):
        return jnp.maximum(rb - 1, 0)

    def next_blk(rb):
        return jnp.minimum(rb + 1, nrb - 1)

    def same_blk(rb):
        return rb

    def tok(f):
        return pl.BlockSpec((1, 1, nq, LANE), lambda bi, rb, hp: (hp, bi, f(rb), 0))

    def chan(f):
        return pl.BlockSpec((LANE, nq), lambda bi, rb, hp: (hp, bi * nrb + f(rb)))

    out = pl.pallas_call(
        functools.partial(_na_kernel, n_rows=n_rows),
        grid=(b, nrb, npair),
        in_specs=[tok(same_blk), tok(prev_blk), tok(same_blk), tok(next_blk),
                  chan(prev_blk), chan(same_blk), chan(next_blk),
                  pl.BlockSpec((1, hpp, lc, HD), lambda bi, rb, hp: (bi, hp, 0, 0)),
                  pl.BlockSpec((1, 1, LANE, lc), lambda bi, rb, hp: (bi, hp, 0, 0)),
                  pl.BlockSpec((hpp,) + bias.shape[1:], lambda bi, rb, hp: (hp, 0, 0))],
        out_specs=tok(same_blk),
        out_shape=jax.ShapeDtypeStruct((npair, b, l, LANE), BF16),
        scratch_shapes=[pltpu.VMEM((NA_QROWS // NA_QTILE,) + bias.shape[1:], F32),
                        pltpu.VMEM((hpp * (NA_QROWS // NA_QTILE), bias.shape[1] + lc, bias.shape[2]), F32)],
        compiler_params=_params("arbitrary", "arbitrary", "arbitrary"),
        name="na_attn",
    )(q4, k4, k4, k4, v_c, v_c, v_c, cache_k, cache_vt, bias)
    return out.reshape(npair, b * l, LANE)


def _ctx_layer0_kernel(x_ref, mod_ref, nw_ref, w_ref, cwa_ref, cba_ref, cwb_ref, cbb_ref, dtb_ref, alog_ref,
                       dskip_ref, ex_ref, naw_ref, wo_ref, gs_ref, hyb_ref, fa_ref, iv_ref, o_ref, fin_ref,
                       *, seq_len, cols):
    tm, d = x_ref.shape
    o_z, o_xbc, o_u, o_g, o_dt = cols
    d_a, d_xbc, d_b = o_xbc - o_z, o_u - o_xbc, o_dt - o_g
    nseq, q, p = tm // seq_len, CHUNK, seq_len
    nc = seq_len // q
    step = 2 * LANE
    m = mod_ref[0]
    x = x_ref[...]
    hb = _bf(_rms(x, nw_ref[...]) * (1.0 + m[:, d:2 * d]) + m[:, 0:d])

    def conv(off, cw_ref, cb_ref, coff):
        return _conv3_rows(_dot(hb, w_ref[:, off:off + step]), 0.0, 0.0, seq_len,
                           cw_ref[:, coff:coff + step], cb_ref[:, coff:coff + step])

    xs = jnp.concatenate([_silu(conv(o_xbc + c0, cwa_ref, cba_ref, c0)) for c0 in range(0, d_xbc, step)], axis=1)
    dt_raw = _dot(hb, w_ref[:, o_dt:o_dt + LANE])
    xa = xs[:, 0:d_a]
    bm = _bf(xs[:, d_a:d_a + N_GROUPS * N_STATE])
    cm = _bf(xs[:, d_a + N_GROUPS * N_STATE:d_xbc])
    dtb, alog = dtb_ref[...], alog_ref[...]
    y_rows = []
    for s in range(nseq):
        state = [jnp.zeros((N_STATE, d_a), F32), jnp.zeros((N_STATE, d_a), F32)]
        y_chunks = [None] * nc
        for direction, order in ((0, range(nc)), (1, reversed(range(nc)))):
            for c in order:
                rows = slice(s * seq_len + c * q, s * seq_len + (c + 1) * q)
                y, state[direction] = _ssd_chunk(direction, xa[rows], bm[rows], cm[rows], dt_raw[rows],
                                                 state[direction], dtb, alog,
                                                 dskip_ref[direction:direction + 1, :], ex_ref[direction])
                y_chunks[c] = y if y_chunks[c] is None else y_chunks[c] + y
        for direction in range(2):
            for t in range(d_a // LANE):
                fin_ref[s, direction, t * LANE:(t + 1) * LANE, :] = state[direction][:, t * LANE:(t + 1) * LANE].T
        y_rows += y_chunks
    z = jnp.concatenate([_dot(hb, w_ref[:, o_z + c0:o_z + c0 + step]) for c0 in range(0, d_a, step)], axis=1)
    ya = _rms(jnp.concatenate(y_rows, axis=0) * _silu(z), naw_ref[...])

    yh_cols = []
    for c0 in range(0, d_b, step):
        wv = conv(o_u + d_b + c0, cwb_ref, cbb_ref, d_b + c0) * conv(o_u + 2 * d_b + c0, cwb_ref, cbb_ref, 2 * d_b + c0)
        gate = conv(o_u + c0, cwb_ref, cbb_ref, c0) * _silu(_dot(hb, w_ref[:, o_g + c0:o_g + c0 + step]))
        spec = jnp.concatenate([gs_ref[c0 // LANE + t, 0] for t in range(step // LANE)], axis=1)
        g_re, g_im = spec[0:p], spec[p:]
        outs = []
        for s in range(nseq):
            rows = slice(s * seq_len, (s + 1) * seq_len)
            u = _dot(fa_ref[...], _bf(wv[rows]))
            u_re, u_im = u[0:p], u[p:]
            prod = jnp.concatenate([g_re * u_re - g_im * u_im, g_re * u_im + g_im * u_re], axis=0)
            lc = _dot(iv_ref[...], _bf(prod))
            outs.append(gate[rows] * (lc + wv[rows] * hyb_ref[:, c0:c0 + step]))
        yh_cols.append(jnp.concatenate(outs, axis=0))
    yh = jnp.concatenate(yh_cols, axis=1)

    acc = _dot(_bf(ya), wo_ref[0:d_a, :]) + _dot(_bf(yh), wo_ref[d_a:, :])
    o_ref[...] = x + m[:, 2 * d:3 * d] * acc


def _ctx_layer0(x2d, b, l, mod, norm_w, w_in_bf, cols, p, w_out_bf, filt_params):
    m_rows, d = x2d.shape
    tm = ROW_TILE
    assert m_rows % tm == 0 and tm % l == 0 and l <= DFT_BLOCK and l % CHUNK == 0
    nseq = tm // l
    dm = N_HEADS * HD
    fa, iv = _dft_tables(l)
    spectra = _hyena_spectra(l, l, fa, *filt_params)
    consts = ([norm_w, w_in_bf, p["conv_a_w"], p["conv_a_b"].reshape(1, -1), p["conv_b_w"], p["conv_b_b"].reshape(1, -1)]
              + list(_ssd_consts(p["dt_bias"], p["a_log"], p["d_skip"]))
              + [p["norm_a_w"], w_out_bf, spectra, p["hy_bias"].reshape(1, -1), fa, iv])

    def const(a):
        nd = a.ndim
        return pl.BlockSpec(a.shape, lambda i: (0,) * nd, pipeline_mode=pl.Buffered(1))

    return pl.pallas_call(
        functools.partial(_ctx_layer0_kernel, seq_len=l, cols=cols),
        grid=(m_rows // tm,),
        in_specs=[pl.BlockSpec((tm, d), lambda i: (i, 0)), pl.BlockSpec((1, 1, 3 * d), lambda i: (0, 0, 0))]
                 + [const(a) for a in consts],
        out_specs=[pl.BlockSpec((tm, d), lambda i: (i, 0)),
                   pl.BlockSpec((nseq, 2, dm, N_STATE), lambda i: (i, 0, 0, 0))],
        out_shape=[jax.ShapeDtypeStruct((m_rows, d), F32), jax.ShapeDtypeStruct((b, 2, dm, N_STATE), F32)],
        compiler_params=_params("arbitrary"),
        name="ctx_layer0",
    )(x2d, mod, *consts)


def _reorder_kernel(w_ref, o_ref, *, o_dt, n_dt):
    rows, n = w_ref.shape
    rest = n - o_dt - n_dt
    o_ref[:, 0:o_dt] = w_ref[:, 0:o_dt].astype(o_ref.dtype)
    o_ref[:, o_dt:o_dt + rest] = w_ref[:, o_dt + n_dt:n].astype(o_ref.dtype)
    tail = jnp.concatenate([w_ref[:, o_dt:o_dt + n_dt], jnp.zeros((rows, LANE - n_dt), F32)], axis=1)
    o_ref[:, o_dt + rest:o_dt + rest + LANE] = tail.astype(o_ref.dtype)


def _reorder_w_in(w, o_dt, n_dt):
    d, n = w.shape
    n_out = n - n_dt + LANE
    tr = LANE
    return pl.pallas_call(
        functools.partial(_reorder_kernel, o_dt=o_dt, n_dt=n_dt),
        grid=(d // tr,),
        in_specs=[pl.BlockSpec((tr, n), lambda i: (i, 0))],
        out_specs=pl.BlockSpec((tr, n_out), lambda i: (i, 0)),
        out_shape=jax.ShapeDtypeStruct((d, n_out), BF16),
        compiler_params=_params("arbitrary"),
        name="reorder_w_in",
    )(w)


def _layer0_cols(d_b):
    dm = N_HEADS * HD
    d_xbc = dm + 2 * N_GROUPS * N_STATE
    return (0, dm, dm + d_xbc, dm + d_xbc + 3 * d_b, dm + d_xbc + 4 * d_b)


def _latent_layer0(x2d, b, l, mod, mod1, mod_base, norm_w, w_in_bf, w_out_bf, p, init, filt_params, norm_w1,
                   w_in1_bf, wv_t):
    cols = _layer0_cols(p["hy_bias"].shape[0])
    consts = _ssd_consts(p["dt_bias"], p["a_log"], p["d_skip"])
    z, xs, w_t, gate_t, dt_raw, y_f = _proj_in0(x2d, mod, norm_w, w_in_bf, cols, p["conv_a_w"], p["conv_a_b"],
                                                p["conv_b_w"], p["conv_b_b"], consts, init, l, l, mod_base)
    yh_t = _hyena(w_t, gate_t, filt_params, p["hy_bias"], b, l)
    return _proj_out0(x2d, y_f, xs, dt_raw, z, yh_t, mod, mod1, p["norm_a_w"], w_out_bf, consts, norm_w1, w_in1_bf,
                      wv_t, init, l, l, mod_base)


def kernel(x_prompt, x_sample, state_ssd, cache_k, cache_v, c, c_ctx, norm_w, w_ada, b_ada, w_in_e, w_out_e, conv_a_w, conv_a_b, dt_bias, a_log, d_skip, norm_a_w, conv_b_w, conv_b_b, hf_w1, hf_b1, hf_w2, hf_b2, hf_w3, hf_freq, hy_bias, w_in_o, w_out_o, rpb, final_norm_w):
    bp, lp, d = x_prompt.shape
    bs, ls, _ = x_sample.shape
    dm = N_HEADS * HD
    d_xbc = dm + 2 * N_GROUPS * N_STATE
    n_dt = 2 * N_HEADS

    cvecs = jnp.concatenate([c_ctx[None], c, jnp.zeros((SUBLANE - 1 - bs, d), F32)], axis=0)
    mods = _ada_mods(cvecs, w_ada, b_ada)

    xp = x_prompt.reshape(bp * lp, d)
    xs = x_sample.reshape(bs * ls, d)

    w_in0 = _reorder_w_in(w_in_e[0], dm + d_xbc, n_dt)
    w_out0 = w_out_e[0].astype(BF16)
    p0 = dict(conv_a_w=conv_a_w[0], conv_a_b=conv_a_b[0], dt_bias=dt_bias[0], a_log=a_log[0], d_skip=d_skip[0],
              norm_a_w=norm_a_w[0].reshape(1, -1), conv_b_w=conv_b_w[0], conv_b_b=conv_b_b[0], hy_bias=hy_bias[0])
    mod0 = mods[0].reshape(SUBLANE, 1, 3 * d)
    nw0 = norm_w[0].reshape(1, d)
    hf = (hf_w1[0], hf_b1[0], hf_w2[0], hf_b2[0], hf_w3[0], hf_freq[0])
    w_in1 = w_in_o[0].astype(BF16)
    w_out1 = w_out_o[0].astype(BF16)
    wv_t = w_in_o[0][:, 2 * d:3 * d].T.astype(BF16)
    mod1 = mods[1].reshape(SUBLANE, 1, 3 * d)
    nw1 = norm_w[1].reshape(1, d)
    xp, fin = _ctx_layer0(xp, bp, lp, mod0, nw0, w_in0, _layer0_cols(hy_bias.shape[1]), p0, w_out0, hf)
    init_s = state_ssd[:, 0].reshape(bs, 2, dm, N_STATE)
    xs, q_t, k_t, v_c, g = _latent_layer0(xs, bs, ls, mod0, mod1, 1, nw0, w_in0, w_out0, p0, init_s, hf, nw1, w_in1,
                                          wv_t)
    new_state_ssd = fin.reshape(bp, 1, 2, N_HEADS, HD, N_STATE)

    fw = final_norm_w.reshape(1, d)
    y_prompt, new_cache_k, new_cache_v = _ctx_layer(xp, bp, lp, mod1, nw1, w_in1, w_out1, fw)
    y_prompt = y_prompt.reshape(bp, lp, d)

    bias = _na_bias_tables(rpb[0])
    lc = cache_v.shape[3]
    cache_vt = jnp.swapaxes(cache_v[:, 0], 2, 3).reshape(bs, d // LANE, LANE, lc).astype(BF16)
    o_t = _na_attn(q_t, k_t, v_c, cache_k[:, 0], cache_vt, bias, bs, ls)
    y_sample = _proj_out1(xs, o_t, g, mod1, fw, w_out1, ls, 1).reshape(bs, ls, d)

    return (y_prompt, y_sample, new_state_ssd, new_cache_k, new_cache_v)
```
